```python
import jax, jax.numpy as jnp
from jax import lax
import numpy as np

D_MODEL = 2048
BATCH = 8
SEQ = 8192
DEPTH = 4

N_MIXERS = 2
N_A_LAYERS = (DEPTH + 1) // 2
N_B_LAYERS = DEPTH // 2
EPS = 1e-6

EXPAND = 2
CHUNK = 128
A_WIDTH = EXPAND * D_MODEL
A_GROUPS = 16
A_GROUP_DIM = A_WIDTH // A_GROUPS

B_HEAD_DIM = 128
B_HEADS = D_MODEL // B_HEAD_DIM
B_WIDTH = B_HEADS * B_HEAD_DIM
Q_BLOCK = 128
FORGET_BIAS_MEAN = 3.0

kernel_name = "hybrid_gmlp_fox_interleaved"


def rms_norm(x, g):
    xf = x.astype(jnp.float32)
    y = xf * lax.rsqrt(jnp.mean(xf * xf, axis=-1, keepdims=True) + EPS)
    return (y * g.astype(jnp.float32)).astype(x.dtype)


def spatial_gating_layer(x, norm_g, w_in, v_norm_g, w_s, b_s, w_out):
    b, s, _ = x.shape
    h = rms_norm(x, norm_g)
    u, v, z = jnp.split(h @ w_in, 3, axis=-1)
    u = jax.nn.gelu(u)
    v = rms_norm(jax.nn.gelu(v), v_norm_g)
    n_chunks = s // CHUNK
    v = v.reshape(b, n_chunks, CHUNK, A_GROUPS, A_GROUP_DIM)
    causal = jnp.tril(jnp.ones((CHUNK, CHUNK), dtype=bool))
    w_causal = jnp.where(causal[None], w_s, 0)
    mixed = jnp.einsum('gts,bcsgd->bctgd', w_causal, v) + b_s.T[None, None, :, :, None]
    mixed = mixed.reshape(b, s, A_WIDTH)
    y = u * mixed * jax.nn.silu(z)
    return y @ w_out


def forgetting_attention_layer(x, norm_g, w_in, f_bias, q_norm_g, k_norm_g, w_out):
    b, s, _ = x.shape
    h = rms_norm(x, norm_g)
    proj = h @ w_in
    q, k, v, z, f_logit = jnp.split(
        proj, [B_WIDTH, 2 * B_WIDTH, 3 * B_WIDTH, 4 * B_WIDTH], axis=-1)
    q = rms_norm(q.reshape(b, s, B_HEADS, B_HEAD_DIM), q_norm_g)
    k = rms_norm(k.reshape(b, s, B_HEADS, B_HEAD_DIM), k_norm_g)
    v = v.reshape(b, s, B_HEADS, B_HEAD_DIM)
    log_f = jax.nn.log_sigmoid((f_logit + f_bias).astype(jnp.float32))
    cum = jnp.cumsum(log_f, axis=1).transpose(0, 2, 1)
    scale = B_HEAD_DIM ** -0.5
    q_t = q.transpose(0, 2, 1, 3)
    k_t = k.transpose(0, 2, 1, 3)
    v_t = v.transpose(0, 2, 1, 3)
    n_blocks = s // Q_BLOCK
    q_blocks = q_t.reshape(b, B_HEADS, n_blocks, Q_BLOCK, B_HEAD_DIM).transpose(2, 0, 1, 3, 4)
    c_blocks = cum.reshape(b, B_HEADS, n_blocks, Q_BLOCK).transpose(2, 0, 1, 3)
    blk_idx = jnp.arange(n_blocks, dtype=jnp.int32)
    k_pos = jnp.arange(s, dtype=jnp.int32)

    def attend_block(args):
        q_blk, c_blk, i = args
        logits = jnp.einsum('bhqd,bhkd->bhqk', q_blk, k_t,
                            preferred_element_type=jnp.float32) * scale
        logits = logits + c_blk[..., :, None] - cum[:, :, None, :]
        q_pos = i * Q_BLOCK + jnp.arange(Q_BLOCK, dtype=jnp.int32)
        mask = k_pos[None, :] <= q_pos[:, None]
        logits = jnp.where(mask, logits, -jnp.inf)
        p = jax.nn.softmax(logits, axis=-1).astype(v_t.dtype)
        return jnp.einsum('bhqk,bhkd->bhqd', p, v_t)

    out = lax.map(attend_block, (q_blocks, c_blocks, blk_idx))
    out = out.transpose(1, 0, 3, 2, 4).reshape(b, s, B_WIDTH)
    y = out * jax.nn.silu(z)
    return y @ w_out


def _fwd_setup_inputs(seed: int = 0) -> dict:
    key = jax.random.key(seed)
    ks = jax.random.split(key, 13)
    f32 = jnp.float32
    x = jax.random.normal(ks[0], (BATCH, SEQ, D_MODEL), f32)
    a_norm_g = 1.0 + 0.05 * jax.random.normal(ks[1], (N_A_LAYERS, D_MODEL), f32)
    a_w_in = jax.random.normal(ks[2], (N_A_LAYERS, D_MODEL, 3 * A_WIDTH), f32) * D_MODEL ** -0.5
    a_v_norm_g = 1.0 + 0.05 * jax.random.normal(ks[3], (N_A_LAYERS, A_WIDTH), f32)
    a_w_s = jax.random.normal(ks[4], (N_A_LAYERS, A_GROUPS, CHUNK, CHUNK), f32) * CHUNK ** -0.5
    a_b_s = 1.0 + 0.1 * jax.random.normal(ks[5], (N_A_LAYERS, A_GROUPS, CHUNK), f32)
    a_w_out = jax.random.normal(ks[6], (N_A_LAYERS, A_WIDTH, D_MODEL), f32) * A_WIDTH ** -0.5
    b_norm_g = 1.0 + 0.05 * jax.random.normal(ks[7], (N_B_LAYERS, D_MODEL), f32)
    b_w_in = jax.random.normal(ks[8], (N_B_LAYERS, D_MODEL, 4 * B_WIDTH + B_HEADS), f32) * D_MODEL ** -0.5
    b_f_bias = FORGET_BIAS_MEAN + 0.5 * jax.random.normal(ks[9], (N_B_LAYERS, B_HEADS), f32)
    b_q_norm_g = 1.0 + 0.05 * jax.random.normal(ks[10], (N_B_LAYERS, B_HEAD_DIM), f32)
    b_k_norm_g = 1.0 + 0.05 * jax.random.normal(ks[11], (N_B_LAYERS, B_HEAD_DIM), f32)
    b_w_out = jax.random.normal(ks[12], (N_B_LAYERS, B_WIDTH, D_MODEL), f32) * B_WIDTH ** -0.5
    return {"x": x, "a_norm_g": a_norm_g, "a_w_in": a_w_in, "a_v_norm_g": a_v_norm_g,
            "a_w_s": a_w_s, "a_b_s": a_b_s, "a_w_out": a_w_out,
            "b_norm_g": b_norm_g, "b_w_in": b_w_in, "b_f_bias": b_f_bias,
            "b_q_norm_g": b_q_norm_g, "b_k_norm_g": b_k_norm_g, "b_w_out": b_w_out}


def _fwd_reference(x, a_norm_g, a_w_in, a_v_norm_g, a_w_s, a_b_s, a_w_out,
              b_norm_g, b_w_in, b_f_bias, b_q_norm_g, b_k_norm_g, b_w_out):
    for i in range(DEPTH):
        j = i // N_MIXERS
        if i % N_MIXERS == 0:
            x = x + spatial_gating_layer(x, a_norm_g[j], a_w_in[j], a_v_norm_g[j],
                                         a_w_s[j], a_b_s[j], a_w_out[j])
        else:
            x = x + forgetting_attention_layer(x, b_norm_g[j], b_w_in[j], b_f_bias[j],
                                               b_q_norm_g[j], b_k_norm_g[j], b_w_out[j])
    return x


import jax as _jax
import jax.numpy as _jnp

TWIN_FORMAT = 'train_step'
FWD_PARAMS = ['x', 'a_norm_g', 'a_w_in', 'a_v_norm_g', 'a_w_s', 'a_b_s', 'a_w_out', 'b_norm_g', 'b_w_in', 'b_f_bias', 'b_q_norm_g', 'b_k_norm_g', 'b_w_out']
TWIN_WEIGHTS = ['a_norm_g', 'a_w_in', 'a_v_norm_g', 'a_w_s', 'a_b_s', 'a_w_out', 'b_norm_g', 'b_w_in', 'b_f_bias', 'b_q_norm_g', 'b_k_norm_g', 'b_w_out']
TWIN_DIFF_INPUT = 'x'
TWIN_INPUTS = ['x', 'a_norm_g', 'a_w_in', 'a_v_norm_g', 'a_w_s', 'a_b_s', 'a_w_out', 'b_norm_g', 'b_w_in', 'b_f_bias', 'b_q_norm_g', 'b_k_norm_g', 'b_w_out', 'loss_target', 'm_a_norm_g', 'm_a_w_in', 'm_a_v_norm_g', 'm_a_w_s', 'm_a_b_s', 'm_a_w_out', 'm_b_norm_g', 'm_b_w_in', 'm_b_f_bias', 'm_b_q_norm_g', 'm_b_k_norm_g', 'm_b_w_out', 'v_a_norm_g', 'v_a_w_in', 'v_a_v_norm_g', 'v_a_w_s', 'v_a_b_s', 'v_a_w_out', 'v_b_norm_g', 'v_b_w_in', 'v_b_f_bias', 'v_b_q_norm_g', 'v_b_k_norm_g', 'v_b_w_out']
TWIN_OUTPUTS = ['loss', 'grad_x', 'grad_a_norm_g', 'grad_a_w_in', 'grad_a_v_norm_g', 'grad_a_w_s', 'grad_a_b_s', 'grad_a_w_out', 'grad_b_norm_g', 'grad_b_w_in', 'grad_b_f_bias', 'grad_b_q_norm_g', 'grad_b_k_norm_g', 'grad_b_w_out', 'delta_a_norm_g', 'delta_a_w_in', 'delta_a_v_norm_g', 'delta_a_w_s', 'delta_a_b_s', 'delta_a_w_out', 'delta_b_norm_g', 'delta_b_w_in', 'delta_b_f_bias', 'delta_b_q_norm_g', 'delta_b_k_norm_g', 'delta_b_w_out', 'new_m_a_norm_g', 'new_m_a_w_in', 'new_m_a_v_norm_g', 'new_m_a_w_s', 'new_m_a_b_s', 'new_m_a_w_out', 'new_m_b_norm_g', 'new_m_b_w_in', 'new_m_b_f_bias', 'new_m_b_q_norm_g', 'new_m_b_k_norm_g', 'new_m_b_w_out', 'new_v_a_norm_g', 'new_v_a_w_in', 'new_v_a_v_norm_g', 'new_v_a_w_s', 'new_v_a_b_s', 'new_v_a_w_out', 'new_v_b_norm_g', 'new_v_b_w_in', 'new_v_b_f_bias', 'new_v_b_q_norm_g', 'new_v_b_k_norm_g', 'new_v_b_w_out']
TWIN_LEAF_KINDS = {'loss': 'loss', 'grad_x': 'grad_x', 'grad_a_norm_g': 'grad_w', 'grad_a_w_in': 'grad_w', 'grad_a_v_norm_g': 'grad_w', 'grad_a_w_s': 'grad_w', 'grad_a_b_s': 'grad_w', 'grad_a_w_out': 'grad_w', 'grad_b_norm_g': 'grad_w', 'grad_b_w_in': 'grad_w', 'grad_b_f_bias': 'grad_w', 'grad_b_q_norm_g': 'grad_w', 'grad_b_k_norm_g': 'grad_w', 'grad_b_w_out': 'grad_w', 'delta_a_norm_g': 'delta_w', 'delta_a_w_in': 'delta_w', 'delta_a_v_norm_g': 'delta_w', 'delta_a_w_s': 'delta_w', 'delta_a_b_s': 'delta_w', 'delta_a_w_out': 'delta_w', 'delta_b_norm_g': 'delta_w', 'delta_b_w_in': 'delta_w', 'delta_b_f_bias': 'delta_w', 'delta_b_q_norm_g': 'delta_w', 'delta_b_k_norm_g': 'delta_w', 'delta_b_w_out': 'delta_w', 'new_m_a_norm_g': 'new_m', 'new_m_a_w_in': 'new_m', 'new_m_a_v_norm_g': 'new_m', 'new_m_a_w_s': 'new_m', 'new_m_a_b_s': 'new_m', 'new_m_a_w_out': 'new_m', 'new_m_b_norm_g': 'new_m', 'new_m_b_w_in': 'new_m', 'new_m_b_f_bias': 'new_m', 'new_m_b_q_norm_g': 'new_m', 'new_m_b_k_norm_g': 'new_m', 'new_m_b_w_out': 'new_m', 'new_v_a_norm_g': 'new_v', 'new_v_a_w_in': 'new_v', 'new_v_a_v_norm_g': 'new_v', 'new_v_a_w_s': 'new_v', 'new_v_a_b_s': 'new_v', 'new_v_a_w_out': 'new_v', 'new_v_b_norm_g': 'new_v', 'new_v_b_w_in': 'new_v', 'new_v_b_f_bias': 'new_v', 'new_v_b_q_norm_g': 'new_v', 'new_v_b_k_norm_g': 'new_v', 'new_v_b_w_out': 'new_v'}


def _forward(args):
    return _fwd_reference(*[args[k] for k in FWD_PARAMS])


def _output_shape():
    def fwd():
        inp = _fwd_setup_inputs(0)
        return _fwd_reference(*[inp[k] for k in FWD_PARAMS])
    out = _jax.eval_shape(fwd)
    return out.shape, out.dtype

N_MICROBATCH = 1
ADAM_LR = 0.001
ADAM_B1 = 0.9
ADAM_B2 = 0.999
ADAM_EPS = 1e-08
ADAM_WD = 0.01
ADAM_STEP = 10
PER_EXAMPLE_BATCH_AXIS = {'x': 0, 'loss_target': 0}
SHARED_INPUTS = []
_WEIGHT_DTYPES = {'a_norm_g': _jnp.float32, 'a_w_in': _jnp.float32, 'a_v_norm_g': _jnp.float32, 'a_w_s': _jnp.float32, 'a_b_s': _jnp.float32, 'a_w_out': _jnp.float32, 'b_norm_g': _jnp.float32, 'b_w_in': _jnp.float32, 'b_f_bias': _jnp.float32, 'b_q_norm_g': _jnp.float32, 'b_k_norm_g': _jnp.float32, 'b_w_out': _jnp.float32}
MOMENT_SCALE = {'a_norm_g': 1.734188e+01, 'a_w_in': 1.551961e-01, 'a_v_norm_g': 1.165812e+00, 'a_w_s': 1.552756e+00, 'a_b_s': 5.131409e+00, 'a_w_out': 3.341769e-01, 'b_norm_g': 2.213291e+00, 'b_w_in': 1.033590e-01, 'b_f_bias': 4.482831e+01, 'b_q_norm_g': 7.609562e+00, 'b_k_norm_g': 7.576061e+00, 'b_w_out': 5.918755e-02}


def _to_microbatches(a, axis):
    t = _jnp.moveaxis(a, axis, 0)
    t = t.reshape((N_MICROBATCH, t.shape[0] // N_MICROBATCH) + t.shape[1:])
    return _jnp.moveaxis(t, 1, axis + 1)


def setup_inputs(seed: int = 0) -> dict:
    inp = _fwd_setup_inputs(seed)
    key = _jax.random.fold_in(_jax.random.key(seed), 7919)
    shape, _ = _output_shape()
    out = dict(inp)
    out["loss_target"] = _jax.random.normal(_jax.random.fold_in(key, 0), shape, _jnp.float32)
    for i, name in enumerate(TWIN_WEIGHTS):
        w = inp[name].astype(_jnp.float32)
        if MOMENT_SCALE is None:
            s = _jnp.sqrt(_jnp.mean(_jnp.square(w)) + 1e-30)
        else:
            s = MOMENT_SCALE[name]
        km, kv = _jax.random.split(_jax.random.fold_in(key, i + 1))
        out[name] = w
        out["m_" + name] = s * _jax.random.normal(km, w.shape, _jnp.float32)
        out["v_" + name] = (s * s) * _jax.random.uniform(kv, w.shape, _jnp.float32, 0.5, 1.5)
    if N_MICROBATCH > 1:
        for name, axis in PER_EXAMPLE_BATCH_AXIS.items():
            out[name] = _to_microbatches(out[name], axis)
    return {'x': out['x'], 'a_norm_g': out['a_norm_g'], 'a_w_in': out['a_w_in'], 'a_v_norm_g': out['a_v_norm_g'], 'a_w_s': out['a_w_s'], 'a_b_s': out['a_b_s'], 'a_w_out': out['a_w_out'], 'b_norm_g': out['b_norm_g'], 'b_w_in': out['b_w_in'], 'b_f_bias': out['b_f_bias'], 'b_q_norm_g': out['b_q_norm_g'], 'b_k_norm_g': out['b_k_norm_g'], 'b_w_out': out['b_w_out'], 'loss_target': out['loss_target'], 'm_a_norm_g': out['m_a_norm_g'], 'm_a_w_in': out['m_a_w_in'], 'm_a_v_norm_g': out['m_a_v_norm_g'], 'm_a_w_s': out['m_a_w_s'], 'm_a_b_s': out['m_a_b_s'], 'm_a_w_out': out['m_a_w_out'], 'm_b_norm_g': out['m_b_norm_g'], 'm_b_w_in': out['m_b_w_in'], 'm_b_f_bias': out['m_b_f_bias'], 'm_b_q_norm_g': out['m_b_q_norm_g'], 'm_b_k_norm_g': out['m_b_k_norm_g'], 'm_b_w_out': out['m_b_w_out'], 'v_a_norm_g': out['v_a_norm_g'], 'v_a_w_in': out['v_a_w_in'], 'v_a_v_norm_g': out['v_a_v_norm_g'], 'v_a_w_s': out['v_a_w_s'], 'v_a_b_s': out['v_a_b_s'], 'v_a_w_out': out['v_a_w_out'], 'v_b_norm_g': out['v_b_norm_g'], 'v_b_w_in': out['v_b_w_in'], 'v_b_f_bias': out['v_b_f_bias'], 'v_b_q_norm_g': out['v_b_q_norm_g'], 'v_b_k_norm_g': out['v_b_k_norm_g'], 'v_b_w_out': out['v_b_w_out']}


def _loss(weights, diff, rest, loss_target):
    with _jax.named_scope("forward"):
        args = {**rest, TWIN_DIFF_INPUT: diff, **{k: w.astype(_WEIGHT_DTYPES[k]) for k, w in weights.items()}}
        y = _forward(args)
    with _jax.named_scope("loss_head"):
        err = _jnp.square(y.astype(_jnp.float32) - loss_target)
        return 0.5 * _jnp.sum(_jnp.mean(err, axis=-1)) if err.ndim else 0.5 * err


def _adamw(w, g, m, v):
    m = ADAM_B1 * m + (1.0 - ADAM_B1) * g
    v = ADAM_B2 * v + (1.0 - ADAM_B2) * _jnp.square(g)
    m_hat = m / (1.0 - ADAM_B1 ** ADAM_STEP)
    v_hat = v / (1.0 - ADAM_B2 ** ADAM_STEP)
    delta = -ADAM_LR * (m_hat / (_jnp.sqrt(v_hat) + ADAM_EPS) + ADAM_WD * w)
    return delta, m, v


def reference(x, a_norm_g, a_w_in, a_v_norm_g, a_w_s, a_b_s, a_w_out, b_norm_g, b_w_in, b_f_bias, b_q_norm_g, b_k_norm_g, b_w_out, loss_target, m_a_norm_g, m_a_w_in, m_a_v_norm_g, m_a_w_s, m_a_b_s, m_a_w_out, m_b_norm_g, m_b_w_in, m_b_f_bias, m_b_q_norm_g, m_b_k_norm_g, m_b_w_out, v_a_norm_g, v_a_w_in, v_a_v_norm_g, v_a_w_s, v_a_b_s, v_a_w_out, v_b_norm_g, v_b_w_in, v_b_f_bias, v_b_q_norm_g, v_b_k_norm_g, v_b_w_out):
    given = dict(x=x, a_norm_g=a_norm_g, a_w_in=a_w_in, a_v_norm_g=a_v_norm_g, a_w_s=a_w_s, a_b_s=a_b_s, a_w_out=a_w_out, b_norm_g=b_norm_g, b_w_in=b_w_in, b_f_bias=b_f_bias, b_q_norm_g=b_q_norm_g, b_k_norm_g=b_k_norm_g, b_w_out=b_w_out, loss_target=loss_target, m_a_norm_g=m_a_norm_g, m_a_w_in=m_a_w_in, m_a_v_norm_g=m_a_v_norm_g, m_a_w_s=m_a_w_s, m_a_b_s=m_a_b_s, m_a_w_out=m_a_w_out, m_b_norm_g=m_b_norm_g, m_b_w_in=m_b_w_in, m_b_f_bias=m_b_f_bias, m_b_q_norm_g=m_b_q_norm_g, m_b_k_norm_g=m_b_k_norm_g, m_b_w_out=m_b_w_out, v_a_norm_g=v_a_norm_g, v_a_w_in=v_a_w_in, v_a_v_norm_g=v_a_v_norm_g, v_a_w_s=v_a_w_s, v_a_b_s=v_a_b_s, v_a_w_out=v_a_w_out, v_b_norm_g=v_b_norm_g, v_b_w_in=v_b_w_in, v_b_f_bias=v_b_f_bias, v_b_q_norm_g=v_b_q_norm_g, v_b_k_norm_g=v_b_k_norm_g, v_b_w_out=v_b_w_out)
    weights = {n: given[n] for n in TWIN_WEIGHTS}
    shared = {n: given[n] for n in SHARED_INPUTS}
    per_example = {n: given[n] for n in ['x']}
    grad_fn = _jax.value_and_grad(_loss, argnums=(0, 1))

    def one_microbatch(ex, loss_target):
        ex = dict(ex)
        diff = ex.pop(TWIN_DIFF_INPUT)
        return grad_fn(weights, diff, {**shared, **ex}, loss_target)

    if N_MICROBATCH == 1:
        loss, (grad_w, grad_x) = one_microbatch(per_example, given["loss_target"])
    else:
        def body(carry, xs):
            loss_sum, grad_sum = carry
            l_k, (gw_k, gx_k) = one_microbatch(xs[0], xs[1])
            with _jax.named_scope("update"):
                return (loss_sum + l_k, _jax.tree.map(_jnp.add, grad_sum, gw_k)), gx_k

        init = (_jnp.zeros((), _jnp.float32), _jax.tree.map(_jnp.zeros_like, weights))
        (loss, grad_w), grad_x = _jax.lax.scan(body, init, (per_example, given["loss_target"]))
    with _jax.named_scope("update"):
        delta_w, new_m, new_v = {}, {}, {}
        for n in TWIN_WEIGHTS:
            delta_w[n], new_m[n], new_v[n] = _adamw(weights[n], grad_w[n], given["m_" + n], given["v_" + n])
    return (loss, grad_x, *[grad_w[n] for n in TWIN_WEIGHTS], *[delta_w[n] for n in TWIN_WEIGHTS],
            *[new_m[n] for n in TWIN_WEIGHTS], *[new_v[n] for n in TWIN_WEIGHTS])
```

```python
import functools

import jax
import jax.numpy as jnp
from jax import lax
from jax.experimental import pallas as pl
from jax.experimental.pallas import tpu as pltpu

F32 = jnp.float32
BF16 = jnp.bfloat16
MESH = pl.DeviceIdType.MESH
AXES = ("x", "y", "c")
N_DEV = 8
NORM_EPS = 1e-6
LANES = 128
VMEM_LIMIT = 56 * 1024 * 1024

ADAM_LR = 0.001
ADAM_B1 = 0.9
ADAM_B2 = 0.999
ADAM_EPS = 1e-08
ADAM_WD = 0.01
ADAM_STEP = 10

GELU_C0 = 0.7978845608028654
GELU_C1 = 0.044715

NT_DIMS = (((1,), (1,)), ((), ()))
TN_DIMS = (((0,), (0,)), ((), ()))


def _params(sem=None):
    return pltpu.CompilerParams(dimension_semantics=sem, vmem_limit_bytes=VMEM_LIMIT)


def _pick(n, target, unit):
    best = None
    for t in range(unit, min(n, target) + 1, unit):
        if n % t == 0:
            best = t
    return n if best is None else best


def _sigmoid(x):
    return 1.0 / (1.0 + jnp.exp(-x))


def _gelu(x):
    return 0.5 * x * (1.0 + jnp.tanh(GELU_C0 * x * (1.0 + GELU_C1 * x * x)))


def _gelu_and_grad(x):
    x2 = x * x
    t = jnp.tanh(GELU_C0 * x * (1.0 + GELU_C1 * x2))
    g = 0.5 * x * (1.0 + t)
    dg = 0.5 * (1.0 + t) + 0.5 * x * (1.0 - t * t) * (GELU_C0 * (1.0 + 3.0 * GELU_C1 * x2))
    return g, dg


def _dot(a, b):
    return jnp.dot(a, b, preferred_element_type=F32)


def _dot_nt(a, b):
    return lax.dot_general(a, b, NT_DIMS, preferred_element_type=F32)


def _dot_tn(a, b):
    return lax.dot_general(a, b, TN_DIMS, preferred_element_type=F32)


def _split3(v):
    hi = v.astype(BF16)
    r1 = v - hi.astype(F32)
    mid = r1.astype(BF16)
    lo = (r1 - mid.astype(F32)).astype(BF16)
    return hi, mid, lo


def _dev_index(p):
    return 4 * p[0] + 2 * p[1] + p[2]


def _all_gather(arrays, name):
    n = len(arrays)

    def body(*refs):
        ins, outs = refs[:n], refs[n:2 * n]
        send_sems, recv_sems, local_sems = refs[2 * n:]
        x, y, c = lax.axis_index("x"), lax.axis_index("y"), lax.axis_index("c")
        me, sibling = (x, y, c), (x, y, 1 - c)
        chips = [(1 - x, y), (x, 1 - y), (1 - x, 1 - y)]

        def copy(a, k, block, to, src=None):
            dst = outs[a].at[_dev_index(block)]
            return pltpu.make_async_remote_copy(
                src_ref=dst if src is None else src, dst_ref=dst,
                send_sem=send_sems.at[a, k], recv_sem=recv_sems.at[a, k],
                device_id=to, device_id_type=MESH)

        mine = [pltpu.make_async_copy(ins[a], outs[a].at[_dev_index(me)], local_sems.at[a])
                for a in range(n)]
        for cp in mine:
            cp.start()
        first = []
        for a in range(n):
            first.append(copy(a, 0, me, sibling, src=ins[a]))
            first += [copy(a, 1 + j, me, (*chip, c), src=ins[a]) for j, chip in enumerate(chips)]
        for cp in first:
            cp.start()
        passed = []
        for a in range(n):
            for j, chip in enumerate(chips):
                copy(a, 1 + j, (*chip, c), me).wait_recv()
                cp = copy(a, 4 + j, (*chip, c), sibling)
                cp.start()
                passed.append(cp)
        for a in range(n):
            copy(a, 0, sibling, me).wait_recv()
            for j, chip in enumerate(chips):
                copy(a, 4 + j, (*chip, 1 - c), me).wait_recv()
        for cp in first + passed:
            cp.wait_send()
        for cp in mine:
            cp.wait()

    any_spec = pl.BlockSpec(memory_space=pl.ANY)
    return pl.pallas_call(
        body, name=name,
        out_shape=[jax.ShapeDtypeStruct((N_DEV,) + a.shape, a.dtype) for a in arrays],
        in_specs=[any_spec] * n, out_specs=[any_spec] * n,
        scratch_shapes=[pltpu.SemaphoreType.DMA((n, 7)), pltpu.SemaphoreType.DMA((n, 7)),
                        pltpu.SemaphoreType.DMA((n,))],
    )(*arrays)


def _all_to_all(arrays, name):
    n = len(arrays)

    def body(*refs):
        ins, outs = refs[:n], refs[n:2 * n]
        send_sems, recv_sems, local_sems = refs[2 * n:]
        x, y, c = lax.axis_index("x"), lax.axis_index("y"), lax.axis_index("c")
        me = _dev_index((x, y, c))
        peers = [((1 - x) if r & 4 else x, (1 - y) if r & 2 else y, (1 - c) if r & 1 else c)
                 for r in range(1, N_DEV)]

        def copy(a, k, peer):
            return pltpu.make_async_remote_copy(
                src_ref=ins[a].at[_dev_index(peer)], dst_ref=outs[a].at[me],
                send_sem=send_sems.at[a, k], recv_sem=recv_sems.at[a, k],
                device_id=peer, device_id_type=MESH)

        def landing(a, k, peer):
            return pltpu.make_async_remote_copy(
                src_ref=ins[a].at[_dev_index(peer)], dst_ref=outs[a].at[_dev_index(peer)],
                send_sem=send_sems.at[a, k], recv_sem=recv_sems.at[a, k],
                device_id=peer, device_id_type=MESH)

        mine = [pltpu.make_async_copy(ins[a].at[me], outs[a].at[me], local_sems.at[a])
                for a in range(n)]
        for cp in mine:
            cp.start()
        sends = [copy(a, k, peer) for a in range(n) for k, peer in enumerate(peers)]
        for cp in sends:
            cp.start()
        for a in range(n):
            for k, peer in enumerate(peers):
                landing(a, k, peer).wait_recv()
        for cp in sends:
            cp.wait_send()
        for cp in mine:
            cp.wait()

    any_spec = pl.BlockSpec(memory_space=pl.ANY)
    return pl.pallas_call(
        body, name=name,
        out_shape=[jax.ShapeDtypeStruct(a.shape, a.dtype) for a in arrays],
        in_specs=[any_spec] * n, out_specs=[any_spec] * n,
        scratch_shapes=[pltpu.SemaphoreType.DMA((n, 7)), pltpu.SemaphoreType.DMA((n, 7)),
                        pltpu.SemaphoreType.DMA((n,))],
    )(*arrays)


def _matmul(a, b, mode, out_dtype, name, tm=1024, tn=1024, tk=1024, residual=None):
    if mode == "tn":
        kdim, m = a.shape
    else:
        m, kdim = a.shape
    n = b.shape[0] if mode == "nt" else b.shape[1]
    tm, tn, tk = _pick(m, tm, LANES), _pick(n, tn, LANES), _pick(kdim, tk, LANES)
    nk = kdim // tk
    if mode == "tn":
        a_spec = pl.BlockSpec((tk, tm), lambda i, j, k: (k, i))
    else:
        a_spec = pl.BlockSpec((tm, tk), lambda i, j, k: (i, k))
    if mode == "nt":
        b_spec = pl.BlockSpec((tn, tk), lambda i, j, k: (j, k))
    else:
        b_spec = pl.BlockSpec((tk, tn), lambda i, j, k: (k, j))
    o_spec = pl.BlockSpec((tm, tn), lambda i, j, k: (i, j))
    dot = {"nn": _dot, "nt": _dot_nt, "tn": _dot_tn}[mode]
    has_res = residual is not None

    def body(*refs):
        if has_res:
            a_ref, b_ref, r_ref, o_ref = refs[:4]
        else:
            a_ref, b_ref, o_ref = refs[:3]
        k = pl.program_id(2)

        def finish(acc):
            if has_res:
                acc = acc + r_ref[...]
            o_ref[...] = acc.astype(out_dtype)

        if nk == 1:
            finish(dot(a_ref[...], b_ref[...]))
            return
        acc_ref = refs[-1]

        @pl.when(k == 0)
        def _():
            acc_ref[...] = jnp.zeros_like(acc_ref)

        acc_ref[...] += dot(a_ref[...], b_ref[...])

        @pl.when(k == nk - 1)
        def _():
            finish(acc_ref[...])

    return pl.pallas_call(
        body, name=name,
        grid=(m // tm, n // tn, nk),
        in_specs=[a_spec, b_spec] + ([o_spec] if has_res else []),
        out_specs=o_spec,
        out_shape=jax.ShapeDtypeStruct((m, n), out_dtype),
        scratch_shapes=[] if nk == 1 else [pltpu.VMEM((tm, tn), F32)],
        compiler_params=_params(("parallel", "parallel", "arbitrary")),
    )(*((a, b, residual) if has_res else (a, b)))


def _rms_fwd(x, gain, name):
    t, d = x.shape
    tr = _pick(t, 256, 16)

    def body(x_ref, g_ref, h_ref):
        xv = x_ref[...]
        r = lax.rsqrt(jnp.mean(xv * xv, axis=1, keepdims=True) + NORM_EPS)
        h_ref[...] = (xv * r * g_ref[...]).astype(BF16)

    row = pl.BlockSpec((tr, d), lambda i: (i, 0))
    return pl.pallas_call(
        body, name=name, grid=(t // tr,),
        in_specs=[row, pl.BlockSpec((1, d), lambda i: (0, 0))],
        out_specs=row, out_shape=jax.ShapeDtypeStruct((t, d), BF16),
        compiler_params=_params(("arbitrary",)),
    )(x, gain)


def _rms_bwd(x, dh, g_in, gain, name):
    t, d = x.shape
    tr = _pick(t, 256, 16)

    def body(x_ref, dh_ref, gin_ref, g_ref, dx_ref, dxb_ref, dg_ref):
        i = pl.program_id(0)
        xv, dhv = x_ref[...], dh_ref[...]
        r = lax.rsqrt(jnp.mean(xv * xv, axis=1, keepdims=True) + NORM_EPS)
        xh = xv * r
        dxh = dhv * g_ref[...]
        dx = gin_ref[...] + r * (dxh - xh * jnp.mean(dxh * xh, axis=1, keepdims=True))
        dx_ref[...] = dx
        dxb_ref[...] = dx.astype(BF16)

        @pl.when(i == 0)
        def _():
            dg_ref[...] = jnp.zeros_like(dg_ref)

        dg_ref[...] += jnp.sum(dhv * xh, axis=0, keepdims=True)

    row = pl.BlockSpec((tr, d), lambda i: (i, 0))
    vec = pl.BlockSpec((1, d), lambda i: (0, 0))
    return pl.pallas_call(
        body, name=name, grid=(t // tr,),
        in_specs=[row, row, row, vec],
        out_specs=[row, row, vec],
        out_shape=[jax.ShapeDtypeStruct((t, d), F32), jax.ShapeDtypeStruct((t, d), BF16),
                   jax.ShapeDtypeStruct((1, d), F32)],
        compiler_params=_params(("arbitrary",)),
    )(x, dh, g_in, gain)


def _loss_grad(y, target, name):
    t, d = y.shape
    tr = _pick(t, 256, 16)

    def body(y_ref, t_ref, s_ref, g_ref, gb_ref):
        i = pl.program_id(0)
        e = y_ref[...] - t_ref[...]
        g = e * (1.0 / d)
        g_ref[...] = g
        gb_ref[...] = g.astype(BF16)

        @pl.when(i == 0)
        def _():
            s_ref[...] = jnp.zeros_like(s_ref)

        s_ref[...] += jnp.sum(jnp.sum(e * e, axis=1, keepdims=True), axis=0, keepdims=True)

    row = pl.BlockSpec((tr, d), lambda i: (i, 0))
    return pl.pallas_call(
        body, name=name, grid=(t // tr,),
        in_specs=[row, row],
        out_specs=[pl.BlockSpec((1, 1), lambda i: (0, 0)), row, row],
        out_shape=[jax.ShapeDtypeStruct((1, 1), F32), jax.ShapeDtypeStruct((t, d), F32),
                   jax.ShapeDtypeStruct((t, d), BF16)],
        compiler_params=_params(("arbitrary",)),
    )(y, target)


def _causal_weights(ws_ref, g, chunk, transposed):
    rows = lax.broadcasted_iota(jnp.int32, (chunk, chunk), 0)
    cols = lax.broadcasted_iota(jnp.int32, (chunk, chunk), 1)
    keep = (cols >= rows) if transposed else (rows >= cols)
    return jnp.where(keep, ws_ref[g], 0.0).astype(BF16)


def _mix_fwd(uvz, v_gain, w_s, b_s, name):
    t, w3 = uvz.shape
    w = w3 // 3
    groups, chunk = w_s.shape[0], w_s.shape[1]
    gd = w // groups

    def body(uvz_ref, gam_ref, ws_ref, bs_ref, y_ref):
        gv = _gelu(uvz_ref[:, w:2 * w])
        r = lax.rsqrt(jnp.mean(gv * gv, axis=1, keepdims=True) + NORM_EPS)
        vn = (gv * r * gam_ref[...]).astype(BF16)
        for g in range(groups):
            sl = slice(g * gd, (g + 1) * gd)
            mixed = _dot(_causal_weights(ws_ref, g, chunk, False), vn[:, sl]) + bs_ref[g]
            u = uvz_ref[:, g * gd:(g + 1) * gd]
            z = uvz_ref[:, 2 * w + g * gd:2 * w + (g + 1) * gd]
            y_ref[:, sl] = (_gelu(u) * mixed * (z * _sigmoid(z))).astype(BF16)

    return pl.pallas_call(
        body, name=name, grid=(t // chunk,),
        in_specs=[pl.BlockSpec((chunk, w3), lambda i: (i, 0)),
                  pl.BlockSpec((1, w), lambda i: (0, 0)),
                  pl.BlockSpec((groups, chunk, chunk), lambda i: (0, 0, 0)),
                  pl.BlockSpec((groups, chunk, 1), lambda i: (0, 0, 0))],
        out_specs=pl.BlockSpec((chunk, w), lambda i: (i, 0)),
        out_shape=jax.ShapeDtypeStruct((t, w), BF16),
        compiler_params=_params(("arbitrary",)),
    )(uvz, v_gain, w_s, b_s)


def _mix_bwd(uvz, dy, v_gain, w_s, w_s_t, b_s, name):
    t, w3 = uvz.shape
    w = w3 // 3
    groups, chunk = w_s.shape[0], w_s.shape[1]
    gd = w // groups

    def body(uvz_ref, dy_ref, gam_ref, ws_ref, wst_ref, bs_ref, d_ref, dws_ref, dbs_ref, dgam_ref,
             dvn_ref):
        i = pl.program_id(0)

        @pl.when(i == 0)
        def _():
            dws_ref[...] = jnp.zeros_like(dws_ref)
            dbs_ref[...] = jnp.zeros_like(dbs_ref)
            dgam_ref[...] = jnp.zeros_like(dgam_ref)

        gv, dgv = _gelu_and_grad(uvz_ref[:, w:2 * w])
        r = lax.rsqrt(jnp.mean(gv * gv, axis=1, keepdims=True) + NORM_EPS)
        vh = gv * r
        gam = gam_ref[...]
        vn = (vh * gam).astype(BF16)
        rows = lax.broadcasted_iota(jnp.int32, (chunk, chunk), 0)
        cols = lax.broadcasted_iota(jnp.int32, (chunk, chunk), 1)
        for g in range(groups):
            sl = slice(g * gd, (g + 1) * gd)
            mixed = _dot(_causal_weights(ws_ref, g, chunk, False), vn[:, sl]) + bs_ref[g]
            gu, dgu = _gelu_and_grad(uvz_ref[:, g * gd:(g + 1) * gd])
            z = uvz_ref[:, 2 * w + g * gd:2 * w + (g + 1) * gd]
            sz = _sigmoid(z)
            silu = z * sz
            dyv = dy_ref[:, sl]
            dmixed = dyv * gu * silu
            d_ref[:, sl] = (dyv * mixed * silu * dgu).astype(BF16)
            d_ref[:, 2 * w + g * gd:2 * w + (g + 1) * gd] = (
                dyv * gu * mixed * (sz * (1.0 + z * (1.0 - sz)))).astype(BF16)
            dmb = dmixed.astype(BF16)
            dws_ref[g] += jnp.where(rows >= cols, _dot_nt(dmb, vn[:, sl]), 0.0)
            dbs_ref[g] += jnp.sum(dmixed, axis=1, keepdims=True)
            dvn_ref[:, sl] = _dot(_causal_weights(wst_ref, g, chunk, True), dmb)
        dvn = dvn_ref[...]
        dgam_ref[...] += jnp.sum(dvn * vh, axis=0, keepdims=True)
        dvh = dvn * gam
        dgvv = r * (dvh - vh * jnp.mean(dvh * vh, axis=1, keepdims=True))
        d_ref[:, w:2 * w] = (dgvv * dgv).astype(BF16)

    return pl.pallas_call(
        body, name=name, grid=(t // chunk,),
        in_specs=[pl.BlockSpec((chunk, w3), lambda i: (i, 0)),
                  pl.BlockSpec((chunk, w), lambda i: (i, 0)),
                  pl.BlockSpec((1, w), lambda i: (0, 0)),
                  pl.BlockSpec((groups, chunk, chunk), lambda i: (0, 0, 0)),
                  pl.BlockSpec((groups, chunk, chunk), lambda i: (0, 0, 0)),
                  pl.BlockSpec((groups, chunk, 1), lambda i: (0, 0, 0))],
        out_specs=[pl.BlockSpec((chunk, w3), lambda i: (i, 0)),
                   pl.BlockSpec((groups, chunk, chunk), lambda i: (0, 0, 0)),
                   pl.BlockSpec((groups, chunk, 1), lambda i: (0, 0, 0)),
                   pl.BlockSpec((1, w), lambda i: (0, 0))],
        out_shape=[jax.ShapeDtypeStruct((t, w3), BF16),
                   jax.ShapeDtypeStruct((groups, chunk, chunk), F32),
                   jax.ShapeDtypeStruct((groups, chunk, 1), F32),
                   jax.ShapeDtypeStruct((1, w), F32)],
        scratch_shapes=[pltpu.VMEM((chunk, w), F32)],
        compiler_params=_params(("arbitrary",)),
    )(uvz, dy, v_gain, w_s, w_s_t, b_s)


def _attn_prep(proj, q_gain, k_gain, f_bias, heads, hd, name):
    t = proj.shape[0]
    bw = heads * hd
    tr = _pick(t, 256, 16)
    fcol = 4 * bw // LANES

    def body(q_ref, k_ref, v_ref, f_ref, gq_ref, gk_ref, fb_ref, qn_ref, kn_ref, vb_ref, cum_ref,
             carry_ref):
        i = pl.program_id(0)

        @pl.when(i == 0)
        def _():
            carry_ref[...] = jnp.zeros_like(carry_ref)

        for src, gain, dst in ((q_ref, gq_ref, qn_ref), (k_ref, gk_ref, kn_ref)):
            for h in range(heads):
                sl = slice(h * hd, (h + 1) * hd)
                v = src[:, sl]
                r = lax.rsqrt(jnp.mean(v * v, axis=1, keepdims=True) + NORM_EPS)
                dst[:, sl] = (v * r * gain[...]).astype(BF16)
        vb_ref[...] = v_ref[...].astype(BF16)
        fl = f_ref[...] + fb_ref[...]
        log_f = jnp.minimum(fl, 0.0) - jnp.log(1.0 + jnp.exp(-jnp.abs(fl)))
        rows = lax.broadcasted_iota(jnp.int32, (tr, tr), 0)
        cols = lax.broadcasted_iota(jnp.int32, (tr, tr), 1)
        lower = jnp.where(rows >= cols, 1.0, 0.0).astype(BF16)
        hi, mid, lo = _split3(log_f)
        cum_ref[...] = (_dot(lower, hi) + _dot(lower, mid) + _dot(lower, lo)) + carry_ref[...]
        carry_ref[...] = cum_ref[tr - 1:tr, :]

    wide = lambda col: pl.BlockSpec((tr, bw), lambda i: (i, col))
    vec = pl.BlockSpec((1, hd), lambda i: (0, 0))
    return pl.pallas_call(
        body, name=name, grid=(t // tr,),
        in_specs=[wide(0), wide(1), wide(2), pl.BlockSpec((tr, LANES), lambda i: (i, fcol)),
                  vec, vec, pl.BlockSpec((1, LANES), lambda i: (0, 0))],
        out_specs=[wide(0), wide(0), wide(0), pl.BlockSpec((tr, LANES), lambda i: (i, 0))],
        out_shape=[jax.ShapeDtypeStruct((t, bw), BF16)] * 3 + [jax.ShapeDtypeStruct((t, LANES), F32)],
        scratch_shapes=[pltpu.VMEM((1, LANES), F32)],
        compiler_params=_params(("arbitrary",)),
    )(proj, proj, proj, proj, q_gain, k_gain, f_bias)


def _flash_fwd(qn, kn, vb, cq, ck, proj, heads, hd, tq, name):
    t = qn.shape[0]
    nq = t // tq
    scale = hd ** -0.5
    zcol = 3 * heads

    def body(q_ref, k_ref, v_ref, cq_ref, ck_ref, z_ref, o_ref, y_ref, lse_ref, m_s, l_s, acc_s):
        i = pl.program_id(1)
        q = q_ref[...]
        cqv = cq_ref[0]
        m_s[...] = jnp.full_like(m_s, -jnp.inf)
        l_s[...] = jnp.zeros_like(l_s)
        acc_s[...] = jnp.zeros_like(acc_s)

        def step(j, masked):
            off = pl.multiple_of(j * tq, tq)
            s = _dot_nt(q, k_ref[pl.ds(off, tq), :]) * scale + (cqv - ck_ref[0, j])
            if masked:
                rows = lax.broadcasted_iota(jnp.int32, (tq, tq), 0)
                cols = lax.broadcasted_iota(jnp.int32, (tq, tq), 1)
                s = jnp.where(rows >= cols, s, -jnp.inf)
            m_prev = m_s[...]
            m_new = jnp.maximum(m_prev, jnp.max(s, axis=1, keepdims=True))
            alpha = jnp.exp(m_prev - m_new)
            p = jnp.exp(s - m_new)
            l_s[...] = alpha * l_s[...] + jnp.sum(p, axis=1, keepdims=True)
            acc_s[...] = alpha * acc_s[...] + _dot(p.astype(BF16), v_ref[pl.ds(off, tq), :])
            m_s[...] = m_new

        def loop_body(j, carry):
            step(j, False)
            return carry

        lax.fori_loop(0, i, loop_body, 0)
        step(i, True)
        l = l_s[...]
        o = acc_s[...] / l
        z = z_ref[...]
        o_ref[...] = o
        y_ref[...] = (o * (z * _sigmoid(z))).astype(BF16)
        lse_ref[0] = m_s[...] + jnp.log(l)

    blk = pl.BlockSpec((tq, hd), lambda h, i: (i, h))
    head = pl.BlockSpec((t, hd), lambda h, i: (0, h))
    col = pl.BlockSpec((1, tq, 1), lambda h, i: (h, i, 0))
    return pl.pallas_call(
        body, name=name, grid=(heads, nq),
        in_specs=[blk, head, head, col,
                  pl.BlockSpec((1, nq, 1, tq), lambda h, i: (h, 0, 0, 0)),
                  pl.BlockSpec((tq, hd), lambda h, i: (i, zcol + h))],
        out_specs=[blk, blk, col],
        out_shape=[jax.ShapeDtypeStruct((t, heads * hd), F32),
                   jax.ShapeDtypeStruct((t, heads * hd), BF16),
                   jax.ShapeDtypeStruct((heads, t, 1), F32)],
        scratch_shapes=[pltpu.VMEM((tq, 1), F32), pltpu.VMEM((tq, 1), F32),
                        pltpu.VMEM((tq, hd), F32)],
        compiler_params=_params(("arbitrary", "arbitrary")),
    )(qn, kn, vb, cq, ck, proj)


def _attn_bwd_prep(dy, proj, o, heads, hd, name):
    t, bw = dy.shape
    tr = _pick(t, 256, 16)

    def body(dy_ref, z_ref, o_ref, do_ref, dz_ref, delta_ref):
        dyv, z, ov = dy_ref[...], z_ref[...], o_ref[...]
        sz = _sigmoid(z)
        do = dyv * (z * sz)
        do_ref[...] = do.astype(BF16)
        dz_ref[...] = (dyv * ov * (sz * (1.0 + z * (1.0 - sz)))).astype(BF16)
        prod = do * ov
        for h in range(heads):
            delta_ref[h] = jnp.sum(prod[:, h * hd:(h + 1) * hd], axis=1, keepdims=True)

    row = pl.BlockSpec((tr, bw), lambda i: (i, 0))
    return pl.pallas_call(
        body, name=name, grid=(t // tr,),
        in_specs=[row, pl.BlockSpec((tr, bw), lambda i: (i, 3)), row],
        out_specs=[row, row, pl.BlockSpec((heads, tr, 1), lambda i: (0, i, 0))],
        out_shape=[jax.ShapeDtypeStruct((t, bw), BF16), jax.ShapeDtypeStruct((t, bw), BF16),
                   jax.ShapeDtypeStruct((heads, t, 1), F32)],
        compiler_params=_params(("arbitrary",)),
    )(dy, proj, o)


def _flash_bwd(qn, kn, vb, do, cq, ck, lse, delta, heads, hd, tq, name):
    t = qn.shape[0]
    nq = t // tq
    scale = hd ** -0.5

    def body(q_ref, k_ref, v_ref, do_ref, cq_ref, ck_ref, lse_ref, delta_ref,
             dq_ref, dk_ref, dv_ref, dcq_ref, dck_ref, dq_s, dcq_s):
        i = pl.program_id(1)

        @pl.when(i == 0)
        def _():
            dk_ref[...] = jnp.zeros_like(dk_ref)
            dv_ref[...] = jnp.zeros_like(dv_ref)
            dck_ref[...] = jnp.zeros_like(dck_ref)

        q, dov = q_ref[...], do_ref[...]
        cqv, lsev, deltav = cq_ref[0], lse_ref[0], delta_ref[0]
        dq_s[...] = jnp.zeros_like(dq_s)
        dcq_s[...] = jnp.zeros_like(dcq_s)

        def step(j, masked):
            off = pl.multiple_of(j * tq, tq)
            kblk = k_ref[pl.ds(off, tq), :]
            s = _dot_nt(q, kblk) * scale + (cqv - ck_ref[0, j])
            if masked:
                rows = lax.broadcasted_iota(jnp.int32, (tq, tq), 0)
                cols = lax.broadcasted_iota(jnp.int32, (tq, tq), 1)
                s = jnp.where(rows >= cols, s, -jnp.inf)
            p = jnp.exp(s - lsev)
            dp = _dot_nt(dov, v_ref[pl.ds(off, tq), :])
            ds = p * (dp - deltav)
            dsb = ds.astype(BF16)
            dv_ref[pl.ds(off, tq), :] += _dot_tn(p.astype(BF16), dov)
            dk_ref[pl.ds(off, tq), :] += _dot_tn(dsb, q) * scale
            dq_s[...] += _dot(dsb, kblk) * scale
            dcq_s[...] += jnp.sum(ds, axis=1, keepdims=True)
            dck_ref[0, j] += jnp.sum(ds, axis=0, keepdims=True)

        def loop_body(j, carry):
            step(j, False)
            return carry

        lax.fori_loop(0, i, loop_body, 0)
        step(i, True)
        dq_ref[...] = dq_s[...]
        dcq_ref[0] = dcq_s[...]

    blk = pl.BlockSpec((tq, hd), lambda h, i: (i, h))
    head = pl.BlockSpec((t, hd), lambda h, i: (0, h))
    col = pl.BlockSpec((1, tq, 1), lambda h, i: (h, i, 0))
    rowv = pl.BlockSpec((1, nq, 1, tq), lambda h, i: (h, 0, 0, 0))
    full = jax.ShapeDtypeStruct((t, heads * hd), F32)
    return pl.pallas_call(
        body, name=name, grid=(heads, nq),
        in_specs=[blk, head, head, blk, col, rowv, col, col],
        out_specs=[blk, head, head, col, rowv],
        out_shape=[full, full, full, jax.ShapeDtypeStruct((heads, t, 1), F32),
                   jax.ShapeDtypeStruct((heads, nq, 1, tq), F32)],
        scratch_shapes=[pltpu.VMEM((tq, hd), F32), pltpu.VMEM((tq, 1), F32)],
        compiler_params=_params(("arbitrary", "arbitrary")),
    )(qn, kn, vb, do, cq, ck, lse, delta)


def _attn_bwd_post(dqn, dkn, dv, dz, proj, dcq, dck, q_gain, k_gain, f_bias, heads, hd, name):
    t, bw = dqn.shape
    tr = _pick(t, 128, 16)
    nb = t // tr
    fcol = 4 * bw // LANES
    width = 4 * bw + LANES

    def body(dq_ref, dk_ref, dv_ref, dz_ref, q_ref, k_ref, f_ref, dcq_ref, dck_ref, gq_ref, gk_ref,
             fb_ref, d_ref, dgq_ref, dgk_ref, dfb_ref, carry_ref, rc_ref):
        i = pl.program_id(0)

        @pl.when(i == 0)
        def _():
            carry_ref[...] = jnp.zeros_like(carry_ref)
            dgq_ref[...] = jnp.zeros_like(dgq_ref)
            dgk_ref[...] = jnp.zeros_like(dgk_ref)
            dfb_ref[...] = jnp.zeros_like(dfb_ref)

        for idx, (g_ref, raw_ref, gain_ref, dgain_ref) in enumerate(
                ((dq_ref, q_ref, gq_ref, dgq_ref), (dk_ref, k_ref, gk_ref, dgk_ref))):
            gain = gain_ref[...]
            dgain = jnp.zeros((1, hd), F32)
            for h in range(heads):
                sl = slice(h * hd, (h + 1) * hd)
                v, dn = raw_ref[:, sl], g_ref[:, sl]
                r = lax.rsqrt(jnp.mean(v * v, axis=1, keepdims=True) + NORM_EPS)
                vh = v * r
                dgain = dgain + jnp.sum(dn * vh, axis=0, keepdims=True)
                dvh = dn * gain
                draw = r * (dvh - vh * jnp.mean(dvh * vh, axis=1, keepdims=True))
                d_ref[:, idx * bw + h * hd:idx * bw + (h + 1) * hd] = draw.astype(BF16)
            dgain_ref[...] += dgain
        d_ref[:, 2 * bw:3 * bw] = dv_ref[...].astype(BF16)
        d_ref[:, 3 * bw:4 * bw] = dz_ref[...]
        rows = lax.broadcasted_iota(jnp.int32, (tr, tr), 0)
        cols = lax.broadcasted_iota(jnp.int32, (tr, tr), 1)
        upper = jnp.where(cols >= rows, 1.0, 0.0).astype(BF16)
        hi, mid, lo = _split3(dcq_ref[...] - dck_ref[...])
        rc_ref[...] = (_dot(upper, hi) + _dot(upper, mid) + _dot(upper, lo)) + carry_ref[...]
        carry_ref[...] = rc_ref[0:1, :]
        df = rc_ref[...] * (1.0 / (1.0 + jnp.exp(f_ref[...] + fb_ref[...])))
        d_ref[:, 4 * bw:] = df.astype(BF16)
        dfb_ref[...] += jnp.sum(df, axis=0, keepdims=True)

    wide = lambda col: pl.BlockSpec((tr, bw), lambda i: (nb - 1 - i, col))
    lane = lambda col: pl.BlockSpec((tr, LANES), lambda i: (nb - 1 - i, col))
    vec = pl.BlockSpec((1, hd), lambda i: (0, 0))
    vecl = pl.BlockSpec((1, LANES), lambda i: (0, 0))
    return pl.pallas_call(
        body, name=name, grid=(nb,),
        in_specs=[wide(0), wide(0), wide(0), wide(0), wide(0), wide(1), lane(fcol), lane(0), lane(0),
                  vec, vec, vecl],
        out_specs=[pl.BlockSpec((tr, width), lambda i: (nb - 1 - i, 0)), vec, vec, vecl],
        out_shape=[jax.ShapeDtypeStruct((t, width), BF16), jax.ShapeDtypeStruct((1, hd), F32),
                   jax.ShapeDtypeStruct((1, hd), F32), jax.ShapeDtypeStruct((1, LANES), F32)],
        scratch_shapes=[pltpu.VMEM((1, LANES), F32), pltpu.VMEM((tr, LANES), F32)],
        compiler_params=_params(("arbitrary",)),
    )(dqn, dkn, dv, dz, proj, proj, proj, dcq, dck, q_gain, k_gain, f_bias)


def _adamw(w, m, v, parts, name):
    r, c = w.shape
    unit = 32 // parts.dtype.itemsize
    row_bytes = c * (7 * 4 + N_DEV * parts.dtype.itemsize)
    tr = _pick(r, max(unit, 12 * 1024 * 1024 // row_bytes), unit)
    c1 = 1.0 / (1.0 - ADAM_B1 ** ADAM_STEP)
    c2 = 1.0 / (1.0 - ADAM_B2 ** ADAM_STEP)

    def body(w_ref, m_ref, v_ref, p_ref, g_ref, d_ref, nm_ref, nv_ref):
        g = p_ref[0].astype(F32)
        for j in range(1, N_DEV):
            g = g + p_ref[j].astype(F32)
        nm = ADAM_B1 * m_ref[...] + (1.0 - ADAM_B1) * g
        nv = ADAM_B2 * v_ref[...] + (1.0 - ADAM_B2) * (g * g)
        g_ref[...] = g
        nm_ref[...] = nm
        nv_ref[...] = nv
        d_ref[...] = -ADAM_LR * ((nm * c1) / (jnp.sqrt(nv * c2) + ADAM_EPS) + ADAM_WD * w_ref[...])

    row = pl.BlockSpec((tr, c), lambda i: (i, 0))
    return pl.pallas_call(
        body, name=name, grid=(r // tr,),
        in_specs=[row, row, row, pl.BlockSpec((N_DEV, tr, c), lambda i: (0, i, 0))],
        out_specs=[row] * 4,
        out_shape=[jax.ShapeDtypeStruct((r, c), F32)] * 4,
        compiler_params=_params(("arbitrary",)),
    )(w, m, v, parts)


def _flat_rows(pieces):
    rows = []
    for p in pieces:
        f = p.reshape(-1)
        f = jnp.pad(f, (0, (-f.shape[0]) % LANES))
        rows.append(f.reshape(-1, LANES))
    out = jnp.concatenate(rows, axis=0)
    return jnp.pad(out, ((0, (-out.shape[0]) % 8), (0, 0)))


def _unflat_rows(flat, shapes):
    outs, r0 = [], 0
    lead = flat.shape[:-2]
    for s in shapes:
        size = 1
        for d in s:
            size *= d
        nr = -(-size // LANES)
        piece = flat[..., r0:r0 + nr, :].reshape(lead + (nr * LANES,))[..., :size]
        outs.append(piece.reshape(lead + tuple(s)))
        r0 += nr
    return outs


def kernel(x, a_norm_g, a_w_in, a_v_norm_g, a_w_s, a_b_s, a_w_out, b_norm_g, b_w_in, b_f_bias, b_q_norm_g, b_k_norm_g, b_w_out, loss_target, m_a_norm_g, m_a_w_in, m_a_v_norm_g, m_a_w_s, m_a_b_s, m_a_w_out, m_b_norm_g, m_b_w_in, m_b_f_bias, m_b_q_norm_g, m_b_k_norm_g, m_b_w_out, v_a_norm_g, v_a_w_in, v_a_v_norm_g, v_a_w_s, v_a_b_s, v_a_w_out, v_b_norm_g, v_b_w_in, v_b_f_bias, v_b_q_norm_g, v_b_k_norm_g, v_b_w_out):
    t, d = x.shape[1], x.shape[2]
    n_a, n_b = a_w_in.shape[0], b_w_in.shape[0]
    depth = n_a + n_b
    aw = a_w_out.shape[1] * N_DEV
    groups, chunk = a_w_s.shape[1], a_w_s.shape[2]
    heads, hd = b_f_bias.shape[1], b_q_norm_g.shape[1]
    bw = heads * hd
    b_cols = b_w_in.shape[2]
    tq = _pick(t, 512, LANES)
    nq = t // tq
    me = _dev_index((lax.axis_index("x"), lax.axis_index("y"), lax.axis_index("c")))

    ga_in, ga_out, gb_in, gb_out, gb_norm = _all_gather(
        [a_w_in.astype(BF16), a_w_out.astype(BF16), b_w_in.astype(BF16), b_w_out.astype(BF16),
         b_norm_g], "gather_weights")
    wa_in = ga_in.transpose(1, 2, 0, 3).reshape(n_a, d, 3 * aw)
    wa_out = ga_out.transpose(1, 0, 2, 3).reshape(n_a, aw, d)
    wb_in = gb_in.transpose(1, 2, 0, 3).reshape(n_b, d, 4 * bw + heads)
    wb_in = jnp.pad(wb_in, ((0, 0), (0, 0), (0, LANES - heads)))
    wb_out = gb_out.transpose(1, 0, 2, 3).reshape(n_b, bw, d)
    b_norm_full = gb_norm.transpose(1, 0, 2).reshape(n_b, d)

    pad_h = lambda v: jnp.pad(v, ((0, 0), (0, LANES - heads)))
    f_bias_p = pad_h(b_f_bias)
    b_s_col = a_b_s.reshape(n_a, groups, chunk, 1)
    w_s_t = a_w_s.transpose(0, 1, 3, 2)

    xs = [x[0]]
    saved = []
    for i in range(depth):
        j = i // 2
        xi = xs[-1]
        if i % 2 == 0:
            h = _rms_fwd(xi, a_norm_g[j:j + 1], f"a{j}_norm")
            uvz = _matmul(h, wa_in[j], "nn", F32, f"a{j}_in", tk=d)
            y = _mix_fwd(uvz, a_v_norm_g[j:j + 1], a_w_s[j], b_s_col[j], f"a{j}_mix")
            xs.append(_matmul(y, wa_out[j], "nn", F32, f"a{j}_out", residual=xi))
            saved.append((h, uvz, y))
        else:
            h = _rms_fwd(xi, b_norm_full[j:j + 1], f"b{j}_norm")
            proj = _matmul(h, wb_in[j], "nn", F32, f"b{j}_in", tm=512, tn=1664, tk=d)
            qn, kn, vb, cum = _attn_prep(proj, b_q_norm_g[j:j + 1], b_k_norm_g[j:j + 1],
                                         f_bias_p[j:j + 1], heads, hd, f"b{j}_prep")
            cum_t = cum[:, :heads].T
            cq = cum_t.reshape(heads, t, 1)
            ck = cum_t.reshape(heads, nq, 1, tq)
            o, y, lse = _flash_fwd(qn, kn, vb, cq, ck, proj, heads, hd, tq, f"b{j}_attn")
            xs.append(_matmul(y, wb_out[j], "nn", F32, f"b{j}_out", residual=xi))
            saved.append((h, proj, y, qn, kn, vb, cq, ck, o, lse))

    sq, g, gb = _loss_grad(xs[-1], loss_target[0], "loss")
    loss = 0.5 * lax.psum(sq[0, 0], AXES) / d

    d_a_in, d_a_out, d_b_in, d_b_out = [None] * n_a, [None] * n_a, [None] * n_b, [None] * n_b
    d_a_norm, d_a_vnorm, d_a_ws, d_a_bs = [None] * n_a, [None] * n_a, [None] * n_a, [None] * n_a
    d_b_norm, d_b_fb, d_b_gq, d_b_gk = [None] * n_b, [None] * n_b, [None] * n_b, [None] * n_b
    for i in reversed(range(depth)):
        j = i // 2
        xi = xs[i]
        if i % 2 == 0:
            h, uvz, y = saved[i]
            dy = _matmul(gb, wa_out[j], "nt", F32, f"a{j}_dy", tk=d)
            d_a_out[j] = _matmul(y, gb, "tn", BF16, f"a{j}_dwout")
            duvz, d_a_ws[j], dbs, d_a_vnorm[j] = _mix_bwd(
                uvz, dy, a_v_norm_g[j:j + 1], a_w_s[j], w_s_t[j], b_s_col[j], f"a{j}_mixbwd")
            d_a_bs[j] = dbs.reshape(groups, chunk)
            dh = _matmul(duvz, wa_in[j], "nt", F32, f"a{j}_dh")
            d_a_in[j] = _matmul(h, duvz, "tn", BF16, f"a{j}_dwin")
            g, gb, d_a_norm[j] = _rms_bwd(xi, dh, g, a_norm_g[j:j + 1], f"a{j}_normbwd")
        else:
            h, proj, y, qn, kn, vb, cq, ck, o, lse = saved[i]
            dy = _matmul(gb, wb_out[j], "nt", F32, f"b{j}_dy", tk=d)
            d_b_out[j] = _matmul(y, gb, "tn", BF16, f"b{j}_dwout")
            do, dz, delta = _attn_bwd_prep(dy, proj, o, heads, hd, f"b{j}_bwdprep")
            dqn, dkn, dv, dcq, dck = _flash_bwd(qn, kn, vb, do, cq, ck, lse, delta, heads, hd, tq,
                                                f"b{j}_attnbwd")
            per_token = lambda v: pad_h(v.reshape(heads, t).T)
            dproj, d_b_gq[j], d_b_gk[j], dfb = _attn_bwd_post(
                dqn, dkn, dv, dz, proj, per_token(dcq), per_token(dck), b_q_norm_g[j:j + 1],
                b_k_norm_g[j:j + 1], f_bias_p[j:j + 1], heads, hd, f"b{j}_bwdpost")
            d_b_fb[j] = dfb[:, :heads]
            dh = _matmul(dproj, wb_in[j], "nt", F32, f"b{j}_dh")
            d_b_in[j] = _matmul(h, dproj, "tn", BF16, f"b{j}_dwin", tn=1664)
            g, gb, d_b_norm[j] = _rms_bwd(xi, dh, g, b_norm_full[j:j + 1], f"b{j}_normbwd")
    grad_x = g[None]

    pa_in = jnp.stack([w.reshape(d, N_DEV, -1).transpose(1, 0, 2) for w in d_a_in], axis=1)
    pa_out = jnp.stack([w.reshape(N_DEV, -1, d) for w in d_a_out], axis=1)
    pb_in = jnp.stack([w[:, :4 * bw + heads].reshape(d, N_DEV, b_cols).transpose(1, 0, 2)
                       for w in d_b_in], axis=1)
    pb_out = jnp.stack([w.reshape(N_DEV, -1, d) for w in d_b_out], axis=1)
    ra_in, ra_out, rb_in, rb_out = _all_to_all([pa_in, pa_out, pb_in, pb_out], "exchange_grads")

    small = [jnp.concatenate(d_a_norm, 0), jnp.concatenate(d_a_vnorm, 0), jnp.stack(d_a_ws, 0),
             jnp.stack(d_a_bs, 0), jnp.concatenate(d_b_fb, 0), jnp.concatenate(d_b_gq, 0),
             jnp.concatenate(d_b_gk, 0)]
    small_w = [a_norm_g, a_v_norm_g, a_w_s, a_b_s, b_f_bias, b_q_norm_g, b_k_norm_g]
    small_m = [m_a_norm_g, m_a_v_norm_g, m_a_w_s, m_a_b_s, m_b_f_bias, m_b_q_norm_g, m_b_k_norm_g]
    small_v = [v_a_norm_g, v_a_v_norm_g, v_a_w_s, v_a_b_s, v_b_f_bias, v_b_q_norm_g, v_b_k_norm_g]
    small_flat = _flat_rows(small)
    n_small = small_flat.shape[0]
    (gathered_small,) = _all_gather(
        [jnp.concatenate([small_flat, _flat_rows([jnp.concatenate(d_b_norm, 0)])], axis=0)],
        "gather_small_grads")
    parts_small = gathered_small[:, :n_small]
    parts_b_norm = gathered_small[:, n_small:n_small + n_b * d // LANES].reshape(N_DEV, n_b, d)
    parts_b_norm = lax.dynamic_slice_in_dim(parts_b_norm, me * (d // N_DEV), d // N_DEV, axis=2)

    def update(w, m, v, parts, name):
        shape = w.shape
        w2 = w.reshape(-1, shape[-1])
        outs = _adamw(w2, m.reshape(w2.shape), v.reshape(w2.shape),
                      parts.reshape((N_DEV,) + w2.shape), name)
        return [o_.reshape(shape) for o_ in outs]

    u_a_in = update(a_w_in, m_a_w_in, v_a_w_in, ra_in, "adamw_a_w_in")
    u_a_out = update(a_w_out, m_a_w_out, v_a_w_out, ra_out, "adamw_a_w_out")
    u_b_in = update(b_w_in, m_b_w_in, v_b_w_in, rb_in, "adamw_b_w_in")
    u_b_out = update(b_w_out, m_b_w_out, v_b_w_out, rb_out, "adamw_b_w_out")
    u_b_norm = update(b_norm_g, m_b_norm_g, v_b_norm_g, parts_b_norm, "adamw_b_norm")
    u_small = _adamw(_flat_rows(small_w), _flat_rows(small_m), _flat_rows(small_v), parts_small,
                     "adamw_small")
    shapes = [w.shape for w in small_w]
    u_small = [_unflat_rows(o_, shapes) for o_ in u_small]

    def per_kind(k):
        s = u_small[k]
        return [s[0], u_a_in[k], s[1], s[2], s[3], u_a_out[k], u_b_norm[k], u_b_in[k], s[4], s[5],
                s[6], u_b_out[k]]

    return (loss, grad_x, *per_kind(0), *per_kind(1), *per_kind(2), *per_kind(3))
```

```python
import functools

import jax
import jax.numpy as jnp
from jax import lax
from jax.experimental import pallas as pl
from jax.experimental.pallas import tpu as pltpu

F32 = jnp.float32
BF16 = jnp.bfloat16
MESH = pl.DeviceIdType.MESH
AXES = ("x", "y", "c")
N_DEV = 8
NORM_EPS = 1e-6
LANES = 128
VMEM_LIMIT = 56 * 1024 * 1024

ADAM_LR = 0.001
ADAM_B1 = 0.9
ADAM_B2 = 0.999
ADAM_EPS = 1e-08
ADAM_WD = 0.01
ADAM_STEP = 10

LOG2E = 1.4426950408889634
SOFTMAX_ROWS = 64

GELU_C0 = 0.7978845608028654
GELU_C1 = 0.044715

NT_DIMS = (((1,), (1,)), ((), ()))
TN_DIMS = (((0,), (0,)), ((), ()))


def _params(sem=None):
    return pltpu.CompilerParams(dimension_semantics=sem, vmem_limit_bytes=VMEM_LIMIT)


def _pick(n, target, unit):
    best = None
    for t in range(unit, min(n, target) + 1, unit):
        if n % t == 0:
            best = t
    return n if best is None else best


def _sigmoid(x):
    return 1.0 / (1.0 + jnp.exp(-x))


def _gelu(x):
    return 0.5 * x * (1.0 + jnp.tanh(GELU_C0 * x * (1.0 + GELU_C1 * x * x)))


def _gelu_and_grad(x):
    x2 = x * x
    t = jnp.tanh(GELU_C0 * x * (1.0 + GELU_C1 * x2))
    g = 0.5 * x * (1.0 + t)
    dg = 0.5 * (1.0 + t) + 0.5 * x * (1.0 - t * t) * (GELU_C0 * (1.0 + 3.0 * GELU_C1 * x2))
    return g, dg


def _dot(a, b):
    return jnp.dot(a, b, preferred_element_type=F32)


def _dot_nt(a, b):
    return lax.dot_general(a, b, NT_DIMS, preferred_element_type=F32)


def _dot_tn(a, b):
    return lax.dot_general(a, b, TN_DIMS, preferred_element_type=F32)


def _split3(v):
    hi = v.astype(BF16)
    r1 = v - hi.astype(F32)
    mid = r1.astype(BF16)
    lo = (r1 - mid.astype(F32)).astype(BF16)
    return hi, mid, lo


def _dev_index(p):
    return 4 * p[0] + 2 * p[1] + p[2]


def _all_gather(arrays, name):
    n = len(arrays)

    def body(*refs):
        ins, outs = refs[:n], refs[n:2 * n]
        send_sems, recv_sems, local_sems = refs[2 * n:]
        x, y, c = lax.axis_index("x"), lax.axis_index("y"), lax.axis_index("c")
        me, sibling = (x, y, c), (x, y, 1 - c)
        chips = [(1 - x, y), (x, 1 - y), (1 - x, 1 - y)]

        def copy(a, k, block, to, src=None):
            dst = outs[a].at[_dev_index(block)]
            return pltpu.make_async_remote_copy(
                src_ref=dst if src is None else src, dst_ref=dst,
                send_sem=send_sems.at[a, k], recv_sem=recv_sems.at[a, k],
                device_id=to, device_id_type=MESH)

        mine = [pltpu.make_async_copy(ins[a], outs[a].at[_dev_index(me)], local_sems.at[a])
                for a in range(n)]
        for cp in mine:
            cp.start()
        first = []
        for a in range(n):
            first.append(copy(a, 0, me, sibling, src=ins[a]))
            first += [copy(a, 1 + j, me, (*chip, c), src=ins[a]) for j, chip in enumerate(chips)]
        for cp in first:
            cp.start()
        passed = []
        for a in range(n):
            for j, chip in enumerate(chips):
                copy(a, 1 + j, (*chip, c), me).wait_recv()
                cp = copy(a, 4 + j, (*chip, c), sibling)
                cp.start()
                passed.append(cp)
        for a in range(n):
            copy(a, 0, sibling, me).wait_recv()
            for j, chip in enumerate(chips):
                copy(a, 4 + j, (*chip, 1 - c), me).wait_recv()
        for cp in first + passed:
            cp.wait_send()
        for cp in mine:
            cp.wait()

    any_spec = pl.BlockSpec(memory_space=pl.ANY)
    return pl.pallas_call(
        body, name=name,
        out_shape=[jax.ShapeDtypeStruct((N_DEV,) + a.shape, a.dtype) for a in arrays],
        in_specs=[any_spec] * n, out_specs=[any_spec] * n,
        scratch_shapes=[pltpu.SemaphoreType.DMA((n, 7)), pltpu.SemaphoreType.DMA((n, 7)),
                        pltpu.SemaphoreType.DMA((n,))],
    )(*arrays)


def _all_to_all(arrays, name):
    n = len(arrays)

    def body(*refs):
        ins, outs = refs[:n], refs[n:2 * n]
        send_sems, recv_sems, local_sems = refs[2 * n:]
        x, y, c = lax.axis_index("x"), lax.axis_index("y"), lax.axis_index("c")
        me = _dev_index((x, y, c))
        peers = [((1 - x) if r & 4 else x, (1 - y) if r & 2 else y, (1 - c) if r & 1 else c)
                 for r in range(1, N_DEV)]

        def copy(a, k, peer):
            return pltpu.make_async_remote_copy(
                src_ref=ins[a].at[_dev_index(peer)], dst_ref=outs[a].at[me],
                send_sem=send_sems.at[a, k], recv_sem=recv_sems.at[a, k],
                device_id=peer, device_id_type=MESH)

        def landing(a, k, peer):
            return pltpu.make_async_remote_copy(
                src_ref=ins[a].at[_dev_index(peer)], dst_ref=outs[a].at[_dev_index(peer)],
                send_sem=send_sems.at[a, k], recv_sem=recv_sems.at[a, k],
                device_id=peer, device_id_type=MESH)

        mine = [pltpu.make_async_copy(ins[a].at[me], outs[a].at[me], local_sems.at[a])
                for a in range(n)]
        for cp in mine:
            cp.start()
        sends = [copy(a, k, peer) for a in range(n) for k, peer in enumerate(peers)]
        for cp in sends:
            cp.start()
        for a in range(n):
            for k, peer in enumerate(peers):
                landing(a, k, peer).wait_recv()
        for cp in sends:
            cp.wait_send()
        for cp in mine:
            cp.wait()

    any_spec = pl.BlockSpec(memory_space=pl.ANY)
    return pl.pallas_call(
        body, name=name,
        out_shape=[jax.ShapeDtypeStruct(a.shape, a.dtype) for a in arrays],
        in_specs=[any_spec] * n, out_specs=[any_spec] * n,
        scratch_shapes=[pltpu.SemaphoreType.DMA((n, 7)), pltpu.SemaphoreType.DMA((n, 7)),
                        pltpu.SemaphoreType.DMA((n,))],
    )(*arrays)


def _matmul(a, b, mode, out_dtype, name, tm=1024, tn=1024, tk=1024, residual=None):
    if mode == "tn":
        kdim, m = a.shape
    else:
        m, kdim = a.shape
    n = b.shape[0] if mode == "nt" else b.shape[1]
    tm, tn, tk = _pick(m, tm, LANES), _pick(n, tn, LANES), _pick(kdim, tk, LANES)
    nk = kdim // tk
    if mode == "tn":
        a_spec = pl.BlockSpec((tk, tm), lambda i, j, k: (k, i))
    else:
        a_spec = pl.BlockSpec((tm, tk), lambda i, j, k: (i, k))
    if mode == "nt":
        b_spec = pl.BlockSpec((tn, tk), lambda i, j, k: (j, k))
    else:
        b_spec = pl.BlockSpec((tk, tn), lambda i, j, k: (k, j))
    o_spec = pl.BlockSpec((tm, tn), lambda i, j, k: (i, j))
    dot = {"nn": _dot, "nt": _dot_nt, "tn": _dot_tn}[mode]
    has_res = residual is not None

    def body(*refs):
        if has_res:
            a_ref, b_ref, r_ref, o_ref = refs[:4]
        else:
            a_ref, b_ref, o_ref = refs[:3]
        k = pl.program_id(2)

        def finish(acc):
            if has_res:
                acc = acc + r_ref[...]
            o_ref[...] = acc.astype(out_dtype)

        if nk == 1:
            finish(dot(a_ref[...], b_ref[...]))
            return
        acc_ref = refs[-1]

        @pl.when(k == 0)
        def _():
            acc_ref[...] = jnp.zeros_like(acc_ref)

        acc_ref[...] += dot(a_ref[...], b_ref[...])

        @pl.when(k == nk - 1)
        def _():
            finish(acc_ref[...])

    return pl.pallas_call(
        body, name=name,
        grid=(m // tm, n // tn, nk),
        in_specs=[a_spec, b_spec] + ([o_spec] if has_res else []),
        out_specs=o_spec,
        out_shape=jax.ShapeDtypeStruct((m, n), out_dtype),
        scratch_shapes=[] if nk == 1 else [pltpu.VMEM((tm, tn), F32)],
        compiler_params=_params(("parallel", "parallel", "arbitrary")),
    )(*((a, b, residual) if has_res else (a, b)))


def _rms_fwd(x, gain, name):
    t, d = x.shape
    tr = _pick(t, 256, 16)

    def body(x_ref, g_ref, h_ref):
        xv = x_ref[...]
        r = lax.rsqrt(jnp.mean(xv * xv, axis=1, keepdims=True) + NORM_EPS)
        h_ref[...] = (xv * r * g_ref[...]).astype(BF16)

    row = pl.BlockSpec((tr, d), lambda i: (i, 0))
    return pl.pallas_call(
        body, name=name, grid=(t // tr,),
        in_specs=[row, pl.BlockSpec((1, d), lambda i: (0, 0))],
        out_specs=row, out_shape=jax.ShapeDtypeStruct((t, d), BF16),
        compiler_params=_params(("arbitrary",)),
    )(x, gain)


def _rms_bwd(x, dh, g_in, gain, name):
    t, d = x.shape
    tr = _pick(t, 256, 16)

    def body(x_ref, dh_ref, gin_ref, g_ref, dx_ref, dxb_ref, dg_ref):
        i = pl.program_id(0)
        xv, dhv = x_ref[...], dh_ref[...]
        r = lax.rsqrt(jnp.mean(xv * xv, axis=1, keepdims=True) + NORM_EPS)
        xh = xv * r
        dxh = dhv * g_ref[...]
        dx = gin_ref[...] + r * (dxh - xh * jnp.mean(dxh * xh, axis=1, keepdims=True))
        dx_ref[...] = dx
        dxb_ref[...] = dx.astype(BF16)

        @pl.when(i == 0)
        def _():
            dg_ref[...] = jnp.zeros_like(dg_ref)

        dg_ref[...] += jnp.sum(dhv * xh, axis=0, keepdims=True)

    row = pl.BlockSpec((tr, d), lambda i: (i, 0))
    vec = pl.BlockSpec((1, d), lambda i: (0, 0))
    return pl.pallas_call(
        body, name=name, grid=(t // tr,),
        in_specs=[row, row, row, vec],
        out_specs=[row, row, vec],
        out_shape=[jax.ShapeDtypeStruct((t, d), F32), jax.ShapeDtypeStruct((t, d), BF16),
                   jax.ShapeDtypeStruct((1, d), F32)],
        compiler_params=_params(("arbitrary",)),
    )(x, dh, g_in, gain)


def _loss_grad(y, target, name):
    t, d = y.shape
    tr = _pick(t, 256, 16)

    def body(y_ref, t_ref, s_ref, g_ref, gb_ref):
        i = pl.program_id(0)
        e = y_ref[...] - t_ref[...]
        g = e * (1.0 / d)
        g_ref[...] = g
        gb_ref[...] = g.astype(BF16)

        @pl.when(i == 0)
        def _():
            s_ref[...] = jnp.zeros_like(s_ref)

        s_ref[...] += jnp.sum(jnp.sum(e * e, axis=1, keepdims=True), axis=0, keepdims=True)

    row = pl.BlockSpec((tr, d), lambda i: (i, 0))
    return pl.pallas_call(
        body, name=name, grid=(t // tr,),
        in_specs=[row, row],
        out_specs=[pl.BlockSpec((1, 1), lambda i: (0, 0)), row, row],
        out_shape=[jax.ShapeDtypeStruct((1, 1), F32), jax.ShapeDtypeStruct((t, d), F32),
                   jax.ShapeDtypeStruct((t, d), BF16)],
        compiler_params=_params(("arbitrary",)),
    )(y, target)


def _causal_weights(ws_ref, g, chunk, transposed):
    rows = lax.broadcasted_iota(jnp.int32, (chunk, chunk), 0)
    cols = lax.broadcasted_iota(jnp.int32, (chunk, chunk), 1)
    keep = (cols >= rows) if transposed else (rows >= cols)
    return jnp.where(keep, ws_ref[g], 0.0).astype(BF16)


def _mix_fwd(uvz, v_gain, w_s, b_s, name):
    t, w3 = uvz.shape
    w = w3 // 3
    groups, chunk = w_s.shape[0], w_s.shape[1]
    gd = w // groups

    def body(uvz_ref, gam_ref, ws_ref, bs_ref, y_ref):
        gv = _gelu(uvz_ref[:, w:2 * w])
        r = lax.rsqrt(jnp.mean(gv * gv, axis=1, keepdims=True) + NORM_EPS)
        vn = (gv * r * gam_ref[...]).astype(BF16)
        for g in range(groups):
            sl = slice(g * gd, (g + 1) * gd)
            mixed = _dot(_causal_weights(ws_ref, g, chunk, False), vn[:, sl]) + bs_ref[g]
            u = uvz_ref[:, g * gd:(g + 1) * gd]
            z = uvz_ref[:, 2 * w + g * gd:2 * w + (g + 1) * gd]
            y_ref[:, sl] = (_gelu(u) * mixed * (z * _sigmoid(z))).astype(BF16)

    return pl.pallas_call(
        body, name=name, grid=(t // chunk,),
        in_specs=[pl.BlockSpec((chunk, w3), lambda i: (i, 0)),
                  pl.BlockSpec((1, w), lambda i: (0, 0)),
                  pl.BlockSpec((groups, chunk, chunk), lambda i: (0, 0, 0)),
                  pl.BlockSpec((groups, chunk, 1), lambda i: (0, 0, 0))],
        out_specs=pl.BlockSpec((chunk, w), lambda i: (i, 0)),
        out_shape=jax.ShapeDtypeStruct((t, w), BF16),
        compiler_params=_params(("arbitrary",)),
    )(uvz, v_gain, w_s, b_s)


def _mix_bwd(uvz, dy, v_gain, w_s, w_s_t, b_s, name):
    t, w3 = uvz.shape
    w = w3 // 3
    groups, chunk = w_s.shape[0], w_s.shape[1]
    gd = w // groups

    def body(uvz_ref, dy_ref, gam_ref, ws_ref, wst_ref, bs_ref, d_ref, dws_ref, dbs_ref, dgam_ref,
             dvn_ref):
        i = pl.program_id(0)

        @pl.when(i == 0)
        def _():
            dws_ref[...] = jnp.zeros_like(dws_ref)
            dbs_ref[...] = jnp.zeros_like(dbs_ref)
            dgam_ref[...] = jnp.zeros_like(dgam_ref)

        gv, dgv = _gelu_and_grad(uvz_ref[:, w:2 * w])
        r = lax.rsqrt(jnp.mean(gv * gv, axis=1, keepdims=True) + NORM_EPS)
        vh = gv * r
        gam = gam_ref[...]
        vn = (vh * gam).astype(BF16)
        rows = lax.broadcasted_iota(jnp.int32, (chunk, chunk), 0)
        cols = lax.broadcasted_iota(jnp.int32, (chunk, chunk), 1)
        for g in range(groups):
            sl = slice(g * gd, (g + 1) * gd)
            mixed = _dot(_causal_weights(ws_ref, g, chunk, False), vn[:, sl]) + bs_ref[g]
            gu, dgu = _gelu_and_grad(uvz_ref[:, g * gd:(g + 1) * gd])
            z = uvz_ref[:, 2 * w + g * gd:2 * w + (g + 1) * gd]
            sz = _sigmoid(z)
            silu = z * sz
            dyv = dy_ref[:, sl]
            dmixed = dyv * gu * silu
            d_ref[:, sl] = (dyv * mixed * silu * dgu).astype(BF16)
            d_ref[:, 2 * w + g * gd:2 * w + (g + 1) * gd] = (
                dyv * gu * mixed * (sz * (1.0 + z * (1.0 - sz)))).astype(BF16)
            dmb = dmixed.astype(BF16)
            dws_ref[g] += jnp.where(rows >= cols, _dot_nt(dmb, vn[:, sl]), 0.0)
            dbs_ref[g] += jnp.sum(dmixed, axis=1, keepdims=True)
            dvn_ref[:, sl] = _dot(_causal_weights(wst_ref, g, chunk, True), dmb)
        dvn = dvn_ref[...]
        dgam_ref[...] += jnp.sum(dvn * vh, axis=0, keepdims=True)
        dvh = dvn * gam
        dgvv = r * (dvh - vh * jnp.mean(dvh * vh, axis=1, keepdims=True))
        d_ref[:, w:2 * w] = (dgvv * dgv).astype(BF16)

    return pl.pallas_call(
        body, name=name, grid=(t // chunk,),
        in_specs=[pl.BlockSpec((chunk, w3), lambda i: (i, 0)),
                  pl.BlockSpec((chunk, w), lambda i: (i, 0)),
                  pl.BlockSpec((1, w), lambda i: (0, 0)),
                  pl.BlockSpec((groups, chunk, chunk), lambda i: (0, 0, 0)),
                  pl.BlockSpec((groups, chunk, chunk), lambda i: (0, 0, 0)),
                  pl.BlockSpec((groups, chunk, 1), lambda i: (0, 0, 0))],
        out_specs=[pl.BlockSpec((chunk, w3), lambda i: (i, 0)),
                   pl.BlockSpec((groups, chunk, chunk), lambda i: (0, 0, 0)),
                   pl.BlockSpec((groups, chunk, 1), lambda i: (0, 0, 0)),
                   pl.BlockSpec((1, w), lambda i: (0, 0))],
        out_shape=[jax.ShapeDtypeStruct((t, w3), BF16),
                   jax.ShapeDtypeStruct((groups, chunk, chunk), F32),
                   jax.ShapeDtypeStruct((groups, chunk, 1), F32),
                   jax.ShapeDtypeStruct((1, w), F32)],
        scratch_shapes=[pltpu.VMEM((chunk, w), F32)],
        compiler_params=_params(("arbitrary",)),
    )(uvz, dy, v_gain, w_s, w_s_t, b_s)


def _attn_prep(proj, q_gain, k_gain, f_bias, heads, hd, name):
    t = proj.shape[0]
    bw = heads * hd
    tr = _pick(t, 256, 16)
    fcol = 4 * bw // LANES

    def body(q_ref, k_ref, v_ref, f_ref, gq_ref, gk_ref, fb_ref, qn_ref, kn_ref, vb_ref, cum_ref,
             carry_ref):
        i = pl.program_id(0)

        @pl.when(i == 0)
        def _():
            carry_ref[...] = jnp.zeros_like(carry_ref)

        for src, gain, dst in ((q_ref, gq_ref, qn_ref), (k_ref, gk_ref, kn_ref)):
            for h in range(heads):
                sl = slice(h * hd, (h + 1) * hd)
                v = src[:, sl]
                r = lax.rsqrt(jnp.mean(v * v, axis=1, keepdims=True) + NORM_EPS)
                dst[:, sl] = (v * r * gain[...]).astype(BF16)
        vb_ref[...] = v_ref[...].astype(BF16)
        fl = f_ref[...] + fb_ref[...]
        log_f = jnp.minimum(fl, 0.0) - jnp.log(1.0 + jnp.exp(-jnp.abs(fl)))
        rows = lax.broadcasted_iota(jnp.int32, (tr, tr), 0)
        cols = lax.broadcasted_iota(jnp.int32, (tr, tr), 1)
        lower = jnp.where(rows >= cols, 1.0, 0.0).astype(BF16)
        hi, mid, lo = _split3(log_f)
        cum_ref[...] = (_dot(lower, hi) + _dot(lower, mid) + _dot(lower, lo)) + carry_ref[...]
        carry_ref[...] = cum_ref[tr - 1:tr, :]

    wide = lambda col: pl.BlockSpec((tr, bw), lambda i: (i, col))
    vec = pl.BlockSpec((1, hd), lambda i: (0, 0))
    return pl.pallas_call(
        body, name=name, grid=(t // tr,),
        in_specs=[wide(0), wide(1), wide(2), pl.BlockSpec((tr, LANES), lambda i: (i, fcol)),
                  vec, vec, pl.BlockSpec((1, LANES), lambda i: (0, 0))],
        out_specs=[wide(0), wide(0), wide(0), pl.BlockSpec((tr, LANES), lambda i: (i, 0))],
        out_shape=[jax.ShapeDtypeStruct((t, bw), BF16)] * 3 + [jax.ShapeDtypeStruct((t, LANES), F32)],
        scratch_shapes=[pltpu.VMEM((1, LANES), F32)],
        compiler_params=_params(("arbitrary",)),
    )(proj, proj, proj, proj, q_gain, k_gain, f_bias)


def _flash_fwd(qn, kn, vb, ck, proj, heads, hd, tq, name):
    t = qn.shape[0]
    nq = t // tq
    th = tq // 2
    scale2 = hd ** -0.5 * LOG2E
    zcol = 3 * heads
    rc = _pick(tq, SOFTMAX_ROWS, 16)
    reps = th // LANES

    def body(q_ref, k_ref, v_ref, ck_ref, z_ref, o_ref, y_ref, lse_ref,
             m_s, l_s, acc_s, s_a, s_b, p_a, p_b, al_a, al_b):
        i = pl.program_id(1)
        bufs = ((s_a, p_a, al_a), (s_b, p_b, al_b))
        m_s[...] = jnp.full_like(m_s, -jnp.inf)
        l_s[...] = jnp.zeros_like(l_s)
        acc_s[...] = jnp.zeros_like(acc_s)
        p_b[...] = jnp.zeros_like(p_b)
        al_b[...] = jnp.ones_like(al_b)

        def scores(j, half):
            off = pl.multiple_of(j * tq + half * th, th)
            bufs[half][0][...] = _dot_nt(q_ref[...], k_ref[pl.ds(off, th), :])

        def values(j, half):
            off = pl.multiple_of(j * tq + half * th, th)
            _, p_buf, al = bufs[half]
            acc_s[...] = (jnp.tile(al[...], (1, hd // LANES)) * acc_s[...]
                          + _dot(p_buf[...], v_ref[pl.ds(off, th), :]))

        def softmax(j, half, masked):
            s_buf, p_buf, al = bufs[half]
            ck2 = ck_ref[0, j][:, half * th:(half + 1) * th] * LOG2E
            for c in range(tq // rc):
                r = slice(c * rc, (c + 1) * rc)
                s = s_buf[r, :] * scale2 - ck2
                if masked:
                    rows = lax.broadcasted_iota(jnp.int32, (rc, th), 0) + c * rc
                    cols = lax.broadcasted_iota(jnp.int32, (rc, th), 1) + half * th
                    s = jnp.where(rows >= cols, s, -jnp.inf)
                s_buf[r, :] = s
                m_prev = m_s[r, :]
                m_new = jnp.maximum(m_prev, jnp.max(s, axis=1, keepdims=True))
                al[r, :] = jnp.exp2(m_prev - m_new)
                m_s[r, :] = m_new
            for c in range(tq // rc):
                r = slice(c * rc, (c + 1) * rc)
                p = jnp.exp2(s_buf[r, :] - jnp.tile(m_s[r, :], (1, reps)))
                p_buf[r, :] = p.astype(BF16)
                lane_sum = p[:, 0:LANES]
                for b in range(1, reps):
                    lane_sum = lane_sum + p[:, b * LANES:(b + 1) * LANES]
                l_s[r, :] = al[r, :] * l_s[r, :] + lane_sum

        scores(0, 0)

        def loop_body(j, carry):
            scores(j, 1)
            values(jnp.maximum(j - 1, 0), 1)
            softmax(j, 0, False)
            scores(j + 1, 0)
            values(j, 0)
            softmax(j, 1, False)
            return carry

        lax.fori_loop(0, i, loop_body, 0)
        scores(i, 1)
        values(jnp.maximum(i - 1, 0), 1)
        softmax(i, 0, True)
        values(i, 0)
        softmax(i, 1, True)
        values(i, 1)
        l = jnp.sum(l_s[...], axis=1, keepdims=True)
        o = acc_s[...] / l
        z = z_ref[...]
        o_ref[...] = o
        y_ref[...] = (o * (z * _sigmoid(z))).astype(BF16)
        lse_ref[0] = m_s[:, 0:1] + jnp.log(l) * LOG2E

    blk = pl.BlockSpec((tq, hd), lambda h, i: (i, h))
    head = pl.BlockSpec((t, hd), lambda h, i: (0, h))
    col = pl.BlockSpec((1, tq, 1), lambda h, i: (h, i, 0))
    stat = pltpu.VMEM((tq, LANES), F32)
    return pl.pallas_call(
        body, name=name, grid=(heads, nq),
        in_specs=[blk, head, head,
                  pl.BlockSpec((1, nq, 1, tq), lambda h, i: (h, 0, 0, 0)),
                  pl.BlockSpec((tq, hd), lambda h, i: (i, zcol + h))],
        out_specs=[blk, blk, col],
        out_shape=[jax.ShapeDtypeStruct((t, heads * hd), F32),
                   jax.ShapeDtypeStruct((t, heads * hd), BF16),
                   jax.ShapeDtypeStruct((heads, t, 1), F32)],
        scratch_shapes=[stat, stat, pltpu.VMEM((tq, hd), F32),
                        pltpu.VMEM((tq, th), F32), pltpu.VMEM((tq, th), F32),
                        pltpu.VMEM((tq, th), BF16), pltpu.VMEM((tq, th), BF16), stat, stat],
        compiler_params=_params(("arbitrary", "arbitrary")),
    )(qn, kn, vb, ck, proj)


def _attn_bwd_prep(dy, proj, o, heads, hd, name):
    t, bw = dy.shape
    tr = _pick(t, 256, 16)

    def body(dy_ref, z_ref, o_ref, do_ref, dz_ref, delta_ref):
        dyv, z, ov = dy_ref[...], z_ref[...], o_ref[...]
        sz = _sigmoid(z)
        do = dyv * (z * sz)
        do_ref[...] = do.astype(BF16)
        dz_ref[...] = (dyv * ov * (sz * (1.0 + z * (1.0 - sz)))).astype(BF16)
        prod = do * ov
        for h in range(heads):
            delta_ref[h] = jnp.sum(prod[:, h * hd:(h + 1) * hd], axis=1, keepdims=True)

    row = pl.BlockSpec((tr, bw), lambda i: (i, 0))
    return pl.pallas_call(
        body, name=name, grid=(t // tr,),
        in_specs=[row, pl.BlockSpec((tr, bw), lambda i: (i, 3)), row],
        out_specs=[row, row, pl.BlockSpec((heads, tr, 1), lambda i: (0, i, 0))],
        out_shape=[jax.ShapeDtypeStruct((t, bw), BF16), jax.ShapeDtypeStruct((t, bw), BF16),
                   jax.ShapeDtypeStruct((heads, t, 1), F32)],
        compiler_params=_params(("arbitrary",)),
    )(dy, proj, o)


def _flash_bwd(qn, kn, vb, do, ck, lse, delta, heads, hd, tq, name):
    t = qn.shape[0]
    nq = t // tq
    scale = hd ** -0.5
    scale2 = scale * LOG2E

    def body(q_ref, k_ref, v_ref, do_ref, ck_ref, lse_ref, delta_ref,
             dq_ref, dk_ref, dv_ref, dcq_ref, dck_ref, dq_s, dcq_s):
        i = pl.program_id(1)

        @pl.when(i == 0)
        def _():
            dk_ref[...] = jnp.zeros_like(dk_ref)
            dv_ref[...] = jnp.zeros_like(dv_ref)
            dck_ref[...] = jnp.zeros_like(dck_ref)

        q, dov = q_ref[...], do_ref[...]
        lsev, deltav = lse_ref[0], delta_ref[0]
        dq_s[...] = jnp.zeros_like(dq_s)
        dcq_s[...] = jnp.zeros_like(dcq_s)

        def step(j, masked):
            off = pl.multiple_of(j * tq, tq)
            kblk = k_ref[pl.ds(off, tq), :]
            s = _dot_nt(q, kblk) * scale2 - ck_ref[0, j] * LOG2E
            if masked:
                rows = lax.broadcasted_iota(jnp.int32, (tq, tq), 0)
                cols = lax.broadcasted_iota(jnp.int32, (tq, tq), 1)
                s = jnp.where(rows >= cols, s, -jnp.inf)
            p = jnp.exp2(s - lsev)
            dp = _dot_nt(dov, v_ref[pl.ds(off, tq), :])
            ds = p * (dp - deltav)
            dsb = ds.astype(BF16)
            dv_ref[pl.ds(off, tq), :] += _dot_tn(p.astype(BF16), dov)
            dk_ref[pl.ds(off, tq), :] += _dot_tn(dsb, q) * scale
            dq_s[...] += _dot(dsb, kblk) * scale
            dcq_s[...] += jnp.sum(ds, axis=1, keepdims=True)
            dck_ref[0, j] += jnp.sum(ds, axis=0, keepdims=True)

        def loop_body(j, carry):
            step(j, False)
            return carry

        lax.fori_loop(0, i, loop_body, 0)
        step(i, True)
        dq_ref[...] = dq_s[...]
        dcq_ref[0] = dcq_s[...]

    blk = pl.BlockSpec((tq, hd), lambda h, i: (i, h))
    head = pl.BlockSpec((t, hd), lambda h, i: (0, h))
    col = pl.BlockSpec((1, tq, 1), lambda h, i: (h, i, 0))
    rowv = pl.BlockSpec((1, nq, 1, tq), lambda h, i: (h, 0, 0, 0))
    full = jax.ShapeDtypeStruct((t, heads * hd), F32)
    return pl.pallas_call(
        body, name=name, grid=(heads, nq),
        in_specs=[blk, head, head, blk, rowv, col, col],
        out_specs=[blk, head, head, col, rowv],
        out_shape=[full, full, full, jax.ShapeDtypeStruct((heads, t, 1), F32),
                   jax.ShapeDtypeStruct((heads, nq, 1, tq), F32)],
        scratch_shapes=[pltpu.VMEM((tq, hd), F32), pltpu.VMEM((tq, 1), F32)],
        compiler_params=_params(("arbitrary", "arbitrary")),
    )(qn, kn, vb, do, ck, lse, delta)


def _attn_bwd_post(dqn, dkn, dv, dz, proj, dcq, dck, q_gain, k_gain, f_bias, heads, hd, name):
    t, bw = dqn.shape
    tr = _pick(t, 128, 16)
    nb = t // tr
    fcol = 4 * bw // LANES
    width = 4 * bw + LANES

    def body(dq_ref, dk_ref, dv_ref, dz_ref, q_ref, k_ref, f_ref, dcq_ref, dck_ref, gq_ref, gk_ref,
             fb_ref, d_ref, dgq_ref, dgk_ref, dfb_ref, carry_ref, rc_ref):
        i = pl.program_id(0)

        @pl.when(i == 0)
        def _():
            carry_ref[...] = jnp.zeros_like(carry_ref)
            dgq_ref[...] = jnp.zeros_like(dgq_ref)
            dgk_ref[...] = jnp.zeros_like(dgk_ref)
            dfb_ref[...] = jnp.zeros_like(dfb_ref)

        for idx, (g_ref, raw_ref, gain_ref, dgain_ref) in enumerate(
                ((dq_ref, q_ref, gq_ref, dgq_ref), (dk_ref, k_ref, gk_ref, dgk_ref))):
            gain = gain_ref[...]
            dgain = jnp.zeros((1, hd), F32)
            for h in range(heads):
                sl = slice(h * hd, (h + 1) * hd)
                v, dn = raw_ref[:, sl], g_ref[:, sl]
                r = lax.rsqrt(jnp.mean(v * v, axis=1, keepdims=True) + NORM_EPS)
                vh = v * r
                dgain = dgain + jnp.sum(dn * vh, axis=0, keepdims=True)
                dvh = dn * gain
                draw = r * (dvh - vh * jnp.mean(dvh * vh, axis=1, keepdims=True))
                d_ref[:, idx * bw + h * hd:idx * bw + (h + 1) * hd] = draw.astype(BF16)
            dgain_ref[...] += dgain
        d_ref[:, 2 * bw:3 * bw] = dv_ref[...].astype(BF16)
        d_ref[:, 3 * bw:4 * bw] = dz_ref[...]
        rows = lax.broadcasted_iota(jnp.int32, (tr, tr), 0)
        cols = lax.broadcasted_iota(jnp.int32, (tr, tr), 1)
        upper = jnp.where(cols >= rows, 1.0, 0.0).astype(BF16)
        hi, mid, lo = _split3(dcq_ref[...] - dck_ref[...])
        rc_ref[...] = (_dot(upper, hi) + _dot(upper, mid) + _dot(upper, lo)) + carry_ref[...]
        carry_ref[...] = rc_ref[0:1, :]
        df = rc_ref[...] * (1.0 / (1.0 + jnp.exp(f_ref[...] + fb_ref[...])))
        d_ref[:, 4 * bw:] = df.astype(BF16)
        dfb_ref[...] += jnp.sum(df, axis=0, keepdims=True)

    wide = lambda col: pl.BlockSpec((tr, bw), lambda i: (nb - 1 - i, col))
    lane = lambda col: pl.BlockSpec((tr, LANES), lambda i: (nb - 1 - i, col))
    vec = pl.BlockSpec((1, hd), lambda i: (0, 0))
    vecl = pl.BlockSpec((1, LANES), lambda i: (0, 0))
    return pl.pallas_call(
        body, name=name, grid=(nb,),
        in_specs=[wide(0), wide(0), wide(0), wide(0), wide(0), wide(1), lane(fcol), lane(0), lane(0),
                  vec, vec, vecl],
        out_specs=[pl.BlockSpec((tr, width), lambda i: (nb - 1 - i, 0)), vec, vec, vecl],
        out_shape=[jax.ShapeDtypeStruct((t, width), BF16), jax.ShapeDtypeStruct((1, hd), F32),
                   jax.ShapeDtypeStruct((1, hd), F32), jax.ShapeDtypeStruct((1, LANES), F32)],
        scratch_shapes=[pltpu.VMEM((1, LANES), F32), pltpu.VMEM((tr, LANES), F32)],
        compiler_params=_params(("arbitrary",)),
    )(dqn, dkn, dv, dz, proj, proj, proj, dcq, dck, q_gain, k_gain, f_bias)


def _adamw(w, m, v, parts, name):
    r, c = w.shape
    unit = 32 // parts.dtype.itemsize
    row_bytes = c * (7 * 4 + N_DEV * parts.dtype.itemsize)
    tr = _pick(r, max(unit, 12 * 1024 * 1024 // row_bytes), unit)
    c1 = 1.0 / (1.0 - ADAM_B1 ** ADAM_STEP)
    c2 = 1.0 / (1.0 - ADAM_B2 ** ADAM_STEP)

    def body(w_ref, m_ref, v_ref, p_ref, g_ref, d_ref, nm_ref, nv_ref):
        g = p_ref[0].astype(F32)
        for j in range(1, N_DEV):
            g = g + p_ref[j].astype(F32)
        nm = ADAM_B1 * m_ref[...] + (1.0 - ADAM_B1) * g
        nv = ADAM_B2 * v_ref[...] + (1.0 - ADAM_B2) * (g * g)
        g_ref[...] = g
        nm_ref[...] = nm
        nv_ref[...] = nv
        d_ref[...] = -ADAM_LR * ((nm * c1) / (jnp.sqrt(nv * c2) + ADAM_EPS) + ADAM_WD * w_ref[...])

    row = pl.BlockSpec((tr, c), lambda i: (i, 0))
    return pl.pallas_call(
        body, name=name, grid=(r // tr,),
        in_specs=[row, row, row, pl.BlockSpec((N_DEV, tr, c), lambda i: (0, i, 0))],
        out_specs=[row] * 4,
        out_shape=[jax.ShapeDtypeStruct((r, c), F32)] * 4,
        compiler_params=_params(("arbitrary",)),
    )(w, m, v, parts)


def _flat_rows(pieces):
    rows = []
    for p in pieces:
        f = p.reshape(-1)
        f = jnp.pad(f, (0, (-f.shape[0]) % LANES))
        rows.append(f.reshape(-1, LANES))
    out = jnp.concatenate(rows, axis=0)
    return jnp.pad(out, ((0, (-out.shape[0]) % 8), (0, 0)))


def _unflat_rows(flat, shapes):
    outs, r0 = [], 0
    lead = flat.shape[:-2]
    for s in shapes:
        size = 1
        for d in s:
            size *= d
        nr = -(-size // LANES)
        piece = flat[..., r0:r0 + nr, :].reshape(lead + (nr * LANES,))[..., :size]
        outs.append(piece.reshape(lead + tuple(s)))
        r0 += nr
    return outs


def kernel(x, a_norm_g, a_w_in, a_v_norm_g, a_w_s, a_b_s, a_w_out, b_norm_g, b_w_in, b_f_bias, b_q_norm_g, b_k_norm_g, b_w_out, loss_target, m_a_norm_g, m_a_w_in, m_a_v_norm_g, m_a_w_s, m_a_b_s, m_a_w_out, m_b_norm_g, m_b_w_in, m_b_f_bias, m_b_q_norm_g, m_b_k_norm_g, m_b_w_out, v_a_norm_g, v_a_w_in, v_a_v_norm_g, v_a_w_s, v_a_b_s, v_a_w_out, v_b_norm_g, v_b_w_in, v_b_f_bias, v_b_q_norm_g, v_b_k_norm_g, v_b_w_out):
    t, d = x.shape[1], x.shape[2]
    n_a, n_b = a_w_in.shape[0], b_w_in.shape[0]
    depth = n_a + n_b
    aw = a_w_out.shape[1] * N_DEV
    groups, chunk = a_w_s.shape[1], a_w_s.shape[2]
    heads, hd = b_f_bias.shape[1], b_q_norm_g.shape[1]
    bw = heads * hd
    b_cols = b_w_in.shape[2]
    tq = _pick(t, 512, LANES)
    nq = t // tq
    me = _dev_index((lax.axis_index("x"), lax.axis_index("y"), lax.axis_index("c")))

    ga_in, ga_out, gb_in, gb_out, gb_norm = _all_gather(
        [a_w_in.astype(BF16), a_w_out.astype(BF16), b_w_in.astype(BF16), b_w_out.astype(BF16),
         b_norm_g], "gather_weights")
    wa_in = ga_in.transpose(1, 2, 0, 3).reshape(n_a, d, 3 * aw)
    wa_out = ga_out.transpose(1, 0, 2, 3).reshape(n_a, aw, d)
    wb_in = gb_in.transpose(1, 2, 0, 3).reshape(n_b, d, 4 * bw + heads)
    wb_in = jnp.pad(wb_in, ((0, 0), (0, 0), (0, LANES - heads)))
    wb_out = gb_out.transpose(1, 0, 2, 3).reshape(n_b, bw, d)
    b_norm_full = gb_norm.transpose(1, 0, 2).reshape(n_b, d)

    pad_h = lambda v: jnp.pad(v, ((0, 0), (0, LANES - heads)))
    f_bias_p = pad_h(b_f_bias)
    b_s_col = a_b_s.reshape(n_a, groups, chunk, 1)
    w_s_t = a_w_s.transpose(0, 1, 3, 2)

    xs = [x[0]]
    saved = []
    for i in range(depth):
        j = i // 2
        xi = xs[-1]
        if i % 2 == 0:
            h = _rms_fwd(xi, a_norm_g[j:j + 1], f"a{j}_norm")
            uvz = _matmul(h, wa_in[j], "nn", F32, f"a{j}_in", tk=d)
            y = _mix_fwd(uvz, a_v_norm_g[j:j + 1], a_w_s[j], b_s_col[j], f"a{j}_mix")
            xs.append(_matmul(y, wa_out[j], "nn", F32, f"a{j}_out", residual=xi))
            saved.append((h, uvz, y))
        else:
            h = _rms_fwd(xi, b_norm_full[j:j + 1], f"b{j}_norm")
            proj = _matmul(h, wb_in[j], "nn", F32, f"b{j}_in", tm=512, tn=1664, tk=d)
            qn, kn, vb, cum = _attn_prep(proj, b_q_norm_g[j:j + 1], b_k_norm_g[j:j + 1],
                                         f_bias_p[j:j + 1], heads, hd, f"b{j}_prep")
            ck = cum[:, :heads].T.reshape(heads, nq, 1, tq)
            o, y, lse = _flash_fwd(qn, kn, vb, ck, proj, heads, hd, tq, f"b{j}_attn")
            xs.append(_matmul(y, wb_out[j], "nn", F32, f"b{j}_out", residual=xi))
            saved.append((h, proj, y, qn, kn, vb, ck, o, lse))

    sq, g, gb = _loss_grad(xs[-1], loss_target[0], "loss")
    loss = 0.5 * lax.psum(sq[0, 0], AXES) / d

    d_a_in, d_a_out, d_b_in, d_b_out = [None] * n_a, [None] * n_a, [None] * n_b, [None] * n_b
    d_a_norm, d_a_vnorm, d_a_ws, d_a_bs = [None] * n_a, [None] * n_a, [None] * n_a, [None] * n_a
    d_b_norm, d_b_fb, d_b_gq, d_b_gk = [None] * n_b, [None] * n_b, [None] * n_b, [None] * n_b
    for i in reversed(range(depth)):
        j = i // 2
        xi = xs[i]
        if i % 2 == 0:
            h, uvz, y = saved[i]
            dy = _matmul(gb, wa_out[j], "nt", F32, f"a{j}_dy", tk=d)
            d_a_out[j] = _matmul(y, gb, "tn", BF16, f"a{j}_dwout")
            duvz, d_a_ws[j], dbs, d_a_vnorm[j] = _mix_bwd(
                uvz, dy, a_v_norm_g[j:j + 1], a_w_s[j], w_s_t[j], b_s_col[j], f"a{j}_mixbwd")
            d_a_bs[j] = dbs.reshape(groups, chunk)
            dh = _matmul(duvz, wa_in[j], "nt", F32, f"a{j}_dh")
            d_a_in[j] = _matmul(h, duvz, "tn", BF16, f"a{j}_dwin")
            g, gb, d_a_norm[j] = _rms_bwd(xi, dh, g, a_norm_g[j:j + 1], f"a{j}_normbwd")
        else:
            h, proj, y, qn, kn, vb, ck, o, lse = saved[i]
            dy = _matmul(gb, wb_out[j], "nt", F32, f"b{j}_dy", tk=d)
            d_b_out[j] = _matmul(y, gb, "tn", BF16, f"b{j}_dwout")
            do, dz, delta = _attn_bwd_prep(dy, proj, o, heads, hd, f"b{j}_bwdprep")
            dqn, dkn, dv, dcq, dck = _flash_bwd(qn, kn, vb, do, ck, lse, delta, heads, hd, tq,
                                                f"b{j}_attnbwd")
            per_token = lambda v: pad_h(v.reshape(heads, t).T)
            dproj, d_b_gq[j], d_b_gk[j], dfb = _attn_bwd_post(
                dqn, dkn, dv, dz, proj, per_token(dcq), per_token(dck), b_q_norm_g[j:j + 1],
                b_k_norm_g[j:j + 1], f_bias_p[j:j + 1], heads, hd, f"b{j}_bwdpost")
            d_b_fb[j] = dfb[:, :heads]
            dh = _matmul(dproj, wb_in[j], "nt", F32, f"b{j}_dh")
            d_b_in[j] = _matmul(h, dproj, "tn", BF16, f"b{j}_dwin", tn=1664)
            g, gb, d_b_norm[j] = _rms_bwd(xi, dh, g, b_norm_full[j:j + 1], f"b{j}_normbwd")
    grad_x = g[None]

    pa_in = jnp.stack([w.reshape(d, N_DEV, -1).transpose(1, 0, 2) for w in d_a_in], axis=1)
    pa_out = jnp.stack([w.reshape(N_DEV, -1, d) for w in d_a_out], axis=1)
    pb_in = jnp.stack([w[:, :4 * bw + heads].reshape(d, N_DEV, b_cols).transpose(1, 0, 2)
                       for w in d_b_in], axis=1)
    pb_out = jnp.stack([w.reshape(N_DEV, -1, d) for w in d_b_out], axis=1)
    ra_in, ra_out, rb_in, rb_out = _all_to_all([pa_in, pa_out, pb_in, pb_out], "exchange_grads")

    small = [jnp.concatenate(d_a_norm, 0), jnp.concatenate(d_a_vnorm, 0), jnp.stack(d_a_ws, 0),
             jnp.stack(d_a_bs, 0), jnp.concatenate(d_b_fb, 0), jnp.concatenate(d_b_gq, 0),
             jnp.concatenate(d_b_gk, 0)]
    small_w = [a_norm_g, a_v_norm_g, a_w_s, a_b_s, b_f_bias, b_q_norm_g, b_k_norm_g]
    small_m = [m_a_norm_g, m_a_v_norm_g, m_a_w_s, m_a_b_s, m_b_f_bias, m_b_q_norm_g, m_b_k_norm_g]
    small_v = [v_a_norm_g, v_a_v_norm_g, v_a_w_s, v_a_b_s, v_b_f_bias, v_b_q_norm_g, v_b_k_norm_g]
    small_flat = _flat_rows(small)
    n_small = small_flat.shape[0]
    (gathered_small,) = _all_gather(
        [jnp.concatenate([small_flat, _flat_rows([jnp.concatenate(d_b_norm, 0)])], axis=0)],
        "gather_small_grads")
    parts_small = gathered_small[:, :n_small]
    parts_b_norm = gathered_small[:, n_small:n_small + n_b * d // LANES].reshape(N_DEV, n_b, d)
    parts_b_norm = lax.dynamic_slice_in_dim(parts_b_norm, me * (d // N_DEV), d // N_DEV, axis=2)

    def update(w, m, v, parts, name):
        shape = w.shape
        w2 = w.reshape(-1, shape[-1])
        outs = _adamw(w2, m.reshape(w2.shape), v.reshape(w2.shape),
                      parts.reshape((N_DEV,) + w2.shape), name)
        return [o_.reshape(shape) for o_ in outs]

    u_a_in = update(a_w_in, m_a_w_in, v_a_w_in, ra_in, "adamw_a_w_in")
    u_a_out = update(a_w_out, m_a_w_out, v_a_w_out, ra_out, "adamw_a_w_out")
    u_b_in = update(b_w_in, m_b_w_in, v_b_w_in, rb_in, "adamw_b_w_in")
    u_b_out = update(b_w_out, m_b_w_out, v_b_w_out, rb_out, "adamw_b_w_out")
    u_b_norm = update(b_norm_g, m_b_norm_g, v_b_norm_g, parts_b_norm, "adamw_b_norm")
    u_small = _adamw(_flat_rows(small_w), _flat_rows(small_m), _flat_rows(small_v), parts_small,
                     "adamw_small")
    shapes = [w.shape for w in small_w]
    u_small = [_unflat_rows(o_, shapes) for o_ in u_small]

    def per_kind(k):
        s = u_small[k]
        return [s[0], u_a_in[k], s[1], s[2], s[3], u_a_out[k], u_b_norm[k], u_b_in[k], s[4], s[5],
                s[6], u_b_out[k]]

    return (loss, grad_x, *per_kind(0), *per_kind(1), *per_kind(2), *per_kind(3))
```

```python
import functools

import jax
import jax.numpy as jnp
from jax import lax
from jax.experimental import pallas as pl
from jax.experimental.pallas import tpu as pltpu

F32 = jnp.float32
BF16 = jnp.bfloat16
MESH = pl.DeviceIdType.MESH
AXES = ("x", "y", "c")
N_DEV = 8
NORM_EPS = 1e-6
LANES = 128
VMEM_LIMIT = 56 * 1024 * 1024

ADAM_LR = 0.001
ADAM_B1 = 0.9
ADAM_B2 = 0.999
ADAM_EPS = 1e-08
ADAM_WD = 0.01
ADAM_STEP = 10

LOG2E = 1.4426950408889634
SOFTMAX_ROWS = 64

GELU_C0 = 0.7978845608028654
GELU_C1 = 0.044715

NT_DIMS = (((1,), (1,)), ((), ()))
TN_DIMS = (((0,), (0,)), ((), ()))


def _params(sem=None):
    return pltpu.CompilerParams(dimension_semantics=sem, vmem_limit_bytes=VMEM_LIMIT)


def _pick(n, target, unit):
    best = None
    for t in range(unit, min(n, target) + 1, unit):
        if n % t == 0:
            best = t
    return n if best is None else best


def _sigmoid(x):
    return 1.0 / (1.0 + jnp.exp(-x))


def _gelu(x):
    return 0.5 * x * (1.0 + jnp.tanh(GELU_C0 * x * (1.0 + GELU_C1 * x * x)))


def _gelu_and_grad(x):
    x2 = x * x
    t = jnp.tanh(GELU_C0 * x * (1.0 + GELU_C1 * x2))
    g = 0.5 * x * (1.0 + t)
    dg = 0.5 * (1.0 + t) + 0.5 * x * (1.0 - t * t) * (GELU_C0 * (1.0 + 3.0 * GELU_C1 * x2))
    return g, dg


def _dot(a, b):
    return jnp.dot(a, b, preferred_element_type=F32)


def _dot_nt(a, b):
    return lax.dot_general(a, b, NT_DIMS, preferred_element_type=F32)


def _dot_tn(a, b):
    return lax.dot_general(a, b, TN_DIMS, preferred_element_type=F32)


def _split3(v):
    hi = v.astype(BF16)
    r1 = v - hi.astype(F32)
    mid = r1.astype(BF16)
    lo = (r1 - mid.astype(F32)).astype(BF16)
    return hi, mid, lo


def _dev_index(p):
    return 4 * p[0] + 2 * p[1] + p[2]


def _block(ref, idx, axis, size):
    if axis is None:
        return ref.at[idx]
    start = pl.multiple_of(idx * size, size)
    return ref.at[(slice(None),) * axis + (pl.ds(start, size),)]


class _Exchange:
    def __init__(self, arrays, axes):
        self.arrays, self.axes, self.n = list(arrays), list(axes), len(arrays)

    def scratch(self):
        return [pltpu.SemaphoreType.DMA((self.n, 7)), pltpu.SemaphoreType.DMA((self.n, 7)),
                pltpu.SemaphoreType.DMA((self.n,))]

    @staticmethod
    def _place():
        x, y, c = lax.axis_index("x"), lax.axis_index("y"), lax.axis_index("c")
        return x, y, c


class _Gather(_Exchange):
    def out_shapes(self):
        outs = []
        for a, ax in zip(self.arrays, self.axes):
            if ax is None:
                shape = (N_DEV,) + a.shape
            else:
                shape = a.shape[:ax] + (N_DEV * a.shape[ax],) + a.shape[ax + 1:]
            outs.append(jax.ShapeDtypeStruct(shape, a.dtype))
        return outs

    def _copy(self, ins, outs, sems, a, k, block, to, src=None):
        ax = self.axes[a]
        dst = _block(outs[a], _dev_index(block), ax, None if ax is None else self.arrays[a].shape[ax])
        return pltpu.make_async_remote_copy(
            src_ref=dst if src is None else src, dst_ref=dst,
            send_sem=sems[0].at[a, k], recv_sem=sems[1].at[a, k],
            device_id=to, device_id_type=MESH)

    def _mine(self, ins, outs, sems, a, me):
        ax = self.axes[a]
        dst = _block(outs[a], _dev_index(me), ax, None if ax is None else self.arrays[a].shape[ax])
        return pltpu.make_async_copy(ins[a], dst, sems[2].at[a])

    def _first(self, ins, outs, sems):
        x, y, c = self._place()
        me, sibling = (x, y, c), (x, y, 1 - c)
        chips = [(1 - x, y), (x, 1 - y), (1 - x, 1 - y)]
        first = []
        for a in range(self.n):
            first.append(self._copy(ins, outs, sems, a, 0, me, sibling, src=ins[a]))
            first += [self._copy(ins, outs, sems, a, 1 + j, me, (*chip, c), src=ins[a])
                      for j, chip in enumerate(chips)]
        return first

    def start(self, ins, outs, sems):
        me = self._place()
        for a in range(self.n):
            self._mine(ins, outs, sems, a, me).start()
        for cp in self._first(ins, outs, sems):
            cp.start()

    def finish(self, ins, outs, sems):
        x, y, c = self._place()
        me, sibling = (x, y, c), (x, y, 1 - c)
        chips = [(1 - x, y), (x, 1 - y), (1 - x, 1 - y)]
        passed = []
        for a in range(self.n):
            for j, chip in enumerate(chips):
                self._copy(ins, outs, sems, a, 1 + j, (*chip, c), me).wait_recv()
                cp = self._copy(ins, outs, sems, a, 4 + j, (*chip, c), sibling)
                cp.start()
                passed.append(cp)
        for a in range(self.n):
            self._copy(ins, outs, sems, a, 0, sibling, me).wait_recv()
            for j, chip in enumerate(chips):
                self._copy(ins, outs, sems, a, 4 + j, (*chip, 1 - c), me).wait_recv()
        for cp in self._first(ins, outs, sems) + passed:
            cp.wait_send()
        for a in range(self.n):
            self._mine(ins, outs, sems, a, me).wait()


class _AllToAll(_Exchange):
    def _blk_shape(self, a):
        arr, ax = self.arrays[a], self.axes[a]
        if ax is None:
            return arr.shape[1:]
        return arr.shape[:ax] + (arr.shape[ax] // N_DEV,) + arr.shape[ax + 1:]

    def out_shapes(self):
        return [jax.ShapeDtypeStruct((N_DEV,) + self._blk_shape(a), self.arrays[a].dtype)
                for a in range(self.n)]

    def _src(self, ins, a, idx):
        ax = self.axes[a]
        return _block(ins[a], idx, ax, None if ax is None else self.arrays[a].shape[ax] // N_DEV)

    def _peers(self):
        x, y, c = self._place()
        return [((1 - x) if r & 4 else x, (1 - y) if r & 2 else y, (1 - c) if r & 1 else c)
                for r in range(1, N_DEV)]

    def _sends(self, ins, outs, sems):
        me = _dev_index(self._place())
        return [pltpu.make_async_remote_copy(
            src_ref=self._src(ins, a, _dev_index(peer)), dst_ref=outs[a].at[me],
            send_sem=sems[0].at[a, k], recv_sem=sems[1].at[a, k],
            device_id=peer, device_id_type=MESH)
            for a in range(self.n) for k, peer in enumerate(self._peers())]

    def _mine(self, ins, outs, sems):
        me = _dev_index(self._place())
        return [pltpu.make_async_copy(self._src(ins, a, me), outs[a].at[me], sems[2].at[a])
                for a in range(self.n)]

    def start(self, ins, outs, sems):
        for cp in self._mine(ins, outs, sems) + self._sends(ins, outs, sems):
            cp.start()

    def finish(self, ins, outs, sems):
        for a in range(self.n):
            for k, peer in enumerate(self._peers()):
                landed = outs[a].at[_dev_index(peer)]
                pltpu.make_async_remote_copy(
                    src_ref=landed, dst_ref=landed, send_sem=sems[0].at[a, k],
                    recv_sem=sems[1].at[a, k], device_id=peer, device_id_type=MESH).wait_recv()
        for cp in self._sends(ins, outs, sems):
            cp.wait_send()
        for cp in self._mine(ins, outs, sems):
            cp.wait()


def _run_exchange(exchange, name):
    n = exchange.n

    def body(*refs):
        ins, outs, sems = refs[:n], refs[n:2 * n], refs[2 * n:]
        exchange.start(ins, outs, sems)
        exchange.finish(ins, outs, sems)

    any_spec = pl.BlockSpec(memory_space=pl.ANY)
    return pl.pallas_call(
        body, name=name, out_shape=exchange.out_shapes(),
        in_specs=[any_spec] * n, out_specs=[any_spec] * n, scratch_shapes=exchange.scratch(),
    )(*exchange.arrays)


def _matmul(a, b, mode, out_dtype, name, tm=1024, tn=1024, tk=1024, residual=None, exchange=None):
    if mode == "tn":
        kdim, m = a.shape
    else:
        m, kdim = a.shape
    n = b.shape[0] if mode == "nt" else b.shape[1]
    tm, tn, tk = _pick(m, tm, LANES), _pick(n, tn, LANES), _pick(kdim, tk, LANES)
    nk = kdim // tk
    if mode == "tn":
        a_spec = pl.BlockSpec((tk, tm), lambda i, j, k: (k, i))
    else:
        a_spec = pl.BlockSpec((tm, tk), lambda i, j, k: (i, k))
    if mode == "nt":
        b_spec = pl.BlockSpec((tn, tk), lambda i, j, k: (j, k))
    else:
        b_spec = pl.BlockSpec((tk, tn), lambda i, j, k: (k, j))
    o_spec = pl.BlockSpec((tm, tn), lambda i, j, k: (i, j))
    dot = {"nn": _dot, "nt": _dot_nt, "tn": _dot_tn}[mode]
    has_res = residual is not None
    n_in = 3 if has_res else 2
    n_ex = 0 if exchange is None else exchange.n
    ni, nj = m // tm, n // tn

    def body(*refs):
        a_ref, b_ref = refs[:2]
        r_ref = refs[2] if has_res else None
        ex_ins = refs[n_in:n_in + n_ex]
        o_ref = refs[n_in + n_ex]
        ex_outs = refs[n_in + n_ex + 1:n_in + 2 * n_ex + 1]
        scratch = refs[n_in + 2 * n_ex + 1:]
        i, j, k = pl.program_id(0), pl.program_id(1), pl.program_id(2)

        if exchange is not None:
            sems = scratch[-3:]

            @pl.when((i == 0) & (j == 0) & (k == 0))
            def _():
                exchange.start(ex_ins, ex_outs, sems)

        def finish(acc):
            if has_res:
                acc = acc + r_ref[...]
            o_ref[...] = acc.astype(out_dtype)

        if nk == 1:
            finish(dot(a_ref[...], b_ref[...]))
        else:
            acc_ref = scratch[0]

            @pl.when(k == 0)
            def _():
                acc_ref[...] = jnp.zeros_like(acc_ref)

            acc_ref[...] += dot(a_ref[...], b_ref[...])

            @pl.when(k == nk - 1)
            def _():
                finish(acc_ref[...])

        if exchange is not None:
            @pl.when((i == ni - 1) & (j == nj - 1) & (k == nk - 1))
            def _():
                exchange.finish(ex_ins, ex_outs, sems)

    any_spec = pl.BlockSpec(memory_space=pl.ANY)
    operands = (a, b, residual) if has_res else (a, b)
    out_shape = jax.ShapeDtypeStruct((m, n), out_dtype)
    scratch_shapes = [] if nk == 1 else [pltpu.VMEM((tm, tn), F32)]
    if exchange is None:
        return pl.pallas_call(
            body, name=name, grid=(ni, nj, nk),
            in_specs=[a_spec, b_spec] + ([o_spec] if has_res else []),
            out_specs=o_spec, out_shape=out_shape, scratch_shapes=scratch_shapes,
            compiler_params=_params(("parallel", "parallel", "arbitrary")),
        )(*operands)
    return pl.pallas_call(
        body, name=name, grid=(ni, nj, nk),
        in_specs=[a_spec, b_spec] + ([o_spec] if has_res else []) + [any_spec] * n_ex,
        out_specs=[o_spec] + [any_spec] * n_ex,
        out_shape=[out_shape] + exchange.out_shapes(),
        scratch_shapes=scratch_shapes + exchange.scratch(),
        compiler_params=_params(("arbitrary", "arbitrary", "arbitrary")),
    )(*operands, *exchange.arrays)


def _rms_fwd(x, gain, name):
    t, d = x.shape
    tr = _pick(t, 256, 16)

    def body(x_ref, g_ref, h_ref):
        xv = x_ref[...]
        r = lax.rsqrt(jnp.mean(xv * xv, axis=1, keepdims=True) + NORM_EPS)
        h_ref[...] = (xv * r * g_ref[...]).astype(BF16)

    row = pl.BlockSpec((tr, d), lambda i: (i, 0))
    return pl.pallas_call(
        body, name=name, grid=(t // tr,),
        in_specs=[row, pl.BlockSpec((1, d), lambda i: (0, 0))],
        out_specs=row, out_shape=jax.ShapeDtypeStruct((t, d), BF16),
        compiler_params=_params(("arbitrary",)),
    )(x, gain)


def _rms_bwd(x, dh, g_in, gain, name):
    t, d = x.shape
    tr = _pick(t, 256, 16)

    def body(x_ref, dh_ref, gin_ref, g_ref, dx_ref, dxb_ref, dg_ref):
        i = pl.program_id(0)
        xv, dhv = x_ref[...], dh_ref[...]
        r = lax.rsqrt(jnp.mean(xv * xv, axis=1, keepdims=True) + NORM_EPS)
        xh = xv * r
        dxh = dhv * g_ref[...]
        dx = gin_ref[...] + r * (dxh - xh * jnp.mean(dxh * xh, axis=1, keepdims=True))
        dx_ref[...] = dx
        dxb_ref[...] = dx.astype(BF16)

        @pl.when(i == 0)
        def _():
            dg_ref[...] = jnp.zeros_like(dg_ref)

        dg_ref[...] += jnp.sum(dhv * xh, axis=0, keepdims=True)

    row = pl.BlockSpec((tr, d), lambda i: (i, 0))
    vec = pl.BlockSpec((1, d), lambda i: (0, 0))
    return pl.pallas_call(
        body, name=name, grid=(t // tr,),
        in_specs=[row, row, row, vec],
        out_specs=[row, row, vec],
        out_shape=[jax.ShapeDtypeStruct((t, d), F32), jax.ShapeDtypeStruct((t, d), BF16),
                   jax.ShapeDtypeStruct((1, d), F32)],
        compiler_params=_params(("arbitrary",)),
    )(x, dh, g_in, gain)


def _loss_grad(y, target, name):
    t, d = y.shape
    tr = _pick(t, 256, 16)

    def body(y_ref, t_ref, s_ref, g_ref, gb_ref):
        i = pl.program_id(0)
        e = y_ref[...] - t_ref[...]
        g = e * (1.0 / d)
        g_ref[...] = g
        gb_ref[...] = g.astype(BF16)

        @pl.when(i == 0)
        def _():
            s_ref[...] = jnp.zeros_like(s_ref)

        s_ref[...] += jnp.sum(jnp.sum(e * e, axis=1, keepdims=True), axis=0, keepdims=True)

    row = pl.BlockSpec((tr, d), lambda i: (i, 0))
    return pl.pallas_call(
        body, name=name, grid=(t // tr,),
        in_specs=[row, row],
        out_specs=[pl.BlockSpec((1, 1), lambda i: (0, 0)), row, row],
        out_shape=[jax.ShapeDtypeStruct((1, 1), F32), jax.ShapeDtypeStruct((t, d), F32),
                   jax.ShapeDtypeStruct((t, d), BF16)],
        compiler_params=_params(("arbitrary",)),
    )(y, target)


def _causal_weights(ws_ref, g, chunk, transposed):
    rows = lax.broadcasted_iota(jnp.int32, (chunk, chunk), 0)
    cols = lax.broadcasted_iota(jnp.int32, (chunk, chunk), 1)
    keep = (cols >= rows) if transposed else (rows >= cols)
    return jnp.where(keep, ws_ref[g], 0.0).astype(BF16)


def _mix_fwd(uvz, v_gain, w_s, b_s, name):
    t, w3 = uvz.shape
    w = w3 // 3
    groups, chunk = w_s.shape[0], w_s.shape[1]
    gd = w // groups

    def body(uvz_ref, gam_ref, ws_ref, bs_ref, y_ref):
        gv = _gelu(uvz_ref[:, w:2 * w])
        r = lax.rsqrt(jnp.mean(gv * gv, axis=1, keepdims=True) + NORM_EPS)
        vn = (gv * r * gam_ref[...]).astype(BF16)
        for g in range(groups):
            sl = slice(g * gd, (g + 1) * gd)
            mixed = _dot(_causal_weights(ws_ref, g, chunk, False), vn[:, sl]) + bs_ref[g]
            u = uvz_ref[:, g * gd:(g + 1) * gd]
            z = uvz_ref[:, 2 * w + g * gd:2 * w + (g + 1) * gd]
            y_ref[:, sl] = (_gelu(u) * mixed * (z * _sigmoid(z))).astype(BF16)

    return pl.pallas_call(
        body, name=name, grid=(t // chunk,),
        in_specs=[pl.BlockSpec((chunk, w3), lambda i: (i, 0)),
                  pl.BlockSpec((1, w), lambda i: (0, 0)),
                  pl.BlockSpec((groups, chunk, chunk), lambda i: (0, 0, 0)),
                  pl.BlockSpec((groups, chunk, 1), lambda i: (0, 0, 0))],
        out_specs=pl.BlockSpec((chunk, w), lambda i: (i, 0)),
        out_shape=jax.ShapeDtypeStruct((t, w), BF16),
        compiler_params=_params(("arbitrary",)),
    )(uvz, v_gain, w_s, b_s)


def _mix_bwd(uvz, dy, v_gain, w_s, w_s_t, b_s, name):
    t, w3 = uvz.shape
    w = w3 // 3
    groups, chunk = w_s.shape[0], w_s.shape[1]
    gd = w // groups

    def body(uvz_ref, dy_ref, gam_ref, ws_ref, wst_ref, bs_ref, d_ref, dws_ref, dbs_ref, dgam_ref,
             dvn_ref):
        i = pl.program_id(0)

        @pl.when(i == 0)
        def _():
            dws_ref[...] = jnp.zeros_like(dws_ref)
            dbs_ref[...] = jnp.zeros_like(dbs_ref)
            dgam_ref[...] = jnp.zeros_like(dgam_ref)

        gv, dgv = _gelu_and_grad(uvz_ref[:, w:2 * w])
        r = lax.rsqrt(jnp.mean(gv * gv, axis=1, keepdims=True) + NORM_EPS)
        vh = gv * r
        gam = gam_ref[...]
        vn = (vh * gam).astype(BF16)
        rows = lax.broadcasted_iota(jnp.int32, (chunk, chunk), 0)
        cols = lax.broadcasted_iota(jnp.int32, (chunk, chunk), 1)
        for g in range(groups):
            sl = slice(g * gd, (g + 1) * gd)
            mixed = _dot(_causal_weights(ws_ref, g, chunk, False), vn[:, sl]) + bs_ref[g]
            gu, dgu = _gelu_and_grad(uvz_ref[:, g * gd:(g + 1) * gd])
            z = uvz_ref[:, 2 * w + g * gd:2 * w + (g + 1) * gd]
            sz = _sigmoid(z)
            silu = z * sz
            dyv = dy_ref[:, sl]
            dmixed = dyv * gu * silu
            d_ref[:, sl] = (dyv * mixed * silu * dgu).astype(BF16)
            d_ref[:, 2 * w + g * gd:2 * w + (g + 1) * gd] = (
                dyv * gu * mixed * (sz * (1.0 + z * (1.0 - sz)))).astype(BF16)
            dmb = dmixed.astype(BF16)
            dws_ref[g] += jnp.where(rows >= cols, _dot_nt(dmb, vn[:, sl]), 0.0)
            dbs_ref[g] += jnp.sum(dmixed, axis=1, keepdims=True)
            dvn_ref[:, sl] = _dot(_causal_weights(wst_ref, g, chunk, True), dmb)
        dvn = dvn_ref[...]
        dgam_ref[...] += jnp.sum(dvn * vh, axis=0, keepdims=True)
        dvh = dvn * gam
        dgvv = r * (dvh - vh * jnp.mean(dvh * vh, axis=1, keepdims=True))
        d_ref[:, w:2 * w] = (dgvv * dgv).astype(BF16)

    return pl.pallas_call(
        body, name=name, grid=(t // chunk,),
        in_specs=[pl.BlockSpec((chunk, w3), lambda i: (i, 0)),
                  pl.BlockSpec((chunk, w), lambda i: (i, 0)),
                  pl.BlockSpec((1, w), lambda i: (0, 0)),
                  pl.BlockSpec((groups, chunk, chunk), lambda i: (0, 0, 0)),
                  pl.BlockSpec((groups, chunk, chunk), lambda i: (0, 0, 0)),
                  pl.BlockSpec((groups, chunk, 1), lambda i: (0, 0, 0))],
        out_specs=[pl.BlockSpec((chunk, w3), lambda i: (i, 0)),
                   pl.BlockSpec((groups, chunk, chunk), lambda i: (0, 0, 0)),
                   pl.BlockSpec((groups, chunk, 1), lambda i: (0, 0, 0)),
                   pl.BlockSpec((1, w), lambda i: (0, 0))],
        out_shape=[jax.ShapeDtypeStruct((t, w3), BF16),
                   jax.ShapeDtypeStruct((groups, chunk, chunk), F32),
                   jax.ShapeDtypeStruct((groups, chunk, 1), F32),
                   jax.ShapeDtypeStruct((1, w), F32)],
        scratch_shapes=[pltpu.VMEM((chunk, w), F32)],
        compiler_params=_params(("arbitrary",)),
    )(uvz, dy, v_gain, w_s, w_s_t, b_s)


def _attn_prep(proj, q_gain, k_gain, f_bias, heads, hd, name):
    t = proj.shape[0]
    bw = heads * hd
    tr = _pick(t, 256, 16)
    fcol = 4 * bw // LANES

    def body(q_ref, k_ref, v_ref, f_ref, gq_ref, gk_ref, fb_ref, qn_ref, kn_ref, vb_ref, cum_ref,
             carry_ref):
        i = pl.program_id(0)

        @pl.when(i == 0)
        def _():
            carry_ref[...] = jnp.zeros_like(carry_ref)

        for src, gain, dst in ((q_ref, gq_ref, qn_ref), (k_ref, gk_ref, kn_ref)):
            for h in range(heads):
                sl = slice(h * hd, (h + 1) * hd)
                v = src[:, sl]
                r = lax.rsqrt(jnp.mean(v * v, axis=1, keepdims=True) + NORM_EPS)
                dst[:, sl] = (v * r * gain[...]).astype(BF16)
        vb_ref[...] = v_ref[...].astype(BF16)
        fl = f_ref[...] + fb_ref[...]
        log_f = jnp.minimum(fl, 0.0) - jnp.log(1.0 + jnp.exp(-jnp.abs(fl)))
        rows = lax.broadcasted_iota(jnp.int32, (tr, tr), 0)
        cols = lax.broadcasted_iota(jnp.int32, (tr, tr), 1)
        lower = jnp.where(rows >= cols, 1.0, 0.0).astype(BF16)
        hi, mid, lo = _split3(log_f)
        cum_ref[...] = (_dot(lower, hi) + _dot(lower, mid) + _dot(lower, lo)) + carry_ref[...]
        carry_ref[...] = cum_ref[tr - 1:tr, :]

    wide = lambda col: pl.BlockSpec((tr, bw), lambda i: (i, col))
    vec = pl.BlockSpec((1, hd), lambda i: (0, 0))
    return pl.pallas_call(
        body, name=name, grid=(t // tr,),
        in_specs=[wide(0), wide(1), wide(2), pl.BlockSpec((tr, LANES), lambda i: (i, fcol)),
                  vec, vec, pl.BlockSpec((1, LANES), lambda i: (0, 0))],
        out_specs=[wide(0), wide(0), wide(0), pl.BlockSpec((tr, LANES), lambda i: (i, 0))],
        out_shape=[jax.ShapeDtypeStruct((t, bw), BF16)] * 3 + [jax.ShapeDtypeStruct((t, LANES), F32)],
        scratch_shapes=[pltpu.VMEM((1, LANES), F32)],
        compiler_params=_params(("arbitrary",)),
    )(proj, proj, proj, proj, q_gain, k_gain, f_bias)


def _flash_fwd(qn, kn, vb, ck, proj, heads, hd, tq, name):
    t = qn.shape[0]
    nq = t // tq
    th = tq // 2
    scale2 = hd ** -0.5 * LOG2E
    zcol = 3 * heads
    rc = _pick(tq, SOFTMAX_ROWS, 16)
    reps = th // LANES

    def body(q_ref, k_ref, v_ref, ck_ref, z_ref, o_ref, y_ref, lse_ref,
             m_s, l_s, acc_s, s_a, s_b, p_a, p_b, al_a, al_b):
        i = pl.program_id(1)
        bufs = ((s_a, p_a, al_a), (s_b, p_b, al_b))
        m_s[...] = jnp.full_like(m_s, -jnp.inf)
        l_s[...] = jnp.zeros_like(l_s)
        acc_s[...] = jnp.zeros_like(acc_s)
        p_b[...] = jnp.zeros_like(p_b)
        al_b[...] = jnp.ones_like(al_b)

        def scores(j, half):
            off = pl.multiple_of(j * tq + half * th, th)
            bufs[half][0][...] = _dot_nt(q_ref[...], k_ref[pl.ds(off, th), :])

        def values(j, half):
            off = pl.multiple_of(j * tq + half * th, th)
            _, p_buf, al = bufs[half]
            acc_s[...] = (jnp.tile(al[...], (1, hd // LANES)) * acc_s[...]
                          + _dot(p_buf[...], v_ref[pl.ds(off, th), :]))

        def softmax(j, half, masked):
            s_buf, p_buf, al = bufs[half]
            ck2 = ck_ref[0, j][:, half * th:(half + 1) * th] * LOG2E
            for c in range(tq // rc):
                r = slice(c * rc, (c + 1) * rc)
                s = s_buf[r, :] * scale2 - ck2
                if masked:
                    rows = lax.broadcasted_iota(jnp.int32, (rc, th), 0) + c * rc
                    cols = lax.broadcasted_iota(jnp.int32, (rc, th), 1) + half * th
                    s = jnp.where(rows >= cols, s, -jnp.inf)
                s_buf[r, :] = s
                m_prev = m_s[r, :]
                m_new = jnp.maximum(m_prev, jnp.max(s, axis=1, keepdims=True))
                al[r, :] = jnp.exp2(m_prev - m_new)
                m_s[r, :] = m_new
            for c in range(tq // rc):
                r = slice(c * rc, (c + 1) * rc)
                p = jnp.exp2(s_buf[r, :] - jnp.tile(m_s[r, :], (1, reps)))
                p_buf[r, :] = p.astype(BF16)
                lane_sum = p[:, 0:LANES]
                for b in range(1, reps):
                    lane_sum = lane_sum + p[:, b * LANES:(b + 1) * LANES]
                l_s[r, :] = al[r, :] * l_s[r, :] + lane_sum

        scores(0, 0)

        def loop_body(j, carry):
            scores(j, 1)
            values(jnp.maximum(j - 1, 0), 1)
            softmax(j, 0, False)
            scores(j + 1, 0)
            values(j, 0)
            softmax(j, 1, False)
            return carry

        lax.fori_loop(0, i, loop_body, 0)
        scores(i, 1)
        values(jnp.maximum(i - 1, 0), 1)
        softmax(i, 0, True)
        values(i, 0)
        softmax(i, 1, True)
        values(i, 1)
        l = jnp.sum(l_s[...], axis=1, keepdims=True)
        o = acc_s[...] / l
        z = z_ref[...]
        o_ref[...] = o
        y_ref[...] = (o * (z * _sigmoid(z))).astype(BF16)
        lse_ref[0] = m_s[:, 0:1] + jnp.log(l) * LOG2E

    blk = pl.BlockSpec((tq, hd), lambda h, i: (i, h))
    head = pl.BlockSpec((t, hd), lambda h, i: (0, h))
    col = pl.BlockSpec((1, tq, 1), lambda h, i: (h, i, 0))
    stat = pltpu.VMEM((tq, LANES), F32)
    return pl.pallas_call(
        body, name=name, grid=(heads, nq),
        in_specs=[blk, head, head,
                  pl.BlockSpec((1, nq, 1, tq), lambda h, i: (h, 0, 0, 0)),
                  pl.BlockSpec((tq, hd), lambda h, i: (i, zcol + h))],
        out_specs=[blk, blk, col],
        out_shape=[jax.ShapeDtypeStruct((t, heads * hd), F32),
                   jax.ShapeDtypeStruct((t, heads * hd), BF16),
                   jax.ShapeDtypeStruct((heads, t, 1), F32)],
        scratch_shapes=[stat, stat, pltpu.VMEM((tq, hd), F32),
                        pltpu.VMEM((tq, th), F32), pltpu.VMEM((tq, th), F32),
                        pltpu.VMEM((tq, th), BF16), pltpu.VMEM((tq, th), BF16), stat, stat],
        compiler_params=_params(("arbitrary", "arbitrary")),
    )(qn, kn, vb, ck, proj)


def _attn_bwd_prep(dy, proj, o, heads, hd, name):
    t, bw = dy.shape
    tr = _pick(t, 256, 16)

    def body(dy_ref, z_ref, o_ref, do_ref, dz_ref, delta_ref):
        dyv, z, ov = dy_ref[...], z_ref[...], o_ref[...]
        sz = _sigmoid(z)
        do = dyv * (z * sz)
        do_ref[...] = do.astype(BF16)
        dz_ref[...] = (dyv * ov * (sz * (1.0 + z * (1.0 - sz)))).astype(BF16)
        prod = do * ov
        for h in range(heads):
            delta_ref[h] = jnp.sum(prod[:, h * hd:(h + 1) * hd], axis=1, keepdims=True)

    row = pl.BlockSpec((tr, bw), lambda i: (i, 0))
    return pl.pallas_call(
        body, name=name, grid=(t // tr,),
        in_specs=[row, pl.BlockSpec((tr, bw), lambda i: (i, 3)), row],
        out_specs=[row, row, pl.BlockSpec((heads, tr, 1), lambda i: (0, i, 0))],
        out_shape=[jax.ShapeDtypeStruct((t, bw), BF16), jax.ShapeDtypeStruct((t, bw), BF16),
                   jax.ShapeDtypeStruct((heads, t, 1), F32)],
        compiler_params=_params(("arbitrary",)),
    )(dy, proj, o)


def _flash_bwd(qn, kn, vb, do, ck, lse, delta, heads, hd, tq, name):
    t = qn.shape[0]
    nq = t // tq
    scale = hd ** -0.5
    scale2 = scale * LOG2E

    def body(q_ref, k_ref, v_ref, do_ref, ck_ref, lse_ref, delta_ref,
             dq_ref, dk_ref, dv_ref, dcq_ref, dck_ref, dq_s, dcq_s):
        i = pl.program_id(1)

        @pl.when(i == 0)
        def _():
            dk_ref[...] = jnp.zeros_like(dk_ref)
            dv_ref[...] = jnp.zeros_like(dv_ref)
            dck_ref[...] = jnp.zeros_like(dck_ref)

        q, dov = q_ref[...], do_ref[...]
        lsev, deltav = lse_ref[0], delta_ref[0]
        dq_s[...] = jnp.zeros_like(dq_s)
        dcq_s[...] = jnp.zeros_like(dcq_s)

        def step(j, masked):
            off = pl.multiple_of(j * tq, tq)
            kblk = k_ref[pl.ds(off, tq), :]
            s = _dot_nt(q, kblk) * scale2 - ck_ref[0, j] * LOG2E
            if masked:
                rows = lax.broadcasted_iota(jnp.int32, (tq, tq), 0)
                cols = lax.broadcasted_iota(jnp.int32, (tq, tq), 1)
                s = jnp.where(rows >= cols, s, -jnp.inf)
            p = jnp.exp2(s - lsev)
            dp = _dot_nt(dov, v_ref[pl.ds(off, tq), :])
            ds = p * (dp - deltav)
            dsb = ds.astype(BF16)
            dv_ref[pl.ds(off, tq), :] += _dot_tn(p.astype(BF16), dov)
            dk_ref[pl.ds(off, tq), :] += _dot_tn(dsb, q) * scale
            dq_s[...] += _dot(dsb, kblk) * scale
            dcq_s[...] += jnp.sum(ds, axis=1, keepdims=True)
            dck_ref[0, j] += jnp.sum(ds, axis=0, keepdims=True)

        def loop_body(j, carry):
            step(j, False)
            return carry

        lax.fori_loop(0, i, loop_body, 0)
        step(i, True)
        dq_ref[...] = dq_s[...]
        dcq_ref[0] = dcq_s[...]

    blk = pl.BlockSpec((tq, hd), lambda h, i: (i, h))
    head = pl.BlockSpec((t, hd), lambda h, i: (0, h))
    col = pl.BlockSpec((1, tq, 1), lambda h, i: (h, i, 0))
    rowv = pl.BlockSpec((1, nq, 1, tq), lambda h, i: (h, 0, 0, 0))
    full = jax.ShapeDtypeStruct((t, heads * hd), F32)
    return pl.pallas_call(
        body, name=name, grid=(heads, nq),
        in_specs=[blk, head, head, blk, rowv, col, col],
        out_specs=[blk, head, head, col, rowv],
        out_shape=[full, full, full, jax.ShapeDtypeStruct((heads, t, 1), F32),
                   jax.ShapeDtypeStruct((heads, nq, 1, tq), F32)],
        scratch_shapes=[pltpu.VMEM((tq, hd), F32), pltpu.VMEM((tq, 1), F32)],
        compiler_params=_params(("arbitrary", "arbitrary")),
    )(qn, kn, vb, do, ck, lse, delta)


def _attn_bwd_post(dqn, dkn, dv, dz, proj, dcq, dck, q_gain, k_gain, f_bias, heads, hd, name):
    t, bw = dqn.shape
    tr = _pick(t, 128, 16)
    nb = t // tr
    fcol = 4 * bw // LANES
    width = 4 * bw + LANES

    def body(dq_ref, dk_ref, dv_ref, dz_ref, q_ref, k_ref, f_ref, dcq_ref, dck_ref, gq_ref, gk_ref,
             fb_ref, d_ref, dgq_ref, dgk_ref, dfb_ref, carry_ref, rc_ref):
        i = pl.program_id(0)

        @pl.when(i == 0)
        def _():
            carry_ref[...] = jnp.zeros_like(carry_ref)
            dgq_ref[...] = jnp.zeros_like(dgq_ref)
            dgk_ref[...] = jnp.zeros_like(dgk_ref)
            dfb_ref[...] = jnp.zeros_like(dfb_ref)

        for idx, (g_ref, raw_ref, gain_ref, dgain_ref) in enumerate(
                ((dq_ref, q_ref, gq_ref, dgq_ref), (dk_ref, k_ref, gk_ref, dgk_ref))):
            gain = gain_ref[...]
            dgain = jnp.zeros((1, hd), F32)
            for h in range(heads):
                sl = slice(h * hd, (h + 1) * hd)
                v, dn = raw_ref[:, sl], g_ref[:, sl]
                r = lax.rsqrt(jnp.mean(v * v, axis=1, keepdims=True) + NORM_EPS)
                vh = v * r
                dgain = dgain + jnp.sum(dn * vh, axis=0, keepdims=True)
                dvh = dn * gain
                draw = r * (dvh - vh * jnp.mean(dvh * vh, axis=1, keepdims=True))
                d_ref[:, idx * bw + h * hd:idx * bw + (h + 1) * hd] = draw.astype(BF16)
            dgain_ref[...] += dgain
        d_ref[:, 2 * bw:3 * bw] = dv_ref[...].astype(BF16)
        d_ref[:, 3 * bw:4 * bw] = dz_ref[...]
        rows = lax.broadcasted_iota(jnp.int32, (tr, tr), 0)
        cols = lax.broadcasted_iota(jnp.int32, (tr, tr), 1)
        upper = jnp.where(cols >= rows, 1.0, 0.0).astype(BF16)
        hi, mid, lo = _split3(dcq_ref[...] - dck_ref[...])
        rc_ref[...] = (_dot(upper, hi) + _dot(upper, mid) + _dot(upper, lo)) + carry_ref[...]
        carry_ref[...] = rc_ref[0:1, :]
        df = rc_ref[...] * (1.0 / (1.0 + jnp.exp(f_ref[...] + fb_ref[...])))
        d_ref[:, 4 * bw:] = df.astype(BF16)
        dfb_ref[...] += jnp.sum(df, axis=0, keepdims=True)

    wide = lambda col: pl.BlockSpec((tr, bw), lambda i: (nb - 1 - i, col))
    lane = lambda col: pl.BlockSpec((tr, LANES), lambda i: (nb - 1 - i, col))
    vec = pl.BlockSpec((1, hd), lambda i: (0, 0))
    vecl = pl.BlockSpec((1, LANES), lambda i: (0, 0))
    return pl.pallas_call(
        body, name=name, grid=(nb,),
        in_specs=[wide(0), wide(0), wide(0), wide(0), wide(0), wide(1), lane(fcol), lane(0), lane(0),
                  vec, vec, vecl],
        out_specs=[pl.BlockSpec((tr, width), lambda i: (nb - 1 - i, 0)), vec, vec, vecl],
        out_shape=[jax.ShapeDtypeStruct((t, width), BF16), jax.ShapeDtypeStruct((1, hd), F32),
                   jax.ShapeDtypeStruct((1, hd), F32), jax.ShapeDtypeStruct((1, LANES), F32)],
        scratch_shapes=[pltpu.VMEM((1, LANES), F32), pltpu.VMEM((tr, LANES), F32)],
        compiler_params=_params(("arbitrary",)),
    )(dqn, dkn, dv, dz, proj, proj, proj, dcq, dck, q_gain, k_gain, f_bias)


def _adamw(w, m, v, parts, name):
    nl, r, c = w.shape
    itemsize = parts[0].dtype.itemsize
    unit = 32 // itemsize
    row_bytes = c * (7 * 4 + N_DEV * itemsize * nl)
    tr = _pick(r, max(unit, 12 * 1024 * 1024 // row_bytes), unit)
    nr = r // tr
    c1 = 1.0 / (1.0 - ADAM_B1 ** ADAM_STEP)
    c2 = 1.0 / (1.0 - ADAM_B2 ** ADAM_STEP)

    def body(*refs):
        w_ref, m_ref, v_ref = refs[:3]
        p_refs = refs[3:3 + nl]
        g_ref, d_ref, nm_ref, nv_ref = refs[3 + nl:]
        layer = pl.program_id(0)

        def partial(j):
            p = p_refs[0][j].astype(F32)
            for q in range(1, nl):
                p = jnp.where(layer == q, p_refs[q][j].astype(F32), p)
            return p

        g = partial(0)
        for j in range(1, N_DEV):
            g = g + partial(j)
        nm = ADAM_B1 * m_ref[0] + (1.0 - ADAM_B1) * g
        nv = ADAM_B2 * v_ref[0] + (1.0 - ADAM_B2) * (g * g)
        g_ref[0] = g
        nm_ref[0] = nm
        nv_ref[0] = nv
        d_ref[0] = -ADAM_LR * ((nm * c1) / (jnp.sqrt(nv * c2) + ADAM_EPS) + ADAM_WD * w_ref[0])

    def part_spec(q):
        rest = 0 if q > 0 else nr - 1
        return pl.BlockSpec((N_DEV, tr, c), lambda l, i: (0, jnp.where(l == q, i, rest), 0))

    row = pl.BlockSpec((1, tr, c), lambda l, i: (l, i, 0))
    return pl.pallas_call(
        body, name=name, grid=(nl, nr),
        in_specs=[row, row, row] + [part_spec(q) for q in range(nl)],
        out_specs=[row] * 4,
        out_shape=[jax.ShapeDtypeStruct((nl, r, c), F32)] * 4,
        compiler_params=_params(("arbitrary", "arbitrary")),
    )(w, m, v, *parts)


def _flat_rows(pieces):
    rows = []
    for p in pieces:
        f = p.reshape(-1)
        f = jnp.pad(f, (0, (-f.shape[0]) % LANES))
        rows.append(f.reshape(-1, LANES))
    out = jnp.concatenate(rows, axis=0)
    return jnp.pad(out, ((0, (-out.shape[0]) % 8), (0, 0)))


def _unflat_rows(flat, shapes):
    outs, r0 = [], 0
    lead = flat.shape[:-2]
    for s in shapes:
        size = 1
        for d in s:
            size *= d
        nr = -(-size // LANES)
        piece = flat[..., r0:r0 + nr, :].reshape(lead + (nr * LANES,))[..., :size]
        outs.append(piece.reshape(lead + tuple(s)))
        r0 += nr
    return outs


def kernel(x, a_norm_g, a_w_in, a_v_norm_g, a_w_s, a_b_s, a_w_out, b_norm_g, b_w_in, b_f_bias, b_q_norm_g, b_k_norm_g, b_w_out, loss_target, m_a_norm_g, m_a_w_in, m_a_v_norm_g, m_a_w_s, m_a_b_s, m_a_w_out, m_b_norm_g, m_b_w_in, m_b_f_bias, m_b_q_norm_g, m_b_k_norm_g, m_b_w_out, v_a_norm_g, v_a_w_in, v_a_v_norm_g, v_a_w_s, v_a_b_s, v_a_w_out, v_b_norm_g, v_b_w_in, v_b_f_bias, v_b_q_norm_g, v_b_k_norm_g, v_b_w_out):
    t, d = x.shape[1], x.shape[2]
    n_a, n_b = a_w_in.shape[0], b_w_in.shape[0]
    depth = n_a + n_b
    aw = a_w_out.shape[1] * N_DEV
    groups, chunk = a_w_s.shape[1], a_w_s.shape[2]
    heads, hd = b_f_bias.shape[1], b_q_norm_g.shape[1]
    bw = heads * hd
    b_cols = b_w_in.shape[2]
    tq = _pick(t, 512, LANES)
    nq = t // tq
    me = _dev_index((lax.axis_index("x"), lax.axis_index("y"), lax.axis_index("c")))

    pad_h = lambda v: jnp.pad(v, ((0, 0), (0, LANES - heads)))
    f_bias_p = pad_h(b_f_bias)
    b_s_col = a_b_s.reshape(n_a, groups, chunk, 1)
    w_s_t = a_w_s.transpose(0, 1, 3, 2)

    def weight_gather(i, extra=()):
        j = i // 2
        if i % 2 == 0:
            return _Gather([a_w_in[j].astype(BF16), a_w_out[j].astype(BF16), *extra],
                           [1, 0] + [1] * len(extra))
        return _Gather([b_w_in[j].astype(BF16), b_w_out[j].astype(BF16), *extra],
                       [None, 0] + [1] * len(extra))

    def whole_weights(i, got):
        w_in, w_out = got[0], got[1]
        if i % 2 == 1:
            w_in = w_in.transpose(1, 0, 2).reshape(d, 4 * bw + heads)
            w_in = jnp.pad(w_in, ((0, 0), (0, LANES - heads)))
        return w_in, w_out

    got = _run_exchange(weight_gather(0, extra=(b_norm_g,)), "gather_weights")
    weights = {0: whole_weights(0, got)}
    b_norm_full = got[2]

    xs = [x[0]]
    saved = []
    for i in range(depth):
        j = i // 2
        xi = xs[-1]
        w_in, w_out = weights[i]
        nxt = weight_gather(i + 1) if i + 1 < depth else None
        if i % 2 == 0:
            h = _rms_fwd(xi, a_norm_g[j:j + 1], f"a{j}_norm")
            res = _matmul(h, w_in, "nn", F32, f"a{j}_in", tk=d, exchange=nxt)
        else:
            h = _rms_fwd(xi, b_norm_full[j:j + 1], f"b{j}_norm")
            res = _matmul(h, w_in, "nn", F32, f"b{j}_in", tm=512, tn=1664, tk=d, exchange=nxt)
        if nxt is None:
            pre = res
        else:
            pre = res[0]
            weights[i + 1] = whole_weights(i + 1, res[1:])
        if i % 2 == 0:
            y = _mix_fwd(pre, a_v_norm_g[j:j + 1], a_w_s[j], b_s_col[j], f"a{j}_mix")
            xs.append(_matmul(y, w_out, "nn", F32, f"a{j}_out", residual=xi))
            saved.append((h, pre, y))
        else:
            qn, kn, vb, cum = _attn_prep(pre, b_q_norm_g[j:j + 1], b_k_norm_g[j:j + 1],
                                         f_bias_p[j:j + 1], heads, hd, f"b{j}_prep")
            ck = cum[:, :heads].T.reshape(heads, nq, 1, tq)
            o, y, lse = _flash_fwd(qn, kn, vb, ck, pre, heads, hd, tq, f"b{j}_attn")
            xs.append(_matmul(y, w_out, "nn", F32, f"b{j}_out", residual=xi))
            saved.append((h, pre, y, qn, kn, vb, ck, o, lse))

    sq, g, gb = _loss_grad(xs[-1], loss_target[0], "loss")
    loss = 0.5 * lax.psum(sq[0, 0], AXES) / d

    d_a_norm, d_a_vnorm, d_a_ws, d_a_bs = [None] * n_a, [None] * n_a, [None] * n_a, [None] * n_a
    d_b_norm, d_b_fb, d_b_gq, d_b_gk = [None] * n_b, [None] * n_b, [None] * n_b, [None] * n_b
    received = {}
    pending = None
    for i in reversed(range(depth)):
        j = i // 2
        xi = xs[i]
        w_in, w_out = weights[i]
        if i % 2 == 0:
            h, uvz, y = saved[i]
            dy = _matmul(gb, w_out, "nt", F32, f"a{j}_dy", tk=d)
            dw_out = _matmul(y, gb, "tn", BF16, f"a{j}_dwout")
            dpre, d_a_ws[j], dbs, d_a_vnorm[j] = _mix_bwd(
                uvz, dy, a_v_norm_g[j:j + 1], a_w_s[j], w_s_t[j], b_s_col[j], f"a{j}_mixbwd")
            d_a_bs[j] = dbs.reshape(groups, chunk)
            name, gain = f"a{j}", a_norm_g[j:j + 1]
        else:
            h, proj, y, qn, kn, vb, ck, o, lse = saved[i]
            dy = _matmul(gb, w_out, "nt", F32, f"b{j}_dy", tk=d)
            dw_out = _matmul(y, gb, "tn", BF16, f"b{j}_dwout")
            do, dz, delta = _attn_bwd_prep(dy, proj, o, heads, hd, f"b{j}_bwdprep")
            dqn, dkn, dv, dcq, dck = _flash_bwd(qn, kn, vb, do, ck, lse, delta, heads, hd, tq,
                                                f"b{j}_attnbwd")
            per_token = lambda v: pad_h(v.reshape(heads, t).T)
            dpre, d_b_gq[j], d_b_gk[j], dfb = _attn_bwd_post(
                dqn, dkn, dv, dz, proj, per_token(dcq), per_token(dck), b_q_norm_g[j:j + 1],
                b_k_norm_g[j:j + 1], f_bias_p[j:j + 1], heads, hd, f"b{j}_bwdpost")
            d_b_fb[j] = dfb[:, :heads]
            name, gain = f"b{j}", b_norm_full[j:j + 1]
        if pending is None:
            dh = _matmul(dpre, w_in, "nt", F32, name + "_dh")
        else:
            dh, *received[pending[0]] = _matmul(dpre, w_in, "nt", F32, name + "_dh",
                                                exchange=pending[1])
        dw_in = _matmul(h, dpre, "tn", BF16, name + "_dwin", tn=1024 if i % 2 == 0 else 1664)
        g, gb, dgain = _rms_bwd(xi, dh, g, gain, name + "_normbwd")
        if i % 2 == 0:
            d_a_norm[j] = dgain
            pending = (i, _AllToAll([dw_in, dw_out], [1, 0]))
        else:
            d_b_norm[j] = dgain
            blocks = dw_in[:, :4 * bw + heads].reshape(d, N_DEV, b_cols).transpose(1, 0, 2)
            pending = (i, _AllToAll([blocks, dw_out], [None, 0]))
    received[pending[0]] = _run_exchange(pending[1], "exchange_grads")
    grad_x = g[None]

    small = [jnp.concatenate(d_a_norm, 0), jnp.concatenate(d_a_vnorm, 0), jnp.stack(d_a_ws, 0),
             jnp.stack(d_a_bs, 0), jnp.concatenate(d_b_fb, 0), jnp.concatenate(d_b_gq, 0),
             jnp.concatenate(d_b_gk, 0)]
    small_w = [a_norm_g, a_v_norm_g, a_w_s, a_b_s, b_f_bias, b_q_norm_g, b_k_norm_g]
    small_m = [m_a_norm_g, m_a_v_norm_g, m_a_w_s, m_a_b_s, m_b_f_bias, m_b_q_norm_g, m_b_k_norm_g]
    small_v = [v_a_norm_g, v_a_v_norm_g, v_a_w_s, v_a_b_s, v_b_f_bias, v_b_q_norm_g, v_b_k_norm_g]
    small_flat = _flat_rows(small)
    n_small = small_flat.shape[0]
    (gathered_small,) = _run_exchange(_Gather(
        [jnp.concatenate([small_flat, _flat_rows([jnp.concatenate(d_b_norm, 0)])], axis=0)],
        [None]), "gather_small_grads")
    parts_small = gathered_small[:, :n_small]
    parts_b_norm = gathered_small[:, n_small:n_small + n_b * d // LANES].reshape(N_DEV, n_b, d)
    parts_b_norm = lax.dynamic_slice_in_dim(parts_b_norm, me * (d // N_DEV), d // N_DEV, axis=2)

    recv_a = [received[2 * j] for j in range(n_a)]
    recv_b = [received[2 * j + 1] for j in range(n_b)]
    u_a_in = _adamw(a_w_in, m_a_w_in, v_a_w_in, [r[0] for r in recv_a], "adamw_a_w_in")
    u_a_out = _adamw(a_w_out, m_a_w_out, v_a_w_out, [r[1] for r in recv_a], "adamw_a_w_out")
    u_b_in = _adamw(b_w_in, m_b_w_in, v_b_w_in, [r[0] for r in recv_b], "adamw_b_w_in")
    u_b_out = _adamw(b_w_out, m_b_w_out, v_b_w_out, [r[1] for r in recv_b], "adamw_b_w_out")
    u_b_norm = [o_[0] for o_ in _adamw(b_norm_g[None], m_b_norm_g[None], v_b_norm_g[None],
                                       [parts_b_norm], "adamw_b_norm")]
    u_small = _adamw(_flat_rows(small_w)[None], _flat_rows(small_m)[None],
                     _flat_rows(small_v)[None], [parts_small], "adamw_small")
    shapes = [w.shape for w in small_w]
    u_small = [_unflat_rows(o_[0], shapes) for o_ in u_small]

    def per_kind(k):
        s = u_small[k]
        return [s[0], u_a_in[k], s[1], s[2], s[3], u_a_out[k], u_b_norm[k], u_b_in[k], s[4], s[5],
                s[6], u_b_out[k]]

    return (loss, grad_x, *per_kind(0), *per_kind(1), *per_kind(2), *per_kind(3))
```

```python
import functools

import jax
import jax.numpy as jnp
from jax import lax
from jax.experimental import pallas as pl
from jax.experimental.pallas import tpu as pltpu

F32 = jnp.float32
BF16 = jnp.bfloat16
MESH = pl.DeviceIdType.MESH
AXES = ("x", "y", "c")
N_DEV = 8
NORM_EPS = 1e-6
LANES = 128
VMEM_LIMIT = 56 * 1024 * 1024

ADAM_LR = 0.001
ADAM_B1 = 0.9
ADAM_B2 = 0.999
ADAM_EPS = 1e-08
ADAM_WD = 0.01
ADAM_STEP = 10

LOG2E = 1.4426950408889634
SOFTMAX_ROWS = 64
UNDERFLOW_LOG2 = -160.0

GELU_C0 = 0.7978845608028654
GELU_C1 = 0.044715

NT_DIMS = (((1,), (1,)), ((), ()))
TN_DIMS = (((0,), (0,)), ((), ()))


def _params(sem=None):
    return pltpu.CompilerParams(dimension_semantics=sem, vmem_limit_bytes=VMEM_LIMIT)


def _pick(n, target, unit):
    best = None
    for t in range(unit, min(n, target) + 1, unit):
        if n % t == 0:
            best = t
    return n if best is None else best


def _sigmoid(x):
    return 1.0 / (1.0 + jnp.exp(-x))


def _gelu(x):
    return 0.5 * x * (1.0 + jnp.tanh(GELU_C0 * x * (1.0 + GELU_C1 * x * x)))


def _gelu_and_grad(x):
    x2 = x * x
    t = jnp.tanh(GELU_C0 * x * (1.0 + GELU_C1 * x2))
    g = 0.5 * x * (1.0 + t)
    dg = 0.5 * (1.0 + t) + 0.5 * x * (1.0 - t * t) * (GELU_C0 * (1.0 + 3.0 * GELU_C1 * x2))
    return g, dg


def _dot(a, b):
    return jnp.dot(a, b, preferred_element_type=F32)


def _dot_nt(a, b):
    return lax.dot_general(a, b, NT_DIMS, preferred_element_type=F32)


def _dot_tn(a, b):
    return lax.dot_general(a, b, TN_DIMS, preferred_element_type=F32)


def _split3(v):
    hi = v.astype(BF16)
    r1 = v - hi.astype(F32)
    mid = r1.astype(BF16)
    lo = (r1 - mid.astype(F32)).astype(BF16)
    return hi, mid, lo


def _dev_index(p):
    return 4 * p[0] + 2 * p[1] + p[2]


def _block(ref, idx, axis, size):
    if axis is None:
        return ref.at[idx]
    start = pl.multiple_of(idx * size, size)
    return ref.at[(slice(None),) * axis + (pl.ds(start, size),)]


class _Exchange:
    def __init__(self, arrays, axes):
        self.arrays, self.axes, self.n = list(arrays), list(axes), len(arrays)

    def scratch(self):
        return [pltpu.SemaphoreType.DMA((self.n, 7)), pltpu.SemaphoreType.DMA((self.n, 7)),
                pltpu.SemaphoreType.DMA((self.n,))]

    @staticmethod
    def _place():
        x, y, c = lax.axis_index("x"), lax.axis_index("y"), lax.axis_index("c")
        return x, y, c


class _Gather(_Exchange):
    def out_shapes(self):
        outs = []
        for a, ax in zip(self.arrays, self.axes):
            if ax is None:
                shape = (N_DEV,) + a.shape
            else:
                shape = a.shape[:ax] + (N_DEV * a.shape[ax],) + a.shape[ax + 1:]
            outs.append(jax.ShapeDtypeStruct(shape, a.dtype))
        return outs

    def _copy(self, ins, outs, sems, a, k, block, to, src=None):
        ax = self.axes[a]
        dst = _block(outs[a], _dev_index(block), ax, None if ax is None else self.arrays[a].shape[ax])
        return pltpu.make_async_remote_copy(
            src_ref=dst if src is None else src, dst_ref=dst,
            send_sem=sems[0].at[a, k], recv_sem=sems[1].at[a, k],
            device_id=to, device_id_type=MESH)

    def _mine(self, ins, outs, sems, a, me):
        ax = self.axes[a]
        dst = _block(outs[a], _dev_index(me), ax, None if ax is None else self.arrays[a].shape[ax])
        return pltpu.make_async_copy(ins[a], dst, sems[2].at[a])

    def _first(self, ins, outs, sems):
        x, y, c = self._place()
        me, sibling = (x, y, c), (x, y, 1 - c)
        chips = [(1 - x, y), (x, 1 - y), (1 - x, 1 - y)]
        first = []
        for a in range(self.n):
            first.append(self._copy(ins, outs, sems, a, 0, me, sibling, src=ins[a]))
            first += [self._copy(ins, outs, sems, a, 1 + j, me, (*chip, c), src=ins[a])
                      for j, chip in enumerate(chips)]
        return first

    def start(self, ins, outs, sems):
        me = self._place()
        for a in range(self.n):
            self._mine(ins, outs, sems, a, me).start()
        for cp in self._first(ins, outs, sems):
            cp.start()

    def finish(self, ins, outs, sems):
        x, y, c = self._place()
        me, sibling = (x, y, c), (x, y, 1 - c)
        chips = [(1 - x, y), (x, 1 - y), (1 - x, 1 - y)]
        passed = []
        for a in range(self.n):
            for j, chip in enumerate(chips):
                self._copy(ins, outs, sems, a, 1 + j, (*chip, c), me).wait_recv()
                cp = self._copy(ins, outs, sems, a, 4 + j, (*chip, c), sibling)
                cp.start()
                passed.append(cp)
        for a in range(self.n):
            self._copy(ins, outs, sems, a, 0, sibling, me).wait_recv()
            for j, chip in enumerate(chips):
                self._copy(ins, outs, sems, a, 4 + j, (*chip, 1 - c), me).wait_recv()
        for cp in self._first(ins, outs, sems) + passed:
            cp.wait_send()
        for a in range(self.n):
            self._mine(ins, outs, sems, a, me).wait()


class _AllToAll(_Exchange):
    def _blk_shape(self, a):
        arr, ax = self.arrays[a], self.axes[a]
        if ax is None:
            return arr.shape[1:]
        return arr.shape[:ax] + (arr.shape[ax] // N_DEV,) + arr.shape[ax + 1:]

    def out_shapes(self):
        return [jax.ShapeDtypeStruct((N_DEV,) + self._blk_shape(a), self.arrays[a].dtype)
                for a in range(self.n)]

    def _src(self, ins, a, idx):
        ax = self.axes[a]
        return _block(ins[a], idx, ax, None if ax is None else self.arrays[a].shape[ax] // N_DEV)

    def _peers(self):
        x, y, c = self._place()
        return [((1 - x) if r & 4 else x, (1 - y) if r & 2 else y, (1 - c) if r & 1 else c)
                for r in range(1, N_DEV)]

    def _sends(self, ins, outs, sems):
        me = _dev_index(self._place())
        return [pltpu.make_async_remote_copy(
            src_ref=self._src(ins, a, _dev_index(peer)), dst_ref=outs[a].at[me],
            send_sem=sems[0].at[a, k], recv_sem=sems[1].at[a, k],
            device_id=peer, device_id_type=MESH)
            for a in range(self.n) for k, peer in enumerate(self._peers())]

    def _mine(self, ins, outs, sems):
        me = _dev_index(self._place())
        return [pltpu.make_async_copy(self._src(ins, a, me), outs[a].at[me], sems[2].at[a])
                for a in range(self.n)]

    def start(self, ins, outs, sems):
        for cp in self._mine(ins, outs, sems) + self._sends(ins, outs, sems):
            cp.start()

    def finish(self, ins, outs, sems):
        for a in range(self.n):
            for k, peer in enumerate(self._peers()):
                landed = outs[a].at[_dev_index(peer)]
                pltpu.make_async_remote_copy(
                    src_ref=landed, dst_ref=landed, send_sem=sems[0].at[a, k],
                    recv_sem=sems[1].at[a, k], device_id=peer, device_id_type=MESH).wait_recv()
        for cp in self._sends(ins, outs, sems):
            cp.wait_send()
        for cp in self._mine(ins, outs, sems):
            cp.wait()


def _run_exchange(exchange, name):
    n = exchange.n

    def body(*refs):
        ins, outs, sems = refs[:n], refs[n:2 * n], refs[2 * n:]
        exchange.start(ins, outs, sems)
        exchange.finish(ins, outs, sems)

    any_spec = pl.BlockSpec(memory_space=pl.ANY)
    return pl.pallas_call(
        body, name=name, out_shape=exchange.out_shapes(),
        in_specs=[any_spec] * n, out_specs=[any_spec] * n, scratch_shapes=exchange.scratch(),
    )(*exchange.arrays)


def _matmul(a, b, mode, out_dtype, name, tm=1024, tn=1024, tk=1024, residual=None, exchange=None):
    if mode == "tn":
        kdim, m = a.shape
    else:
        m, kdim = a.shape
    n = b.shape[0] if mode == "nt" else b.shape[1]
    tm, tn, tk = _pick(m, tm, LANES), _pick(n, tn, LANES), _pick(kdim, tk, LANES)
    nk = kdim // tk
    if mode == "tn":
        a_spec = pl.BlockSpec((tk, tm), lambda i, j, k: (k, i))
    else:
        a_spec = pl.BlockSpec((tm, tk), lambda i, j, k: (i, k))
    if mode == "nt":
        b_spec = pl.BlockSpec((tn, tk), lambda i, j, k: (j, k))
    else:
        b_spec = pl.BlockSpec((tk, tn), lambda i, j, k: (k, j))
    o_spec = pl.BlockSpec((tm, tn), lambda i, j, k: (i, j))
    dot = {"nn": _dot, "nt": _dot_nt, "tn": _dot_tn}[mode]
    has_res = residual is not None
    n_in = 3 if has_res else 2
    n_ex = 0 if exchange is None else exchange.n
    ni, nj = m // tm, n // tn

    def body(*refs):
        a_ref, b_ref = refs[:2]
        r_ref = refs[2] if has_res else None
        ex_ins = refs[n_in:n_in + n_ex]
        o_ref = refs[n_in + n_ex]
        ex_outs = refs[n_in + n_ex + 1:n_in + 2 * n_ex + 1]
        scratch = refs[n_in + 2 * n_ex + 1:]
        i, j, k = pl.program_id(0), pl.program_id(1), pl.program_id(2)

        if exchange is not None:
            sems = scratch[-3:]

            @pl.when((i == 0) & (j == 0) & (k == 0))
            def _():
                exchange.start(ex_ins, ex_outs, sems)

        def finish(acc):
            if has_res:
                acc = acc + r_ref[...]
            o_ref[...] = acc.astype(out_dtype)

        if nk == 1:
            finish(dot(a_ref[...], b_ref[...]))
        else:
            acc_ref = scratch[0]

            @pl.when(k == 0)
            def _():
                acc_ref[...] = jnp.zeros_like(acc_ref)

            acc_ref[...] += dot(a_ref[...], b_ref[...])

            @pl.when(k == nk - 1)
            def _():
                finish(acc_ref[...])

        if exchange is not None:
            @pl.when((i == ni - 1) & (j == nj - 1) & (k == nk - 1))
            def _():
                exchange.finish(ex_ins, ex_outs, sems)

    any_spec = pl.BlockSpec(memory_space=pl.ANY)
    operands = (a, b, residual) if has_res else (a, b)
    out_shape = jax.ShapeDtypeStruct((m, n), out_dtype)
    scratch_shapes = [] if nk == 1 else [pltpu.VMEM((tm, tn), F32)]
    if exchange is None:
        return pl.pallas_call(
            body, name=name, grid=(ni, nj, nk),
            in_specs=[a_spec, b_spec] + ([o_spec] if has_res else []),
            out_specs=o_spec, out_shape=out_shape, scratch_shapes=scratch_shapes,
            compiler_params=_params(("parallel", "parallel", "arbitrary")),
        )(*operands)
    return pl.pallas_call(
        body, name=name, grid=(ni, nj, nk),
        in_specs=[a_spec, b_spec] + ([o_spec] if has_res else []) + [any_spec] * n_ex,
        out_specs=[o_spec] + [any_spec] * n_ex,
        out_shape=[out_shape] + exchange.out_shapes(),
        scratch_shapes=scratch_shapes + exchange.scratch(),
        compiler_params=_params(("arbitrary", "arbitrary", "arbitrary")),
    )(*operands, *exchange.arrays)


def _rms_fwd(x, gain, name):
    t, d = x.shape
    tr = _pick(t, 256, 16)

    def body(x_ref, g_ref, h_ref):
        xv = x_ref[...]
        r = lax.rsqrt(jnp.mean(xv * xv, axis=1, keepdims=True) + NORM_EPS)
        h_ref[...] = (xv * r * g_ref[...]).astype(BF16)

    row = pl.BlockSpec((tr, d), lambda i: (i, 0))
    return pl.pallas_call(
        body, name=name, grid=(t // tr,),
        in_specs=[row, pl.BlockSpec((1, d), lambda i: (0, 0))],
        out_specs=row, out_shape=jax.ShapeDtypeStruct((t, d), BF16),
        compiler_params=_params(("arbitrary",)),
    )(x, gain)


def _rms_bwd(x, dh, g_in, gain, name):
    t, d = x.shape
    tr = _pick(t, 256, 16)

    def body(x_ref, dh_ref, gin_ref, g_ref, dx_ref, dxb_ref, dg_ref):
        i = pl.program_id(0)
        xv, dhv = x_ref[...], dh_ref[...]
        r = lax.rsqrt(jnp.mean(xv * xv, axis=1, keepdims=True) + NORM_EPS)
        xh = xv * r
        dxh = dhv * g_ref[...]
        dx = gin_ref[...] + r * (dxh - xh * jnp.mean(dxh * xh, axis=1, keepdims=True))
        dx_ref[...] = dx
        dxb_ref[...] = dx.astype(BF16)

        @pl.when(i == 0)
        def _():
            dg_ref[...] = jnp.zeros_like(dg_ref)

        dg_ref[...] += jnp.sum(dhv * xh, axis=0, keepdims=True)

    row = pl.BlockSpec((tr, d), lambda i: (i, 0))
    vec = pl.BlockSpec((1, d), lambda i: (0, 0))
    return pl.pallas_call(
        body, name=name, grid=(t // tr,),
        in_specs=[row, row, row, vec],
        out_specs=[row, row, vec],
        out_shape=[jax.ShapeDtypeStruct((t, d), F32), jax.ShapeDtypeStruct((t, d), BF16),
                   jax.ShapeDtypeStruct((1, d), F32)],
        compiler_params=_params(("arbitrary",)),
    )(x, dh, g_in, gain)


def _loss_grad(y, target, name):
    t, d = y.shape
    tr = _pick(t, 256, 16)

    def body(y_ref, t_ref, s_ref, g_ref, gb_ref):
        i = pl.program_id(0)
        e = y_ref[...] - t_ref[...]
        g = e * (1.0 / d)
        g_ref[...] = g
        gb_ref[...] = g.astype(BF16)

        @pl.when(i == 0)
        def _():
            s_ref[...] = jnp.zeros_like(s_ref)

        s_ref[...] += jnp.sum(jnp.sum(e * e, axis=1, keepdims=True), axis=0, keepdims=True)

    row = pl.BlockSpec((tr, d), lambda i: (i, 0))
    return pl.pallas_call(
        body, name=name, grid=(t // tr,),
        in_specs=[row, row],
        out_specs=[pl.BlockSpec((1, 1), lambda i: (0, 0)), row, row],
        out_shape=[jax.ShapeDtypeStruct((1, 1), F32), jax.ShapeDtypeStruct((t, d), F32),
                   jax.ShapeDtypeStruct((t, d), BF16)],
        compiler_params=_params(("arbitrary",)),
    )(y, target)


def _causal_weights(ws_ref, g, chunk, transposed):
    rows = lax.broadcasted_iota(jnp.int32, (chunk, chunk), 0)
    cols = lax.broadcasted_iota(jnp.int32, (chunk, chunk), 1)
    keep = (cols >= rows) if transposed else (rows >= cols)
    return jnp.where(keep, ws_ref[g], 0.0).astype(BF16)


def _mix_fwd(uvz, v_gain, w_s, b_s, name):
    t, w3 = uvz.shape
    w = w3 // 3
    groups, chunk = w_s.shape[0], w_s.shape[1]
    gd = w // groups

    def body(uvz_ref, gam_ref, ws_ref, bs_ref, y_ref):
        gv = _gelu(uvz_ref[:, w:2 * w])
        r = lax.rsqrt(jnp.mean(gv * gv, axis=1, keepdims=True) + NORM_EPS)
        vn = (gv * r * gam_ref[...]).astype(BF16)
        for g in range(groups):
            sl = slice(g * gd, (g + 1) * gd)
            mixed = _dot(_causal_weights(ws_ref, g, chunk, False), vn[:, sl]) + bs_ref[g]
            u = uvz_ref[:, g * gd:(g + 1) * gd]
            z = uvz_ref[:, 2 * w + g * gd:2 * w + (g + 1) * gd]
            y_ref[:, sl] = (_gelu(u) * mixed * (z * _sigmoid(z))).astype(BF16)

    return pl.pallas_call(
        body, name=name, grid=(t // chunk,),
        in_specs=[pl.BlockSpec((chunk, w3), lambda i: (i, 0)),
                  pl.BlockSpec((1, w), lambda i: (0, 0)),
                  pl.BlockSpec((groups, chunk, chunk), lambda i: (0, 0, 0)),
                  pl.BlockSpec((groups, chunk, 1), lambda i: (0, 0, 0))],
        out_specs=pl.BlockSpec((chunk, w), lambda i: (i, 0)),
        out_shape=jax.ShapeDtypeStruct((t, w), BF16),
        compiler_params=_params(("arbitrary",)),
    )(uvz, v_gain, w_s, b_s)


def _mix_bwd(uvz, dy, v_gain, w_s, w_s_t, b_s, name):
    t, w3 = uvz.shape
    w = w3 // 3
    groups, chunk = w_s.shape[0], w_s.shape[1]
    gd = w // groups

    def body(uvz_ref, dy_ref, gam_ref, ws_ref, wst_ref, bs_ref, d_ref, dws_ref, dbs_ref, dgam_ref,
             dvn_ref):
        i = pl.program_id(0)

        @pl.when(i == 0)
        def _():
            dws_ref[...] = jnp.zeros_like(dws_ref)
            dbs_ref[...] = jnp.zeros_like(dbs_ref)
            dgam_ref[...] = jnp.zeros_like(dgam_ref)

        gv, dgv = _gelu_and_grad(uvz_ref[:, w:2 * w])
        r = lax.rsqrt(jnp.mean(gv * gv, axis=1, keepdims=True) + NORM_EPS)
        vh = gv * r
        gam = gam_ref[...]
        vn = (vh * gam).astype(BF16)
        rows = lax.broadcasted_iota(jnp.int32, (chunk, chunk), 0)
        cols = lax.broadcasted_iota(jnp.int32, (chunk, chunk), 1)
        for g in range(groups):
            sl = slice(g * gd, (g + 1) * gd)
            mixed = _dot(_causal_weights(ws_ref, g, chunk, False), vn[:, sl]) + bs_ref[g]
            gu, dgu = _gelu_and_grad(uvz_ref[:, g * gd:(g + 1) * gd])
            z = uvz_ref[:, 2 * w + g * gd:2 * w + (g + 1) * gd]
            sz = _sigmoid(z)
            silu = z * sz
            dyv = dy_ref[:, sl]
            dmixed = dyv * gu * silu
            d_ref[:, sl] = (dyv * mixed * silu * dgu).astype(BF16)
            d_ref[:, 2 * w + g * gd:2 * w + (g + 1) * gd] = (
                dyv * gu * mixed * (sz * (1.0 + z * (1.0 - sz)))).astype(BF16)
            dmb = dmixed.astype(BF16)
            dws_ref[g] += jnp.where(rows >= cols, _dot_nt(dmb, vn[:, sl]), 0.0)
            dbs_ref[g] += jnp.sum(dmixed, axis=1, keepdims=True)
            dvn_ref[:, sl] = _dot(_causal_weights(wst_ref, g, chunk, True), dmb)
        dvn = dvn_ref[...]
        dgam_ref[...] += jnp.sum(dvn * vh, axis=0, keepdims=True)
        dvh = dvn * gam
        dgvv = r * (dvh - vh * jnp.mean(dvh * vh, axis=1, keepdims=True))
        d_ref[:, w:2 * w] = (dgvv * dgv).astype(BF16)

    return pl.pallas_call(
        body, name=name, grid=(t // chunk,),
        in_specs=[pl.BlockSpec((chunk, w3), lambda i: (i, 0)),
                  pl.BlockSpec((chunk, w), lambda i: (i, 0)),
                  pl.BlockSpec((1, w), lambda i: (0, 0)),
                  pl.BlockSpec((groups, chunk, chunk), lambda i: (0, 0, 0)),
                  pl.BlockSpec((groups, chunk, chunk), lambda i: (0, 0, 0)),
                  pl.BlockSpec((groups, chunk, 1), lambda i: (0, 0, 0))],
        out_specs=[pl.BlockSpec((chunk, w3), lambda i: (i, 0)),
                   pl.BlockSpec((groups, chunk, chunk), lambda i: (0, 0, 0)),
                   pl.BlockSpec((groups, chunk, 1), lambda i: (0, 0, 0)),
                   pl.BlockSpec((1, w), lambda i: (0, 0))],
        out_shape=[jax.ShapeDtypeStruct((t, w3), BF16),
                   jax.ShapeDtypeStruct((groups, chunk, chunk), F32),
                   jax.ShapeDtypeStruct((groups, chunk, 1), F32),
                   jax.ShapeDtypeStruct((1, w), F32)],
        scratch_shapes=[pltpu.VMEM((chunk, w), F32)],
        compiler_params=_params(("arbitrary",)),
    )(uvz, dy, v_gain, w_s, w_s_t, b_s)


def _attn_prep(proj, q_gain, k_gain, f_bias, heads, hd, name):
    t = proj.shape[0]
    bw = heads * hd
    tr = _pick(t, 256, 16)
    fcol = 4 * bw // LANES

    def body(q_ref, k_ref, v_ref, f_ref, gq_ref, gk_ref, fb_ref, qn_ref, kn_ref, vb_ref, cum_ref,
             qsq_ref, ksq_ref, first_ref, last_ref, carry_ref):
        i = pl.program_id(0)

        @pl.when(i == 0)
        def _():
            carry_ref[...] = jnp.zeros_like(carry_ref)

        lane = lax.broadcasted_iota(jnp.int32, (1, LANES), 1)
        for src, gain, dst, sq_ref in ((q_ref, gq_ref, qn_ref, qsq_ref),
                                       (k_ref, gk_ref, kn_ref, ksq_ref)):
            sq_row = jnp.zeros((1, LANES), F32)
            for h in range(heads):
                sl = slice(h * hd, (h + 1) * hd)
                v = src[:, sl]
                r = lax.rsqrt(jnp.mean(v * v, axis=1, keepdims=True) + NORM_EPS)
                normed = (v * r * gain[...]).astype(BF16)
                dst[:, sl] = normed
                nf = normed.astype(F32)
                sq = jnp.max(jnp.sum(nf * nf, axis=1, keepdims=True), axis=0, keepdims=True)
                sq_row = jnp.where(lane == h, sq, sq_row)
            sq_ref[0] = sq_row
        vb_ref[...] = v_ref[...].astype(BF16)
        fl = f_ref[...] + fb_ref[...]
        log_f = jnp.minimum(fl, 0.0) - jnp.log(1.0 + jnp.exp(-jnp.abs(fl)))
        rows = lax.broadcasted_iota(jnp.int32, (tr, tr), 0)
        cols = lax.broadcasted_iota(jnp.int32, (tr, tr), 1)
        lower = jnp.where(rows >= cols, 1.0, 0.0).astype(BF16)
        hi, mid, lo = _split3(log_f)
        cum_ref[...] = (_dot(lower, hi) + _dot(lower, mid) + _dot(lower, lo)) + carry_ref[...]
        carry_ref[...] = cum_ref[tr - 1:tr, :]
        first_ref[0] = cum_ref[0:1, :]
        last_ref[0] = cum_ref[tr - 1:tr, :]

    wide = lambda col: pl.BlockSpec((tr, bw), lambda i: (i, col))
    vec = pl.BlockSpec((1, hd), lambda i: (0, 0))
    stat = pl.BlockSpec((1, 1, LANES), lambda i: (i, 0, 0))
    return pl.pallas_call(
        body, name=name, grid=(t // tr,),
        in_specs=[wide(0), wide(1), wide(2), pl.BlockSpec((tr, LANES), lambda i: (i, fcol)),
                  vec, vec, pl.BlockSpec((1, LANES), lambda i: (0, 0))],
        out_specs=[wide(0), wide(0), wide(0), pl.BlockSpec((tr, LANES), lambda i: (i, 0))]
        + [stat] * 4,
        out_shape=[jax.ShapeDtypeStruct((t, bw), BF16)] * 3 + [jax.ShapeDtypeStruct((t, LANES), F32)]
        + [jax.ShapeDtypeStruct((t // tr, 1, LANES), F32)] * 4,
        scratch_shapes=[pltpu.VMEM((1, LANES), F32)],
        compiler_params=_params(("arbitrary",)),
    )(proj, proj, proj, proj, q_gain, k_gain, f_bias)


def _skip_plan(qsq, ksq, first, last, t, tq, hd, name):
    nt = qsq.shape[0]
    nq = t // tq
    r = nt // nq
    scale2 = hd ** -0.5 * LOG2E

    def body(qsq_ref, ksq_ref, first_ref, last_ref, out_ref):
        kmax = ksq_ref[0]
        for tile in range(1, nt):
            kmax = jnp.maximum(kmax, ksq_ref[tile])
        for i in range(nq):
            qmax = qsq_ref[i * r]
            for tile in range(i * r + 1, (i + 1) * r):
                qmax = jnp.maximum(qmax, qsq_ref[tile])
            coef = 2.0 * scale2 * jnp.sqrt(qmax * kmax)
            start = first_ref[i * r]
            count = jnp.zeros((1, LANES), jnp.int32)
            for j in range(i):
                bound = coef + (start - last_ref[(j + 1) * r - 1]) * LOG2E
                count = count + jnp.where(bound <= UNDERFLOW_LOG2, 1, 0)
            out_ref[i:i + 1, :] = count

    return pl.pallas_call(
        body, name=name, out_shape=jax.ShapeDtypeStruct((nq, LANES), jnp.int32),
    )(qsq, ksq, first, last)


def _flash_fwd(first_block, qn, kn, vb, ck, proj, heads, hd, tq, name):
    t = qn.shape[0]
    nq = t // tq
    th = tq // 2
    scale2 = hd ** -0.5 * LOG2E
    zcol = 3 * heads
    rc = _pick(tq, SOFTMAX_ROWS, 16)
    reps = th // LANES

    def body(first_ref, q_ref, k_ref, v_ref, ck_ref, z_ref, o_ref, y_ref, lse_ref,
             m_s, l_s, acc_s, s_a, s_b, p_a, p_b, al_a, al_b):
        i = pl.program_id(1)
        j0 = first_ref[pl.program_id(0), i]
        bufs = ((s_a, p_a, al_a), (s_b, p_b, al_b))
        m_s[...] = jnp.full_like(m_s, -jnp.inf)
        l_s[...] = jnp.zeros_like(l_s)
        acc_s[...] = jnp.zeros_like(acc_s)
        p_b[...] = jnp.zeros_like(p_b)
        al_b[...] = jnp.ones_like(al_b)

        def scores(j, half):
            off = pl.multiple_of(j * tq + half * th, th)
            bufs[half][0][...] = _dot_nt(q_ref[...], k_ref[pl.ds(off, th), :])

        def values(j, half):
            off = pl.multiple_of(j * tq + half * th, th)
            _, p_buf, al = bufs[half]
            acc_s[...] = (jnp.tile(al[...], (1, hd // LANES)) * acc_s[...]
                          + _dot(p_buf[...], v_ref[pl.ds(off, th), :]))

        def softmax(j, half, masked):
            s_buf, p_buf, al = bufs[half]
            ck2 = ck_ref[0, j][:, half * th:(half + 1) * th] * LOG2E
            for c in range(tq // rc):
                r = slice(c * rc, (c + 1) * rc)
                s = s_buf[r, :] * scale2 - ck2
                if masked:
                    rows = lax.broadcasted_iota(jnp.int32, (rc, th), 0) + c * rc
                    cols = lax.broadcasted_iota(jnp.int32, (rc, th), 1) + half * th
                    s = jnp.where(rows >= cols, s, -jnp.inf)
                s_buf[r, :] = s
                m_prev = m_s[r, :]
                m_new = jnp.maximum(m_prev, jnp.max(s, axis=1, keepdims=True))
                al[r, :] = jnp.exp2(m_prev - m_new)
                m_s[r, :] = m_new
            for c in range(tq // rc):
                r = slice(c * rc, (c + 1) * rc)
                p = jnp.exp2(s_buf[r, :] - jnp.tile(m_s[r, :], (1, reps)))
                p_buf[r, :] = p.astype(BF16)
                lane_sum = p[:, 0:LANES]
                for b in range(1, reps):
                    lane_sum = lane_sum + p[:, b * LANES:(b + 1) * LANES]
                l_s[r, :] = al[r, :] * l_s[r, :] + lane_sum

        scores(j0, 0)

        def loop_body(j, carry):
            scores(j, 1)
            values(jnp.maximum(j - 1, 0), 1)
            softmax(j, 0, False)
            scores(j + 1, 0)
            values(j, 0)
            softmax(j, 1, False)
            return carry

        lax.fori_loop(j0, i, loop_body, 0)
        scores(i, 1)
        values(jnp.maximum(i - 1, 0), 1)
        softmax(i, 0, True)
        values(i, 0)
        softmax(i, 1, True)
        values(i, 1)
        l = jnp.sum(l_s[...], axis=1, keepdims=True)
        o = acc_s[...] / l
        z = z_ref[...]
        o_ref[...] = o
        y_ref[...] = (o * (z * _sigmoid(z))).astype(BF16)
        lse_ref[0] = m_s[:, 0:1] + jnp.log(l) * LOG2E

    blk = pl.BlockSpec((tq, hd), lambda h, i, first: (i, h))
    head = pl.BlockSpec((t, hd), lambda h, i, first: (0, h))
    col = pl.BlockSpec((1, tq, 1), lambda h, i, first: (h, i, 0))
    stat = pltpu.VMEM((tq, LANES), F32)
    return pl.pallas_call(
        body, name=name,
        grid_spec=pltpu.PrefetchScalarGridSpec(
            num_scalar_prefetch=1, grid=(heads, nq),
            in_specs=[blk, head, head,
                      pl.BlockSpec((1, nq, 1, tq), lambda h, i, first: (h, 0, 0, 0)),
                      pl.BlockSpec((tq, hd), lambda h, i, first: (i, zcol + h))],
            out_specs=[blk, blk, col],
            scratch_shapes=[stat, stat, pltpu.VMEM((tq, hd), F32),
                            pltpu.VMEM((tq, th), F32), pltpu.VMEM((tq, th), F32),
                            pltpu.VMEM((tq, th), BF16), pltpu.VMEM((tq, th), BF16), stat, stat]),
        out_shape=[jax.ShapeDtypeStruct((t, heads * hd), F32),
                   jax.ShapeDtypeStruct((t, heads * hd), BF16),
                   jax.ShapeDtypeStruct((heads, t, 1), F32)],
        compiler_params=_params(("arbitrary", "arbitrary")),
    )(first_block, qn, kn, vb, ck, proj)


def _attn_bwd_prep(dy, proj, o, heads, hd, name):
    t, bw = dy.shape
    tr = _pick(t, 256, 16)

    def body(dy_ref, z_ref, o_ref, do_ref, dz_ref, delta_ref):
        dyv, z, ov = dy_ref[...], z_ref[...], o_ref[...]
        sz = _sigmoid(z)
        do = dyv * (z * sz)
        do_ref[...] = do.astype(BF16)
        dz_ref[...] = (dyv * ov * (sz * (1.0 + z * (1.0 - sz)))).astype(BF16)
        prod = do * ov
        for h in range(heads):
            delta_ref[h] = jnp.sum(prod[:, h * hd:(h + 1) * hd], axis=1, keepdims=True)

    row = pl.BlockSpec((tr, bw), lambda i: (i, 0))
    return pl.pallas_call(
        body, name=name, grid=(t // tr,),
        in_specs=[row, pl.BlockSpec((tr, bw), lambda i: (i, 3)), row],
        out_specs=[row, row, pl.BlockSpec((heads, tr, 1), lambda i: (0, i, 0))],
        out_shape=[jax.ShapeDtypeStruct((t, bw), BF16), jax.ShapeDtypeStruct((t, bw), BF16),
                   jax.ShapeDtypeStruct((heads, t, 1), F32)],
        compiler_params=_params(("arbitrary",)),
    )(dy, proj, o)


def _flash_bwd(first_block, qn, kn, vb, do, ck, lse, delta, heads, hd, tq, name):
    t = qn.shape[0]
    nq = t // tq
    scale = hd ** -0.5
    scale2 = scale * LOG2E

    def body(first_ref, q_ref, k_ref, v_ref, do_ref, ck_ref, lse_ref, delta_ref,
             dq_ref, dk_ref, dv_ref, dcq_ref, dck_ref, dq_s, dcq_s):
        i = pl.program_id(1)
        j0 = first_ref[pl.program_id(0), i]

        @pl.when(i == 0)
        def _():
            dk_ref[...] = jnp.zeros_like(dk_ref)
            dv_ref[...] = jnp.zeros_like(dv_ref)
            dck_ref[...] = jnp.zeros_like(dck_ref)

        q, dov = q_ref[...], do_ref[...]
        lsev, deltav = lse_ref[0], delta_ref[0]
        dq_s[...] = jnp.zeros_like(dq_s)
        dcq_s[...] = jnp.zeros_like(dcq_s)

        def step(j, masked):
            off = pl.multiple_of(j * tq, tq)
            kblk = k_ref[pl.ds(off, tq), :]
            s = _dot_nt(q, kblk) * scale2 - ck_ref[0, j] * LOG2E
            if masked:
                rows = lax.broadcasted_iota(jnp.int32, (tq, tq), 0)
                cols = lax.broadcasted_iota(jnp.int32, (tq, tq), 1)
                s = jnp.where(rows >= cols, s, -jnp.inf)
            p = jnp.exp2(s - lsev)
            dp = _dot_nt(dov, v_ref[pl.ds(off, tq), :])
            ds = p * (dp - deltav)
            dsb = ds.astype(BF16)
            dv_ref[pl.ds(off, tq), :] += _dot_tn(p.astype(BF16), dov)
            dk_ref[pl.ds(off, tq), :] += _dot_tn(dsb, q) * scale
            dq_s[...] += _dot(dsb, kblk) * scale
            dcq_s[...] += jnp.sum(ds, axis=1, keepdims=True)
            dck_ref[0, j] += jnp.sum(ds, axis=0, keepdims=True)

        def loop_body(j, carry):
            step(j, False)
            return carry

        lax.fori_loop(j0, i, loop_body, 0)
        step(i, True)
        dq_ref[...] = dq_s[...]
        dcq_ref[0] = dcq_s[...]

    blk = pl.BlockSpec((tq, hd), lambda h, i, first: (i, h))
    head = pl.BlockSpec((t, hd), lambda h, i, first: (0, h))
    col = pl.BlockSpec((1, tq, 1), lambda h, i, first: (h, i, 0))
    rowv = pl.BlockSpec((1, nq, 1, tq), lambda h, i, first: (h, 0, 0, 0))
    full = jax.ShapeDtypeStruct((t, heads * hd), F32)
    return pl.pallas_call(
        body, name=name,
        grid_spec=pltpu.PrefetchScalarGridSpec(
            num_scalar_prefetch=1, grid=(heads, nq),
            in_specs=[blk, head, head, blk, rowv, col, col],
            out_specs=[blk, head, head, col, rowv],
            scratch_shapes=[pltpu.VMEM((tq, hd), F32), pltpu.VMEM((tq, 1), F32)]),
        out_shape=[full, full, full, jax.ShapeDtypeStruct((heads, t, 1), F32),
                   jax.ShapeDtypeStruct((heads, nq, 1, tq), F32)],
        compiler_params=_params(("arbitrary", "arbitrary")),
    )(first_block, qn, kn, vb, do, ck, lse, delta)


def _attn_bwd_post(dqn, dkn, dv, dz, proj, dcq, dck, q_gain, k_gain, f_bias, heads, hd, name):
    t, bw = dqn.shape
    tr = _pick(t, 128, 16)
    nb = t // tr
    fcol = 4 * bw // LANES
    width = 4 * bw + LANES

    def body(dq_ref, dk_ref, dv_ref, dz_ref, q_ref, k_ref, f_ref, dcq_ref, dck_ref, gq_ref, gk_ref,
             fb_ref, d_ref, dgq_ref, dgk_ref, dfb_ref, carry_ref, rc_ref):
        i = pl.program_id(0)

        @pl.when(i == 0)
        def _():
            carry_ref[...] = jnp.zeros_like(carry_ref)
            dgq_ref[...] = jnp.zeros_like(dgq_ref)
            dgk_ref[...] = jnp.zeros_like(dgk_ref)
            dfb_ref[...] = jnp.zeros_like(dfb_ref)

        for idx, (g_ref, raw_ref, gain_ref, dgain_ref) in enumerate(
                ((dq_ref, q_ref, gq_ref, dgq_ref), (dk_ref, k_ref, gk_ref, dgk_ref))):
            gain = gain_ref[...]
            dgain = jnp.zeros((1, hd), F32)
            for h in range(heads):
                sl = slice(h * hd, (h + 1) * hd)
                v, dn = raw_ref[:, sl], g_ref[:, sl]
                r = lax.rsqrt(jnp.mean(v * v, axis=1, keepdims=True) + NORM_EPS)
                vh = v * r
                dgain = dgain + jnp.sum(dn * vh, axis=0, keepdims=True)
                dvh = dn * gain
                draw = r * (dvh - vh * jnp.mean(dvh * vh, axis=1, keepdims=True))
                d_ref[:, idx * bw + h * hd:idx * bw + (h + 1) * hd] = draw.astype(BF16)
            dgain_ref[...] += dgain
        d_ref[:, 2 * bw:3 * bw] = dv_ref[...].astype(BF16)
        d_ref[:, 3 * bw:4 * bw] = dz_ref[...]
        rows = lax.broadcasted_iota(jnp.int32, (tr, tr), 0)
        cols = lax.broadcasted_iota(jnp.int32, (tr, tr), 1)
        upper = jnp.where(cols >= rows, 1.0, 0.0).astype(BF16)
        hi, mid, lo = _split3(dcq_ref[...] - dck_ref[...])
        rc_ref[...] = (_dot(upper, hi) + _dot(upper, mid) + _dot(upper, lo)) + carry_ref[...]
        carry_ref[...] = rc_ref[0:1, :]
        df = rc_ref[...] * (1.0 / (1.0 + jnp.exp(f_ref[...] + fb_ref[...])))
        d_ref[:, 4 * bw:] = df.astype(BF16)
        dfb_ref[...] += jnp.sum(df, axis=0, keepdims=True)

    wide = lambda col: pl.BlockSpec((tr, bw), lambda i: (nb - 1 - i, col))
    lane = lambda col: pl.BlockSpec((tr, LANES), lambda i: (nb - 1 - i, col))
    vec = pl.BlockSpec((1, hd), lambda i: (0, 0))
    vecl = pl.BlockSpec((1, LANES), lambda i: (0, 0))
    return pl.pallas_call(
        body, name=name, grid=(nb,),
        in_specs=[wide(0), wide(0), wide(0), wide(0), wide(0), wide(1), lane(fcol), lane(0), lane(0),
                  vec, vec, vecl],
        out_specs=[pl.BlockSpec((tr, width), lambda i: (nb - 1 - i, 0)), vec, vec, vecl],
        out_shape=[jax.ShapeDtypeStruct((t, width), BF16), jax.ShapeDtypeStruct((1, hd), F32),
                   jax.ShapeDtypeStruct((1, hd), F32), jax.ShapeDtypeStruct((1, LANES), F32)],
        scratch_shapes=[pltpu.VMEM((1, LANES), F32), pltpu.VMEM((tr, LANES), F32)],
        compiler_params=_params(("arbitrary",)),
    )(dqn, dkn, dv, dz, proj, proj, proj, dcq, dck, q_gain, k_gain, f_bias)


def _adamw(w, m, v, parts, name):
    nl, r, c = w.shape
    itemsize = parts[0].dtype.itemsize
    unit = 32 // itemsize
    row_bytes = c * (7 * 4 + N_DEV * itemsize * nl)
    tr = _pick(r, max(unit, 12 * 1024 * 1024 // row_bytes), unit)
    nr = r // tr
    c1 = 1.0 / (1.0 - ADAM_B1 ** ADAM_STEP)
    c2 = 1.0 / (1.0 - ADAM_B2 ** ADAM_STEP)

    def body(*refs):
        w_ref, m_ref, v_ref = refs[:3]
        p_refs = refs[3:3 + nl]
        g_ref, d_ref, nm_ref, nv_ref = refs[3 + nl:]
        layer = pl.program_id(0)

        def partial(j):
            p = p_refs[0][j].astype(F32)
            for q in range(1, nl):
                p = jnp.where(layer == q, p_refs[q][j].astype(F32), p)
            return p

        g = partial(0)
        for j in range(1, N_DEV):
            g = g + partial(j)
        nm = ADAM_B1 * m_ref[0] + (1.0 - ADAM_B1) * g
        nv = ADAM_B2 * v_ref[0] + (1.0 - ADAM_B2) * (g * g)
        g_ref[0] = g
        nm_ref[0] = nm
        nv_ref[0] = nv
        d_ref[0] = -ADAM_LR * ((nm * c1) / (jnp.sqrt(nv * c2) + ADAM_EPS) + ADAM_WD * w_ref[0])

    def part_spec(q):
        rest = 0 if q > 0 else nr - 1
        return pl.BlockSpec((N_DEV, tr, c), lambda l, i: (0, jnp.where(l == q, i, rest), 0))

    row = pl.BlockSpec((1, tr, c), lambda l, i: (l, i, 0))
    return pl.pallas_call(
        body, name=name, grid=(nl, nr),
        in_specs=[row, row, row] + [part_spec(q) for q in range(nl)],
        out_specs=[row] * 4,
        out_shape=[jax.ShapeDtypeStruct((nl, r, c), F32)] * 4,
        compiler_params=_params(("arbitrary", "arbitrary")),
    )(w, m, v, *parts)


def _flat_rows(pieces):
    rows = []
    for p in pieces:
        f = p.reshape(-1)
        f = jnp.pad(f, (0, (-f.shape[0]) % LANES))
        rows.append(f.reshape(-1, LANES))
    out = jnp.concatenate(rows, axis=0)
    return jnp.pad(out, ((0, (-out.shape[0]) % 8), (0, 0)))


def _unflat_rows(flat, shapes):
    outs, r0 = [], 0
    lead = flat.shape[:-2]
    for s in shapes:
        size = 1
        for d in s:
            size *= d
        nr = -(-size // LANES)
        piece = flat[..., r0:r0 + nr, :].reshape(lead + (nr * LANES,))[..., :size]
        outs.append(piece.reshape(lead + tuple(s)))
        r0 += nr
    return outs


def kernel(x, a_norm_g, a_w_in, a_v_norm_g, a_w_s, a_b_s, a_w_out, b_norm_g, b_w_in, b_f_bias, b_q_norm_g, b_k_norm_g, b_w_out, loss_target, m_a_norm_g, m_a_w_in, m_a_v_norm_g, m_a_w_s, m_a_b_s, m_a_w_out, m_b_norm_g, m_b_w_in, m_b_f_bias, m_b_q_norm_g, m_b_k_norm_g, m_b_w_out, v_a_norm_g, v_a_w_in, v_a_v_norm_g, v_a_w_s, v_a_b_s, v_a_w_out, v_b_norm_g, v_b_w_in, v_b_f_bias, v_b_q_norm_g, v_b_k_norm_g, v_b_w_out):
    t, d = x.shape[1], x.shape[2]
    n_a, n_b = a_w_in.shape[0], b_w_in.shape[0]
    depth = n_a + n_b
    aw = a_w_out.shape[1] * N_DEV
    groups, chunk = a_w_s.shape[1], a_w_s.shape[2]
    heads, hd = b_f_bias.shape[1], b_q_norm_g.shape[1]
    bw = heads * hd
    b_cols = b_w_in.shape[2]
    tq = _pick(t, 512, LANES)
    nq = t // tq
    me = _dev_index((lax.axis_index("x"), lax.axis_index("y"), lax.axis_index("c")))

    pad_h = lambda v: jnp.pad(v, ((0, 0), (0, LANES - heads)))
    f_bias_p = pad_h(b_f_bias)
    b_s_col = a_b_s.reshape(n_a, groups, chunk, 1)
    w_s_t = a_w_s.transpose(0, 1, 3, 2)

    def weight_gather(i, extra=()):
        j = i // 2
        if i % 2 == 0:
            return _Gather([a_w_in[j].astype(BF16), a_w_out[j].astype(BF16), *extra],
                           [1, 0] + [1] * len(extra))
        return _Gather([b_w_in[j].astype(BF16), b_w_out[j].astype(BF16), *extra],
                       [None, 0] + [1] * len(extra))

    def whole_weights(i, got):
        w_in, w_out = got[0], got[1]
        if i % 2 == 1:
            w_in = w_in.transpose(1, 0, 2).reshape(d, 4 * bw + heads)
            w_in = jnp.pad(w_in, ((0, 0), (0, LANES - heads)))
        return w_in, w_out

    got = _run_exchange(weight_gather(0, extra=(b_norm_g,)), "gather_weights")
    weights = {0: whole_weights(0, got)}
    b_norm_full = got[2]

    xs = [x[0]]
    saved = []
    for i in range(depth):
        j = i // 2
        xi = xs[-1]
        w_in, w_out = weights[i]
        nxt = weight_gather(i + 1) if i + 1 < depth else None
        if i % 2 == 0:
            h = _rms_fwd(xi, a_norm_g[j:j + 1], f"a{j}_norm")
            res = _matmul(h, w_in, "nn", F32, f"a{j}_in", tk=d, exchange=nxt)
        else:
            h = _rms_fwd(xi, b_norm_full[j:j + 1], f"b{j}_norm")
            res = _matmul(h, w_in, "nn", F32, f"b{j}_in", tm=512, tn=1664, tk=d, exchange=nxt)
        if nxt is None:
            pre = res
        else:
            pre = res[0]
            weights[i + 1] = whole_weights(i + 1, res[1:])
        if i % 2 == 0:
            y = _mix_fwd(pre, a_v_norm_g[j:j + 1], a_w_s[j], b_s_col[j], f"a{j}_mix")
            xs.append(_matmul(y, w_out, "nn", F32, f"a{j}_out", residual=xi))
            saved.append((h, pre, y))
        else:
            qn, kn, vb, cum, qsq, ksq, cum_first, cum_last = _attn_prep(
                pre, b_q_norm_g[j:j + 1], b_k_norm_g[j:j + 1], f_bias_p[j:j + 1], heads, hd,
                f"b{j}_prep")
            first = _skip_plan(qsq, ksq, cum_first, cum_last, t, tq, hd, f"b{j}_plan")
            first = first[:, :heads].T
            ck = cum[:, :heads].T.reshape(heads, nq, 1, tq)
            o, y, lse = _flash_fwd(first, qn, kn, vb, ck, pre, heads, hd, tq, f"b{j}_attn")
            xs.append(_matmul(y, w_out, "nn", F32, f"b{j}_out", residual=xi))
            saved.append((h, pre, y, qn, kn, vb, ck, o, lse, first))

    sq, g, gb = _loss_grad(xs[-1], loss_target[0], "loss")
    loss = 0.5 * lax.psum(sq[0, 0], AXES) / d

    d_a_norm, d_a_vnorm, d_a_ws, d_a_bs = [None] * n_a, [None] * n_a, [None] * n_a, [None] * n_a
    d_b_norm, d_b_fb, d_b_gq, d_b_gk = [None] * n_b, [None] * n_b, [None] * n_b, [None] * n_b
    received = {}
    pending = None
    for i in reversed(range(depth)):
        j = i // 2
        xi = xs[i]
        w_in, w_out = weights[i]
        if i % 2 == 0:
            h, uvz, y = saved[i]
            dy = _matmul(gb, w_out, "nt", F32, f"a{j}_dy", tk=d)
            dw_out = _matmul(y, gb, "tn", BF16, f"a{j}_dwout")
            dpre, d_a_ws[j], dbs, d_a_vnorm[j] = _mix_bwd(
                uvz, dy, a_v_norm_g[j:j + 1], a_w_s[j], w_s_t[j], b_s_col[j], f"a{j}_mixbwd")
            d_a_bs[j] = dbs.reshape(groups, chunk)
            name, gain = f"a{j}", a_norm_g[j:j + 1]
        else:
            h, proj, y, qn, kn, vb, ck, o, lse, first = saved[i]
            dy = _matmul(gb, w_out, "nt", F32, f"b{j}_dy", tk=d)
            dw_out = _matmul(y, gb, "tn", BF16, f"b{j}_dwout")
            do, dz, delta = _attn_bwd_prep(dy, proj, o, heads, hd, f"b{j}_bwdprep")
            dqn, dkn, dv, dcq, dck = _flash_bwd(first, qn, kn, vb, do, ck, lse, delta, heads, hd,
                                                tq, f"b{j}_attnbwd")
            per_token = lambda v: pad_h(v.reshape(heads, t).T)
            dpre, d_b_gq[j], d_b_gk[j], dfb = _attn_bwd_post(
                dqn, dkn, dv, dz, proj, per_token(dcq), per_token(dck), b_q_norm_g[j:j + 1],
                b_k_norm_g[j:j + 1], f_bias_p[j:j + 1], heads, hd, f"b{j}_bwdpost")
            d_b_fb[j] = dfb[:, :heads]
            name, gain = f"b{j}", b_norm_full[j:j + 1]
        if pending is None:
            dh = _matmul(dpre, w_in, "nt", F32, name + "_dh")
        else:
            dh, *received[pending[0]] = _matmul(dpre, w_in, "nt", F32, name + "_dh",
                                                exchange=pending[1])
        dw_in = _matmul(h, dpre, "tn", BF16, name + "_dwin", tn=1024 if i % 2 == 0 else 1664)
        g, gb, dgain = _rms_bwd(xi, dh, g, gain, name + "_normbwd")
        if i % 2 == 0:
            d_a_norm[j] = dgain
            pending = (i, _AllToAll([dw_in, dw_out], [1, 0]))
        else:
            d_b_norm[j] = dgain
            blocks = dw_in[:, :4 * bw + heads].reshape(d, N_DEV, b_cols).transpose(1, 0, 2)
            pending = (i, _AllToAll([blocks, dw_out], [None, 0]))
    received[pending[0]] = _run_exchange(pending[1], "exchange_grads")
    grad_x = g[None]

    small = [jnp.concatenate(d_a_norm, 0), jnp.concatenate(d_a_vnorm, 0), jnp.stack(d_a_ws, 0),
             jnp.stack(d_a_bs, 0), jnp.concatenate(d_b_fb, 0), jnp.concatenate(d_b_gq, 0),
             jnp.concatenate(d_b_gk, 0)]
    small_w = [a_norm_g, a_v_norm_g, a_w_s, a_b_s, b_f_bias, b_q_norm_g, b_k_norm_g]
    small_m = [m_a_norm_g, m_a_v_norm_g, m_a_w_s, m_a_b_s, m_b_f_bias, m_b_q_norm_g, m_b_k_norm_g]
    small_v = [v_a_norm_g, v_a_v_norm_g, v_a_w_s, v_a_b_s, v_b_f_bias, v_b_q_norm_g, v_b_k_norm_g]
    small_flat = _flat_rows(small)
    n_small = small_flat.shape[0]
    (gathered_small,) = _run_exchange(_Gather(
        [jnp.concatenate([small_flat, _flat_rows([jnp.concatenate(d_b_norm, 0)])], axis=0)],
        [None]), "gather_small_grads")
    parts_small = gathered_small[:, :n_small]
    parts_b_norm = gathered_small[:, n_small:n_small + n_b * d // LANES].reshape(N_DEV, n_b, d)
    parts_b_norm = lax.dynamic_slice_in_dim(parts_b_norm, me * (d // N_DEV), d // N_DEV, axis=2)

    recv_a = [received[2 * j] for j in range(n_a)]
    recv_b = [received[2 * j + 1] for j in range(n_b)]
    u_a_in = _adamw(a_w_in, m_a_w_in, v_a_w_in, [r[0] for r in recv_a], "adamw_a_w_in")
    u_a_out = _adamw(a_w_out, m_a_w_out, v_a_w_out, [r[1] for r in recv_a], "adamw_a_w_out")
    u_b_in = _adamw(b_w_in, m_b_w_in, v_b_w_in, [r[0] for r in recv_b], "adamw_b_w_in")
    u_b_out = _adamw(b_w_out, m_b_w_out, v_b_w_out, [r[1] for r in recv_b], "adamw_b_w_out")
    u_b_norm = [o_[0] for o_ in _adamw(b_norm_g[None], m_b_norm_g[None], v_b_norm_g[None],
                                       [parts_b_norm], "adamw_b_norm")]
    u_small = _adamw(_flat_rows(small_w)[None], _flat_rows(small_m)[None],
                     _flat_rows(small_v)[None], [parts_small], "adamw_small")
    shapes = [w.shape for w in small_w]
    u_small = [_unflat_rows(o_[0], shapes) for o_ in u_small]

    def per_kind(k):
        s = u_small[k]
        return [s[0], u_a_in[k], s[1], s[2], s[3], u_a_out[k], u_b_norm[k], u_b_in[k], s[4], s[5],
                s[6], u_b_out[k]]

    return (loss, grad_x, *per_kind(0), *per_kind(1), *per_kind(2), *per_kind(3))
```

```python
import functools

import jax
import jax.numpy as jnp
from jax import lax
from jax.experimental import pallas as pl
from jax.experimental.pallas import tpu as pltpu

F32 = jnp.float32
BF16 = jnp.bfloat16
MESH = pl.DeviceIdType.MESH
AXES = ("x", "y", "c")
N_DEV = 8
NORM_EPS = 1e-6
LANES = 128
VMEM_LIMIT = 56 * 1024 * 1024

ADAM_LR = 0.001
ADAM_B1 = 0.9
ADAM_B2 = 0.999
ADAM_EPS = 1e-08
ADAM_WD = 0.01
ADAM_STEP = 10

LOG2E = 1.4426950408889634
SOFTMAX_ROWS = 64
UNDERFLOW_LOG2 = -160.0

GELU_C0 = 0.7978845608028654
GELU_C1 = 0.044715

NT_DIMS = (((1,), (1,)), ((), ()))
TN_DIMS = (((0,), (0,)), ((), ()))


def _params(sem=None):
    return pltpu.CompilerParams(dimension_semantics=sem, vmem_limit_bytes=VMEM_LIMIT)


def _pick(n, target, unit):
    best = None
    for t in range(unit, min(n, target) + 1, unit):
        if n % t == 0:
            best = t
    return n if best is None else best


def _sigmoid(x):
    return 1.0 / (1.0 + jnp.exp(-x))


def _gelu(x):
    return 0.5 * x * (1.0 + jnp.tanh(GELU_C0 * x * (1.0 + GELU_C1 * x * x)))


def _gelu_and_grad(x):
    x2 = x * x
    t = jnp.tanh(GELU_C0 * x * (1.0 + GELU_C1 * x2))
    g = 0.5 * x * (1.0 + t)
    dg = 0.5 * (1.0 + t) + 0.5 * x * (1.0 - t * t) * (GELU_C0 * (1.0 + 3.0 * GELU_C1 * x2))
    return g, dg


def _dot(a, b):
    return jnp.dot(a, b, preferred_element_type=F32)


def _dot_nt(a, b):
    return lax.dot_general(a, b, NT_DIMS, preferred_element_type=F32)


def _dot_tn(a, b):
    return lax.dot_general(a, b, TN_DIMS, preferred_element_type=F32)


def _split3(v):
    hi = v.astype(BF16)
    r1 = v - hi.astype(F32)
    mid = r1.astype(BF16)
    lo = (r1 - mid.astype(F32)).astype(BF16)
    return hi, mid, lo


def _dev_index(p):
    return 4 * p[0] + 2 * p[1] + p[2]


def _block(ref, idx, axis, size):
    if axis is None:
        return ref.at[idx]
    start = pl.multiple_of(idx * size, size)
    return ref.at[(slice(None),) * axis + (pl.ds(start, size),)]


class _Exchange:
    def __init__(self, arrays, axes):
        self.arrays, self.axes, self.n = list(arrays), list(axes), len(arrays)

    def scratch(self):
        return [pltpu.SemaphoreType.DMA((self.n, 7)), pltpu.SemaphoreType.DMA((self.n, 7)),
                pltpu.SemaphoreType.DMA((self.n,))]

    @staticmethod
    def _place():
        x, y, c = lax.axis_index("x"), lax.axis_index("y"), lax.axis_index("c")
        return x, y, c


class _Gather(_Exchange):
    def out_shapes(self):
        outs = []
        for a, ax in zip(self.arrays, self.axes):
            if ax is None:
                shape = (N_DEV,) + a.shape
            else:
                shape = a.shape[:ax] + (N_DEV * a.shape[ax],) + a.shape[ax + 1:]
            outs.append(jax.ShapeDtypeStruct(shape, a.dtype))
        return outs

    def _copy(self, ins, outs, sems, a, k, block, to, src=None):
        ax = self.axes[a]
        dst = _block(outs[a], _dev_index(block), ax, None if ax is None else self.arrays[a].shape[ax])
        return pltpu.make_async_remote_copy(
            src_ref=dst if src is None else src, dst_ref=dst,
            send_sem=sems[0].at[a, k], recv_sem=sems[1].at[a, k],
            device_id=to, device_id_type=MESH)

    def _mine(self, ins, outs, sems, a, me):
        ax = self.axes[a]
        dst = _block(outs[a], _dev_index(me), ax, None if ax is None else self.arrays[a].shape[ax])
        return pltpu.make_async_copy(ins[a], dst, sems[2].at[a])

    def _first(self, ins, outs, sems):
        x, y, c = self._place()
        me, sibling = (x, y, c), (x, y, 1 - c)
        chips = [(1 - x, y), (x, 1 - y), (1 - x, 1 - y)]
        first = []
        for a in range(self.n):
            first.append(self._copy(ins, outs, sems, a, 0, me, sibling, src=ins[a]))
            first += [self._copy(ins, outs, sems, a, 1 + j, me, (*chip, c), src=ins[a])
                      for j, chip in enumerate(chips)]
        return first

    def start(self, ins, outs, sems):
        me = self._place()
        for a in range(self.n):
            self._mine(ins, outs, sems, a, me).start()
        for cp in self._first(ins, outs, sems):
            cp.start()

    def finish(self, ins, outs, sems):
        x, y, c = self._place()
        me, sibling = (x, y, c), (x, y, 1 - c)
        chips = [(1 - x, y), (x, 1 - y), (1 - x, 1 - y)]
        passed = []
        for a in range(self.n):
            for j, chip in enumerate(chips):
                self._copy(ins, outs, sems, a, 1 + j, (*chip, c), me).wait_recv()
                cp = self._copy(ins, outs, sems, a, 4 + j, (*chip, c), sibling)
                cp.start()
                passed.append(cp)
        for a in range(self.n):
            self._copy(ins, outs, sems, a, 0, sibling, me).wait_recv()
            for j, chip in enumerate(chips):
                self._copy(ins, outs, sems, a, 4 + j, (*chip, 1 - c), me).wait_recv()
        for cp in self._first(ins, outs, sems) + passed:
            cp.wait_send()
        for a in range(self.n):
            self._mine(ins, outs, sems, a, me).wait()


class _AllToAll(_Exchange):
    def _blk_shape(self, a):
        arr, ax = self.arrays[a], self.axes[a]
        if ax is None:
            return arr.shape[1:]
        return arr.shape[:ax] + (arr.shape[ax] // N_DEV,) + arr.shape[ax + 1:]

    def out_shapes(self):
        return [jax.ShapeDtypeStruct((N_DEV,) + self._blk_shape(a), self.arrays[a].dtype)
                for a in range(self.n)]

    def _src(self, ins, a, idx):
        ax = self.axes[a]
        return _block(ins[a], idx, ax, None if ax is None else self.arrays[a].shape[ax] // N_DEV)

    def _peers(self):
        x, y, c = self._place()
        return [((1 - x) if r & 4 else x, (1 - y) if r & 2 else y, (1 - c) if r & 1 else c)
                for r in range(1, N_DEV)]

    def _sends(self, ins, outs, sems):
        me = _dev_index(self._place())
        return [pltpu.make_async_remote_copy(
            src_ref=self._src(ins, a, _dev_index(peer)), dst_ref=outs[a].at[me],
            send_sem=sems[0].at[a, k], recv_sem=sems[1].at[a, k],
            device_id=peer, device_id_type=MESH)
            for a in range(self.n) for k, peer in enumerate(self._peers())]

    def _mine(self, ins, outs, sems):
        me = _dev_index(self._place())
        return [pltpu.make_async_copy(self._src(ins, a, me), outs[a].at[me], sems[2].at[a])
                for a in range(self.n)]

    def start(self, ins, outs, sems):
        for cp in self._mine(ins, outs, sems) + self._sends(ins, outs, sems):
            cp.start()

    def finish(self, ins, outs, sems):
        for a in range(self.n):
            for k, peer in enumerate(self._peers()):
                landed = outs[a].at[_dev_index(peer)]
                pltpu.make_async_remote_copy(
                    src_ref=landed, dst_ref=landed, send_sem=sems[0].at[a, k],
                    recv_sem=sems[1].at[a, k], device_id=peer, device_id_type=MESH).wait_recv()
        for cp in self._sends(ins, outs, sems):
            cp.wait_send()
        for cp in self._mine(ins, outs, sems):
            cp.wait()


def _run_exchange(exchange, name):
    n = exchange.n

    def body(*refs):
        ins, outs, sems = refs[:n], refs[n:2 * n], refs[2 * n:]
        exchange.start(ins, outs, sems)
        exchange.finish(ins, outs, sems)

    any_spec = pl.BlockSpec(memory_space=pl.ANY)
    return pl.pallas_call(
        body, name=name, out_shape=exchange.out_shapes(),
        in_specs=[any_spec] * n, out_specs=[any_spec] * n, scratch_shapes=exchange.scratch(),
    )(*exchange.arrays)


def _matmul(a, b, mode, out_dtype, name, tm=1024, tn=1024, tk=1024, residual=None, exchange=None):
    if mode == "tn":
        kdim, m = a.shape
    else:
        m, kdim = a.shape
    n = b.shape[0] if mode == "nt" else b.shape[1]
    tm, tn, tk = _pick(m, tm, LANES), _pick(n, tn, LANES), _pick(kdim, tk, LANES)
    nk = kdim // tk
    if mode == "tn":
        a_spec = pl.BlockSpec((tk, tm), lambda i, j, k: (k, i))
    else:
        a_spec = pl.BlockSpec((tm, tk), lambda i, j, k: (i, k))
    if mode == "nt":
        b_spec = pl.BlockSpec((tn, tk), lambda i, j, k: (j, k))
    else:
        b_spec = pl.BlockSpec((tk, tn), lambda i, j, k: (k, j))
    o_spec = pl.BlockSpec((tm, tn), lambda i, j, k: (i, j))
    dot = {"nn": _dot, "nt": _dot_nt, "tn": _dot_tn}[mode]
    has_res = residual is not None
    n_in = 3 if has_res else 2
    n_ex = 0 if exchange is None else exchange.n
    ni, nj = m // tm, n // tn

    def body(*refs):
        a_ref, b_ref = refs[:2]
        r_ref = refs[2] if has_res else None
        ex_ins = refs[n_in:n_in + n_ex]
        o_ref = refs[n_in + n_ex]
        ex_outs = refs[n_in + n_ex + 1:n_in + 2 * n_ex + 1]
        scratch = refs[n_in + 2 * n_ex + 1:]
        i, j, k = pl.program_id(0), pl.program_id(1), pl.program_id(2)

        if exchange is not None:
            sems = scratch[-3:]

            @pl.when((i == 0) & (j == 0) & (k == 0))
            def _():
                exchange.start(ex_ins, ex_outs, sems)

        def finish(acc):
            if has_res:
                acc = acc + r_ref[...]
            o_ref[...] = acc.astype(out_dtype)

        if nk == 1:
            finish(dot(a_ref[...], b_ref[...]))
        else:
            acc_ref = scratch[0]

            @pl.when(k == 0)
            def _():
                acc_ref[...] = jnp.zeros_like(acc_ref)

            acc_ref[...] += dot(a_ref[...], b_ref[...])

            @pl.when(k == nk - 1)
            def _():
                finish(acc_ref[...])

        if exchange is not None:
            @pl.when((i == ni - 1) & (j == nj - 1) & (k == nk - 1))
            def _():
                exchange.finish(ex_ins, ex_outs, sems)

    any_spec = pl.BlockSpec(memory_space=pl.ANY)
    operands = (a, b, residual) if has_res else (a, b)
    out_shape = jax.ShapeDtypeStruct((m, n), out_dtype)
    scratch_shapes = [] if nk == 1 else [pltpu.VMEM((tm, tn), F32)]
    if exchange is None:
        return pl.pallas_call(
            body, name=name, grid=(ni, nj, nk),
            in_specs=[a_spec, b_spec] + ([o_spec] if has_res else []),
            out_specs=o_spec, out_shape=out_shape, scratch_shapes=scratch_shapes,
            compiler_params=_params(("parallel", "parallel", "arbitrary")),
        )(*operands)
    return pl.pallas_call(
        body, name=name, grid=(ni, nj, nk),
        in_specs=[a_spec, b_spec] + ([o_spec] if has_res else []) + [any_spec] * n_ex,
        out_specs=[o_spec] + [any_spec] * n_ex,
        out_shape=[out_shape] + exchange.out_shapes(),
        scratch_shapes=scratch_shapes + exchange.scratch(),
        compiler_params=_params(("arbitrary", "arbitrary", "arbitrary")),
    )(*operands, *exchange.arrays)


def _rms_fwd(x, gain, name):
    t, d = x.shape
    tr = _pick(t, 256, 16)

    def body(x_ref, g_ref, h_ref):
        xv = x_ref[...]
        r = lax.rsqrt(jnp.mean(xv * xv, axis=1, keepdims=True) + NORM_EPS)
        h_ref[...] = (xv * r * g_ref[...]).astype(BF16)

    row = pl.BlockSpec((tr, d), lambda i: (i, 0))
    return pl.pallas_call(
        body, name=name, grid=(t // tr,),
        in_specs=[row, pl.BlockSpec((1, d), lambda i: (0, 0))],
        out_specs=row, out_shape=jax.ShapeDtypeStruct((t, d), BF16),
        compiler_params=_params(("arbitrary",)),
    )(x, gain)


def _rms_bwd(x, dh, g_in, gain, name):
    t, d = x.shape
    tr = _pick(t, 256, 16)

    def body(x_ref, dh_ref, gin_ref, g_ref, dx_ref, dxb_ref, dg_ref):
        i = pl.program_id(0)
        xv, dhv = x_ref[...], dh_ref[...]
        r = lax.rsqrt(jnp.mean(xv * xv, axis=1, keepdims=True) + NORM_EPS)
        xh = xv * r
        dxh = dhv * g_ref[...]
        dx = gin_ref[...] + r * (dxh - xh * jnp.mean(dxh * xh, axis=1, keepdims=True))
        dx_ref[...] = dx
        dxb_ref[...] = dx.astype(BF16)

        @pl.when(i == 0)
        def _():
            dg_ref[...] = jnp.zeros_like(dg_ref)

        dg_ref[...] += jnp.sum(dhv * xh, axis=0, keepdims=True)

    row = pl.BlockSpec((tr, d), lambda i: (i, 0))
    vec = pl.BlockSpec((1, d), lambda i: (0, 0))
    return pl.pallas_call(
        body, name=name, grid=(t // tr,),
        in_specs=[row, row, row, vec],
        out_specs=[row, row, vec],
        out_shape=[jax.ShapeDtypeStruct((t, d), F32), jax.ShapeDtypeStruct((t, d), BF16),
                   jax.ShapeDtypeStruct((1, d), F32)],
        compiler_params=_params(("arbitrary",)),
    )(x, dh, g_in, gain)


def _loss_grad(y, target, name):
    t, d = y.shape
    tr = _pick(t, 256, 16)

    def body(y_ref, t_ref, s_ref, g_ref, gb_ref):
        i = pl.program_id(0)
        e = y_ref[...] - t_ref[...]
        g = e * (1.0 / d)
        g_ref[...] = g
        gb_ref[...] = g.astype(BF16)

        @pl.when(i == 0)
        def _():
            s_ref[...] = jnp.zeros_like(s_ref)

        s_ref[...] += jnp.sum(jnp.sum(e * e, axis=1, keepdims=True), axis=0, keepdims=True)

    row = pl.BlockSpec((tr, d), lambda i: (i, 0))
    return pl.pallas_call(
        body, name=name, grid=(t // tr,),
        in_specs=[row, row],
        out_specs=[pl.BlockSpec((1, 1), lambda i: (0, 0)), row, row],
        out_shape=[jax.ShapeDtypeStruct((1, 1), F32), jax.ShapeDtypeStruct((t, d), F32),
                   jax.ShapeDtypeStruct((t, d), BF16)],
        compiler_params=_params(("arbitrary",)),
    )(y, target)


def _causal_weights(ws_ref, g, chunk, transposed):
    rows = lax.broadcasted_iota(jnp.int32, (chunk, chunk), 0)
    cols = lax.broadcasted_iota(jnp.int32, (chunk, chunk), 1)
    keep = (cols >= rows) if transposed else (rows >= cols)
    return jnp.where(keep, ws_ref[g], 0.0).astype(BF16)


def _mix_fwd(uvz, v_gain, w_s, b_s, name):
    t, w3 = uvz.shape
    w = w3 // 3
    groups, chunk = w_s.shape[0], w_s.shape[1]
    gd = w // groups

    def body(uvz_ref, gam_ref, ws_ref, bs_ref, y_ref):
        gv = _gelu(uvz_ref[:, w:2 * w])
        r = lax.rsqrt(jnp.mean(gv * gv, axis=1, keepdims=True) + NORM_EPS)
        vn = (gv * r * gam_ref[...]).astype(BF16)
        for g in range(groups):
            sl = slice(g * gd, (g + 1) * gd)
            mixed = _dot(_causal_weights(ws_ref, g, chunk, False), vn[:, sl]) + bs_ref[g]
            u = uvz_ref[:, g * gd:(g + 1) * gd]
            z = uvz_ref[:, 2 * w + g * gd:2 * w + (g + 1) * gd]
            y_ref[:, sl] = (_gelu(u) * mixed * (z * _sigmoid(z))).astype(BF16)

    return pl.pallas_call(
        body, name=name, grid=(t // chunk,),
        in_specs=[pl.BlockSpec((chunk, w3), lambda i: (i, 0)),
                  pl.BlockSpec((1, w), lambda i: (0, 0)),
                  pl.BlockSpec((groups, chunk, chunk), lambda i: (0, 0, 0)),
                  pl.BlockSpec((groups, chunk, 1), lambda i: (0, 0, 0))],
        out_specs=pl.BlockSpec((chunk, w), lambda i: (i, 0)),
        out_shape=jax.ShapeDtypeStruct((t, w), BF16),
        compiler_params=_params(("arbitrary",)),
    )(uvz, v_gain, w_s, b_s)


def _mix_bwd(uvz, dy, v_gain, w_s, w_s_t, b_s, name):
    t, w3 = uvz.shape
    w = w3 // 3
    groups, chunk = w_s.shape[0], w_s.shape[1]
    gd = w // groups

    def body(uvz_ref, dy_ref, gam_ref, ws_ref, wst_ref, bs_ref, d_ref, dws_ref, dbs_ref, dgam_ref,
             dvn_ref):
        i = pl.program_id(0)

        @pl.when(i == 0)
        def _():
            dws_ref[...] = jnp.zeros_like(dws_ref)
            dbs_ref[...] = jnp.zeros_like(dbs_ref)
            dgam_ref[...] = jnp.zeros_like(dgam_ref)

        gv, dgv = _gelu_and_grad(uvz_ref[:, w:2 * w])
        r = lax.rsqrt(jnp.mean(gv * gv, axis=1, keepdims=True) + NORM_EPS)
        vh = gv * r
        gam = gam_ref[...]
        vn = (vh * gam).astype(BF16)
        rows = lax.broadcasted_iota(jnp.int32, (chunk, chunk), 0)
        cols = lax.broadcasted_iota(jnp.int32, (chunk, chunk), 1)
        for g in range(groups):
            sl = slice(g * gd, (g + 1) * gd)
            mixed = _dot(_causal_weights(ws_ref, g, chunk, False), vn[:, sl]) + bs_ref[g]
            gu, dgu = _gelu_and_grad(uvz_ref[:, g * gd:(g + 1) * gd])
            z = uvz_ref[:, 2 * w + g * gd:2 * w + (g + 1) * gd]
            sz = _sigmoid(z)
            silu = z * sz
            dyv = dy_ref[:, sl]
            dmixed = dyv * gu * silu
            d_ref[:, sl] = (dyv * mixed * silu * dgu).astype(BF16)
            d_ref[:, 2 * w + g * gd:2 * w + (g + 1) * gd] = (
                dyv * gu * mixed * (sz * (1.0 + z * (1.0 - sz)))).astype(BF16)
            dmb = dmixed.astype(BF16)
            dws_ref[g] += jnp.where(rows >= cols, _dot_nt(dmb, vn[:, sl]), 0.0)
            dbs_ref[g] += jnp.sum(dmixed, axis=1, keepdims=True)
            dvn_ref[:, sl] = _dot(_causal_weights(wst_ref, g, chunk, True), dmb)
        dvn = dvn_ref[...]
        dgam_ref[...] += jnp.sum(dvn * vh, axis=0, keepdims=True)
        dvh = dvn * gam
        dgvv = r * (dvh - vh * jnp.mean(dvh * vh, axis=1, keepdims=True))
        d_ref[:, w:2 * w] = (dgvv * dgv).astype(BF16)

    return pl.pallas_call(
        body, name=name, grid=(t // chunk,),
        in_specs=[pl.BlockSpec((chunk, w3), lambda i: (i, 0)),
                  pl.BlockSpec((chunk, w), lambda i: (i, 0)),
                  pl.BlockSpec((1, w), lambda i: (0, 0)),
                  pl.BlockSpec((groups, chunk, chunk), lambda i: (0, 0, 0)),
                  pl.BlockSpec((groups, chunk, chunk), lambda i: (0, 0, 0)),
                  pl.BlockSpec((groups, chunk, 1), lambda i: (0, 0, 0))],
        out_specs=[pl.BlockSpec((chunk, w3), lambda i: (i, 0)),
                   pl.BlockSpec((groups, chunk, chunk), lambda i: (0, 0, 0)),
                   pl.BlockSpec((groups, chunk, 1), lambda i: (0, 0, 0)),
                   pl.BlockSpec((1, w), lambda i: (0, 0))],
        out_shape=[jax.ShapeDtypeStruct((t, w3), BF16),
                   jax.ShapeDtypeStruct((groups, chunk, chunk), F32),
                   jax.ShapeDtypeStruct((groups, chunk, 1), F32),
                   jax.ShapeDtypeStruct((1, w), F32)],
        scratch_shapes=[pltpu.VMEM((chunk, w), F32)],
        compiler_params=_params(("arbitrary",)),
    )(uvz, dy, v_gain, w_s, w_s_t, b_s)


def _attn_prep(proj, q_gain, k_gain, f_bias, heads, hd, name):
    t = proj.shape[0]
    bw = heads * hd
    tr = _pick(t, 256, 16)
    fcol = 4 * bw // LANES

    def body(q_ref, k_ref, v_ref, f_ref, gq_ref, gk_ref, fb_ref, qn_ref, kn_ref, vb_ref, cum_ref,
             qsq_ref, ksq_ref, first_ref, last_ref, carry_ref):
        i = pl.program_id(0)

        @pl.when(i == 0)
        def _():
            carry_ref[...] = jnp.zeros_like(carry_ref)

        lane = lax.broadcasted_iota(jnp.int32, (1, LANES), 1)
        for src, gain, dst, sq_ref in ((q_ref, gq_ref, qn_ref, qsq_ref),
                                       (k_ref, gk_ref, kn_ref, ksq_ref)):
            sq_row = jnp.zeros((1, LANES), F32)
            for h in range(heads):
                sl = slice(h * hd, (h + 1) * hd)
                v = src[:, sl]
                r = lax.rsqrt(jnp.mean(v * v, axis=1, keepdims=True) + NORM_EPS)
                normed = (v * r * gain[...]).astype(BF16)
                dst[:, sl] = normed
                nf = normed.astype(F32)
                sq = jnp.max(jnp.sum(nf * nf, axis=1, keepdims=True), axis=0, keepdims=True)
                sq_row = jnp.where(lane == h, sq, sq_row)
            sq_ref[0] = sq_row
        vb_ref[...] = v_ref[...].astype(BF16)
        fl = f_ref[...] + fb_ref[...]
        log_f = jnp.minimum(fl, 0.0) - jnp.log(1.0 + jnp.exp(-jnp.abs(fl)))
        rows = lax.broadcasted_iota(jnp.int32, (tr, tr), 0)
        cols = lax.broadcasted_iota(jnp.int32, (tr, tr), 1)
        lower = jnp.where(rows >= cols, 1.0, 0.0).astype(BF16)
        hi, mid, lo = _split3(log_f)
        cum_ref[...] = (_dot(lower, hi) + _dot(lower, mid) + _dot(lower, lo)) + carry_ref[...]
        carry_ref[...] = cum_ref[tr - 1:tr, :]
        first_ref[0] = cum_ref[0:1, :]
        last_ref[0] = cum_ref[tr - 1:tr, :]

    wide = lambda col: pl.BlockSpec((tr, bw), lambda i: (i, col))
    vec = pl.BlockSpec((1, hd), lambda i: (0, 0))
    stat = pl.BlockSpec((1, 1, LANES), lambda i: (i, 0, 0))
    return pl.pallas_call(
        body, name=name, grid=(t // tr,),
        in_specs=[wide(0), wide(1), wide(2), pl.BlockSpec((tr, LANES), lambda i: (i, fcol)),
                  vec, vec, pl.BlockSpec((1, LANES), lambda i: (0, 0))],
        out_specs=[wide(0), wide(0), wide(0), pl.BlockSpec((tr, LANES), lambda i: (i, 0))]
        + [stat] * 4,
        out_shape=[jax.ShapeDtypeStruct((t, bw), BF16)] * 3 + [jax.ShapeDtypeStruct((t, LANES), F32)]
        + [jax.ShapeDtypeStruct((t // tr, 1, LANES), F32)] * 4,
        scratch_shapes=[pltpu.VMEM((1, LANES), F32)],
        compiler_params=_params(("arbitrary",)),
    )(proj, proj, proj, proj, q_gain, k_gain, f_bias)


def _skip_plan(qsq, ksq, first, last, t, tq, hd, name):
    nt = qsq.shape[0]
    nq = t // tq
    r = nt // nq
    scale2 = hd ** -0.5 * LOG2E

    def body(qsq_ref, ksq_ref, first_ref, last_ref, out_ref):
        kmax = ksq_ref[0]
        for tile in range(1, nt):
            kmax = jnp.maximum(kmax, ksq_ref[tile])
        for i in range(nq):
            qmax = qsq_ref[i * r]
            for tile in range(i * r + 1, (i + 1) * r):
                qmax = jnp.maximum(qmax, qsq_ref[tile])
            coef = 2.0 * scale2 * jnp.sqrt(qmax * kmax)
            start = first_ref[i * r]
            count = jnp.zeros((1, LANES), jnp.int32)
            for j in range(i):
                bound = coef + (start - last_ref[(j + 1) * r - 1]) * LOG2E
                count = count + jnp.where(bound <= UNDERFLOW_LOG2, 1, 0)
            out_ref[i:i + 1, :] = count

    return pl.pallas_call(
        body, name=name, out_shape=jax.ShapeDtypeStruct((nq, LANES), jnp.int32),
    )(qsq, ksq, first, last)


def _flash_fwd(first_block, qn, kn, vb, ck, proj, heads, hd, tq, name):
    t = qn.shape[0]
    nq = t // tq
    th = tq // 2
    scale2 = hd ** -0.5 * LOG2E
    zcol = 3 * heads
    rc = _pick(tq, SOFTMAX_ROWS, 16)
    reps = th // LANES

    def body(first_ref, q_ref, k_ref, v_ref, ck_ref, z_ref, o_ref, y_ref, lse_ref,
             m_s, l_s, acc_s, s_a, s_b, p_a, p_b, al_a, al_b):
        i = pl.program_id(1)
        j0 = first_ref[pl.program_id(0), i]
        bufs = ((s_a, p_a, al_a), (s_b, p_b, al_b))
        m_s[...] = jnp.full_like(m_s, -jnp.inf)
        l_s[...] = jnp.zeros_like(l_s)
        acc_s[...] = jnp.zeros_like(acc_s)
        p_b[...] = jnp.zeros_like(p_b)
        al_b[...] = jnp.ones_like(al_b)

        def scores(j, half):
            off = pl.multiple_of(j * tq + half * th, th)
            bufs[half][0][...] = _dot_nt(q_ref[...], k_ref[pl.ds(off, th), :])

        def values(j, half):
            off = pl.multiple_of(j * tq + half * th, th)
            _, p_buf, al = bufs[half]
            acc_s[...] = (jnp.tile(al[...], (1, hd // LANES)) * acc_s[...]
                          + _dot(p_buf[...], v_ref[pl.ds(off, th), :]))

        def softmax(j, half, masked):
            s_buf, p_buf, al = bufs[half]
            ck2 = ck_ref[0, j][:, half * th:(half + 1) * th] * LOG2E
            for c in range(tq // rc):
                r = slice(c * rc, (c + 1) * rc)
                s = s_buf[r, :] * scale2 - ck2
                if masked:
                    rows = lax.broadcasted_iota(jnp.int32, (rc, th), 0) + c * rc
                    cols = lax.broadcasted_iota(jnp.int32, (rc, th), 1) + half * th
                    s = jnp.where(rows >= cols, s, -jnp.inf)
                s_buf[r, :] = s
                m_prev = m_s[r, :]
                m_new = jnp.maximum(m_prev, jnp.max(s, axis=1, keepdims=True))
                al[r, :] = jnp.exp2(m_prev - m_new)
                m_s[r, :] = m_new
            for c in range(tq // rc):
                r = slice(c * rc, (c + 1) * rc)
                p = jnp.exp2(s_buf[r, :] - jnp.tile(m_s[r, :], (1, reps)))
                p_buf[r, :] = p.astype(BF16)
                lane_sum = p[:, 0:LANES]
                for b in range(1, reps):
                    lane_sum = lane_sum + p[:, b * LANES:(b + 1) * LANES]
                l_s[r, :] = al[r, :] * l_s[r, :] + lane_sum

        scores(j0, 0)

        def loop_body(j, carry):
            scores(j, 1)
            values(jnp.maximum(j - 1, 0), 1)
            softmax(j, 0, False)
            scores(j + 1, 0)
            values(j, 0)
            softmax(j, 1, False)
            return carry

        lax.fori_loop(j0, i, loop_body, 0)
        scores(i, 1)
        values(jnp.maximum(i - 1, 0), 1)
        softmax(i, 0, True)
        values(i, 0)
        softmax(i, 1, True)
        values(i, 1)
        l = jnp.sum(l_s[...], axis=1, keepdims=True)
        o = acc_s[...] / l
        z = z_ref[...]
        o_ref[...] = o
        y_ref[...] = (o * (z * _sigmoid(z))).astype(BF16)
        lse_ref[0] = m_s[:, 0:1] + jnp.log(l) * LOG2E

    blk = pl.BlockSpec((tq, hd), lambda h, i, first: (i, h))
    head = pl.BlockSpec((t, hd), lambda h, i, first: (0, h))
    col = pl.BlockSpec((1, tq, 1), lambda h, i, first: (h, i, 0))
    stat = pltpu.VMEM((tq, LANES), F32)
    return pl.pallas_call(
        body, name=name,
        grid_spec=pltpu.PrefetchScalarGridSpec(
            num_scalar_prefetch=1, grid=(heads, nq),
            in_specs=[blk, head, head,
                      pl.BlockSpec((1, nq, 1, tq), lambda h, i, first: (h, 0, 0, 0)),
                      pl.BlockSpec((tq, hd), lambda h, i, first: (i, zcol + h))],
            out_specs=[blk, blk, col],
            scratch_shapes=[stat, stat, pltpu.VMEM((tq, hd), F32),
                            pltpu.VMEM((tq, th), F32), pltpu.VMEM((tq, th), F32),
                            pltpu.VMEM((tq, th), BF16), pltpu.VMEM((tq, th), BF16), stat, stat]),
        out_shape=[jax.ShapeDtypeStruct((t, heads * hd), F32),
                   jax.ShapeDtypeStruct((t, heads * hd), BF16),
                   jax.ShapeDtypeStruct((heads, t, 1), F32)],
        compiler_params=_params(("arbitrary", "arbitrary")),
    )(first_block, qn, kn, vb, ck, proj)


def _attn_bwd_prep(dy, proj, o, heads, hd, name):
    t, bw = dy.shape
    tr = _pick(t, 256, 16)

    def body(dy_ref, z_ref, o_ref, do_ref, dz_ref, delta_ref):
        dyv, z, ov = dy_ref[...], z_ref[...], o_ref[...]
        sz = _sigmoid(z)
        do = dyv * (z * sz)
        do_ref[...] = do.astype(BF16)
        dz_ref[...] = (dyv * ov * (sz * (1.0 + z * (1.0 - sz)))).astype(BF16)
        prod = do * ov
        for h in range(heads):
            delta_ref[h] = jnp.sum(prod[:, h * hd:(h + 1) * hd], axis=1, keepdims=True)

    row = pl.BlockSpec((tr, bw), lambda i: (i, 0))
    return pl.pallas_call(
        body, name=name, grid=(t // tr,),
        in_specs=[row, pl.BlockSpec((tr, bw), lambda i: (i, 3)), row],
        out_specs=[row, row, pl.BlockSpec((heads, tr, 1), lambda i: (0, i, 0))],
        out_shape=[jax.ShapeDtypeStruct((t, bw), BF16), jax.ShapeDtypeStruct((t, bw), BF16),
                   jax.ShapeDtypeStruct((heads, t, 1), F32)],
        compiler_params=_params(("arbitrary",)),
    )(dy, proj, o)


def _flash_bwd(first_block, qn, kn, vb, do, ck, lse, delta, heads, hd, tq, name):
    t = qn.shape[0]
    nq = t // tq
    th = tq // 2
    scale = hd ** -0.5
    scale2 = scale * LOG2E
    rc = _pick(tq, SOFTMAX_ROWS, 16)
    reps = th // LANES

    def body(first_ref, q_ref, k_ref, v_ref, do_ref, ck_ref, lse_ref, delta_ref,
             dq_ref, dk_ref, dv_ref, dcq_ref, dck_ref,
             dq_s, dcq_s, lse_s, delta_s, diff_s, s_a, s_b, dp_a, dp_b, p_a, p_b, ds_a, ds_b,
             col_a, col_b):
        i = pl.program_id(1)
        j0 = first_ref[pl.program_id(0), i]
        bufs = ((s_a, dp_a, p_a, ds_a, col_a), (s_b, dp_b, p_b, ds_b, col_b))

        @pl.when(i == 0)
        def _():
            dk_ref[...] = jnp.zeros_like(dk_ref)
            dv_ref[...] = jnp.zeros_like(dv_ref)
            dck_ref[...] = jnp.zeros_like(dck_ref)

        dq_s[...] = jnp.zeros_like(dq_s)
        dcq_s[...] = jnp.zeros_like(dcq_s)
        lse_s[...] = jnp.broadcast_to(lse_ref[0], (tq, LANES))
        delta_s[...] = jnp.broadcast_to(delta_ref[0], (tq, LANES))
        diff_s[...] = (lax.broadcasted_iota(jnp.int32, (tq, th), 0)
                       - lax.broadcasted_iota(jnp.int32, (tq, th), 1))

        def scores(j, half):
            off = pl.multiple_of(j * tq + half * th, th)
            bufs[half][0][...] = _dot_nt(q_ref[...], k_ref[pl.ds(off, th), :])
            bufs[half][1][...] = _dot_nt(do_ref[...], v_ref[pl.ds(off, th), :])

        def elementwise(j, half):
            s_buf, dp_buf, p_buf, ds_buf, col = bufs[half]
            ck2 = ck_ref[0, j][:, half * th:(half + 1) * th] * LOG2E
            first_visible = (j - i) * tq + half * th
            col[...] = jnp.zeros_like(col)
            for c in range(tq // rc):
                r = slice(c * rc, (c + 1) * rc)
                s = s_buf[r, :] * scale2 - ck2
                s = jnp.where(diff_s[r, :] >= first_visible, s, -jnp.inf)
                p = jnp.exp2(s - jnp.tile(lse_s[r, :], (1, reps)))
                ds = p * (dp_buf[r, :] - jnp.tile(delta_s[r, :], (1, reps)))
                p_buf[r, :] = p.astype(BF16)
                ds_buf[r, :] = ds.astype(BF16)
                lane_sum = ds[:, 0:LANES]
                for b in range(1, reps):
                    lane_sum = lane_sum + ds[:, b * LANES:(b + 1) * LANES]
                dcq_s[r, :] += lane_sum
                group_sum = ds[0:8, :]
                for g in range(1, rc // 8):
                    group_sum = group_sum + ds[g * 8:(g + 1) * 8, :]
                col[...] += group_sum
            dck_ref[0, j, :, half * th:(half + 1) * th] += jnp.sum(col[...], axis=0, keepdims=True)

        def gradients(j, half):
            off = pl.multiple_of(j * tq + half * th, th)
            _, _, p_buf, ds_buf, _ = bufs[half]
            dv_ref[pl.ds(off, th), :] += _dot_tn(p_buf[...], do_ref[...])
            dk_ref[pl.ds(off, th), :] += _dot_tn(ds_buf[...], q_ref[...]) * scale
            dq_s[...] += _dot(ds_buf[...], k_ref[pl.ds(off, th), :]) * scale

        scores(j0, 0)
        scores(j0, 1)
        elementwise(j0, 0)

        def loop_body(j, carry):
            scores(j + 1, 0)
            elementwise(j, 1)
            gradients(j, 0)
            scores(j + 1, 1)
            elementwise(j + 1, 0)
            gradients(j, 1)
            return carry

        lax.fori_loop(j0, i, loop_body, 0)
        elementwise(i, 1)
        gradients(i, 0)
        gradients(i, 1)
        dq_ref[...] = dq_s[...]
        dcq_ref[0] = jnp.sum(dcq_s[...], axis=1, keepdims=True)

    blk = pl.BlockSpec((tq, hd), lambda h, i, first: (i, h))
    head = pl.BlockSpec((t, hd), lambda h, i, first: (0, h))
    col = pl.BlockSpec((1, tq, 1), lambda h, i, first: (h, i, 0))
    rowv = pl.BlockSpec((1, nq, 1, tq), lambda h, i, first: (h, 0, 0, 0))
    full = jax.ShapeDtypeStruct((t, heads * hd), F32)
    return pl.pallas_call(
        body, name=name,
        grid_spec=pltpu.PrefetchScalarGridSpec(
            num_scalar_prefetch=1, grid=(heads, nq),
            in_specs=[blk, head, head, blk, rowv, col, col],
            out_specs=[blk, head, head, col, rowv],
            scratch_shapes=[pltpu.VMEM((tq, hd), F32)] + [pltpu.VMEM((tq, LANES), F32)] * 3
            + [pltpu.VMEM((tq, th), jnp.int32)] + [pltpu.VMEM((tq, th), F32)] * 4
            + [pltpu.VMEM((tq, th), BF16)] * 4 + [pltpu.VMEM((8, th), F32)] * 2),
        out_shape=[full, full, full, jax.ShapeDtypeStruct((heads, t, 1), F32),
                   jax.ShapeDtypeStruct((heads, nq, 1, tq), F32)],
        compiler_params=_params(("arbitrary", "arbitrary")),
    )(first_block, qn, kn, vb, do, ck, lse, delta)


def _attn_bwd_post(dqn, dkn, dv, dz, proj, dcq, dck, q_gain, k_gain, f_bias, heads, hd, name):
    t, bw = dqn.shape
    tr = _pick(t, 128, 16)
    nb = t // tr
    fcol = 4 * bw // LANES
    width = 4 * bw + LANES

    def body(dq_ref, dk_ref, dv_ref, dz_ref, q_ref, k_ref, f_ref, dcq_ref, dck_ref, gq_ref, gk_ref,
             fb_ref, d_ref, dgq_ref, dgk_ref, dfb_ref, carry_ref, rc_ref):
        i = pl.program_id(0)

        @pl.when(i == 0)
        def _():
            carry_ref[...] = jnp.zeros_like(carry_ref)
            dgq_ref[...] = jnp.zeros_like(dgq_ref)
            dgk_ref[...] = jnp.zeros_like(dgk_ref)
            dfb_ref[...] = jnp.zeros_like(dfb_ref)

        for idx, (g_ref, raw_ref, gain_ref, dgain_ref) in enumerate(
                ((dq_ref, q_ref, gq_ref, dgq_ref), (dk_ref, k_ref, gk_ref, dgk_ref))):
            gain = gain_ref[...]
            dgain = jnp.zeros((1, hd), F32)
            for h in range(heads):
                sl = slice(h * hd, (h + 1) * hd)
                v, dn = raw_ref[:, sl], g_ref[:, sl]
                r = lax.rsqrt(jnp.mean(v * v, axis=1, keepdims=True) + NORM_EPS)
                vh = v * r
                dgain = dgain + jnp.sum(dn * vh, axis=0, keepdims=True)
                dvh = dn * gain
                draw = r * (dvh - vh * jnp.mean(dvh * vh, axis=1, keepdims=True))
                d_ref[:, idx * bw + h * hd:idx * bw + (h + 1) * hd] = draw.astype(BF16)
            dgain_ref[...] += dgain
        d_ref[:, 2 * bw:3 * bw] = dv_ref[...].astype(BF16)
        d_ref[:, 3 * bw:4 * bw] = dz_ref[...]
        rows = lax.broadcasted_iota(jnp.int32, (tr, tr), 0)
        cols = lax.broadcasted_iota(jnp.int32, (tr, tr), 1)
        upper = jnp.where(cols >= rows, 1.0, 0.0).astype(BF16)
        hi, mid, lo = _split3(dcq_ref[...] - dck_ref[...])
        rc_ref[...] = (_dot(upper, hi) + _dot(upper, mid) + _dot(upper, lo)) + carry_ref[...]
        carry_ref[...] = rc_ref[0:1, :]
        df = rc_ref[...] * (1.0 / (1.0 + jnp.exp(f_ref[...] + fb_ref[...])))
        d_ref[:, 4 * bw:] = df.astype(BF16)
        dfb_ref[...] += jnp.sum(df, axis=0, keepdims=True)

    wide = lambda col: pl.BlockSpec((tr, bw), lambda i: (nb - 1 - i, col))
    lane = lambda col: pl.BlockSpec((tr, LANES), lambda i: (nb - 1 - i, col))
    vec = pl.BlockSpec((1, hd), lambda i: (0, 0))
    vecl = pl.BlockSpec((1, LANES), lambda i: (0, 0))
    return pl.pallas_call(
        body, name=name, grid=(nb,),
        in_specs=[wide(0), wide(0), wide(0), wide(0), wide(0), wide(1), lane(fcol), lane(0), lane(0),
                  vec, vec, vecl],
        out_specs=[pl.BlockSpec((tr, width), lambda i: (nb - 1 - i, 0)), vec, vec, vecl],
        out_shape=[jax.ShapeDtypeStruct((t, width), BF16), jax.ShapeDtypeStruct((1, hd), F32),
                   jax.ShapeDtypeStruct((1, hd), F32), jax.ShapeDtypeStruct((1, LANES), F32)],
        scratch_shapes=[pltpu.VMEM((1, LANES), F32), pltpu.VMEM((tr, LANES), F32)],
        compiler_params=_params(("arbitrary",)),
    )(dqn, dkn, dv, dz, proj, proj, proj, dcq, dck, q_gain, k_gain, f_bias)


def _adamw(w, m, v, parts, name):
    nl, r, c = w.shape
    itemsize = parts[0].dtype.itemsize
    unit = 32 // itemsize
    row_bytes = c * (7 * 4 + N_DEV * itemsize * nl)
    tr = _pick(r, max(unit, 12 * 1024 * 1024 // row_bytes), unit)
    nr = r // tr
    c1 = 1.0 / (1.0 - ADAM_B1 ** ADAM_STEP)
    c2 = 1.0 / (1.0 - ADAM_B2 ** ADAM_STEP)

    def body(*refs):
        w_ref, m_ref, v_ref = refs[:3]
        p_refs = refs[3:3 + nl]
        g_ref, d_ref, nm_ref, nv_ref = refs[3 + nl:]
        layer = pl.program_id(0)

        def partial(j):
            p = p_refs[0][j].astype(F32)
            for q in range(1, nl):
                p = jnp.where(layer == q, p_refs[q][j].astype(F32), p)
            return p

        g = partial(0)
        for j in range(1, N_DEV):
            g = g + partial(j)
        nm = ADAM_B1 * m_ref[0] + (1.0 - ADAM_B1) * g
        nv = ADAM_B2 * v_ref[0] + (1.0 - ADAM_B2) * (g * g)
        g_ref[0] = g
        nm_ref[0] = nm
        nv_ref[0] = nv
        d_ref[0] = -ADAM_LR * ((nm * c1) / (jnp.sqrt(nv * c2) + ADAM_EPS) + ADAM_WD * w_ref[0])

    def part_spec(q):
        rest = 0 if q > 0 else nr - 1
        return pl.BlockSpec((N_DEV, tr, c), lambda l, i: (0, jnp.where(l == q, i, rest), 0))

    row = pl.BlockSpec((1, tr, c), lambda l, i: (l, i, 0))
    return pl.pallas_call(
        body, name=name, grid=(nl, nr),
        in_specs=[row, row, row] + [part_spec(q) for q in range(nl)],
        out_specs=[row] * 4,
        out_shape=[jax.ShapeDtypeStruct((nl, r, c), F32)] * 4,
        compiler_params=_params(("arbitrary", "arbitrary")),
    )(w, m, v, *parts)


def _flat_rows(pieces):
    rows = []
    for p in pieces:
        f = p.reshape(-1)
        f = jnp.pad(f, (0, (-f.shape[0]) % LANES))
        rows.append(f.reshape(-1, LANES))
    out = jnp.concatenate(rows, axis=0)
    return jnp.pad(out, ((0, (-out.shape[0]) % 8), (0, 0)))


def _unflat_rows(flat, shapes):
    outs, r0 = [], 0
    lead = flat.shape[:-2]
    for s in shapes:
        size = 1
        for d in s:
            size *= d
        nr = -(-size // LANES)
        piece = flat[..., r0:r0 + nr, :].reshape(lead + (nr * LANES,))[..., :size]
        outs.append(piece.reshape(lead + tuple(s)))
        r0 += nr
    return outs


def kernel(x, a_norm_g, a_w_in, a_v_norm_g, a_w_s, a_b_s, a_w_out, b_norm_g, b_w_in, b_f_bias, b_q_norm_g, b_k_norm_g, b_w_out, loss_target, m_a_norm_g, m_a_w_in, m_a_v_norm_g, m_a_w_s, m_a_b_s, m_a_w_out, m_b_norm_g, m_b_w_in, m_b_f_bias, m_b_q_norm_g, m_b_k_norm_g, m_b_w_out, v_a_norm_g, v_a_w_in, v_a_v_norm_g, v_a_w_s, v_a_b_s, v_a_w_out, v_b_norm_g, v_b_w_in, v_b_f_bias, v_b_q_norm_g, v_b_k_norm_g, v_b_w_out):
    t, d = x.shape[1], x.shape[2]
    n_a, n_b = a_w_in.shape[0], b_w_in.shape[0]
    depth = n_a + n_b
    aw = a_w_out.shape[1] * N_DEV
    groups, chunk = a_w_s.shape[1], a_w_s.shape[2]
    heads, hd = b_f_bias.shape[1], b_q_norm_g.shape[1]
    bw = heads * hd
    b_cols = b_w_in.shape[2]
    tq = _pick(t, 512, LANES)
    nq = t // tq
    me = _dev_index((lax.axis_index("x"), lax.axis_index("y"), lax.axis_index("c")))

    pad_h = lambda v: jnp.pad(v, ((0, 0), (0, LANES - heads)))
    f_bias_p = pad_h(b_f_bias)
    b_s_col = a_b_s.reshape(n_a, groups, chunk, 1)
    w_s_t = a_w_s.transpose(0, 1, 3, 2)

    def weight_gather(i, extra=()):
        j = i // 2
        if i % 2 == 0:
            return _Gather([a_w_in[j].astype(BF16), a_w_out[j].astype(BF16)] + [a for a, _ in extra],
                           [1, 0] + [ax for _, ax in extra])
        return _Gather([b_w_in[j].astype(BF16), b_w_out[j].astype(BF16)] + [a for a, _ in extra],
                       [None, 0] + [ax for _, ax in extra])

    def whole_w_in(i, w_in):
        if i % 2 == 1:
            w_in = w_in.transpose(1, 0, 2).reshape(d, 4 * bw + heads)
            w_in = jnp.pad(w_in, ((0, 0), (0, LANES - heads)))
        return w_in

    (w_in0,) = _run_exchange(_Gather([a_w_in[0].astype(BF16)], [1]), "gather_weights")
    weights = {0: [w_in0, None]}
    late = ((a_w_out[0].astype(BF16), 0), (b_norm_g, 1))

    xs = [x[0]]
    saved = []
    for i in range(depth):
        j = i // 2
        xi = xs[-1]
        nxt = weight_gather(i + 1, extra=late if i == 0 else ()) if i + 1 < depth else None
        if i % 2 == 0:
            h = _rms_fwd(xi, a_norm_g[j:j + 1], f"a{j}_norm")
            res = _matmul(h, weights[i][0], "nn", F32, f"a{j}_in", tk=d, exchange=nxt)
        else:
            h = _rms_fwd(xi, b_norm_full[j:j + 1], f"b{j}_norm")
            res = _matmul(h, weights[i][0], "nn", F32, f"b{j}_in", tm=512, tn=1664, tk=d,
                          exchange=nxt)
        if nxt is None:
            pre = res
        else:
            pre = res[0]
            weights[i + 1] = [whole_w_in(i + 1, res[1]), res[2]]
            if i == 0:
                weights[0][1], b_norm_full = res[3], res[4]
        w_in, w_out = weights[i]
        if i % 2 == 0:
            y = _mix_fwd(pre, a_v_norm_g[j:j + 1], a_w_s[j], b_s_col[j], f"a{j}_mix")
            xs.append(_matmul(y, w_out, "nn", F32, f"a{j}_out", residual=xi))
            saved.append((h, pre, y))
        else:
            qn, kn, vb, cum, qsq, ksq, cum_first, cum_last = _attn_prep(
                pre, b_q_norm_g[j:j + 1], b_k_norm_g[j:j + 1], f_bias_p[j:j + 1], heads, hd,
                f"b{j}_prep")
            first = _skip_plan(qsq, ksq, cum_first, cum_last, t, tq, hd, f"b{j}_plan")
            first = first[:, :heads].T
            ck = cum[:, :heads].T.reshape(heads, nq, 1, tq)
            o, y, lse = _flash_fwd(first, qn, kn, vb, ck, pre, heads, hd, tq, f"b{j}_attn")
            xs.append(_matmul(y, w_out, "nn", F32, f"b{j}_out", residual=xi))
            saved.append((h, pre, y, qn, kn, vb, ck, o, lse, first))

    sq, g, gb = _loss_grad(xs[-1], loss_target[0], "loss")
    loss = 0.5 * lax.psum(sq[0, 0], AXES) / d

    d_a_norm, d_a_vnorm, d_a_ws, d_a_bs = [None] * n_a, [None] * n_a, [None] * n_a, [None] * n_a
    d_b_norm, d_b_fb, d_b_gq, d_b_gk = [None] * n_b, [None] * n_b, [None] * n_b, [None] * n_b
    recv_in, recv_out = {}, {}
    pending = None

    def in_blocks(i, dw_in):
        if i % 2 == 0:
            return _AllToAll([dw_in], [1])
        rows = dw_in.shape[0]
        return _AllToAll(
            [dw_in[:, :4 * bw + heads].reshape(rows, N_DEV, b_cols).transpose(1, 0, 2)], [None])

    for i in reversed(range(depth)):
        j = i // 2
        xi = xs[i]
        w_in, w_out = weights[i]
        if i % 2 == 0:
            h, uvz, y = saved[i]
            dy = _matmul(gb, w_out, "nt", F32, f"a{j}_dy", tk=d)
            dw_out = _matmul(y, gb, "tn", BF16, f"a{j}_dwout")
            dpre, d_a_ws[j], dbs, d_a_vnorm[j] = _mix_bwd(
                uvz, dy, a_v_norm_g[j:j + 1], a_w_s[j], w_s_t[j], b_s_col[j], f"a{j}_mixbwd")
            d_a_bs[j] = dbs.reshape(groups, chunk)
            name, gain = f"a{j}", a_norm_g[j:j + 1]
        else:
            h, proj, y, qn, kn, vb, ck, o, lse, first = saved[i]
            dy = _matmul(gb, w_out, "nt", F32, f"b{j}_dy", tk=d)
            dw_out = _matmul(y, gb, "tn", BF16, f"b{j}_dwout")
            do, dz, delta = _attn_bwd_prep(dy, proj, o, heads, hd, f"b{j}_bwdprep")
            dqn, dkn, dv, dcq, dck = _flash_bwd(first, qn, kn, vb, do, ck, lse, delta, heads, hd,
                                                tq, f"b{j}_attnbwd")
            per_token = lambda v: pad_h(v.reshape(heads, t).T)
            dpre, d_b_gq[j], d_b_gk[j], dfb = _attn_bwd_post(
                dqn, dkn, dv, dz, proj, per_token(dcq), per_token(dck), b_q_norm_g[j:j + 1],
                b_k_norm_g[j:j + 1], f_bias_p[j:j + 1], heads, hd, f"b{j}_bwdpost")
            d_b_fb[j] = dfb[:, :heads]
            name, gain = f"b{j}", b_norm_full[j:j + 1]
        if pending is None:
            dh = _matmul(dpre, w_in, "nt", F32, name + "_dh")
        else:
            dh, recv_in[pending[0]] = _matmul(dpre, w_in, "nt", F32, name + "_dh",
                                              exchange=pending[1])
        tn = 1024 if i % 2 == 0 else 1664
        out_blocks = _AllToAll([dw_out], [0])
        if i > 0:
            dw_in, recv_out[i] = _matmul(h, dpre, "tn", BF16, name + "_dwin", tn=tn,
                                         exchange=out_blocks)
            pending = (i, in_blocks(i, dw_in))
        else:
            half = d // 2
            lo, recv_out[i] = _matmul(h[:, :half], dpre, "tn", BF16, name + "_dwin_lo", tn=tn,
                                      exchange=out_blocks)
            hi, recv_lo = _matmul(h[:, half:], dpre, "tn", BF16, name + "_dwin_hi", tn=tn,
                                  exchange=in_blocks(i, lo))
            (recv_hi,) = _run_exchange(in_blocks(i, hi), "exchange_grads")
            recv_in[i] = jnp.concatenate([recv_lo, recv_hi], axis=1)
        g, gb, dgain = _rms_bwd(xi, dh, g, gain, name + "_normbwd")
        if i % 2 == 0:
            d_a_norm[j] = dgain
        else:
            d_b_norm[j] = dgain
    grad_x = g[None]

    small = [jnp.concatenate(d_a_norm, 0), jnp.concatenate(d_a_vnorm, 0), jnp.stack(d_a_ws, 0),
             jnp.stack(d_a_bs, 0), jnp.concatenate(d_b_fb, 0), jnp.concatenate(d_b_gq, 0),
             jnp.concatenate(d_b_gk, 0)]
    small_w = [a_norm_g, a_v_norm_g, a_w_s, a_b_s, b_f_bias, b_q_norm_g, b_k_norm_g]
    small_m = [m_a_norm_g, m_a_v_norm_g, m_a_w_s, m_a_b_s, m_b_f_bias, m_b_q_norm_g, m_b_k_norm_g]
    small_v = [v_a_norm_g, v_a_v_norm_g, v_a_w_s, v_a_b_s, v_b_f_bias, v_b_q_norm_g, v_b_k_norm_g]
    small_flat = _flat_rows(small)
    n_small = small_flat.shape[0]
    (gathered_small,) = _run_exchange(_Gather(
        [jnp.concatenate([small_flat, _flat_rows([jnp.concatenate(d_b_norm, 0)])], axis=0)],
        [None]), "gather_small_grads")
    parts_small = gathered_small[:, :n_small]
    parts_b_norm = gathered_small[:, n_small:n_small + n_b * d // LANES].reshape(N_DEV, n_b, d)
    parts_b_norm = lax.dynamic_slice_in_dim(parts_b_norm, me * (d // N_DEV), d // N_DEV, axis=2)

    a_layers, b_layers = range(0, depth, 2), range(1, depth, 2)
    u_a_in = _adamw(a_w_in, m_a_w_in, v_a_w_in, [recv_in[i] for i in a_layers], "adamw_a_w_in")
    u_a_out = _adamw(a_w_out, m_a_w_out, v_a_w_out, [recv_out[i] for i in a_layers],
                     "adamw_a_w_out")
    u_b_in = _adamw(b_w_in, m_b_w_in, v_b_w_in, [recv_in[i] for i in b_layers], "adamw_b_w_in")
    u_b_out = _adamw(b_w_out, m_b_w_out, v_b_w_out, [recv_out[i] for i in b_layers],
                     "adamw_b_w_out")
    u_b_norm = [o_[0] for o_ in _adamw(b_norm_g[None], m_b_norm_g[None], v_b_norm_g[None],
                                       [parts_b_norm], "adamw_b_norm")]
    u_small = _adamw(_flat_rows(small_w)[None], _flat_rows(small_m)[None],
                     _flat_rows(small_v)[None], [parts_small], "adamw_small")
    shapes = [w.shape for w in small_w]
    u_small = [_unflat_rows(o_[0], shapes) for o_ in u_small]

    def per_kind(k):
        s = u_small[k]
        return [s[0], u_a_in[k], s[1], s[2], s[3], u_a_out[k], u_b_norm[k], u_b_in[k], s[4], s[5],
                s[6], u_b_out[k]]

    return (loss, grad_x, *per_kind(0), *per_kind(1), *per_kind(2), *per_kind(3))
```

```python
import functools

import jax
import jax.numpy as jnp
from jax import lax
from jax.experimental import pallas as pl
from jax.experimental.pallas import tpu as pltpu

F32 = jnp.float32
BF16 = jnp.bfloat16
MESH = pl.DeviceIdType.MESH
AXES = ("x", "y", "c")
N_DEV = 8
NORM_EPS = 1e-6
LANES = 128
VMEM_LIMIT = 56 * 1024 * 1024

ADAM_LR = 0.001
ADAM_B1 = 0.9
ADAM_B2 = 0.999
ADAM_EPS = 1e-08
ADAM_WD = 0.01
ADAM_STEP = 10

LOG2E = 1.4426950408889634
SOFTMAX_ROWS = 64
UNDERFLOW_LOG2 = -160.0

GELU_C0 = 0.7978845608028654
GELU_C1 = 0.044715

NT_DIMS = (((1,), (1,)), ((), ()))
TN_DIMS = (((0,), (0,)), ((), ()))


def _params(sem=None):
    return pltpu.CompilerParams(dimension_semantics=sem, vmem_limit_bytes=VMEM_LIMIT)


def _pick(n, target, unit):
    best = None
    for t in range(unit, min(n, target) + 1, unit):
        if n % t == 0:
            best = t
    return n if best is None else best


def _sigmoid(x):
    return 1.0 / (1.0 + jnp.exp(-x))


def _gelu(x):
    return 0.5 * x * (1.0 + jnp.tanh(GELU_C0 * x * (1.0 + GELU_C1 * x * x)))


def _gelu_and_grad(x):
    x2 = x * x
    t = jnp.tanh(GELU_C0 * x * (1.0 + GELU_C1 * x2))
    g = 0.5 * x * (1.0 + t)
    dg = 0.5 * (1.0 + t) + 0.5 * x * (1.0 - t * t) * (GELU_C0 * (1.0 + 3.0 * GELU_C1 * x2))
    return g, dg


def _dot(a, b):
    return jnp.dot(a, b, preferred_element_type=F32)


def _dot_nt(a, b):
    return lax.dot_general(a, b, NT_DIMS, preferred_element_type=F32)


def _dot_tn(a, b):
    return lax.dot_general(a, b, TN_DIMS, preferred_element_type=F32)


def _split3(v):
    hi = v.astype(BF16)
    r1 = v - hi.astype(F32)
    mid = r1.astype(BF16)
    lo = (r1 - mid.astype(F32)).astype(BF16)
    return hi, mid, lo


def _dev_index(p):
    return 4 * p[0] + 2 * p[1] + p[2]


def _block(ref, idx, axis, size):
    if axis is None:
        return ref.at[idx]
    start = pl.multiple_of(idx * size, size)
    return ref.at[(slice(None),) * axis + (pl.ds(start, size),)]


class _Exchange:
    def __init__(self, arrays, axes):
        self.arrays, self.axes, self.n = list(arrays), list(axes), len(arrays)

    def scratch(self):
        return [pltpu.SemaphoreType.DMA((self.n, 7)), pltpu.SemaphoreType.DMA((self.n, 7)),
                pltpu.SemaphoreType.DMA((self.n,))]

    @staticmethod
    def _place():
        x, y, c = lax.axis_index("x"), lax.axis_index("y"), lax.axis_index("c")
        return x, y, c


class _Gather(_Exchange):
    def out_shapes(self):
        outs = []
        for a, ax in zip(self.arrays, self.axes):
            if ax is None:
                shape = (N_DEV,) + a.shape
            else:
                shape = a.shape[:ax] + (N_DEV * a.shape[ax],) + a.shape[ax + 1:]
            outs.append(jax.ShapeDtypeStruct(shape, a.dtype))
        return outs

    def _copy(self, ins, outs, sems, a, k, block, to, src=None):
        ax = self.axes[a]
        dst = _block(outs[a], _dev_index(block), ax, None if ax is None else self.arrays[a].shape[ax])
        return pltpu.make_async_remote_copy(
            src_ref=dst if src is None else src, dst_ref=dst,
            send_sem=sems[0].at[a, k], recv_sem=sems[1].at[a, k],
            device_id=to, device_id_type=MESH)

    def _mine(self, ins, outs, sems, a, me):
        ax = self.axes[a]
        dst = _block(outs[a], _dev_index(me), ax, None if ax is None else self.arrays[a].shape[ax])
        return pltpu.make_async_copy(ins[a], dst, sems[2].at[a])

    def _first(self, ins, outs, sems):
        x, y, c = self._place()
        me, sibling = (x, y, c), (x, y, 1 - c)
        chips = [(1 - x, y), (x, 1 - y), (1 - x, 1 - y)]
        first = []
        for a in range(self.n):
            first.append(self._copy(ins, outs, sems, a, 0, me, sibling, src=ins[a]))
            first += [self._copy(ins, outs, sems, a, 1 + j, me, (*chip, c), src=ins[a])
                      for j, chip in enumerate(chips)]
        return first

    def start(self, ins, outs, sems):
        me = self._place()
        for a in range(self.n):
            self._mine(ins, outs, sems, a, me).start()
        for cp in self._first(ins, outs, sems):
            cp.start()

    def finish(self, ins, outs, sems):
        x, y, c = self._place()
        me, sibling = (x, y, c), (x, y, 1 - c)
        chips = [(1 - x, y), (x, 1 - y), (1 - x, 1 - y)]
        passed = []
        for a in range(self.n):
            for j, chip in enumerate(chips):
                self._copy(ins, outs, sems, a, 1 + j, (*chip, c), me).wait_recv()
                cp = self._copy(ins, outs, sems, a, 4 + j, (*chip, c), sibling)
                cp.start()
                passed.append(cp)
        for a in range(self.n):
            self._copy(ins, outs, sems, a, 0, sibling, me).wait_recv()
            for j, chip in enumerate(chips):
                self._copy(ins, outs, sems, a, 4 + j, (*chip, 1 - c), me).wait_recv()
        for cp in self._first(ins, outs, sems) + passed:
            cp.wait_send()
        for a in range(self.n):
            self._mine(ins, outs, sems, a, me).wait()


class _AllToAll(_Exchange):
    def _blk_shape(self, a):
        arr, ax = self.arrays[a], self.axes[a]
        if ax is None:
            return arr.shape[1:]
        return arr.shape[:ax] + (arr.shape[ax] // N_DEV,) + arr.shape[ax + 1:]

    def out_shapes(self):
        return [jax.ShapeDtypeStruct((N_DEV,) + self._blk_shape(a), self.arrays[a].dtype)
                for a in range(self.n)]

    def _src(self, ins, a, idx):
        ax = self.axes[a]
        return _block(ins[a], idx, ax, None if ax is None else self.arrays[a].shape[ax] // N_DEV)

    def _peers(self):
        x, y, c = self._place()
        return [((1 - x) if r & 4 else x, (1 - y) if r & 2 else y, (1 - c) if r & 1 else c)
                for r in range(1, N_DEV)]

    def _sends(self, ins, outs, sems):
        me = _dev_index(self._place())
        return [pltpu.make_async_remote_copy(
            src_ref=self._src(ins, a, _dev_index(peer)), dst_ref=outs[a].at[me],
            send_sem=sems[0].at[a, k], recv_sem=sems[1].at[a, k],
            device_id=peer, device_id_type=MESH)
            for a in range(self.n) for k, peer in enumerate(self._peers())]

    def _mine(self, ins, outs, sems):
        me = _dev_index(self._place())
        return [pltpu.make_async_copy(self._src(ins, a, me), outs[a].at[me], sems[2].at[a])
                for a in range(self.n)]

    def start(self, ins, outs, sems):
        for cp in self._mine(ins, outs, sems) + self._sends(ins, outs, sems):
            cp.start()

    def finish(self, ins, outs, sems):
        for a in range(self.n):
            for k, peer in enumerate(self._peers()):
                landed = outs[a].at[_dev_index(peer)]
                pltpu.make_async_remote_copy(
                    src_ref=landed, dst_ref=landed, send_sem=sems[0].at[a, k],
                    recv_sem=sems[1].at[a, k], device_id=peer, device_id_type=MESH).wait_recv()
        for cp in self._sends(ins, outs, sems):
            cp.wait_send()
        for cp in self._mine(ins, outs, sems):
            cp.wait()


def _run_exchange(exchange, name):
    n = exchange.n

    def body(*refs):
        ins, outs, sems = refs[:n], refs[n:2 * n], refs[2 * n:]
        exchange.start(ins, outs, sems)
        exchange.finish(ins, outs, sems)

    any_spec = pl.BlockSpec(memory_space=pl.ANY)
    return pl.pallas_call(
        body, name=name, out_shape=exchange.out_shapes(),
        in_specs=[any_spec] * n, out_specs=[any_spec] * n, scratch_shapes=exchange.scratch(),
    )(*exchange.arrays)


def _matmul(a, b, mode, out_dtype, name, tm=1024, tn=1024, tk=1024, residual=None, exchange=None):
    if mode == "tn":
        kdim, m = a.shape
    else:
        m, kdim = a.shape
    n = b.shape[0] if mode == "nt" else b.shape[1]
    tm, tn, tk = _pick(m, tm, LANES), _pick(n, tn, LANES), _pick(kdim, tk, LANES)
    nk = kdim // tk
    if mode == "tn":
        a_spec = pl.BlockSpec((tk, tm), lambda i, j, k: (k, i))
    else:
        a_spec = pl.BlockSpec((tm, tk), lambda i, j, k: (i, k))
    if mode == "nt":
        b_spec = pl.BlockSpec((tn, tk), lambda i, j, k: (j, k))
    else:
        b_spec = pl.BlockSpec((tk, tn), lambda i, j, k: (k, j))
    o_spec = pl.BlockSpec((tm, tn), lambda i, j, k: (i, j))
    dot = {"nn": _dot, "nt": _dot_nt, "tn": _dot_tn}[mode]
    has_res = residual is not None
    n_in = 3 if has_res else 2
    n_ex = 0 if exchange is None else exchange.n
    ni, nj = m // tm, n // tn

    def body(*refs):
        a_ref, b_ref = refs[:2]
        r_ref = refs[2] if has_res else None
        ex_ins = refs[n_in:n_in + n_ex]
        o_ref = refs[n_in + n_ex]
        ex_outs = refs[n_in + n_ex + 1:n_in + 2 * n_ex + 1]
        scratch = refs[n_in + 2 * n_ex + 1:]
        i, j, k = pl.program_id(0), pl.program_id(1), pl.program_id(2)

        if exchange is not None:
            sems = scratch[-3:]

            @pl.when((i == 0) & (j == 0) & (k == 0))
            def _():
                exchange.start(ex_ins, ex_outs, sems)

        def finish(acc):
            if has_res:
                acc = acc + r_ref[...]
            o_ref[...] = acc.astype(out_dtype)

        if nk == 1:
            finish(dot(a_ref[...], b_ref[...]))
        else:
            acc_ref = scratch[0]

            @pl.when(k == 0)
            def _():
                acc_ref[...] = jnp.zeros_like(acc_ref)

            acc_ref[...] += dot(a_ref[...], b_ref[...])

            @pl.when(k == nk - 1)
            def _():
                finish(acc_ref[...])

        if exchange is not None:
            @pl.when((i == ni - 1) & (j == nj - 1) & (k == nk - 1))
            def _():
                exchange.finish(ex_ins, ex_outs, sems)

    any_spec = pl.BlockSpec(memory_space=pl.ANY)
    operands = (a, b, residual) if has_res else (a, b)
    out_shape = jax.ShapeDtypeStruct((m, n), out_dtype)
    scratch_shapes = [] if nk == 1 else [pltpu.VMEM((tm, tn), F32)]
    if exchange is None:
        return pl.pallas_call(
            body, name=name, grid=(ni, nj, nk),
            in_specs=[a_spec, b_spec] + ([o_spec] if has_res else []),
            out_specs=o_spec, out_shape=out_shape, scratch_shapes=scratch_shapes,
            compiler_params=_params(("parallel", "parallel", "arbitrary")),
        )(*operands)
    return pl.pallas_call(
        body, name=name, grid=(ni, nj, nk),
        in_specs=[a_spec, b_spec] + ([o_spec] if has_res else []) + [any_spec] * n_ex,
        out_specs=[o_spec] + [any_spec] * n_ex,
        out_shape=[out_shape] + exchange.out_shapes(),
        scratch_shapes=scratch_shapes + exchange.scratch(),
        compiler_params=_params(("arbitrary", "arbitrary", "arbitrary")),
    )(*operands, *exchange.arrays)


def _rms_fwd(x, gain, name):
    t, d = x.shape
    tr = _pick(t, 256, 16)

    def body(x_ref, g_ref, h_ref):
        xv = x_ref[...]
        r = lax.rsqrt(jnp.mean(xv * xv, axis=1, keepdims=True) + NORM_EPS)
        h_ref[...] = (xv * r * g_ref[...]).astype(BF16)

    row = pl.BlockSpec((tr, d), lambda i: (i, 0))
    return pl.pallas_call(
        body, name=name, grid=(t // tr,),
        in_specs=[row, pl.BlockSpec((1, d), lambda i: (0, 0))],
        out_specs=row, out_shape=jax.ShapeDtypeStruct((t, d), BF16),
        compiler_params=_params(("arbitrary",)),
    )(x, gain)


def _rms_bwd(x, dh, g_in, gain, name):
    t, d = x.shape
    tr = _pick(t, 256, 16)

    def body(x_ref, dh_ref, gin_ref, g_ref, dx_ref, dxb_ref, dg_ref):
        i = pl.program_id(0)
        xv, dhv = x_ref[...], dh_ref[...]
        r = lax.rsqrt(jnp.mean(xv * xv, axis=1, keepdims=True) + NORM_EPS)
        xh = xv * r
        dxh = dhv * g_ref[...]
        dx = gin_ref[...] + r * (dxh - xh * jnp.mean(dxh * xh, axis=1, keepdims=True))
        dx_ref[...] = dx
        dxb_ref[...] = dx.astype(BF16)

        @pl.when(i == 0)
        def _():
            dg_ref[...] = jnp.zeros_like(dg_ref)

        dg_ref[...] += jnp.sum(dhv * xh, axis=0, keepdims=True)

    row = pl.BlockSpec((tr, d), lambda i: (i, 0))
    vec = pl.BlockSpec((1, d), lambda i: (0, 0))
    return pl.pallas_call(
        body, name=name, grid=(t // tr,),
        in_specs=[row, row, row, vec],
        out_specs=[row, row, vec],
        out_shape=[jax.ShapeDtypeStruct((t, d), F32), jax.ShapeDtypeStruct((t, d), BF16),
                   jax.ShapeDtypeStruct((1, d), F32)],
        compiler_params=_params(("arbitrary",)),
    )(x, dh, g_in, gain)


def _loss_grad(y, target, name):
    t, d = y.shape
    tr = _pick(t, 256, 16)

    def body(y_ref, t_ref, s_ref, g_ref, gb_ref):
        i = pl.program_id(0)
        e = y_ref[...] - t_ref[...]
        g = e * (1.0 / d)
        g_ref[...] = g
        gb_ref[...] = g.astype(BF16)

        @pl.when(i == 0)
        def _():
            s_ref[...] = jnp.zeros_like(s_ref)

        s_ref[...] += jnp.sum(jnp.sum(e * e, axis=1, keepdims=True), axis=0, keepdims=True)

    row = pl.BlockSpec((tr, d), lambda i: (i, 0))
    return pl.pallas_call(
        body, name=name, grid=(t // tr,),
        in_specs=[row, row],
        out_specs=[pl.BlockSpec((1, 1), lambda i: (0, 0)), row, row],
        out_shape=[jax.ShapeDtypeStruct((1, 1), F32), jax.ShapeDtypeStruct((t, d), F32),
                   jax.ShapeDtypeStruct((t, d), BF16)],
        compiler_params=_params(("arbitrary",)),
    )(y, target)


def _causal_weights(ws_ref, g, chunk, transposed):
    rows = lax.broadcasted_iota(jnp.int32, (chunk, chunk), 0)
    cols = lax.broadcasted_iota(jnp.int32, (chunk, chunk), 1)
    keep = (cols >= rows) if transposed else (rows >= cols)
    return jnp.where(keep, ws_ref[g], 0.0).astype(BF16)


def _mix_fwd(uvz, v_gain, w_s, b_s, name):
    t, w3 = uvz.shape
    w = w3 // 3
    groups, chunk = w_s.shape[0], w_s.shape[1]
    gd = w // groups

    def body(uvz_ref, gam_ref, ws_ref, bs_ref, y_ref):
        gv = _gelu(uvz_ref[:, w:2 * w])
        r = lax.rsqrt(jnp.mean(gv * gv, axis=1, keepdims=True) + NORM_EPS)
        vn = (gv * r * gam_ref[...]).astype(BF16)
        for g in range(groups):
            sl = slice(g * gd, (g + 1) * gd)
            mixed = _dot(_causal_weights(ws_ref, g, chunk, False), vn[:, sl]) + bs_ref[g]
            u = uvz_ref[:, g * gd:(g + 1) * gd]
            z = uvz_ref[:, 2 * w + g * gd:2 * w + (g + 1) * gd]
            y_ref[:, sl] = (_gelu(u) * mixed * (z * _sigmoid(z))).astype(BF16)

    return pl.pallas_call(
        body, name=name, grid=(t // chunk,),
        in_specs=[pl.BlockSpec((chunk, w3), lambda i: (i, 0)),
                  pl.BlockSpec((1, w), lambda i: (0, 0)),
                  pl.BlockSpec((groups, chunk, chunk), lambda i: (0, 0, 0)),
                  pl.BlockSpec((groups, chunk, 1), lambda i: (0, 0, 0))],
        out_specs=pl.BlockSpec((chunk, w), lambda i: (i, 0)),
        out_shape=jax.ShapeDtypeStruct((t, w), BF16),
        compiler_params=_params(("arbitrary",)),
    )(uvz, v_gain, w_s, b_s)


def _mix_bwd(uvz, dy, v_gain, w_s, w_s_t, b_s, name):
    t, w3 = uvz.shape
    w = w3 // 3
    groups, chunk = w_s.shape[0], w_s.shape[1]
    gd = w // groups

    def body(uvz_ref, dy_ref, gam_ref, ws_ref, wst_ref, bs_ref, d_ref, dws_ref, dbs_ref, dgam_ref,
             dvn_ref):
        i = pl.program_id(0)

        @pl.when(i == 0)
        def _():
            dws_ref[...] = jnp.zeros_like(dws_ref)
            dbs_ref[...] = jnp.zeros_like(dbs_ref)
            dgam_ref[...] = jnp.zeros_like(dgam_ref)

        gv, dgv = _gelu_and_grad(uvz_ref[:, w:2 * w])
        r = lax.rsqrt(jnp.mean(gv * gv, axis=1, keepdims=True) + NORM_EPS)
        vh = gv * r
        gam = gam_ref[...]
        vn = (vh * gam).astype(BF16)
        rows = lax.broadcasted_iota(jnp.int32, (chunk, chunk), 0)
        cols = lax.broadcasted_iota(jnp.int32, (chunk, chunk), 1)
        for g in range(groups):
            sl = slice(g * gd, (g + 1) * gd)
            mixed = _dot(_causal_weights(ws_ref, g, chunk, False), vn[:, sl]) + bs_ref[g]
            gu, dgu = _gelu_and_grad(uvz_ref[:, g * gd:(g + 1) * gd])
            z = uvz_ref[:, 2 * w + g * gd:2 * w + (g + 1) * gd]
            sz = _sigmoid(z)
            silu = z * sz
            dyv = dy_ref[:, sl]
            dmixed = dyv * gu * silu
            d_ref[:, sl] = (dyv * mixed * silu * dgu).astype(BF16)
            d_ref[:, 2 * w + g * gd:2 * w + (g + 1) * gd] = (
                dyv * gu * mixed * (sz * (1.0 + z * (1.0 - sz)))).astype(BF16)
            dmb = dmixed.astype(BF16)
            dws_ref[g] += jnp.where(rows >= cols, _dot_nt(dmb, vn[:, sl]), 0.0)
            dbs_ref[g] += jnp.sum(dmixed, axis=1, keepdims=True)
            dvn_ref[:, sl] = _dot(_causal_weights(wst_ref, g, chunk, True), dmb)
        dvn = dvn_ref[...]
        dgam_ref[...] += jnp.sum(dvn * vh, axis=0, keepdims=True)
        dvh = dvn * gam
        dgvv = r * (dvh - vh * jnp.mean(dvh * vh, axis=1, keepdims=True))
        d_ref[:, w:2 * w] = (dgvv * dgv).astype(BF16)

    return pl.pallas_call(
        body, name=name, grid=(t // chunk,),
        in_specs=[pl.BlockSpec((chunk, w3), lambda i: (i, 0)),
                  pl.BlockSpec((chunk, w), lambda i: (i, 0)),
                  pl.BlockSpec((1, w), lambda i: (0, 0)),
                  pl.BlockSpec((groups, chunk, chunk), lambda i: (0, 0, 0)),
                  pl.BlockSpec((groups, chunk, chunk), lambda i: (0, 0, 0)),
                  pl.BlockSpec((groups, chunk, 1), lambda i: (0, 0, 0))],
        out_specs=[pl.BlockSpec((chunk, w3), lambda i: (i, 0)),
                   pl.BlockSpec((groups, chunk, chunk), lambda i: (0, 0, 0)),
                   pl.BlockSpec((groups, chunk, 1), lambda i: (0, 0, 0)),
                   pl.BlockSpec((1, w), lambda i: (0, 0))],
        out_shape=[jax.ShapeDtypeStruct((t, w3), BF16),
                   jax.ShapeDtypeStruct((groups, chunk, chunk), F32),
                   jax.ShapeDtypeStruct((groups, chunk, 1), F32),
                   jax.ShapeDtypeStruct((1, w), F32)],
        scratch_shapes=[pltpu.VMEM((chunk, w), F32)],
        compiler_params=_params(("arbitrary",)),
    )(uvz, dy, v_gain, w_s, w_s_t, b_s)


def _attn_prep(proj, q_gain, k_gain, f_bias, heads, hd, name):
    t = proj.shape[0]
    bw = heads * hd
    tr = _pick(t, 256, 16)
    fcol = 4 * bw // LANES

    def body(q_ref, k_ref, v_ref, f_ref, gq_ref, gk_ref, fb_ref, qn_ref, kn_ref, vb_ref, cum_ref,
             qsq_ref, ksq_ref, first_ref, last_ref, carry_ref):
        i = pl.program_id(0)

        @pl.when(i == 0)
        def _():
            carry_ref[...] = jnp.zeros_like(carry_ref)

        lane = lax.broadcasted_iota(jnp.int32, (1, LANES), 1)
        for src, gain, dst, sq_ref in ((q_ref, gq_ref, qn_ref, qsq_ref),
                                       (k_ref, gk_ref, kn_ref, ksq_ref)):
            sq_row = jnp.zeros((1, LANES), F32)
            for h in range(heads):
                sl = slice(h * hd, (h + 1) * hd)
                v = src[:, sl]
                r = lax.rsqrt(jnp.mean(v * v, axis=1, keepdims=True) + NORM_EPS)
                normed = (v * r * gain[...]).astype(BF16)
                dst[:, sl] = normed
                nf = normed.astype(F32)
                sq = jnp.max(jnp.sum(nf * nf, axis=1, keepdims=True), axis=0, keepdims=True)
                sq_row = jnp.where(lane == h, sq, sq_row)
            sq_ref[0] = sq_row
        vb_ref[...] = v_ref[...].astype(BF16)
        fl = f_ref[...] + fb_ref[...]
        log_f = jnp.minimum(fl, 0.0) - jnp.log(1.0 + jnp.exp(-jnp.abs(fl)))
        rows = lax.broadcasted_iota(jnp.int32, (tr, tr), 0)
        cols = lax.broadcasted_iota(jnp.int32, (tr, tr), 1)
        lower = jnp.where(rows >= cols, 1.0, 0.0).astype(BF16)
        hi, mid, lo = _split3(log_f)
        cum_ref[...] = (_dot(lower, hi) + _dot(lower, mid) + _dot(lower, lo)) + carry_ref[...]
        carry_ref[...] = cum_ref[tr - 1:tr, :]
        first_ref[0] = cum_ref[0:1, :]
        last_ref[0] = cum_ref[tr - 1:tr, :]

    wide = lambda col: pl.BlockSpec((tr, bw), lambda i: (i, col))
    vec = pl.BlockSpec((1, hd), lambda i: (0, 0))
    stat = pl.BlockSpec((1, 1, LANES), lambda i: (i, 0, 0))
    return pl.pallas_call(
        body, name=name, grid=(t // tr,),
        in_specs=[wide(0), wide(1), wide(2), pl.BlockSpec((tr, LANES), lambda i: (i, fcol)),
                  vec, vec, pl.BlockSpec((1, LANES), lambda i: (0, 0))],
        out_specs=[wide(0), wide(0), wide(0), pl.BlockSpec((tr, LANES), lambda i: (i, 0))]
        + [stat] * 4,
        out_shape=[jax.ShapeDtypeStruct((t, bw), BF16)] * 3 + [jax.ShapeDtypeStruct((t, LANES), F32)]
        + [jax.ShapeDtypeStruct((t // tr, 1, LANES), F32)] * 4,
        scratch_shapes=[pltpu.VMEM((1, LANES), F32)],
        compiler_params=_params(("arbitrary",)),
    )(proj, proj, proj, proj, q_gain, k_gain, f_bias)


def _skip_plan(qsq, ksq, first, last, t, tq, hd, name):
    nt = qsq.shape[0]
    nq = t // tq
    r = nt // nq
    scale2 = hd ** -0.5 * LOG2E

    def body(qsq_ref, ksq_ref, first_ref, last_ref, out_ref):
        kmax = ksq_ref[0]
        for tile in range(1, nt):
            kmax = jnp.maximum(kmax, ksq_ref[tile])
        for i in range(nq):
            qmax = qsq_ref[i * r]
            for tile in range(i * r + 1, (i + 1) * r):
                qmax = jnp.maximum(qmax, qsq_ref[tile])
            coef = 2.0 * scale2 * jnp.sqrt(qmax * kmax)
            start = first_ref[i * r]
            count = jnp.zeros((1, LANES), jnp.int32)
            for j in range(i):
                bound = coef + (start - last_ref[(j + 1) * r - 1]) * LOG2E
                count = count + jnp.where(bound <= UNDERFLOW_LOG2, 1, 0)
            out_ref[i:i + 1, :] = count

    return pl.pallas_call(
        body, name=name, out_shape=jax.ShapeDtypeStruct((nq, LANES), jnp.int32),
    )(qsq, ksq, first, last)


def _flash_fwd(first_block, qn, kn, vb, ck, proj, heads, hd, tq, name):
    t = qn.shape[0]
    nq = t // tq
    th = tq // 2
    scale2 = hd ** -0.5 * LOG2E
    zcol = 3 * heads
    rc = _pick(tq, SOFTMAX_ROWS, 16)
    reps = th // LANES

    def body(first_ref, q_ref, k_ref, v_ref, ck_ref, z_ref, o_ref, y_ref, lse_ref,
             m_s, l_s, acc_s, s_a, s_b, p_a, p_b, al_a, al_b):
        i = pl.program_id(1)
        j0 = first_ref[pl.program_id(0), i]
        bufs = ((s_a, p_a, al_a), (s_b, p_b, al_b))
        m_s[...] = jnp.full_like(m_s, -jnp.inf)
        l_s[...] = jnp.zeros_like(l_s)
        acc_s[...] = jnp.zeros_like(acc_s)
        p_b[...] = jnp.zeros_like(p_b)
        al_b[...] = jnp.ones_like(al_b)

        def scores(j, half):
            off = pl.multiple_of(j * tq + half * th, th)
            bufs[half][0][...] = _dot_nt(q_ref[...], k_ref[pl.ds(off, th), :])

        def values(j, half):
            off = pl.multiple_of(j * tq + half * th, th)
            _, p_buf, al = bufs[half]
            acc_s[...] = (jnp.tile(al[...], (1, hd // LANES)) * acc_s[...]
                          + _dot(p_buf[...], v_ref[pl.ds(off, th), :]))

        def softmax(j, half, masked):
            s_buf, p_buf, al = bufs[half]
            ck2 = ck_ref[0, j][:, half * th:(half + 1) * th] * LOG2E
            for c in range(tq // rc):
                r = slice(c * rc, (c + 1) * rc)
                s = s_buf[r, :] * scale2 - ck2
                if masked:
                    rows = lax.broadcasted_iota(jnp.int32, (rc, th), 0) + c * rc
                    cols = lax.broadcasted_iota(jnp.int32, (rc, th), 1) + half * th
                    s = jnp.where(rows >= cols, s, -jnp.inf)
                s_buf[r, :] = s
                m_prev = m_s[r, :]
                m_new = jnp.maximum(m_prev, jnp.max(s, axis=1, keepdims=True))
                al[r, :] = jnp.exp2(m_prev - m_new)
                m_s[r, :] = m_new
            for c in range(tq // rc):
                r = slice(c * rc, (c + 1) * rc)
                p = jnp.exp2(s_buf[r, :] - jnp.tile(m_s[r, :], (1, reps)))
                p_buf[r, :] = p.astype(BF16)
                lane_sum = p[:, 0:LANES]
                for b in range(1, reps):
                    lane_sum = lane_sum + p[:, b * LANES:(b + 1) * LANES]
                l_s[r, :] = al[r, :] * l_s[r, :] + lane_sum

        scores(j0, 0)

        def loop_body(j, carry):
            scores(j, 1)
            values(jnp.maximum(j - 1, 0), 1)
            softmax(j, 0, False)
            scores(j + 1, 0)
            values(j, 0)
            softmax(j, 1, False)
            return carry

        lax.fori_loop(j0, i, loop_body, 0)
        scores(i, 1)
        values(jnp.maximum(i - 1, 0), 1)
        softmax(i, 0, True)
        values(i, 0)
        softmax(i, 1, True)
        values(i, 1)
        l = jnp.sum(l_s[...], axis=1, keepdims=True)
        o = acc_s[...] / l
        z = z_ref[...]
        o_ref[...] = o
        y_ref[...] = (o * (z * _sigmoid(z))).astype(BF16)
        lse_ref[0] = m_s[:, 0:1] + jnp.log(l) * LOG2E

    blk = pl.BlockSpec((tq, hd), lambda h, i, first: (i, h))
    head = pl.BlockSpec((t, hd), lambda h, i, first: (0, h))
    col = pl.BlockSpec((1, tq, 1), lambda h, i, first: (h, i, 0))
    stat = pltpu.VMEM((tq, LANES), F32)
    return pl.pallas_call(
        body, name=name,
        grid_spec=pltpu.PrefetchScalarGridSpec(
            num_scalar_prefetch=1, grid=(heads, nq),
            in_specs=[blk, head, head,
                      pl.BlockSpec((1, nq, 1, tq), lambda h, i, first: (h, 0, 0, 0)),
                      pl.BlockSpec((tq, hd), lambda h, i, first: (i, zcol + h))],
            out_specs=[blk, blk, col],
            scratch_shapes=[stat, stat, pltpu.VMEM((tq, hd), F32),
                            pltpu.VMEM((tq, th), F32), pltpu.VMEM((tq, th), F32),
                            pltpu.VMEM((tq, th), BF16), pltpu.VMEM((tq, th), BF16), stat, stat]),
        out_shape=[jax.ShapeDtypeStruct((t, heads * hd), F32),
                   jax.ShapeDtypeStruct((t, heads * hd), BF16),
                   jax.ShapeDtypeStruct((heads, t, 1), F32)],
        compiler_params=_params(("arbitrary", "arbitrary")),
    )(first_block, qn, kn, vb, ck, proj)


def _attn_bwd_prep(dy, proj, o, heads, hd, name):
    t, bw = dy.shape
    tr = _pick(t, 256, 16)

    def body(dy_ref, z_ref, o_ref, do_ref, dz_ref, delta_ref):
        dyv, z, ov = dy_ref[...], z_ref[...], o_ref[...]
        sz = _sigmoid(z)
        do = dyv * (z * sz)
        do_ref[...] = do.astype(BF16)
        dz_ref[...] = (dyv * ov * (sz * (1.0 + z * (1.0 - sz)))).astype(BF16)
        prod = do * ov
        for h in range(heads):
            delta_ref[h] = jnp.sum(prod[:, h * hd:(h + 1) * hd], axis=1, keepdims=True)

    row = pl.BlockSpec((tr, bw), lambda i: (i, 0))
    return pl.pallas_call(
        body, name=name, grid=(t // tr,),
        in_specs=[row, pl.BlockSpec((tr, bw), lambda i: (i, 3)), row],
        out_specs=[row, row, pl.BlockSpec((heads, tr, 1), lambda i: (0, i, 0))],
        out_shape=[jax.ShapeDtypeStruct((t, bw), BF16), jax.ShapeDtypeStruct((t, bw), BF16),
                   jax.ShapeDtypeStruct((heads, t, 1), F32)],
        compiler_params=_params(("arbitrary",)),
    )(dy, proj, o)


def _flash_bwd(first_block, qn, kn, vb, do, ck, lse, delta, heads, hd, tq, name):
    t = qn.shape[0]
    nq = t // tq
    scale = hd ** -0.5
    scale2 = scale * LOG2E

    def body(first_ref, q_ref, k_ref, v_ref, do_ref, ck_ref, lse_ref, delta_ref,
             dq_ref, dk_ref, dv_ref, dcq_ref, dck_ref, dq_s, dcq_s):
        i = pl.program_id(1)
        j0 = first_ref[pl.program_id(0), i]

        @pl.when(i == 0)
        def _():
            dk_ref[...] = jnp.zeros_like(dk_ref)
            dv_ref[...] = jnp.zeros_like(dv_ref)
            dck_ref[...] = jnp.zeros_like(dck_ref)

        q, dov = q_ref[...], do_ref[...]
        lsev, deltav = lse_ref[0], delta_ref[0]
        dq_s[...] = jnp.zeros_like(dq_s)
        dcq_s[...] = jnp.zeros_like(dcq_s)

        def step(j, masked):
            off = pl.multiple_of(j * tq, tq)
            kblk = k_ref[pl.ds(off, tq), :]
            s = _dot_nt(q, kblk) * scale2 - ck_ref[0, j] * LOG2E
            if masked:
                rows = lax.broadcasted_iota(jnp.int32, (tq, tq), 0)
                cols = lax.broadcasted_iota(jnp.int32, (tq, tq), 1)
                s = jnp.where(rows >= cols, s, -jnp.inf)
            p = jnp.exp2(s - lsev)
            dp = _dot_nt(dov, v_ref[pl.ds(off, tq), :])
            ds = p * (dp - deltav)
            dsb = ds.astype(BF16)
            dv_ref[pl.ds(off, tq), :] += _dot_tn(p.astype(BF16), dov)
            dk_ref[pl.ds(off, tq), :] += _dot_tn(dsb, q) * scale
            dq_s[...] += _dot(dsb, kblk) * scale
            dcq_s[...] += jnp.sum(ds, axis=1, keepdims=True)
            dck_ref[0, j] += jnp.sum(ds, axis=0, keepdims=True)

        def loop_body(j, carry):
            step(j, False)
            return carry

        lax.fori_loop(j0, i, loop_body, 0)
        step(i, True)
        dq_ref[...] = dq_s[...]
        dcq_ref[0] = dcq_s[...]

    blk = pl.BlockSpec((tq, hd), lambda h, i, first: (i, h))
    head = pl.BlockSpec((t, hd), lambda h, i, first: (0, h))
    col = pl.BlockSpec((1, tq, 1), lambda h, i, first: (h, i, 0))
    rowv = pl.BlockSpec((1, nq, 1, tq), lambda h, i, first: (h, 0, 0, 0))
    full = jax.ShapeDtypeStruct((t, heads * hd), F32)
    return pl.pallas_call(
        body, name=name,
        grid_spec=pltpu.PrefetchScalarGridSpec(
            num_scalar_prefetch=1, grid=(heads, nq),
            in_specs=[blk, head, head, blk, rowv, col, col],
            out_specs=[blk, head, head, col, rowv],
            scratch_shapes=[pltpu.VMEM((tq, hd), F32), pltpu.VMEM((tq, 1), F32)]),
        out_shape=[full, full, full, jax.ShapeDtypeStruct((heads, t, 1), F32),
                   jax.ShapeDtypeStruct((heads, nq, 1, tq), F32)],
        compiler_params=_params(("arbitrary", "arbitrary")),
    )(first_block, qn, kn, vb, do, ck, lse, delta)


def _attn_bwd_post(dqn, dkn, dv, dz, proj, dcq, dck, q_gain, k_gain, f_bias, heads, hd, name):
    t, bw = dqn.shape
    tr = _pick(t, 128, 16)
    nb = t // tr
    fcol = 4 * bw // LANES
    width = 4 * bw + LANES

    def body(dq_ref, dk_ref, dv_ref, dz_ref, q_ref, k_ref, f_ref, dcq_ref, dck_ref, gq_ref, gk_ref,
             fb_ref, d_ref, dgq_ref, dgk_ref, dfb_ref, carry_ref, rc_ref):
        i = pl.program_id(0)

        @pl.when(i == 0)
        def _():
            carry_ref[...] = jnp.zeros_like(carry_ref)
            dgq_ref[...] = jnp.zeros_like(dgq_ref)
            dgk_ref[...] = jnp.zeros_like(dgk_ref)
            dfb_ref[...] = jnp.zeros_like(dfb_ref)

        for idx, (g_ref, raw_ref, gain_ref, dgain_ref) in enumerate(
                ((dq_ref, q_ref, gq_ref, dgq_ref), (dk_ref, k_ref, gk_ref, dgk_ref))):
            gain = gain_ref[...]
            dgain = jnp.zeros((1, hd), F32)
            for h in range(heads):
                sl = slice(h * hd, (h + 1) * hd)
                v, dn = raw_ref[:, sl], g_ref[:, sl]
                r = lax.rsqrt(jnp.mean(v * v, axis=1, keepdims=True) + NORM_EPS)
                vh = v * r
                dgain = dgain + jnp.sum(dn * vh, axis=0, keepdims=True)
                dvh = dn * gain
                draw = r * (dvh - vh * jnp.mean(dvh * vh, axis=1, keepdims=True))
                d_ref[:, idx * bw + h * hd:idx * bw + (h + 1) * hd] = draw.astype(BF16)
            dgain_ref[...] += dgain
        d_ref[:, 2 * bw:3 * bw] = dv_ref[...].astype(BF16)
        d_ref[:, 3 * bw:4 * bw] = dz_ref[...]
        rows = lax.broadcasted_iota(jnp.int32, (tr, tr), 0)
        cols = lax.broadcasted_iota(jnp.int32, (tr, tr), 1)
        upper = jnp.where(cols >= rows, 1.0, 0.0).astype(BF16)
        hi, mid, lo = _split3(dcq_ref[...] - dck_ref[...])
        rc_ref[...] = (_dot(upper, hi) + _dot(upper, mid) + _dot(upper, lo)) + carry_ref[...]
        carry_ref[...] = rc_ref[0:1, :]
        df = rc_ref[...] * (1.0 / (1.0 + jnp.exp(f_ref[...] + fb_ref[...])))
        d_ref[:, 4 * bw:] = df.astype(BF16)
        dfb_ref[...] += jnp.sum(df, axis=0, keepdims=True)

    wide = lambda col: pl.BlockSpec((tr, bw), lambda i: (nb - 1 - i, col))
    lane = lambda col: pl.BlockSpec((tr, LANES), lambda i: (nb - 1 - i, col))
    vec = pl.BlockSpec((1, hd), lambda i: (0, 0))
    vecl = pl.BlockSpec((1, LANES), lambda i: (0, 0))
    return pl.pallas_call(
        body, name=name, grid=(nb,),
        in_specs=[wide(0), wide(0), wide(0), wide(0), wide(0), wide(1), lane(fcol), lane(0), lane(0),
                  vec, vec, vecl],
        out_specs=[pl.BlockSpec((tr, width), lambda i: (nb - 1 - i, 0)), vec, vec, vecl],
        out_shape=[jax.ShapeDtypeStruct((t, width), BF16), jax.ShapeDtypeStruct((1, hd), F32),
                   jax.ShapeDtypeStruct((1, hd), F32), jax.ShapeDtypeStruct((1, LANES), F32)],
        scratch_shapes=[pltpu.VMEM((1, LANES), F32), pltpu.VMEM((tr, LANES), F32)],
        compiler_params=_params(("arbitrary",)),
    )(dqn, dkn, dv, dz, proj, proj, proj, dcq, dck, q_gain, k_gain, f_bias)


def _adamw(w, m, v, parts, name):
    nl, r, c = w.shape
    itemsize = parts[0].dtype.itemsize
    unit = 32 // itemsize
    row_bytes = c * (7 * 4 + N_DEV * itemsize * nl)
    tr = _pick(r, max(unit, 12 * 1024 * 1024 // row_bytes), unit)
    nr = r // tr
    c1 = 1.0 / (1.0 - ADAM_B1 ** ADAM_STEP)
    c2 = 1.0 / (1.0 - ADAM_B2 ** ADAM_STEP)

    def body(*refs):
        w_ref, m_ref, v_ref = refs[:3]
        p_refs = refs[3:3 + nl]
        g_ref, d_ref, nm_ref, nv_ref = refs[3 + nl:]
        layer = pl.program_id(0)

        def partial(j):
            p = p_refs[0][j].astype(F32)
            for q in range(1, nl):
                p = jnp.where(layer == q, p_refs[q][j].astype(F32), p)
            return p

        g = partial(0)
        for j in range(1, N_DEV):
            g = g + partial(j)
        nm = ADAM_B1 * m_ref[0] + (1.0 - ADAM_B1) * g
        nv = ADAM_B2 * v_ref[0] + (1.0 - ADAM_B2) * (g * g)
        g_ref[0] = g
        nm_ref[0] = nm
        nv_ref[0] = nv
        d_ref[0] = -ADAM_LR * ((nm * c1) / (jnp.sqrt(nv * c2) + ADAM_EPS) + ADAM_WD * w_ref[0])

    def part_spec(q):
        rest = 0 if q > 0 else nr - 1
        return pl.BlockSpec((N_DEV, tr, c), lambda l, i: (0, jnp.where(l == q, i, rest), 0))

    row = pl.BlockSpec((1, tr, c), lambda l, i: (l, i, 0))
    return pl.pallas_call(
        body, name=name, grid=(nl, nr),
        in_specs=[row, row, row] + [part_spec(q) for q in range(nl)],
        out_specs=[row] * 4,
        out_shape=[jax.ShapeDtypeStruct((nl, r, c), F32)] * 4,
        compiler_params=_params(("arbitrary", "arbitrary")),
    )(w, m, v, *parts)


def _flat_rows(pieces):
    rows = []
    for p in pieces:
        f = p.reshape(-1)
        f = jnp.pad(f, (0, (-f.shape[0]) % LANES))
        rows.append(f.reshape(-1, LANES))
    out = jnp.concatenate(rows, axis=0)
    return jnp.pad(out, ((0, (-out.shape[0]) % 8), (0, 0)))


def _unflat_rows(flat, shapes):
    outs, r0 = [], 0
    lead = flat.shape[:-2]
    for s in shapes:
        size = 1
        for d in s:
            size *= d
        nr = -(-size // LANES)
        piece = flat[..., r0:r0 + nr, :].reshape(lead + (nr * LANES,))[..., :size]
        outs.append(piece.reshape(lead + tuple(s)))
        r0 += nr
    return outs


def kernel(x, a_norm_g, a_w_in, a_v_norm_g, a_w_s, a_b_s, a_w_out, b_norm_g, b_w_in, b_f_bias, b_q_norm_g, b_k_norm_g, b_w_out, loss_target, m_a_norm_g, m_a_w_in, m_a_v_norm_g, m_a_w_s, m_a_b_s, m_a_w_out, m_b_norm_g, m_b_w_in, m_b_f_bias, m_b_q_norm_g, m_b_k_norm_g, m_b_w_out, v_a_norm_g, v_a_w_in, v_a_v_norm_g, v_a_w_s, v_a_b_s, v_a_w_out, v_b_norm_g, v_b_w_in, v_b_f_bias, v_b_q_norm_g, v_b_k_norm_g, v_b_w_out):
    t, d = x.shape[1], x.shape[2]
    n_a, n_b = a_w_in.shape[0], b_w_in.shape[0]
    depth = n_a + n_b
    aw = a_w_out.shape[1] * N_DEV
    groups, chunk = a_w_s.shape[1], a_w_s.shape[2]
    heads, hd = b_f_bias.shape[1], b_q_norm_g.shape[1]
    bw = heads * hd
    b_cols = b_w_in.shape[2]
    tq = _pick(t, 512, LANES)
    nq = t // tq
    me = _dev_index((lax.axis_index("x"), lax.axis_index("y"), lax.axis_index("c")))

    pad_h = lambda v: jnp.pad(v, ((0, 0), (0, LANES - heads)))
    f_bias_p = pad_h(b_f_bias)
    b_s_col = a_b_s.reshape(n_a, groups, chunk, 1)
    w_s_t = a_w_s.transpose(0, 1, 3, 2)

    def weight_gather(i, extra=()):
        j = i // 2
        if i % 2 == 0:
            return _Gather([a_w_in[j].astype(BF16), a_w_out[j].astype(BF16)] + [a for a, _ in extra],
                           [1, 0] + [ax for _, ax in extra])
        return _Gather([b_w_in[j].astype(BF16), b_w_out[j].astype(BF16)] + [a for a, _ in extra],
                       [None, 0] + [ax for _, ax in extra])

    def whole_w_in(i, w_in):
        if i % 2 == 1:
            w_in = w_in.transpose(1, 0, 2).reshape(d, 4 * bw + heads)
            w_in = jnp.pad(w_in, ((0, 0), (0, LANES - heads)))
        return w_in

    (w_in0,) = _run_exchange(_Gather([a_w_in[0].astype(BF16)], [1]), "gather_weights")
    weights = {0: [w_in0, None]}
    late = ((a_w_out[0].astype(BF16), 0), (b_norm_g, 1))

    xs = [x[0]]
    saved = []
    for i in range(depth):
        j = i // 2
        xi = xs[-1]
        nxt = weight_gather(i + 1, extra=late if i == 0 else ()) if i + 1 < depth else None
        if i % 2 == 0:
            h = _rms_fwd(xi, a_norm_g[j:j + 1], f"a{j}_norm")
            res = _matmul(h, weights[i][0], "nn", F32, f"a{j}_in", tk=d, exchange=nxt)
        else:
            h = _rms_fwd(xi, b_norm_full[j:j + 1], f"b{j}_norm")
            res = _matmul(h, weights[i][0], "nn", F32, f"b{j}_in", tm=512, tn=1664, tk=d,
                          exchange=nxt)
        if nxt is None:
            pre = res
        else:
            pre = res[0]
            weights[i + 1] = [whole_w_in(i + 1, res[1]), res[2]]
            if i == 0:
                weights[0][1], b_norm_full = res[3], res[4]
        w_in, w_out = weights[i]
        if i % 2 == 0:
            y = _mix_fwd(pre, a_v_norm_g[j:j + 1], a_w_s[j], b_s_col[j], f"a{j}_mix")
            xs.append(_matmul(y, w_out, "nn", F32, f"a{j}_out", residual=xi))
            saved.append((h, pre, y))
        else:
            qn, kn, vb, cum, qsq, ksq, cum_first, cum_last = _attn_prep(
                pre, b_q_norm_g[j:j + 1], b_k_norm_g[j:j + 1], f_bias_p[j:j + 1], heads, hd,
                f"b{j}_prep")
            first = _skip_plan(qsq, ksq, cum_first, cum_last, t, tq, hd, f"b{j}_plan")
            first = first[:, :heads].T
            ck = cum[:, :heads].T.reshape(heads, nq, 1, tq)
            o, y, lse = _flash_fwd(first, qn, kn, vb, ck, pre, heads, hd, tq, f"b{j}_attn")
            xs.append(_matmul(y, w_out, "nn", F32, f"b{j}_out", residual=xi))
            saved.append((h, pre, y, qn, kn, vb, ck, o, lse, first))

    sq, g, gb = _loss_grad(xs[-1], loss_target[0], "loss")
    loss = 0.5 * lax.psum(sq[0, 0], AXES) / d

    d_a_norm, d_a_vnorm, d_a_ws, d_a_bs = [None] * n_a, [None] * n_a, [None] * n_a, [None] * n_a
    d_b_norm, d_b_fb, d_b_gq, d_b_gk = [None] * n_b, [None] * n_b, [None] * n_b, [None] * n_b
    recv_in, recv_out = {}, {}
    pending = None

    def in_blocks(i, dw_in):
        if i % 2 == 0:
            return _AllToAll([dw_in], [1])
        rows = dw_in.shape[0]
        return _AllToAll(
            [dw_in[:, :4 * bw + heads].reshape(rows, N_DEV, b_cols).transpose(1, 0, 2)], [None])

    for i in reversed(range(depth)):
        j = i // 2
        xi = xs[i]
        w_in, w_out = weights[i]
        if i % 2 == 0:
            h, uvz, y = saved[i]
            dy = _matmul(gb, w_out, "nt", F32, f"a{j}_dy", tk=d)
            dw_out = _matmul(y, gb, "tn", BF16, f"a{j}_dwout")
            dpre, d_a_ws[j], dbs, d_a_vnorm[j] = _mix_bwd(
                uvz, dy, a_v_norm_g[j:j + 1], a_w_s[j], w_s_t[j], b_s_col[j], f"a{j}_mixbwd")
            d_a_bs[j] = dbs.reshape(groups, chunk)
            name, gain = f"a{j}", a_norm_g[j:j + 1]
        else:
            h, proj, y, qn, kn, vb, ck, o, lse, first = saved[i]
            dy = _matmul(gb, w_out, "nt", F32, f"b{j}_dy", tk=d)
            dw_out = _matmul(y, gb, "tn", BF16, f"b{j}_dwout")
            do, dz, delta = _attn_bwd_prep(dy, proj, o, heads, hd, f"b{j}_bwdprep")
            dqn, dkn, dv, dcq, dck = _flash_bwd(first, qn, kn, vb, do, ck, lse, delta, heads, hd,
                                                tq, f"b{j}_attnbwd")
            per_token = lambda v: pad_h(v.reshape(heads, t).T)
            dpre, d_b_gq[j], d_b_gk[j], dfb = _attn_bwd_post(
                dqn, dkn, dv, dz, proj, per_token(dcq), per_token(dck), b_q_norm_g[j:j + 1],
                b_k_norm_g[j:j + 1], f_bias_p[j:j + 1], heads, hd, f"b{j}_bwdpost")
            d_b_fb[j] = dfb[:, :heads]
            name, gain = f"b{j}", b_norm_full[j:j + 1]
        if pending is None:
            dh = _matmul(dpre, w_in, "nt", F32, name + "_dh")
        else:
            dh, got = _matmul(dpre, w_in, "nt", F32, name + "_dh", exchange=pending[1])
            parts_in = pending[2] + [got]
            recv_in[pending[0]] = parts_in[0] if len(parts_in) == 1 else jnp.concatenate(
                parts_in, axis=1)
        n_parts = (4 if i == 0 else 2) if i % 2 == 0 else 1
        rows = d // n_parts
        riding = _AllToAll([dw_out], [0])
        landed = []
        for part in range(n_parts):
            h_part = h if n_parts == 1 else h[:, part * rows:(part + 1) * rows]
            dw_part, got = _matmul(h_part, dpre, "tn", BF16, f"{name}_dwin{part}",
                                   tn=1024 if i % 2 == 0 else 1664, exchange=riding)
            if part == 0:
                recv_out[i] = got
            else:
                landed.append(got)
            riding = in_blocks(i, dw_part)
        if i > 0:
            pending = (i, riding, landed)
        else:
            landed.append(_run_exchange(riding, "exchange_grads")[0])
            recv_in[i] = jnp.concatenate(landed, axis=1)
        g, gb, dgain = _rms_bwd(xi, dh, g, gain, name + "_normbwd")
        if i % 2 == 0:
            d_a_norm[j] = dgain
        else:
            d_b_norm[j] = dgain
    grad_x = g[None]

    small = [jnp.concatenate(d_a_norm, 0), jnp.concatenate(d_a_vnorm, 0), jnp.stack(d_a_ws, 0),
             jnp.stack(d_a_bs, 0), jnp.concatenate(d_b_fb, 0), jnp.concatenate(d_b_gq, 0),
             jnp.concatenate(d_b_gk, 0)]
    small_w = [a_norm_g, a_v_norm_g, a_w_s, a_b_s, b_f_bias, b_q_norm_g, b_k_norm_g]
    small_m = [m_a_norm_g, m_a_v_norm_g, m_a_w_s, m_a_b_s, m_b_f_bias, m_b_q_norm_g, m_b_k_norm_g]
    small_v = [v_a_norm_g, v_a_v_norm_g, v_a_w_s, v_a_b_s, v_b_f_bias, v_b_q_norm_g, v_b_k_norm_g]
    small_flat = _flat_rows(small)
    n_small = small_flat.shape[0]
    (gathered_small,) = _run_exchange(_Gather(
        [jnp.concatenate([small_flat, _flat_rows([jnp.concatenate(d_b_norm, 0)])], axis=0)],
        [None]), "gather_small_grads")
    parts_small = gathered_small[:, :n_small]
    parts_b_norm = gathered_small[:, n_small:n_small + n_b * d // LANES].reshape(N_DEV, n_b, d)
    parts_b_norm = lax.dynamic_slice_in_dim(parts_b_norm, me * (d // N_DEV), d // N_DEV, axis=2)

    a_layers, b_layers = range(0, depth, 2), range(1, depth, 2)
    u_a_in = _adamw(a_w_in, m_a_w_in, v_a_w_in, [recv_in[i] for i in a_layers], "adamw_a_w_in")
    u_a_out = _adamw(a_w_out, m_a_w_out, v_a_w_out, [recv_out[i] for i in a_layers],
                     "adamw_a_w_out")
    u_b_in = _adamw(b_w_in, m_b_w_in, v_b_w_in, [recv_in[i] for i in b_layers], "adamw_b_w_in")
    u_b_out = _adamw(b_w_out, m_b_w_out, v_b_w_out, [recv_out[i] for i in b_layers],
                     "adamw_b_w_out")
    u_b_norm = [o_[0] for o_ in _adamw(b_norm_g[None], m_b_norm_g[None], v_b_norm_g[None],
                                       [parts_b_norm], "adamw_b_norm")]
    u_small = _adamw(_flat_rows(small_w)[None], _flat_rows(small_m)[None],
                     _flat_rows(small_v)[None], [parts_small], "adamw_small")
    shapes = [w.shape for w in small_w]
    u_small = [_unflat_rows(o_[0], shapes) for o_ in u_small]

    def per_kind(k):
        s = u_small[k]
        return [s[0], u_a_in[k], s[1], s[2], s[3], u_a_out[k], u_b_norm[k], u_b_in[k], s[4], s[5],
                s[6], u_b_out[k]]

    return (loss, grad_x, *per_kind(0), *per_kind(1), *per_kind(2), *per_kind(3))
```

```python
import functools

import jax
import jax.numpy as jnp
from jax import lax
from jax.experimental import pallas as pl
from jax.experimental.pallas import tpu as pltpu

F32 = jnp.float32
BF16 = jnp.bfloat16
MESH = pl.DeviceIdType.MESH
AXES = ("x", "y", "c")
N_DEV = 8
NORM_EPS = 1e-6
LANES = 128
VMEM_LIMIT = 56 * 1024 * 1024

ADAM_LR = 0.001
ADAM_B1 = 0.9
ADAM_B2 = 0.999
ADAM_EPS = 1e-08
ADAM_WD = 0.01
ADAM_STEP = 10

LOG2E = 1.4426950408889634
SOFTMAX_ROWS = 64
UNDERFLOW_LOG2 = -160.0

GELU_C0 = 0.7978845608028654
GELU_C1 = 0.044715

NT_DIMS = (((1,), (1,)), ((), ()))
TN_DIMS = (((0,), (0,)), ((), ()))


def _params(sem=None):
    return pltpu.CompilerParams(dimension_semantics=sem, vmem_limit_bytes=VMEM_LIMIT)


def _pick(n, target, unit):
    best = None
    for t in range(unit, min(n, target) + 1, unit):
        if n % t == 0:
            best = t
    return n if best is None else best


def _sigmoid(x):
    return 1.0 / (1.0 + jnp.exp(-x))


def _gelu(x):
    return 0.5 * x * (1.0 + jnp.tanh(GELU_C0 * x * (1.0 + GELU_C1 * x * x)))


def _gelu_and_grad(x):
    x2 = x * x
    t = jnp.tanh(GELU_C0 * x * (1.0 + GELU_C1 * x2))
    g = 0.5 * x * (1.0 + t)
    dg = 0.5 * (1.0 + t) + 0.5 * x * (1.0 - t * t) * (GELU_C0 * (1.0 + 3.0 * GELU_C1 * x2))
    return g, dg


def _dot(a, b):
    return jnp.dot(a, b, preferred_element_type=F32)


def _dot_nt(a, b):
    return lax.dot_general(a, b, NT_DIMS, preferred_element_type=F32)


def _dot_tn(a, b):
    return lax.dot_general(a, b, TN_DIMS, preferred_element_type=F32)


def _split3(v):
    hi = v.astype(BF16)
    r1 = v - hi.astype(F32)
    mid = r1.astype(BF16)
    lo = (r1 - mid.astype(F32)).astype(BF16)
    return hi, mid, lo


def _dev_index(p):
    return 4 * p[0] + 2 * p[1] + p[2]


def _block(ref, idx, axis, size):
    if axis is None:
        return ref.at[idx]
    start = pl.multiple_of(idx * size, size)
    return ref.at[(slice(None),) * axis + (pl.ds(start, size),)]


class _Exchange:
    def __init__(self, arrays, axes):
        self.arrays, self.axes, self.n = list(arrays), list(axes), len(arrays)

    def scratch(self):
        return [pltpu.SemaphoreType.DMA((self.n, 7)), pltpu.SemaphoreType.DMA((self.n, 7)),
                pltpu.SemaphoreType.DMA((self.n,))]

    @staticmethod
    def _place():
        x, y, c = lax.axis_index("x"), lax.axis_index("y"), lax.axis_index("c")
        return x, y, c


class _Gather(_Exchange):
    def out_shapes(self):
        outs = []
        for a, ax in zip(self.arrays, self.axes):
            if ax is None:
                shape = (N_DEV,) + a.shape
            else:
                shape = a.shape[:ax] + (N_DEV * a.shape[ax],) + a.shape[ax + 1:]
            outs.append(jax.ShapeDtypeStruct(shape, a.dtype))
        return outs

    def _copy(self, ins, outs, sems, a, k, block, to, src=None):
        ax = self.axes[a]
        dst = _block(outs[a], _dev_index(block), ax, None if ax is None else self.arrays[a].shape[ax])
        return pltpu.make_async_remote_copy(
            src_ref=dst if src is None else src, dst_ref=dst,
            send_sem=sems[0].at[a, k], recv_sem=sems[1].at[a, k],
            device_id=to, device_id_type=MESH)

    def _mine(self, ins, outs, sems, a, me):
        ax = self.axes[a]
        dst = _block(outs[a], _dev_index(me), ax, None if ax is None else self.arrays[a].shape[ax])
        return pltpu.make_async_copy(ins[a], dst, sems[2].at[a])

    def _first(self, ins, outs, sems):
        x, y, c = self._place()
        me, sibling = (x, y, c), (x, y, 1 - c)
        chips = [(1 - x, y), (x, 1 - y), (1 - x, 1 - y)]
        first = []
        for a in range(self.n):
            first.append(self._copy(ins, outs, sems, a, 0, me, sibling, src=ins[a]))
            first += [self._copy(ins, outs, sems, a, 1 + j, me, (*chip, c), src=ins[a])
                      for j, chip in enumerate(chips)]
        return first

    def start(self, ins, outs, sems):
        me = self._place()
        for a in range(self.n):
            self._mine(ins, outs, sems, a, me).start()
        for cp in self._first(ins, outs, sems):
            cp.start()

    def finish(self, ins, outs, sems):
        x, y, c = self._place()
        me, sibling = (x, y, c), (x, y, 1 - c)
        chips = [(1 - x, y), (x, 1 - y), (1 - x, 1 - y)]
        passed = []
        for a in range(self.n):
            for j, chip in enumerate(chips):
                self._copy(ins, outs, sems, a, 1 + j, (*chip, c), me).wait_recv()
                cp = self._copy(ins, outs, sems, a, 4 + j, (*chip, c), sibling)
                cp.start()
                passed.append(cp)
        for a in range(self.n):
            self._copy(ins, outs, sems, a, 0, sibling, me).wait_recv()
            for j, chip in enumerate(chips):
                self._copy(ins, outs, sems, a, 4 + j, (*chip, 1 - c), me).wait_recv()
        for cp in self._first(ins, outs, sems) + passed:
            cp.wait_send()
        for a in range(self.n):
            self._mine(ins, outs, sems, a, me).wait()


class _AllToAll(_Exchange):
    def _blk_shape(self, a):
        arr, ax = self.arrays[a], self.axes[a]
        if ax is None:
            return arr.shape[1:]
        return arr.shape[:ax] + (arr.shape[ax] // N_DEV,) + arr.shape[ax + 1:]

    def out_shapes(self):
        return [jax.ShapeDtypeStruct((N_DEV,) + self._blk_shape(a), self.arrays[a].dtype)
                for a in range(self.n)]

    def _src(self, ins, a, idx):
        ax = self.axes[a]
        return _block(ins[a], idx, ax, None if ax is None else self.arrays[a].shape[ax] // N_DEV)

    def _peers(self):
        x, y, c = self._place()
        return [((1 - x) if r & 4 else x, (1 - y) if r & 2 else y, (1 - c) if r & 1 else c)
                for r in range(1, N_DEV)]

    def _sends(self, ins, outs, sems):
        me = _dev_index(self._place())
        return [pltpu.make_async_remote_copy(
            src_ref=self._src(ins, a, _dev_index(peer)), dst_ref=outs[a].at[me],
            send_sem=sems[0].at[a, k], recv_sem=sems[1].at[a, k],
            device_id=peer, device_id_type=MESH)
            for a in range(self.n) for k, peer in enumerate(self._peers())]

    def _mine(self, ins, outs, sems):
        me = _dev_index(self._place())
        return [pltpu.make_async_copy(self._src(ins, a, me), outs[a].at[me], sems[2].at[a])
                for a in range(self.n)]

    def start(self, ins, outs, sems):
        for cp in self._mine(ins, outs, sems) + self._sends(ins, outs, sems):
            cp.start()

    def finish(self, ins, outs, sems):
        for a in range(self.n):
            for k, peer in enumerate(self._peers()):
                landed = outs[a].at[_dev_index(peer)]
                pltpu.make_async_remote_copy(
                    src_ref=landed, dst_ref=landed, send_sem=sems[0].at[a, k],
                    recv_sem=sems[1].at[a, k], device_id=peer, device_id_type=MESH).wait_recv()
        for cp in self._sends(ins, outs, sems):
            cp.wait_send()
        for cp in self._mine(ins, outs, sems):
            cp.wait()


def _run_exchange(exchange, name):
    n = exchange.n

    def body(*refs):
        ins, outs, sems = refs[:n], refs[n:2 * n], refs[2 * n:]
        exchange.start(ins, outs, sems)
        exchange.finish(ins, outs, sems)

    any_spec = pl.BlockSpec(memory_space=pl.ANY)
    return pl.pallas_call(
        body, name=name, out_shape=exchange.out_shapes(),
        in_specs=[any_spec] * n, out_specs=[any_spec] * n, scratch_shapes=exchange.scratch(),
    )(*exchange.arrays)


def _matmul(a, b, mode, out_dtype, name, tm=1024, tn=1024, tk=2048, residual=None, exchange=None):
    if mode == "tn":
        kdim, m = a.shape
    else:
        m, kdim = a.shape
    n = b.shape[0] if mode == "nt" else b.shape[1]
    tm, tn, tk = _pick(m, tm, LANES), _pick(n, tn, LANES), _pick(kdim, tk, LANES)
    nk = kdim // tk
    if mode == "tn":
        a_spec = pl.BlockSpec((tk, tm), lambda i, j, k: (k, i))
    else:
        a_spec = pl.BlockSpec((tm, tk), lambda i, j, k: (i, k))
    if mode == "nt":
        b_spec = pl.BlockSpec((tn, tk), lambda i, j, k: (j, k))
    else:
        b_spec = pl.BlockSpec((tk, tn), lambda i, j, k: (k, j))
    o_spec = pl.BlockSpec((tm, tn), lambda i, j, k: (i, j))
    dot = {"nn": _dot, "nt": _dot_nt, "tn": _dot_tn}[mode]
    has_res = residual is not None
    n_in = 3 if has_res else 2
    n_ex = 0 if exchange is None else exchange.n
    ni, nj = m // tm, n // tn

    def body(*refs):
        a_ref, b_ref = refs[:2]
        r_ref = refs[2] if has_res else None
        ex_ins = refs[n_in:n_in + n_ex]
        o_ref = refs[n_in + n_ex]
        ex_outs = refs[n_in + n_ex + 1:n_in + 2 * n_ex + 1]
        scratch = refs[n_in + 2 * n_ex + 1:]
        i, j, k = pl.program_id(0), pl.program_id(1), pl.program_id(2)

        if exchange is not None:
            sems = scratch[-3:]

            @pl.when((i == 0) & (j == 0) & (k == 0))
            def _():
                exchange.start(ex_ins, ex_outs, sems)

        def finish(acc):
            if has_res:
                acc = acc + r_ref[...]
            o_ref[...] = acc.astype(out_dtype)

        if nk == 1:
            finish(dot(a_ref[...], b_ref[...]))
        else:
            acc_ref = scratch[0]

            @pl.when(k == 0)
            def _():
                acc_ref[...] = jnp.zeros_like(acc_ref)

            acc_ref[...] += dot(a_ref[...], b_ref[...])

            @pl.when(k == nk - 1)
            def _():
                finish(acc_ref[...])

        if exchange is not None:
            @pl.when((i == ni - 1) & (j == nj - 1) & (k == nk - 1))
            def _():
                exchange.finish(ex_ins, ex_outs, sems)

    any_spec = pl.BlockSpec(memory_space=pl.ANY)
    operands = (a, b, residual) if has_res else (a, b)
    out_shape = jax.ShapeDtypeStruct((m, n), out_dtype)
    scratch_shapes = [] if nk == 1 else [pltpu.VMEM((tm, tn), F32)]
    if exchange is None:
        return pl.pallas_call(
            body, name=name, grid=(ni, nj, nk),
            in_specs=[a_spec, b_spec] + ([o_spec] if has_res else []),
            out_specs=o_spec, out_shape=out_shape, scratch_shapes=scratch_shapes,
            compiler_params=_params(("parallel", "parallel", "arbitrary")),
        )(*operands)
    return pl.pallas_call(
        body, name=name, grid=(ni, nj, nk),
        in_specs=[a_spec, b_spec] + ([o_spec] if has_res else []) + [any_spec] * n_ex,
        out_specs=[o_spec] + [any_spec] * n_ex,
        out_shape=[out_shape] + exchange.out_shapes(),
        scratch_shapes=scratch_shapes + exchange.scratch(),
        compiler_params=_params(("arbitrary", "arbitrary", "arbitrary")),
    )(*operands, *exchange.arrays)


def _rms_fwd(x, gain, name):
    t, d = x.shape
    tr = _pick(t, 256, 16)

    def body(x_ref, g_ref, h_ref):
        xv = x_ref[...]
        r = lax.rsqrt(jnp.mean(xv * xv, axis=1, keepdims=True) + NORM_EPS)
        h_ref[...] = (xv * r * g_ref[...]).astype(BF16)

    row = pl.BlockSpec((tr, d), lambda i: (i, 0))
    return pl.pallas_call(
        body, name=name, grid=(t // tr,),
        in_specs=[row, pl.BlockSpec((1, d), lambda i: (0, 0))],
        out_specs=row, out_shape=jax.ShapeDtypeStruct((t, d), BF16),
        compiler_params=_params(("arbitrary",)),
    )(x, gain)


def _rms_bwd(x, dh, g_in, gain, name):
    t, d = x.shape
    tr = _pick(t, 256, 16)

    def body(x_ref, dh_ref, gin_ref, g_ref, dx_ref, dxb_ref, dg_ref):
        i = pl.program_id(0)
        xv, dhv = x_ref[...], dh_ref[...]
        r = lax.rsqrt(jnp.mean(xv * xv, axis=1, keepdims=True) + NORM_EPS)
        xh = xv * r
        dxh = dhv * g_ref[...]
        dx = gin_ref[...] + r * (dxh - xh * jnp.mean(dxh * xh, axis=1, keepdims=True))
        dx_ref[...] = dx
        dxb_ref[...] = dx.astype(BF16)

        @pl.when(i == 0)
        def _():
            dg_ref[...] = jnp.zeros_like(dg_ref)

        dg_ref[...] += jnp.sum(dhv * xh, axis=0, keepdims=True)

    row = pl.BlockSpec((tr, d), lambda i: (i, 0))
    vec = pl.BlockSpec((1, d), lambda i: (0, 0))
    return pl.pallas_call(
        body, name=name, grid=(t // tr,),
        in_specs=[row, row, row, vec],
        out_specs=[row, row, vec],
        out_shape=[jax.ShapeDtypeStruct((t, d), F32), jax.ShapeDtypeStruct((t, d), BF16),
                   jax.ShapeDtypeStruct((1, d), F32)],
        compiler_params=_params(("arbitrary",)),
    )(x, dh, g_in, gain)


def _loss_grad(y, target, name):
    t, d = y.shape
    tr = _pick(t, 256, 16)

    def body(y_ref, t_ref, s_ref, g_ref, gb_ref):
        i = pl.program_id(0)
        e = y_ref[...] - t_ref[...]
        g = e * (1.0 / d)
        g_ref[...] = g
        gb_ref[...] = g.astype(BF16)

        @pl.when(i == 0)
        def _():
            s_ref[...] = jnp.zeros_like(s_ref)

        s_ref[...] += jnp.sum(jnp.sum(e * e, axis=1, keepdims=True), axis=0, keepdims=True)

    row = pl.BlockSpec((tr, d), lambda i: (i, 0))
    return pl.pallas_call(
        body, name=name, grid=(t // tr,),
        in_specs=[row, row],
        out_specs=[pl.BlockSpec((1, 1), lambda i: (0, 0)), row, row],
        out_shape=[jax.ShapeDtypeStruct((1, 1), F32), jax.ShapeDtypeStruct((t, d), F32),
                   jax.ShapeDtypeStruct((t, d), BF16)],
        compiler_params=_params(("arbitrary",)),
    )(y, target)


def _causal_weights(ws_ref, g, chunk, transposed):
    rows = lax.broadcasted_iota(jnp.int32, (chunk, chunk), 0)
    cols = lax.broadcasted_iota(jnp.int32, (chunk, chunk), 1)
    keep = (cols >= rows) if transposed else (rows >= cols)
    return jnp.where(keep, ws_ref[g], 0.0).astype(BF16)


def _mix_fwd(uvz, v_gain, w_s, b_s, name):
    t, w3 = uvz.shape
    w = w3 // 3
    groups, chunk = w_s.shape[0], w_s.shape[1]
    gd = w // groups

    def body(uvz_ref, gam_ref, ws_ref, bs_ref, y_ref):
        gv = _gelu(uvz_ref[:, w:2 * w])
        r = lax.rsqrt(jnp.mean(gv * gv, axis=1, keepdims=True) + NORM_EPS)
        vn = (gv * r * gam_ref[...]).astype(BF16)
        for g in range(groups):
            sl = slice(g * gd, (g + 1) * gd)
            mixed = _dot(_causal_weights(ws_ref, g, chunk, False), vn[:, sl]) + bs_ref[g]
            u = uvz_ref[:, g * gd:(g + 1) * gd]
            z = uvz_ref[:, 2 * w + g * gd:2 * w + (g + 1) * gd]
            y_ref[:, sl] = (_gelu(u) * mixed * (z * _sigmoid(z))).astype(BF16)

    return pl.pallas_call(
        body, name=name, grid=(t // chunk,),
        in_specs=[pl.BlockSpec((chunk, w3), lambda i: (i, 0)),
                  pl.BlockSpec((1, w), lambda i: (0, 0)),
                  pl.BlockSpec((groups, chunk, chunk), lambda i: (0, 0, 0)),
                  pl.BlockSpec((groups, chunk, 1), lambda i: (0, 0, 0))],
        out_specs=pl.BlockSpec((chunk, w), lambda i: (i, 0)),
        out_shape=jax.ShapeDtypeStruct((t, w), BF16),
        compiler_params=_params(("arbitrary",)),
    )(uvz, v_gain, w_s, b_s)


def _mix_bwd(uvz, dy, v_gain, w_s, w_s_t, b_s, name):
    t, w3 = uvz.shape
    w = w3 // 3
    groups, chunk = w_s.shape[0], w_s.shape[1]
    gd = w // groups

    def body(uvz_ref, dy_ref, gam_ref, ws_ref, wst_ref, bs_ref, d_ref, dws_ref, dbs_ref, dgam_ref,
             dvn_ref):
        i = pl.program_id(0)

        @pl.when(i == 0)
        def _():
            dws_ref[...] = jnp.zeros_like(dws_ref)
            dbs_ref[...] = jnp.zeros_like(dbs_ref)
            dgam_ref[...] = jnp.zeros_like(dgam_ref)

        gv, dgv = _gelu_and_grad(uvz_ref[:, w:2 * w])
        r = lax.rsqrt(jnp.mean(gv * gv, axis=1, keepdims=True) + NORM_EPS)
        vh = gv * r
        gam = gam_ref[...]
        vn = (vh * gam).astype(BF16)
        rows = lax.broadcasted_iota(jnp.int32, (chunk, chunk), 0)
        cols = lax.broadcasted_iota(jnp.int32, (chunk, chunk), 1)
        for g in range(groups):
            sl = slice(g * gd, (g + 1) * gd)
            mixed = _dot(_causal_weights(ws_ref, g, chunk, False), vn[:, sl]) + bs_ref[g]
            gu, dgu = _gelu_and_grad(uvz_ref[:, g * gd:(g + 1) * gd])
            z = uvz_ref[:, 2 * w + g * gd:2 * w + (g + 1) * gd]
            sz = _sigmoid(z)
            silu = z * sz
            dyv = dy_ref[:, sl]
            dmixed = dyv * gu * silu
            d_ref[:, sl] = (dyv * mixed * silu * dgu).astype(BF16)
            d_ref[:, 2 * w + g * gd:2 * w + (g + 1) * gd] = (
                dyv * gu * mixed * (sz * (1.0 + z * (1.0 - sz)))).astype(BF16)
            dmb = dmixed.astype(BF16)
            dws_ref[g] += jnp.where(rows >= cols, _dot_nt(dmb, vn[:, sl]), 0.0)
            dbs_ref[g] += jnp.sum(dmixed, axis=1, keepdims=True)
            dvn_ref[:, sl] = _dot(_causal_weights(wst_ref, g, chunk, True), dmb)
        dvn = dvn_ref[...]
        dgam_ref[...] += jnp.sum(dvn * vh, axis=0, keepdims=True)
        dvh = dvn * gam
        dgvv = r * (dvh - vh * jnp.mean(dvh * vh, axis=1, keepdims=True))
        d_ref[:, w:2 * w] = (dgvv * dgv).astype(BF16)

    return pl.pallas_call(
        body, name=name, grid=(t // chunk,),
        in_specs=[pl.BlockSpec((chunk, w3), lambda i: (i, 0)),
                  pl.BlockSpec((chunk, w), lambda i: (i, 0)),
                  pl.BlockSpec((1, w), lambda i: (0, 0)),
                  pl.BlockSpec((groups, chunk, chunk), lambda i: (0, 0, 0)),
                  pl.BlockSpec((groups, chunk, chunk), lambda i: (0, 0, 0)),
                  pl.BlockSpec((groups, chunk, 1), lambda i: (0, 0, 0))],
        out_specs=[pl.BlockSpec((chunk, w3), lambda i: (i, 0)),
                   pl.BlockSpec((groups, chunk, chunk), lambda i: (0, 0, 0)),
                   pl.BlockSpec((groups, chunk, 1), lambda i: (0, 0, 0)),
                   pl.BlockSpec((1, w), lambda i: (0, 0))],
        out_shape=[jax.ShapeDtypeStruct((t, w3), BF16),
                   jax.ShapeDtypeStruct((groups, chunk, chunk), F32),
                   jax.ShapeDtypeStruct((groups, chunk, 1), F32),
                   jax.ShapeDtypeStruct((1, w), F32)],
        scratch_shapes=[pltpu.VMEM((chunk, w), F32)],
        compiler_params=_params(("arbitrary",)),
    )(uvz, dy, v_gain, w_s, w_s_t, b_s)


def _attn_prep(proj, q_gain, k_gain, f_bias, heads, hd, name):
    t = proj.shape[0]
    bw = heads * hd
    tr = _pick(t, 256, 16)
    fcol = 4 * bw // LANES

    def body(q_ref, k_ref, v_ref, f_ref, gq_ref, gk_ref, fb_ref, qn_ref, kn_ref, vb_ref, cum_ref,
             qsq_ref, ksq_ref, first_ref, last_ref, carry_ref):
        i = pl.program_id(0)

        @pl.when(i == 0)
        def _():
            carry_ref[...] = jnp.zeros_like(carry_ref)

        lane = lax.broadcasted_iota(jnp.int32, (1, LANES), 1)
        for src, gain, dst, sq_ref in ((q_ref, gq_ref, qn_ref, qsq_ref),
                                       (k_ref, gk_ref, kn_ref, ksq_ref)):
            sq_row = jnp.zeros((1, LANES), F32)
            for h in range(heads):
                sl = slice(h * hd, (h + 1) * hd)
                v = src[:, sl]
                r = lax.rsqrt(jnp.mean(v * v, axis=1, keepdims=True) + NORM_EPS)
                normed = (v * r * gain[...]).astype(BF16)
                dst[:, sl] = normed
                nf = normed.astype(F32)
                sq = jnp.max(jnp.sum(nf * nf, axis=1, keepdims=True), axis=0, keepdims=True)
                sq_row = jnp.where(lane == h, sq, sq_row)
            sq_ref[0] = sq_row
        vb_ref[...] = v_ref[...].astype(BF16)
        fl = f_ref[...] + fb_ref[...]
        log_f = jnp.minimum(fl, 0.0) - jnp.log(1.0 + jnp.exp(-jnp.abs(fl)))
        rows = lax.broadcasted_iota(jnp.int32, (tr, tr), 0)
        cols = lax.broadcasted_iota(jnp.int32, (tr, tr), 1)
        lower = jnp.where(rows >= cols, 1.0, 0.0).astype(BF16)
        hi, mid, lo = _split3(log_f)
        cum_ref[...] = (_dot(lower, hi) + _dot(lower, mid) + _dot(lower, lo)) + carry_ref[...]
        carry_ref[...] = cum_ref[tr - 1:tr, :]
        first_ref[0] = cum_ref[0:1, :]
        last_ref[0] = cum_ref[tr - 1:tr, :]

    wide = lambda col: pl.BlockSpec((tr, bw), lambda i: (i, col))
    vec = pl.BlockSpec((1, hd), lambda i: (0, 0))
    stat = pl.BlockSpec((1, 1, LANES), lambda i: (i, 0, 0))
    return pl.pallas_call(
        body, name=name, grid=(t // tr,),
        in_specs=[wide(0), wide(1), wide(2), pl.BlockSpec((tr, LANES), lambda i: (i, fcol)),
                  vec, vec, pl.BlockSpec((1, LANES), lambda i: (0, 0))],
        out_specs=[wide(0), wide(0), wide(0), pl.BlockSpec((tr, LANES), lambda i: (i, 0))]
        + [stat] * 4,
        out_shape=[jax.ShapeDtypeStruct((t, bw), BF16)] * 3 + [jax.ShapeDtypeStruct((t, LANES), F32)]
        + [jax.ShapeDtypeStruct((t // tr, 1, LANES), F32)] * 4,
        scratch_shapes=[pltpu.VMEM((1, LANES), F32)],
        compiler_params=_params(("arbitrary",)),
    )(proj, proj, proj, proj, q_gain, k_gain, f_bias)


def _skip_plan(qsq, ksq, first, last, t, tq, hd, name):
    nt = qsq.shape[0]
    nq = t // tq
    r = nt // nq
    scale2 = hd ** -0.5 * LOG2E

    def body(qsq_ref, ksq_ref, first_ref, last_ref, out_ref):
        kmax = ksq_ref[0]
        for tile in range(1, nt):
            kmax = jnp.maximum(kmax, ksq_ref[tile])
        for i in range(nq):
            qmax = qsq_ref[i * r]
            for tile in range(i * r + 1, (i + 1) * r):
                qmax = jnp.maximum(qmax, qsq_ref[tile])
            coef = 2.0 * scale2 * jnp.sqrt(qmax * kmax)
            start = first_ref[i * r]
            count = jnp.zeros((1, LANES), jnp.int32)
            for j in range(i):
                bound = coef + (start - last_ref[(j + 1) * r - 1]) * LOG2E
                count = count + jnp.where(bound <= UNDERFLOW_LOG2, 1, 0)
            out_ref[i:i + 1, :] = count

    return pl.pallas_call(
        body, name=name, out_shape=jax.ShapeDtypeStruct((nq, LANES), jnp.int32),
    )(qsq, ksq, first, last)


def _flash_fwd(first_block, qn, kn, vb, ck, proj, heads, hd, tq, name):
    t = qn.shape[0]
    nq = t // tq
    th = tq // 2
    scale2 = hd ** -0.5 * LOG2E
    zcol = 3 * heads
    rc = _pick(tq, SOFTMAX_ROWS, 16)
    reps = th // LANES

    def body(first_ref, q_ref, k_ref, v_ref, ck_ref, z_ref, o_ref, y_ref, lse_ref,
             m_s, l_s, acc_s, s_a, s_b, p_a, p_b, al_a, al_b):
        i = pl.program_id(1)
        j0 = first_ref[pl.program_id(0), i]
        bufs = ((s_a, p_a, al_a), (s_b, p_b, al_b))
        m_s[...] = jnp.full_like(m_s, -jnp.inf)
        l_s[...] = jnp.zeros_like(l_s)
        acc_s[...] = jnp.zeros_like(acc_s)
        p_b[...] = jnp.zeros_like(p_b)
        al_b[...] = jnp.ones_like(al_b)

        def scores(j, half, first_row=0):
            off = pl.multiple_of(j * tq + half * th, th)
            bufs[half][0][first_row:, :] = _dot_nt(q_ref[first_row:, :], k_ref[pl.ds(off, th), :])

        def values(j, half, first_row=0):
            off = pl.multiple_of(j * tq + half * th, th)
            _, p_buf, al = bufs[half]
            acc_s[first_row:, :] = (
                jnp.tile(al[first_row:, :], (1, hd // LANES)) * acc_s[first_row:, :]
                + _dot(p_buf[first_row:, :], v_ref[pl.ds(off, th), :]))

        def softmax(j, half, masked):
            s_buf, p_buf, al = bufs[half]
            ck2 = ck_ref[0, j][:, half * th:(half + 1) * th] * LOG2E
            chunks = range((half * th) // rc if masked else 0, tq // rc)
            for c in chunks:
                r = slice(c * rc, (c + 1) * rc)
                s = s_buf[r, :] * scale2 - ck2
                if masked and c * rc < (half + 1) * th:
                    rows = lax.broadcasted_iota(jnp.int32, (rc, th), 0) + c * rc
                    cols = lax.broadcasted_iota(jnp.int32, (rc, th), 1) + half * th
                    s = jnp.where(rows >= cols, s, -jnp.inf)
                s_buf[r, :] = s
                m_prev = m_s[r, :]
                m_new = jnp.maximum(m_prev, jnp.max(s, axis=1, keepdims=True))
                al[r, :] = jnp.exp2(m_prev - m_new)
                m_s[r, :] = m_new
            for c in chunks:
                r = slice(c * rc, (c + 1) * rc)
                p = jnp.exp2(s_buf[r, :] - jnp.tile(m_s[r, :], (1, reps)))
                p_buf[r, :] = p.astype(BF16)
                lane_sum = p[:, 0:LANES]
                for b in range(1, reps):
                    lane_sum = lane_sum + p[:, b * LANES:(b + 1) * LANES]
                l_s[r, :] = al[r, :] * l_s[r, :] + lane_sum

        scores(j0, 0)

        def loop_body(j, carry):
            scores(j, 1)
            values(jnp.maximum(j - 1, 0), 1)
            softmax(j, 0, False)
            scores(j + 1, 0)
            values(j, 0)
            softmax(j, 1, False)
            return carry

        lax.fori_loop(j0, i, loop_body, 0)
        values(jnp.maximum(i - 1, 0), 1)
        scores(i, 1, first_row=th)
        softmax(i, 0, True)
        values(i, 0)
        softmax(i, 1, True)
        values(i, 1, first_row=th)
        l = jnp.sum(l_s[...], axis=1, keepdims=True)
        o = acc_s[...] / l
        z = z_ref[...]
        o_ref[...] = o
        y_ref[...] = (o * (z * _sigmoid(z))).astype(BF16)
        lse_ref[0] = m_s[:, 0:1] + jnp.log(l) * LOG2E

    blk = pl.BlockSpec((tq, hd), lambda h, i, first: (i, h))
    head = pl.BlockSpec((t, hd), lambda h, i, first: (0, h))
    col = pl.BlockSpec((1, tq, 1), lambda h, i, first: (h, i, 0))
    stat = pltpu.VMEM((tq, LANES), F32)
    return pl.pallas_call(
        body, name=name,
        grid_spec=pltpu.PrefetchScalarGridSpec(
            num_scalar_prefetch=1, grid=(heads, nq),
            in_specs=[blk, head, head,
                      pl.BlockSpec((1, nq, 1, tq), lambda h, i, first: (h, 0, 0, 0)),
                      pl.BlockSpec((tq, hd), lambda h, i, first: (i, zcol + h))],
            out_specs=[blk, blk, col],
            scratch_shapes=[stat, stat, pltpu.VMEM((tq, hd), F32),
                            pltpu.VMEM((tq, th), F32), pltpu.VMEM((tq, th), F32),
                            pltpu.VMEM((tq, th), BF16), pltpu.VMEM((tq, th), BF16), stat, stat]),
        out_shape=[jax.ShapeDtypeStruct((t, heads * hd), F32),
                   jax.ShapeDtypeStruct((t, heads * hd), BF16),
                   jax.ShapeDtypeStruct((heads, t, 1), F32)],
        compiler_params=_params(("arbitrary", "arbitrary")),
    )(first_block, qn, kn, vb, ck, proj)


def _attn_bwd_prep(dy, proj, o, heads, hd, name):
    t, bw = dy.shape
    tr = _pick(t, 256, 16)

    def body(dy_ref, z_ref, o_ref, do_ref, dz_ref, delta_ref):
        dyv, z, ov = dy_ref[...], z_ref[...], o_ref[...]
        sz = _sigmoid(z)
        do = dyv * (z * sz)
        do_ref[...] = do.astype(BF16)
        dz_ref[...] = (dyv * ov * (sz * (1.0 + z * (1.0 - sz)))).astype(BF16)
        prod = do * ov
        for h in range(heads):
            delta_ref[h] = jnp.sum(prod[:, h * hd:(h + 1) * hd], axis=1, keepdims=True)

    row = pl.BlockSpec((tr, bw), lambda i: (i, 0))
    return pl.pallas_call(
        body, name=name, grid=(t // tr,),
        in_specs=[row, pl.BlockSpec((tr, bw), lambda i: (i, 3)), row],
        out_specs=[row, row, pl.BlockSpec((heads, tr, 1), lambda i: (0, i, 0))],
        out_shape=[jax.ShapeDtypeStruct((t, bw), BF16), jax.ShapeDtypeStruct((t, bw), BF16),
                   jax.ShapeDtypeStruct((heads, t, 1), F32)],
        compiler_params=_params(("arbitrary",)),
    )(dy, proj, o)


def _flash_bwd(first_block, qn, kn, vb, do, ck, lse, delta, heads, hd, tq, name):
    t = qn.shape[0]
    nq = t // tq
    scale = hd ** -0.5
    scale2 = scale * LOG2E

    def body(first_ref, q_ref, k_ref, v_ref, do_ref, ck_ref, lse_ref, delta_ref,
             dq_ref, dk_ref, dv_ref, dcq_ref, dck_ref, dq_s, dcq_s):
        i = pl.program_id(1)
        j0 = first_ref[pl.program_id(0), i]

        @pl.when(i == 0)
        def _():
            dk_ref[...] = jnp.zeros_like(dk_ref)
            dv_ref[...] = jnp.zeros_like(dv_ref)
            dck_ref[...] = jnp.zeros_like(dck_ref)

        dq_s[...] = jnp.zeros_like(dq_s)
        dcq_s[...] = jnp.zeros_like(dcq_s)

        def step(j, masked, row0=0, col0=0, ncol=tq):
            off = pl.multiple_of(j * tq + col0, ncol)
            kblk = k_ref[pl.ds(off, ncol), :]
            q, dov = q_ref[row0:, :], do_ref[row0:, :]
            s = _dot_nt(q, kblk) * scale2 - ck_ref[0, j][:, col0:col0 + ncol] * LOG2E
            if masked:
                rows = lax.broadcasted_iota(jnp.int32, (tq - row0, ncol), 0) + row0
                cols = lax.broadcasted_iota(jnp.int32, (tq - row0, ncol), 1) + col0
                s = jnp.where(rows >= cols, s, -jnp.inf)
            p = jnp.exp2(s - lse_ref[0, row0:, :])
            dp = _dot_nt(dov, v_ref[pl.ds(off, ncol), :])
            ds = p * (dp - delta_ref[0, row0:, :])
            dsb = ds.astype(BF16)
            dv_ref[pl.ds(off, ncol), :] += _dot_tn(p.astype(BF16), dov)
            dk_ref[pl.ds(off, ncol), :] += _dot_tn(dsb, q) * scale
            dq_s[row0:, :] += _dot(dsb, kblk) * scale
            dcq_s[row0:, :] += jnp.sum(ds, axis=1, keepdims=True)
            dck_ref[0, j, :, col0:col0 + ncol] += jnp.sum(ds, axis=0, keepdims=True)

        def loop_body(j, carry):
            step(j, False)
            return carry

        lax.fori_loop(j0, i, loop_body, 0)
        step(i, True, 0, 0, tq // 2)
        step(i, True, tq // 2, tq // 2, tq // 2)
        dq_ref[...] = dq_s[...]
        dcq_ref[0] = dcq_s[...]

    blk = pl.BlockSpec((tq, hd), lambda h, i, first: (i, h))
    head = pl.BlockSpec((t, hd), lambda h, i, first: (0, h))
    col = pl.BlockSpec((1, tq, 1), lambda h, i, first: (h, i, 0))
    rowv = pl.BlockSpec((1, nq, 1, tq), lambda h, i, first: (h, 0, 0, 0))
    full = jax.ShapeDtypeStruct((t, heads * hd), F32)
    return pl.pallas_call(
        body, name=name,
        grid_spec=pltpu.PrefetchScalarGridSpec(
            num_scalar_prefetch=1, grid=(heads, nq),
            in_specs=[blk, head, head, blk, rowv, col, col],
            out_specs=[blk, head, head, col, rowv],
            scratch_shapes=[pltpu.VMEM((tq, hd), F32), pltpu.VMEM((tq, 1), F32)]),
        out_shape=[full, full, full, jax.ShapeDtypeStruct((heads, t, 1), F32),
                   jax.ShapeDtypeStruct((heads, nq, 1, tq), F32)],
        compiler_params=_params(("arbitrary", "arbitrary")),
    )(first_block, qn, kn, vb, do, ck, lse, delta)


def _attn_bwd_post(dqn, dkn, dv, dz, proj, dcq, dck, q_gain, k_gain, f_bias, heads, hd, name):
    t, bw = dqn.shape
    tr = _pick(t, 128, 16)
    nb = t // tr
    fcol = 4 * bw // LANES
    width = 4 * bw + LANES

    def body(dq_ref, dk_ref, dv_ref, dz_ref, q_ref, k_ref, f_ref, dcq_ref, dck_ref, gq_ref, gk_ref,
             fb_ref, d_ref, dgq_ref, dgk_ref, dfb_ref, carry_ref, rc_ref):
        i = pl.program_id(0)

        @pl.when(i == 0)
        def _():
            carry_ref[...] = jnp.zeros_like(carry_ref)
            dgq_ref[...] = jnp.zeros_like(dgq_ref)
            dgk_ref[...] = jnp.zeros_like(dgk_ref)
            dfb_ref[...] = jnp.zeros_like(dfb_ref)

        for idx, (g_ref, raw_ref, gain_ref, dgain_ref) in enumerate(
                ((dq_ref, q_ref, gq_ref, dgq_ref), (dk_ref, k_ref, gk_ref, dgk_ref))):
            gain = gain_ref[...]
            dgain = jnp.zeros((1, hd), F32)
            for h in range(heads):
                sl = slice(h * hd, (h + 1) * hd)
                v, dn = raw_ref[:, sl], g_ref[:, sl]
                r = lax.rsqrt(jnp.mean(v * v, axis=1, keepdims=True) + NORM_EPS)
                vh = v * r
                dgain = dgain + jnp.sum(dn * vh, axis=0, keepdims=True)
                dvh = dn * gain
                draw = r * (dvh - vh * jnp.mean(dvh * vh, axis=1, keepdims=True))
                d_ref[:, idx * bw + h * hd:idx * bw + (h + 1) * hd] = draw.astype(BF16)
            dgain_ref[...] += dgain
        d_ref[:, 2 * bw:3 * bw] = dv_ref[...].astype(BF16)
        d_ref[:, 3 * bw:4 * bw] = dz_ref[...]
        rows = lax.broadcasted_iota(jnp.int32, (tr, tr), 0)
        cols = lax.broadcasted_iota(jnp.int32, (tr, tr), 1)
        upper = jnp.where(cols >= rows, 1.0, 0.0).astype(BF16)
        hi, mid, lo = _split3(dcq_ref[...] - dck_ref[...])
        rc_ref[...] = (_dot(upper, hi) + _dot(upper, mid) + _dot(upper, lo)) + carry_ref[...]
        carry_ref[...] = rc_ref[0:1, :]
        df = rc_ref[...] * (1.0 / (1.0 + jnp.exp(f_ref[...] + fb_ref[...])))
        d_ref[:, 4 * bw:] = df.astype(BF16)
        dfb_ref[...] += jnp.sum(df, axis=0, keepdims=True)

    wide = lambda col: pl.BlockSpec((tr, bw), lambda i: (nb - 1 - i, col))
    lane = lambda col: pl.BlockSpec((tr, LANES), lambda i: (nb - 1 - i, col))
    vec = pl.BlockSpec((1, hd), lambda i: (0, 0))
    vecl = pl.BlockSpec((1, LANES), lambda i: (0, 0))
    return pl.pallas_call(
        body, name=name, grid=(nb,),
        in_specs=[wide(0), wide(0), wide(0), wide(0), wide(0), wide(1), lane(fcol), lane(0), lane(0),
                  vec, vec, vecl],
        out_specs=[pl.BlockSpec((tr, width), lambda i: (nb - 1 - i, 0)), vec, vec, vecl],
        out_shape=[jax.ShapeDtypeStruct((t, width), BF16), jax.ShapeDtypeStruct((1, hd), F32),
                   jax.ShapeDtypeStruct((1, hd), F32), jax.ShapeDtypeStruct((1, LANES), F32)],
        scratch_shapes=[pltpu.VMEM((1, LANES), F32), pltpu.VMEM((tr, LANES), F32)],
        compiler_params=_params(("arbitrary",)),
    )(dqn, dkn, dv, dz, proj, proj, proj, dcq, dck, q_gain, k_gain, f_bias)


def _adamw(w, m, v, parts, name):
    nl, r, c = w.shape
    itemsize = parts[0].dtype.itemsize
    unit = 32 // itemsize
    row_bytes = c * (7 * 4 + N_DEV * itemsize * nl)
    tr = _pick(r, max(unit, 12 * 1024 * 1024 // row_bytes), unit)
    nr = r // tr
    c1 = 1.0 / (1.0 - ADAM_B1 ** ADAM_STEP)
    c2 = 1.0 / (1.0 - ADAM_B2 ** ADAM_STEP)

    def body(*refs):
        w_ref, m_ref, v_ref = refs[:3]
        p_refs = refs[3:3 + nl]
        g_ref, d_ref, nm_ref, nv_ref = refs[3 + nl:]
        layer = pl.program_id(0)

        def partial(j):
            p = p_refs[0][j].astype(F32)
            for q in range(1, nl):
                p = jnp.where(layer == q, p_refs[q][j].astype(F32), p)
            return p

        g = partial(0)
        for j in range(1, N_DEV):
            g = g + partial(j)
        nm = ADAM_B1 * m_ref[0] + (1.0 - ADAM_B1) * g
        nv = ADAM_B2 * v_ref[0] + (1.0 - ADAM_B2) * (g * g)
        g_ref[0] = g
        nm_ref[0] = nm
        nv_ref[0] = nv
        d_ref[0] = -ADAM_LR * ((nm * c1) / (jnp.sqrt(nv * c2) + ADAM_EPS) + ADAM_WD * w_ref[0])

    def part_spec(q):
        rest = 0 if q > 0 else nr - 1
        return pl.BlockSpec((N_DEV, tr, c), lambda l, i: (0, jnp.where(l == q, i, rest), 0))

    row = pl.BlockSpec((1, tr, c), lambda l, i: (l, i, 0))
    return pl.pallas_call(
        body, name=name, grid=(nl, nr),
        in_specs=[row, row, row] + [part_spec(q) for q in range(nl)],
        out_specs=[row] * 4,
        out_shape=[jax.ShapeDtypeStruct((nl, r, c), F32)] * 4,
        compiler_params=_params(("arbitrary", "arbitrary")),
    )(w, m, v, *parts)


def _flat_rows(pieces):
    rows = []
    for p in pieces:
        f = p.reshape(-1)
        f = jnp.pad(f, (0, (-f.shape[0]) % LANES))
        rows.append(f.reshape(-1, LANES))
    out = jnp.concatenate(rows, axis=0)
    return jnp.pad(out, ((0, (-out.shape[0]) % 8), (0, 0)))


def _unflat_rows(flat, shapes):
    outs, r0 = [], 0
    lead = flat.shape[:-2]
    for s in shapes:
        size = 1
        for d in s:
            size *= d
        nr = -(-size // LANES)
        piece = flat[..., r0:r0 + nr, :].reshape(lead + (nr * LANES,))[..., :size]
        outs.append(piece.reshape(lead + tuple(s)))
        r0 += nr
    return outs


def kernel(x, a_norm_g, a_w_in, a_v_norm_g, a_w_s, a_b_s, a_w_out, b_norm_g, b_w_in, b_f_bias, b_q_norm_g, b_k_norm_g, b_w_out, loss_target, m_a_norm_g, m_a_w_in, m_a_v_norm_g, m_a_w_s, m_a_b_s, m_a_w_out, m_b_norm_g, m_b_w_in, m_b_f_bias, m_b_q_norm_g, m_b_k_norm_g, m_b_w_out, v_a_norm_g, v_a_w_in, v_a_v_norm_g, v_a_w_s, v_a_b_s, v_a_w_out, v_b_norm_g, v_b_w_in, v_b_f_bias, v_b_q_norm_g, v_b_k_norm_g, v_b_w_out):
    t, d = x.shape[1], x.shape[2]
    n_a, n_b = a_w_in.shape[0], b_w_in.shape[0]
    depth = n_a + n_b
    aw = a_w_out.shape[1] * N_DEV
    groups, chunk = a_w_s.shape[1], a_w_s.shape[2]
    heads, hd = b_f_bias.shape[1], b_q_norm_g.shape[1]
    bw = heads * hd
    b_cols = b_w_in.shape[2]
    tq = _pick(t, 512, LANES)
    nq = t // tq
    me = _dev_index((lax.axis_index("x"), lax.axis_index("y"), lax.axis_index("c")))

    pad_h = lambda v: jnp.pad(v, ((0, 0), (0, LANES - heads)))
    f_bias_p = pad_h(b_f_bias)
    b_s_col = a_b_s.reshape(n_a, groups, chunk, 1)
    w_s_t = a_w_s.transpose(0, 1, 3, 2)

    def weight_gather(i, extra=()):
        j = i // 2
        if i % 2 == 0:
            return _Gather([a_w_in[j].astype(BF16), a_w_out[j].astype(BF16)] + [a for a, _ in extra],
                           [1, 0] + [ax for _, ax in extra])
        return _Gather([b_w_in[j].astype(BF16), b_w_out[j].astype(BF16)] + [a for a, _ in extra],
                       [None, 0] + [ax for _, ax in extra])

    def whole_w_in(i, w_in):
        if i % 2 == 1:
            w_in = w_in.transpose(1, 0, 2).reshape(d, 4 * bw + heads)
            w_in = jnp.pad(w_in, ((0, 0), (0, LANES - heads)))
        return w_in

    (w_in0,) = _run_exchange(_Gather([a_w_in[0].astype(BF16)], [1]), "gather_weights")
    weights = {0: [w_in0, None]}
    late = ((a_w_out[0].astype(BF16), 0), (b_norm_g, 1))

    xs = [x[0]]
    saved = []
    for i in range(depth):
        j = i // 2
        xi = xs[-1]
        nxt = weight_gather(i + 1, extra=late if i == 0 else ()) if i + 1 < depth else None
        if i % 2 == 0:
            h = _rms_fwd(xi, a_norm_g[j:j + 1], f"a{j}_norm")
            res = _matmul(h, weights[i][0], "nn", F32, f"a{j}_in", tn=2048, tk=d, exchange=nxt)
        else:
            h = _rms_fwd(xi, b_norm_full[j:j + 1], f"b{j}_norm")
            res = _matmul(h, weights[i][0], "nn", F32, f"b{j}_in", tn=1664, tk=d, exchange=nxt)
        if nxt is None:
            pre = res
        else:
            pre = res[0]
            weights[i + 1] = [whole_w_in(i + 1, res[1]), res[2]]
            if i == 0:
                weights[0][1], b_norm_full = res[3], res[4]
        w_in, w_out = weights[i]
        if i % 2 == 0:
            y = _mix_fwd(pre, a_v_norm_g[j:j + 1], a_w_s[j], b_s_col[j], f"a{j}_mix")
            xs.append(_matmul(y, w_out, "nn", F32, f"a{j}_out", residual=xi))
            saved.append((h, pre, y))
        else:
            qn, kn, vb, cum, qsq, ksq, cum_first, cum_last = _attn_prep(
                pre, b_q_norm_g[j:j + 1], b_k_norm_g[j:j + 1], f_bias_p[j:j + 1], heads, hd,
                f"b{j}_prep")
            first = _skip_plan(qsq, ksq, cum_first, cum_last, t, tq, hd, f"b{j}_plan")
            first = first[:, :heads].T
            ck = cum[:, :heads].T.reshape(heads, nq, 1, tq)
            o, y, lse = _flash_fwd(first, qn, kn, vb, ck, pre, heads, hd, tq, f"b{j}_attn")
            xs.append(_matmul(y, w_out, "nn", F32, f"b{j}_out", residual=xi))
            saved.append((h, pre, y, qn, kn, vb, ck, o, lse, first))

    sq, g, gb = _loss_grad(xs[-1], loss_target[0], "loss")
    loss = 0.5 * lax.psum(sq[0, 0], AXES) / d

    d_a_norm, d_a_vnorm, d_a_ws, d_a_bs = [None] * n_a, [None] * n_a, [None] * n_a, [None] * n_a
    d_b_norm, d_b_fb, d_b_gq, d_b_gk = [None] * n_b, [None] * n_b, [None] * n_b, [None] * n_b
    recv_in, recv_out = {}, {}
    pending = None

    def in_blocks(i, dw_in):
        if i % 2 == 0:
            return _AllToAll([dw_in], [1])
        rows = dw_in.shape[0]
        return _AllToAll(
            [dw_in[:, :4 * bw + heads].reshape(rows, N_DEV, b_cols).transpose(1, 0, 2)], [None])

    for i in reversed(range(depth)):
        j = i // 2
        xi = xs[i]
        w_in, w_out = weights[i]
        if i % 2 == 0:
            h, uvz, y = saved[i]
            dy = _matmul(gb, w_out, "nt", F32, f"a{j}_dy", tn=2048, tk=d)
            dw_out = _matmul(y, gb, "tn", BF16, f"a{j}_dwout")
            dpre, d_a_ws[j], dbs, d_a_vnorm[j] = _mix_bwd(
                uvz, dy, a_v_norm_g[j:j + 1], a_w_s[j], w_s_t[j], b_s_col[j], f"a{j}_mixbwd")
            d_a_bs[j] = dbs.reshape(groups, chunk)
            name, gain = f"a{j}", a_norm_g[j:j + 1]
        else:
            h, proj, y, qn, kn, vb, ck, o, lse, first = saved[i]
            dy = _matmul(gb, w_out, "nt", F32, f"b{j}_dy", tn=2048, tk=d)
            dw_out = _matmul(y, gb, "tn", BF16, f"b{j}_dwout")
            do, dz, delta = _attn_bwd_prep(dy, proj, o, heads, hd, f"b{j}_bwdprep")
            dqn, dkn, dv, dcq, dck = _flash_bwd(first, qn, kn, vb, do, ck, lse, delta, heads, hd,
                                                tq, f"b{j}_attnbwd")
            per_token = lambda v: pad_h(v.reshape(heads, t).T)
            dpre, d_b_gq[j], d_b_gk[j], dfb = _attn_bwd_post(
                dqn, dkn, dv, dz, proj, per_token(dcq), per_token(dck), b_q_norm_g[j:j + 1],
                b_k_norm_g[j:j + 1], f_bias_p[j:j + 1], heads, hd, f"b{j}_bwdpost")
            d_b_fb[j] = dfb[:, :heads]
            name, gain = f"b{j}", b_norm_full[j:j + 1]
        if pending is None:
            dh = _matmul(dpre, w_in, "nt", F32, name + "_dh")
        else:
            dh, got = _matmul(dpre, w_in, "nt", F32, name + "_dh", exchange=pending[1])
            parts_in = pending[2] + [got]
            recv_in[pending[0]] = parts_in[0] if len(parts_in) == 1 else jnp.concatenate(
                parts_in, axis=1)
        n_parts = (4 if i == 0 else 2) if i % 2 == 0 else 1
        rows = d // n_parts
        riding = _AllToAll([dw_out], [0])
        landed = []
        for part in range(n_parts):
            h_part = h if n_parts == 1 else h[:, part * rows:(part + 1) * rows]
            dw_part, got = _matmul(h_part, dpre, "tn", BF16, f"{name}_dwin{part}",
                                   tn=1024 if i % 2 == 0 else 1664, exchange=riding)
            if part == 0:
                recv_out[i] = got
            else:
                landed.append(got)
            riding = in_blocks(i, dw_part)
        if i > 0:
            pending = (i, riding, landed)
        else:
            landed.append(_run_exchange(riding, "exchange_grads")[0])
            recv_in[i] = jnp.concatenate(landed, axis=1)
        g, gb, dgain = _rms_bwd(xi, dh, g, gain, name + "_normbwd")
        if i % 2 == 0:
            d_a_norm[j] = dgain
        else:
            d_b_norm[j] = dgain
    grad_x = g[None]

    small = [jnp.concatenate(d_a_norm, 0), jnp.concatenate(d_a_vnorm, 0), jnp.stack(d_a_ws, 0),
             jnp.stack(d_a_bs, 0), jnp.concatenate(d_b_fb, 0), jnp.concatenate(d_b_gq, 0),
             jnp.concatenate(d_b_gk, 0)]
    small_w = [a_norm_g, a_v_norm_g, a_w_s, a_b_s, b_f_bias, b_q_norm_g, b_k_norm_g]
    small_m = [m_a_norm_g, m_a_v_norm_g, m_a_w_s, m_a_b_s, m_b_f_bias, m_b_q_norm_g, m_b_k_norm_g]
    small_v = [v_a_norm_g, v_a_v_norm_g, v_a_w_s, v_a_b_s, v_b_f_bias, v_b_q_norm_g, v_b_k_norm_g]
    small_flat = _flat_rows(small)
    n_small = small_flat.shape[0]
    (gathered_small,) = _run_exchange(_Gather(
        [jnp.concatenate([small_flat, _flat_rows([jnp.concatenate(d_b_norm, 0)])], axis=0)],
        [None]), "gather_small_grads")
    parts_small = gathered_small[:, :n_small]
    parts_b_norm = gathered_small[:, n_small:n_small + n_b * d // LANES].reshape(N_DEV, n_b, d)
    parts_b_norm = lax.dynamic_slice_in_dim(parts_b_norm, me * (d // N_DEV), d // N_DEV, axis=2)

    a_layers, b_layers = range(0, depth, 2), range(1, depth, 2)
    u_a_in = _adamw(a_w_in, m_a_w_in, v_a_w_in, [recv_in[i] for i in a_layers], "adamw_a_w_in")
    u_a_out = _adamw(a_w_out, m_a_w_out, v_a_w_out, [recv_out[i] for i in a_layers],
                     "adamw_a_w_out")
    u_b_in = _adamw(b_w_in, m_b_w_in, v_b_w_in, [recv_in[i] for i in b_layers], "adamw_b_w_in")
    u_b_out = _adamw(b_w_out, m_b_w_out, v_b_w_out, [recv_out[i] for i in b_layers],
                     "adamw_b_w_out")
    u_b_norm = [o_[0] for o_ in _adamw(b_norm_g[None], m_b_norm_g[None], v_b_norm_g[None],
                                       [parts_b_norm], "adamw_b_norm")]
    u_small = _adamw(_flat_rows(small_w)[None], _flat_rows(small_m)[None],
                     _flat_rows(small_v)[None], [parts_small], "adamw_small")
    shapes = [w.shape for w in small_w]
    u_small = [_unflat_rows(o_[0], shapes) for o_ in u_small]

    def per_kind(k):
        s = u_small[k]
        return [s[0], u_a_in[k], s[1], s[2], s[3], u_a_out[k], u_b_norm[k], u_b_in[k], s[4], s[5],
                s[6], u_b_out[k]]

    return (loss, grad_x, *per_kind(0), *per_kind(1), *per_kind(2), *per_kind(3))
```

```python
import functools

import jax
import jax.numpy as jnp
from jax import lax
from jax.experimental import pallas as pl
from jax.experimental.pallas import tpu as pltpu

F32 = jnp.float32
BF16 = jnp.bfloat16
MESH = pl.DeviceIdType.MESH
AXES = ("x", "y", "c")
N_DEV = 8
NORM_EPS = 1e-6
LANES = 128
VMEM_LIMIT = 56 * 1024 * 1024

ADAM_LR = 0.001
ADAM_B1 = 0.9
ADAM_B2 = 0.999
ADAM_EPS = 1e-08
ADAM_WD = 0.01
ADAM_STEP = 10

LOG2E = 1.4426950408889634
SOFTMAX_ROWS = 64
UNDERFLOW_LOG2 = -160.0

GELU_C0 = 0.7978845608028654
GELU_C1 = 0.044715

NT_DIMS = (((1,), (1,)), ((), ()))
TN_DIMS = (((0,), (0,)), ((), ()))


def _params(sem=None):
    return pltpu.CompilerParams(dimension_semantics=sem, vmem_limit_bytes=VMEM_LIMIT)


def _pick(n, target, unit):
    best = None
    for t in range(unit, min(n, target) + 1, unit):
        if n % t == 0:
            best = t
    return n if best is None else best


def _sigmoid(x):
    return 1.0 / (1.0 + jnp.exp(-x))


def _gelu(x):
    return 0.5 * x * (1.0 + jnp.tanh(GELU_C0 * x * (1.0 + GELU_C1 * x * x)))


def _gelu_and_grad(x):
    x2 = x * x
    t = jnp.tanh(GELU_C0 * x * (1.0 + GELU_C1 * x2))
    g = 0.5 * x * (1.0 + t)
    dg = 0.5 * (1.0 + t) + 0.5 * x * (1.0 - t * t) * (GELU_C0 * (1.0 + 3.0 * GELU_C1 * x2))
    return g, dg


def _dot(a, b):
    return jnp.dot(a, b, preferred_element_type=F32)


def _dot_nt(a, b):
    return lax.dot_general(a, b, NT_DIMS, preferred_element_type=F32)


def _dot_tn(a, b):
    return lax.dot_general(a, b, TN_DIMS, preferred_element_type=F32)


def _split3(v):
    hi = v.astype(BF16)
    r1 = v - hi.astype(F32)
    mid = r1.astype(BF16)
    lo = (r1 - mid.astype(F32)).astype(BF16)
    return hi, mid, lo


def _dev_index(p):
    return 4 * p[0] + 2 * p[1] + p[2]


def _block(ref, idx, axis, size):
    if axis is None:
        return ref.at[idx]
    start = pl.multiple_of(idx * size, size)
    return ref.at[(slice(None),) * axis + (pl.ds(start, size),)]


class _Exchange:
    def __init__(self, arrays, axes):
        self.arrays, self.axes, self.n = list(arrays), list(axes), len(arrays)

    def scratch(self):
        return [pltpu.SemaphoreType.DMA((self.n, 7)), pltpu.SemaphoreType.DMA((self.n, 7)),
                pltpu.SemaphoreType.DMA((self.n,))]

    @staticmethod
    def _place():
        x, y, c = lax.axis_index("x"), lax.axis_index("y"), lax.axis_index("c")
        return x, y, c


class _Gather(_Exchange):
    def out_shapes(self):
        outs = []
        for a, ax in zip(self.arrays, self.axes):
            if ax is None:
                shape = (N_DEV,) + a.shape
            else:
                shape = a.shape[:ax] + (N_DEV * a.shape[ax],) + a.shape[ax + 1:]
            outs.append(jax.ShapeDtypeStruct(shape, a.dtype))
        return outs

    def _copy(self, ins, outs, sems, a, k, block, to, src=None):
        ax = self.axes[a]
        dst = _block(outs[a], _dev_index(block), ax, None if ax is None else self.arrays[a].shape[ax])
        return pltpu.make_async_remote_copy(
            src_ref=dst if src is None else src, dst_ref=dst,
            send_sem=sems[0].at[a, k], recv_sem=sems[1].at[a, k],
            device_id=to, device_id_type=MESH)

    def _mine(self, ins, outs, sems, a, me):
        ax = self.axes[a]
        dst = _block(outs[a], _dev_index(me), ax, None if ax is None else self.arrays[a].shape[ax])
        return pltpu.make_async_copy(ins[a], dst, sems[2].at[a])

    def _first(self, ins, outs, sems):
        x, y, c = self._place()
        me, sibling = (x, y, c), (x, y, 1 - c)
        chips = [(1 - x, y), (x, 1 - y), (1 - x, 1 - y)]
        first = []
        for a in range(self.n):
            first.append(self._copy(ins, outs, sems, a, 0, me, sibling, src=ins[a]))
            first += [self._copy(ins, outs, sems, a, 1 + j, me, (*chip, c), src=ins[a])
                      for j, chip in enumerate(chips)]
        return first

    def start(self, ins, outs, sems):
        me = self._place()
        for a in range(self.n):
            self._mine(ins, outs, sems, a, me).start()
        for cp in self._first(ins, outs, sems):
            cp.start()

    def finish(self, ins, outs, sems):
        x, y, c = self._place()
        me, sibling = (x, y, c), (x, y, 1 - c)
        chips = [(1 - x, y), (x, 1 - y), (1 - x, 1 - y)]
        passed = []
        for a in range(self.n):
            for j, chip in enumerate(chips):
                self._copy(ins, outs, sems, a, 1 + j, (*chip, c), me).wait_recv()
                cp = self._copy(ins, outs, sems, a, 4 + j, (*chip, c), sibling)
                cp.start()
                passed.append(cp)
        for a in range(self.n):
            self._copy(ins, outs, sems, a, 0, sibling, me).wait_recv()
            for j, chip in enumerate(chips):
                self._copy(ins, outs, sems, a, 4 + j, (*chip, 1 - c), me).wait_recv()
        for cp in self._first(ins, outs, sems) + passed:
            cp.wait_send()
        for a in range(self.n):
            self._mine(ins, outs, sems, a, me).wait()


class _AllToAll(_Exchange):
    def _blk_shape(self, a):
        arr, ax = self.arrays[a], self.axes[a]
        if ax is None:
            return arr.shape[1:]
        return arr.shape[:ax] + (arr.shape[ax] // N_DEV,) + arr.shape[ax + 1:]

    def out_shapes(self):
        return [jax.ShapeDtypeStruct((N_DEV,) + self._blk_shape(a), self.arrays[a].dtype)
                for a in range(self.n)]

    def _src(self, ins, a, idx):
        ax = self.axes[a]
        return _block(ins[a], idx, ax, None if ax is None else self.arrays[a].shape[ax] // N_DEV)

    def _peers(self):
        x, y, c = self._place()
        return [((1 - x) if r & 4 else x, (1 - y) if r & 2 else y, (1 - c) if r & 1 else c)
                for r in range(1, N_DEV)]

    def _sends(self, ins, outs, sems):
        me = _dev_index(self._place())
        return [pltpu.make_async_remote_copy(
            src_ref=self._src(ins, a, _dev_index(peer)), dst_ref=outs[a].at[me],
            send_sem=sems[0].at[a, k], recv_sem=sems[1].at[a, k],
            device_id=peer, device_id_type=MESH)
            for a in range(self.n) for k, peer in enumerate(self._peers())]

    def _mine(self, ins, outs, sems):
        me = _dev_index(self._place())
        return [pltpu.make_async_copy(self._src(ins, a, me), outs[a].at[me], sems[2].at[a])
                for a in range(self.n)]

    def start(self, ins, outs, sems):
        for cp in self._mine(ins, outs, sems) + self._sends(ins, outs, sems):
            cp.start()

    def finish(self, ins, outs, sems):
        for a in range(self.n):
            for k, peer in enumerate(self._peers()):
                landed = outs[a].at[_dev_index(peer)]
                pltpu.make_async_remote_copy(
                    src_ref=landed, dst_ref=landed, send_sem=sems[0].at[a, k],
                    recv_sem=sems[1].at[a, k], device_id=peer, device_id_type=MESH).wait_recv()
        for cp in self._sends(ins, outs, sems):
            cp.wait_send()
        for cp in self._mine(ins, outs, sems):
            cp.wait()


def _run_exchange(exchange, name):
    n = exchange.n

    def body(*refs):
        ins, outs, sems = refs[:n], refs[n:2 * n], refs[2 * n:]
        exchange.start(ins, outs, sems)
        exchange.finish(ins, outs, sems)

    any_spec = pl.BlockSpec(memory_space=pl.ANY)
    return pl.pallas_call(
        body, name=name, out_shape=exchange.out_shapes(),
        in_specs=[any_spec] * n, out_specs=[any_spec] * n, scratch_shapes=exchange.scratch(),
    )(*exchange.arrays)


def _matmul(a, b, mode, out_dtype, name, tm=1024, tn=1024, tk=2048, residual=None, exchange=None):
    if mode == "tn":
        kdim, m = a.shape
    else:
        m, kdim = a.shape
    n = b.shape[0] if mode == "nt" else b.shape[1]
    tm, tn, tk = _pick(m, tm, LANES), _pick(n, tn, LANES), _pick(kdim, tk, LANES)
    nk = kdim // tk
    if mode == "tn":
        a_spec = pl.BlockSpec((tk, tm), lambda i, j, k: (k, i))
    else:
        a_spec = pl.BlockSpec((tm, tk), lambda i, j, k: (i, k))
    if mode == "nt":
        b_spec = pl.BlockSpec((tn, tk), lambda i, j, k: (j, k))
    else:
        b_spec = pl.BlockSpec((tk, tn), lambda i, j, k: (k, j))
    o_spec = pl.BlockSpec((tm, tn), lambda i, j, k: (i, j))
    dot = {"nn": _dot, "nt": _dot_nt, "tn": _dot_tn}[mode]
    has_res = residual is not None
    n_in = 3 if has_res else 2
    n_ex = 0 if exchange is None else exchange.n
    ni, nj = m // tm, n // tn

    def body(*refs):
        a_ref, b_ref = refs[:2]
        r_ref = refs[2] if has_res else None
        ex_ins = refs[n_in:n_in + n_ex]
        o_ref = refs[n_in + n_ex]
        ex_outs = refs[n_in + n_ex + 1:n_in + 2 * n_ex + 1]
        scratch = refs[n_in + 2 * n_ex + 1:]
        i, j, k = pl.program_id(0), pl.program_id(1), pl.program_id(2)

        if exchange is not None:
            sems = scratch[-3:]

            @pl.when((i == 0) & (j == 0) & (k == 0))
            def _():
                exchange.start(ex_ins, ex_outs, sems)

        def finish(acc):
            if has_res:
                acc = acc + r_ref[...]
            o_ref[...] = acc.astype(out_dtype)

        if nk == 1:
            finish(dot(a_ref[...], b_ref[...]))
        else:
            acc_ref = scratch[0]

            @pl.when(k == 0)
            def _():
                acc_ref[...] = jnp.zeros_like(acc_ref)

            acc_ref[...] += dot(a_ref[...], b_ref[...])

            @pl.when(k == nk - 1)
            def _():
                finish(acc_ref[...])

        if exchange is not None:
            @pl.when((i == ni - 1) & (j == nj - 1) & (k == nk - 1))
            def _():
                exchange.finish(ex_ins, ex_outs, sems)

    any_spec = pl.BlockSpec(memory_space=pl.ANY)
    operands = (a, b, residual) if has_res else (a, b)
    out_shape = jax.ShapeDtypeStruct((m, n), out_dtype)
    scratch_shapes = [] if nk == 1 else [pltpu.VMEM((tm, tn), F32)]
    if exchange is None:
        return pl.pallas_call(
            body, name=name, grid=(ni, nj, nk),
            in_specs=[a_spec, b_spec] + ([o_spec] if has_res else []),
            out_specs=o_spec, out_shape=out_shape, scratch_shapes=scratch_shapes,
            compiler_params=_params(("parallel", "parallel", "arbitrary")),
        )(*operands)
    return pl.pallas_call(
        body, name=name, grid=(ni, nj, nk),
        in_specs=[a_spec, b_spec] + ([o_spec] if has_res else []) + [any_spec] * n_ex,
        out_specs=[o_spec] + [any_spec] * n_ex,
        out_shape=[out_shape] + exchange.out_shapes(),
        scratch_shapes=scratch_shapes + exchange.scratch(),
        compiler_params=_params(("arbitrary", "arbitrary", "arbitrary")),
    )(*operands, *exchange.arrays)


def _rms_fwd(x, gain, name):
    t, d = x.shape
    tr = _pick(t, 256, 16)

    def body(x_ref, g_ref, h_ref):
        xv = x_ref[...]
        r = lax.rsqrt(jnp.mean(xv * xv, axis=1, keepdims=True) + NORM_EPS)
        h_ref[...] = (xv * r * g_ref[...]).astype(BF16)

    row = pl.BlockSpec((tr, d), lambda i: (i, 0))
    return pl.pallas_call(
        body, name=name, grid=(t // tr,),
        in_specs=[row, pl.BlockSpec((1, d), lambda i: (0, 0))],
        out_specs=row, out_shape=jax.ShapeDtypeStruct((t, d), BF16),
        compiler_params=_params(("arbitrary",)),
    )(x, gain)


def _carry_exchange(body, exchange, n_in, n_out, n_scratch, is_first, is_last):
    n = exchange.n

    def carrying(*refs):
        ins, ex_ins = refs[:n_in], refs[n_in:n_in + n]
        outs = refs[n_in + n:n_in + n + n_out]
        ex_outs = refs[n_in + n + n_out:n_in + 2 * n + n_out]
        scratch = refs[n_in + 2 * n + n_out:n_in + 2 * n + n_out + n_scratch]
        sems = refs[n_in + 2 * n + n_out + n_scratch:]

        @pl.when(is_first())
        def _():
            exchange.start(ex_ins, ex_outs, sems)

        body(*ins, *outs, *scratch)

        @pl.when(is_last())
        def _():
            exchange.finish(ex_ins, ex_outs, sems)

    return carrying


def _rms_bwd(x, dh, g_in, gain, name, exchange=None):
    t, d = x.shape
    tr = _pick(t, 256, 16)
    nt = t // tr

    def body(x_ref, dh_ref, gin_ref, g_ref, dx_ref, dxb_ref, dg_ref):
        i = pl.program_id(0)
        xv, dhv = x_ref[...], dh_ref[...]
        r = lax.rsqrt(jnp.mean(xv * xv, axis=1, keepdims=True) + NORM_EPS)
        xh = xv * r
        dxh = dhv * g_ref[...]
        dx = gin_ref[...] + r * (dxh - xh * jnp.mean(dxh * xh, axis=1, keepdims=True))
        dx_ref[...] = dx
        dxb_ref[...] = dx.astype(BF16)

        @pl.when(i == 0)
        def _():
            dg_ref[...] = jnp.zeros_like(dg_ref)

        dg_ref[...] += jnp.sum(dhv * xh, axis=0, keepdims=True)

    row = pl.BlockSpec((tr, d), lambda i: (i, 0))
    vec = pl.BlockSpec((1, d), lambda i: (0, 0))
    in_specs, out_specs = [row, row, row, vec], [row, row, vec]
    out_shape = [jax.ShapeDtypeStruct((t, d), F32), jax.ShapeDtypeStruct((t, d), BF16),
                 jax.ShapeDtypeStruct((1, d), F32)]
    operands, scratch = [x, dh, g_in, gain], []
    if exchange is not None:
        any_spec = pl.BlockSpec(memory_space=pl.ANY)
        body = _carry_exchange(body, exchange, 4, 3, 0, lambda: pl.program_id(0) == 0,
                               lambda: pl.program_id(0) == nt - 1)
        in_specs, out_specs = in_specs + [any_spec] * exchange.n, out_specs + [any_spec] * exchange.n
        out_shape, operands = out_shape + exchange.out_shapes(), operands + exchange.arrays
        scratch = exchange.scratch()
    return pl.pallas_call(
        body, name=name, grid=(nt,), in_specs=in_specs, out_specs=out_specs, out_shape=out_shape,
        scratch_shapes=scratch, compiler_params=_params(("arbitrary",)),
    )(*operands)


def _loss_grad(y, target, name):
    t, d = y.shape
    tr = _pick(t, 256, 16)

    def body(y_ref, t_ref, s_ref, g_ref, gb_ref):
        i = pl.program_id(0)
        e = y_ref[...] - t_ref[...]
        g = e * (1.0 / d)
        g_ref[...] = g
        gb_ref[...] = g.astype(BF16)

        @pl.when(i == 0)
        def _():
            s_ref[...] = jnp.zeros_like(s_ref)

        s_ref[...] += jnp.sum(jnp.sum(e * e, axis=1, keepdims=True), axis=0, keepdims=True)

    row = pl.BlockSpec((tr, d), lambda i: (i, 0))
    return pl.pallas_call(
        body, name=name, grid=(t // tr,),
        in_specs=[row, row],
        out_specs=[pl.BlockSpec((1, 1), lambda i: (0, 0)), row, row],
        out_shape=[jax.ShapeDtypeStruct((1, 1), F32), jax.ShapeDtypeStruct((t, d), F32),
                   jax.ShapeDtypeStruct((t, d), BF16)],
        compiler_params=_params(("arbitrary",)),
    )(y, target)


def _causal_weights(ws_ref, g, chunk, transposed):
    rows = lax.broadcasted_iota(jnp.int32, (chunk, chunk), 0)
    cols = lax.broadcasted_iota(jnp.int32, (chunk, chunk), 1)
    keep = (cols >= rows) if transposed else (rows >= cols)
    return jnp.where(keep, ws_ref[g], 0.0).astype(BF16)


def _mix_fwd(uvz, v_gain, w_s, b_s, name):
    t, w3 = uvz.shape
    w = w3 // 3
    groups, chunk = w_s.shape[0], w_s.shape[1]
    gd = w // groups

    def body(uvz_ref, gam_ref, ws_ref, bs_ref, y_ref):
        gv = _gelu(uvz_ref[:, w:2 * w])
        r = lax.rsqrt(jnp.mean(gv * gv, axis=1, keepdims=True) + NORM_EPS)
        vn = (gv * r * gam_ref[...]).astype(BF16)
        for g in range(groups):
            sl = slice(g * gd, (g + 1) * gd)
            mixed = _dot(_causal_weights(ws_ref, g, chunk, False), vn[:, sl]) + bs_ref[g]
            u = uvz_ref[:, g * gd:(g + 1) * gd]
            z = uvz_ref[:, 2 * w + g * gd:2 * w + (g + 1) * gd]
            y_ref[:, sl] = (_gelu(u) * mixed * (z * _sigmoid(z))).astype(BF16)

    return pl.pallas_call(
        body, name=name, grid=(t // chunk,),
        in_specs=[pl.BlockSpec((chunk, w3), lambda i: (i, 0)),
                  pl.BlockSpec((1, w), lambda i: (0, 0)),
                  pl.BlockSpec((groups, chunk, chunk), lambda i: (0, 0, 0)),
                  pl.BlockSpec((groups, chunk, 1), lambda i: (0, 0, 0))],
        out_specs=pl.BlockSpec((chunk, w), lambda i: (i, 0)),
        out_shape=jax.ShapeDtypeStruct((t, w), BF16),
        compiler_params=_params(("arbitrary",)),
    )(uvz, v_gain, w_s, b_s)


def _mix_bwd(uvz, dy, v_gain, w_s, w_s_t, b_s, name):
    t, w3 = uvz.shape
    w = w3 // 3
    groups, chunk = w_s.shape[0], w_s.shape[1]
    gd = w // groups

    def body(uvz_ref, dy_ref, gam_ref, ws_ref, wst_ref, bs_ref, d_ref, dws_ref, dbs_ref, dgam_ref,
             dvn_ref):
        i = pl.program_id(0)

        @pl.when(i == 0)
        def _():
            dws_ref[...] = jnp.zeros_like(dws_ref)
            dbs_ref[...] = jnp.zeros_like(dbs_ref)
            dgam_ref[...] = jnp.zeros_like(dgam_ref)

        gv, dgv = _gelu_and_grad(uvz_ref[:, w:2 * w])
        r = lax.rsqrt(jnp.mean(gv * gv, axis=1, keepdims=True) + NORM_EPS)
        vh = gv * r
        gam = gam_ref[...]
        vn = (vh * gam).astype(BF16)
        rows = lax.broadcasted_iota(jnp.int32, (chunk, chunk), 0)
        cols = lax.broadcasted_iota(jnp.int32, (chunk, chunk), 1)
        for g in range(groups):
            sl = slice(g * gd, (g + 1) * gd)
            mixed = _dot(_causal_weights(ws_ref, g, chunk, False), vn[:, sl]) + bs_ref[g]
            gu, dgu = _gelu_and_grad(uvz_ref[:, g * gd:(g + 1) * gd])
            z = uvz_ref[:, 2 * w + g * gd:2 * w + (g + 1) * gd]
            sz = _sigmoid(z)
            silu = z * sz
            dyv = dy_ref[:, sl]
            dmixed = dyv * gu * silu
            d_ref[:, sl] = (dyv * mixed * silu * dgu).astype(BF16)
            d_ref[:, 2 * w + g * gd:2 * w + (g + 1) * gd] = (
                dyv * gu * mixed * (sz * (1.0 + z * (1.0 - sz)))).astype(BF16)
            dmb = dmixed.astype(BF16)
            dws_ref[g] += jnp.where(rows >= cols, _dot_nt(dmb, vn[:, sl]), 0.0)
            dbs_ref[g] += jnp.sum(dmixed, axis=1, keepdims=True)
            dvn_ref[:, sl] = _dot(_causal_weights(wst_ref, g, chunk, True), dmb)
        dvn = dvn_ref[...]
        dgam_ref[...] += jnp.sum(dvn * vh, axis=0, keepdims=True)
        dvh = dvn * gam
        dgvv = r * (dvh - vh * jnp.mean(dvh * vh, axis=1, keepdims=True))
        d_ref[:, w:2 * w] = (dgvv * dgv).astype(BF16)

    return pl.pallas_call(
        body, name=name, grid=(t // chunk,),
        in_specs=[pl.BlockSpec((chunk, w3), lambda i: (i, 0)),
                  pl.BlockSpec((chunk, w), lambda i: (i, 0)),
                  pl.BlockSpec((1, w), lambda i: (0, 0)),
                  pl.BlockSpec((groups, chunk, chunk), lambda i: (0, 0, 0)),
                  pl.BlockSpec((groups, chunk, chunk), lambda i: (0, 0, 0)),
                  pl.BlockSpec((groups, chunk, 1), lambda i: (0, 0, 0))],
        out_specs=[pl.BlockSpec((chunk, w3), lambda i: (i, 0)),
                   pl.BlockSpec((groups, chunk, chunk), lambda i: (0, 0, 0)),
                   pl.BlockSpec((groups, chunk, 1), lambda i: (0, 0, 0)),
                   pl.BlockSpec((1, w), lambda i: (0, 0))],
        out_shape=[jax.ShapeDtypeStruct((t, w3), BF16),
                   jax.ShapeDtypeStruct((groups, chunk, chunk), F32),
                   jax.ShapeDtypeStruct((groups, chunk, 1), F32),
                   jax.ShapeDtypeStruct((1, w), F32)],
        scratch_shapes=[pltpu.VMEM((chunk, w), F32)],
        compiler_params=_params(("arbitrary",)),
    )(uvz, dy, v_gain, w_s, w_s_t, b_s)


def _attn_prep(proj, q_gain, k_gain, f_bias, heads, hd, name):
    t = proj.shape[0]
    bw = heads * hd
    tr = _pick(t, 256, 16)
    fcol = 4 * bw // LANES

    def body(q_ref, k_ref, v_ref, f_ref, gq_ref, gk_ref, fb_ref, qn_ref, kn_ref, vb_ref, cum_ref,
             qsq_ref, ksq_ref, first_ref, last_ref, carry_ref):
        i = pl.program_id(0)

        @pl.when(i == 0)
        def _():
            carry_ref[...] = jnp.zeros_like(carry_ref)

        lane = lax.broadcasted_iota(jnp.int32, (1, LANES), 1)
        for src, gain, dst, sq_ref in ((q_ref, gq_ref, qn_ref, qsq_ref),
                                       (k_ref, gk_ref, kn_ref, ksq_ref)):
            sq_row = jnp.zeros((1, LANES), F32)
            for h in range(heads):
                sl = slice(h * hd, (h + 1) * hd)
                v = src[:, sl]
                r = lax.rsqrt(jnp.mean(v * v, axis=1, keepdims=True) + NORM_EPS)
                normed = (v * r * gain[...]).astype(BF16)
                dst[:, sl] = normed
                nf = normed.astype(F32)
                sq = jnp.max(jnp.sum(nf * nf, axis=1, keepdims=True), axis=0, keepdims=True)
                sq_row = jnp.where(lane == h, sq, sq_row)
            sq_ref[0] = sq_row
        vb_ref[...] = v_ref[...].astype(BF16)
        fl = f_ref[...] + fb_ref[...]
        log_f = jnp.minimum(fl, 0.0) - jnp.log(1.0 + jnp.exp(-jnp.abs(fl)))
        rows = lax.broadcasted_iota(jnp.int32, (tr, tr), 0)
        cols = lax.broadcasted_iota(jnp.int32, (tr, tr), 1)
        lower = jnp.where(rows >= cols, 1.0, 0.0).astype(BF16)
        hi, mid, lo = _split3(log_f)
        cum_ref[...] = (_dot(lower, hi) + _dot(lower, mid) + _dot(lower, lo)) + carry_ref[...]
        carry_ref[...] = cum_ref[tr - 1:tr, :]
        first_ref[0] = cum_ref[0:1, :]
        last_ref[0] = cum_ref[tr - 1:tr, :]

    wide = lambda col: pl.BlockSpec((tr, bw), lambda i: (i, col))
    vec = pl.BlockSpec((1, hd), lambda i: (0, 0))
    stat = pl.BlockSpec((1, 1, LANES), lambda i: (i, 0, 0))
    return pl.pallas_call(
        body, name=name, grid=(t // tr,),
        in_specs=[wide(0), wide(1), wide(2), pl.BlockSpec((tr, LANES), lambda i: (i, fcol)),
                  vec, vec, pl.BlockSpec((1, LANES), lambda i: (0, 0))],
        out_specs=[wide(0), wide(0), wide(0), pl.BlockSpec((tr, LANES), lambda i: (i, 0))]
        + [stat] * 4,
        out_shape=[jax.ShapeDtypeStruct((t, bw), BF16)] * 3 + [jax.ShapeDtypeStruct((t, LANES), F32)]
        + [jax.ShapeDtypeStruct((t // tr, 1, LANES), F32)] * 4,
        scratch_shapes=[pltpu.VMEM((1, LANES), F32)],
        compiler_params=_params(("arbitrary",)),
    )(proj, proj, proj, proj, q_gain, k_gain, f_bias)


def _skip_plan(qsq, ksq, first, last, t, tq, hd, name):
    nt = qsq.shape[0]
    nq = t // tq
    r = nt // nq
    scale2 = hd ** -0.5 * LOG2E

    def body(qsq_ref, ksq_ref, first_ref, last_ref, out_ref):
        kmax = ksq_ref[0]
        for tile in range(1, nt):
            kmax = jnp.maximum(kmax, ksq_ref[tile])
        for i in range(nq):
            qmax = qsq_ref[i * r]
            for tile in range(i * r + 1, (i + 1) * r):
                qmax = jnp.maximum(qmax, qsq_ref[tile])
            coef = 2.0 * scale2 * jnp.sqrt(qmax * kmax)
            start = first_ref[i * r]
            count = jnp.zeros((1, LANES), jnp.int32)
            for j in range(i):
                bound = coef + (start - last_ref[(j + 1) * r - 1]) * LOG2E
                count = count + jnp.where(bound <= UNDERFLOW_LOG2, 1, 0)
            out_ref[i:i + 1, :] = count

    return pl.pallas_call(
        body, name=name, out_shape=jax.ShapeDtypeStruct((nq, LANES), jnp.int32),
    )(qsq, ksq, first, last)


def _flash_fwd(first_block, qn, kn, vb, ck, proj, heads, hd, tq, name):
    t = qn.shape[0]
    nq = t // tq
    th = tq // 2
    scale2 = hd ** -0.5 * LOG2E
    zcol = 3 * heads
    rc = _pick(tq, SOFTMAX_ROWS, 16)
    reps = th // LANES

    def body(first_ref, q_ref, k_ref, v_ref, ck_ref, z_ref, o_ref, y_ref, lse_ref,
             m_s, l_s, acc_s, s_a, s_b, p_a, p_b, al_a, al_b):
        i = pl.program_id(1)
        j0 = first_ref[pl.program_id(0), i]
        bufs = ((s_a, p_a, al_a), (s_b, p_b, al_b))
        m_s[...] = jnp.full_like(m_s, -jnp.inf)
        l_s[...] = jnp.zeros_like(l_s)
        acc_s[...] = jnp.zeros_like(acc_s)
        p_b[...] = jnp.zeros_like(p_b)
        al_b[...] = jnp.ones_like(al_b)

        def scores(j, half, first_row=0):
            off = pl.multiple_of(j * tq + half * th, th)
            bufs[half][0][first_row:, :] = _dot_nt(q_ref[first_row:, :], k_ref[pl.ds(off, th), :])

        def values(j, half, first_row=0):
            off = pl.multiple_of(j * tq + half * th, th)
            _, p_buf, al = bufs[half]
            acc_s[first_row:, :] = (
                jnp.tile(al[first_row:, :], (1, hd // LANES)) * acc_s[first_row:, :]
                + _dot(p_buf[first_row:, :], v_ref[pl.ds(off, th), :]))

        def softmax(j, half, masked):
            s_buf, p_buf, al = bufs[half]
            ck2 = ck_ref[0, j][:, half * th:(half + 1) * th] * LOG2E
            chunks = range((half * th) // rc if masked else 0, tq // rc)
            for c in chunks:
                r = slice(c * rc, (c + 1) * rc)
                s = s_buf[r, :] * scale2 - ck2
                if masked and c * rc < (half + 1) * th:
                    rows = lax.broadcasted_iota(jnp.int32, (rc, th), 0) + c * rc
                    cols = lax.broadcasted_iota(jnp.int32, (rc, th), 1) + half * th
                    s = jnp.where(rows >= cols, s, -jnp.inf)
                s_buf[r, :] = s
                m_prev = m_s[r, :]
                m_new = jnp.maximum(m_prev, jnp.max(s, axis=1, keepdims=True))
                al[r, :] = jnp.exp2(m_prev - m_new)
                m_s[r, :] = m_new
            for c in chunks:
                r = slice(c * rc, (c + 1) * rc)
                p = jnp.exp2(s_buf[r, :] - jnp.tile(m_s[r, :], (1, reps)))
                p_buf[r, :] = p.astype(BF16)
                lane_sum = p[:, 0:LANES]
                for b in range(1, reps):
                    lane_sum = lane_sum + p[:, b * LANES:(b + 1) * LANES]
                l_s[r, :] = al[r, :] * l_s[r, :] + lane_sum

        scores(j0, 0)

        def loop_body(j, carry):
            scores(j, 1)
            values(jnp.maximum(j - 1, 0), 1)
            softmax(j, 0, False)
            scores(j + 1, 0)
            values(j, 0)
            softmax(j, 1, False)
            return carry

        lax.fori_loop(j0, i, loop_body, 0)
        values(jnp.maximum(i - 1, 0), 1)
        scores(i, 1, first_row=th)
        softmax(i, 0, True)
        values(i, 0)
        softmax(i, 1, True)
        values(i, 1, first_row=th)
        l = jnp.sum(l_s[...], axis=1, keepdims=True)
        o = acc_s[...] / l
        z = z_ref[...]
        o_ref[...] = o
        y_ref[...] = (o * (z * _sigmoid(z))).astype(BF16)
        lse_ref[0] = m_s[:, 0:1] + jnp.log(l) * LOG2E

    blk = pl.BlockSpec((tq, hd), lambda h, i, first: (i, h))
    head = pl.BlockSpec((t, hd), lambda h, i, first: (0, h))
    col = pl.BlockSpec((1, tq, 1), lambda h, i, first: (h, i, 0))
    stat = pltpu.VMEM((tq, LANES), F32)
    return pl.pallas_call(
        body, name=name,
        grid_spec=pltpu.PrefetchScalarGridSpec(
            num_scalar_prefetch=1, grid=(heads, nq),
            in_specs=[blk, head, head,
                      pl.BlockSpec((1, nq, 1, tq), lambda h, i, first: (h, 0, 0, 0)),
                      pl.BlockSpec((tq, hd), lambda h, i, first: (i, zcol + h))],
            out_specs=[blk, blk, col],
            scratch_shapes=[stat, stat, pltpu.VMEM((tq, hd), F32),
                            pltpu.VMEM((tq, th), F32), pltpu.VMEM((tq, th), F32),
                            pltpu.VMEM((tq, th), BF16), pltpu.VMEM((tq, th), BF16), stat, stat]),
        out_shape=[jax.ShapeDtypeStruct((t, heads * hd), F32),
                   jax.ShapeDtypeStruct((t, heads * hd), BF16),
                   jax.ShapeDtypeStruct((heads, t, 1), F32)],
        compiler_params=_params(("arbitrary", "arbitrary")),
    )(first_block, qn, kn, vb, ck, proj)


def _attn_bwd_prep(dy, proj, o, heads, hd, name):
    t, bw = dy.shape
    tr = _pick(t, 256, 16)

    def body(dy_ref, z_ref, o_ref, do_ref, dz_ref, delta_ref):
        dyv, z, ov = dy_ref[...], z_ref[...], o_ref[...]
        sz = _sigmoid(z)
        do = dyv * (z * sz)
        do_ref[...] = do.astype(BF16)
        dz_ref[...] = (dyv * ov * (sz * (1.0 + z * (1.0 - sz)))).astype(BF16)
        prod = do * ov
        for h in range(heads):
            delta_ref[h] = jnp.sum(prod[:, h * hd:(h + 1) * hd], axis=1, keepdims=True)

    row = pl.BlockSpec((tr, bw), lambda i: (i, 0))
    return pl.pallas_call(
        body, name=name, grid=(t // tr,),
        in_specs=[row, pl.BlockSpec((tr, bw), lambda i: (i, 3)), row],
        out_specs=[row, row, pl.BlockSpec((heads, tr, 1), lambda i: (0, i, 0))],
        out_shape=[jax.ShapeDtypeStruct((t, bw), BF16), jax.ShapeDtypeStruct((t, bw), BF16),
                   jax.ShapeDtypeStruct((heads, t, 1), F32)],
        compiler_params=_params(("arbitrary",)),
    )(dy, proj, o)


def _flash_bwd(first_block, qn, kn, vb, do, ck, lse, delta, heads, hd, tq, name):
    t = qn.shape[0]
    nq = t // tq
    scale = hd ** -0.5
    scale2 = scale * LOG2E

    def body(first_ref, q_ref, k_ref, v_ref, do_ref, ck_ref, lse_ref, delta_ref,
             dq_ref, dk_ref, dv_ref, dcq_ref, dck_ref, dq_s, dcq_s):
        i = pl.program_id(1)
        j0 = first_ref[pl.program_id(0), i]

        @pl.when(i == 0)
        def _():
            dk_ref[...] = jnp.zeros_like(dk_ref)
            dv_ref[...] = jnp.zeros_like(dv_ref)
            dck_ref[...] = jnp.zeros_like(dck_ref)

        dq_s[...] = jnp.zeros_like(dq_s)
        dcq_s[...] = jnp.zeros_like(dcq_s)

        def step(j, masked, row0=0, col0=0, ncol=tq):
            off = pl.multiple_of(j * tq + col0, ncol)
            kblk = k_ref[pl.ds(off, ncol), :]
            q, dov = q_ref[row0:, :], do_ref[row0:, :]
            s = _dot_nt(q, kblk) * scale2 - ck_ref[0, j][:, col0:col0 + ncol] * LOG2E
            if masked:
                rows = lax.broadcasted_iota(jnp.int32, (tq - row0, ncol), 0) + row0
                cols = lax.broadcasted_iota(jnp.int32, (tq - row0, ncol), 1) + col0
                s = jnp.where(rows >= cols, s, -jnp.inf)
            p = jnp.exp2(s - lse_ref[0, row0:, :])
            dp = _dot_nt(dov, v_ref[pl.ds(off, ncol), :])
            ds = p * (dp - delta_ref[0, row0:, :])
            dsb = ds.astype(BF16)
            dv_ref[pl.ds(off, ncol), :] += _dot_tn(p.astype(BF16), dov)
            dk_ref[pl.ds(off, ncol), :] += _dot_tn(dsb, q) * scale
            dq_s[row0:, :] += _dot(dsb, kblk) * scale
            dcq_s[row0:, :] += jnp.sum(ds, axis=1, keepdims=True)
            dck_ref[0, j, :, col0:col0 + ncol] += jnp.sum(ds, axis=0, keepdims=True)

        def loop_body(j, carry):
            step(j, False)
            return carry

        lax.fori_loop(j0, i, loop_body, 0)
        step(i, True, 0, 0, tq // 2)
        step(i, True, tq // 2, tq // 2, tq // 2)
        dq_ref[...] = dq_s[...]
        dcq_ref[0] = dcq_s[...]

    blk = pl.BlockSpec((tq, hd), lambda h, i, first: (i, h))
    head = pl.BlockSpec((t, hd), lambda h, i, first: (0, h))
    col = pl.BlockSpec((1, tq, 1), lambda h, i, first: (h, i, 0))
    rowv = pl.BlockSpec((1, nq, 1, tq), lambda h, i, first: (h, 0, 0, 0))
    full = jax.ShapeDtypeStruct((t, heads * hd), F32)
    return pl.pallas_call(
        body, name=name,
        grid_spec=pltpu.PrefetchScalarGridSpec(
            num_scalar_prefetch=1, grid=(heads, nq),
            in_specs=[blk, head, head, blk, rowv, col, col],
            out_specs=[blk, head, head, col, rowv],
            scratch_shapes=[pltpu.VMEM((tq, hd), F32), pltpu.VMEM((tq, 1), F32)]),
        out_shape=[full, full, full, jax.ShapeDtypeStruct((heads, t, 1), F32),
                   jax.ShapeDtypeStruct((heads, nq, 1, tq), F32)],
        compiler_params=_params(("arbitrary", "arbitrary")),
    )(first_block, qn, kn, vb, do, ck, lse, delta)


def _attn_bwd_post(dqn, dkn, dv, dz, proj, dcq, dck, q_gain, k_gain, f_bias, heads, hd, name):
    t, bw = dqn.shape
    tr = _pick(t, 128, 16)
    nb = t // tr
    fcol = 4 * bw // LANES
    width = 4 * bw + LANES

    def body(dq_ref, dk_ref, dv_ref, dz_ref, q_ref, k_ref, f_ref, dcq_ref, dck_ref, gq_ref, gk_ref,
             fb_ref, d_ref, dgq_ref, dgk_ref, dfb_ref, carry_ref, rc_ref):
        i = pl.program_id(0)

        @pl.when(i == 0)
        def _():
            carry_ref[...] = jnp.zeros_like(carry_ref)
            dgq_ref[...] = jnp.zeros_like(dgq_ref)
            dgk_ref[...] = jnp.zeros_like(dgk_ref)
            dfb_ref[...] = jnp.zeros_like(dfb_ref)

        for idx, (g_ref, raw_ref, gain_ref, dgain_ref) in enumerate(
                ((dq_ref, q_ref, gq_ref, dgq_ref), (dk_ref, k_ref, gk_ref, dgk_ref))):
            gain = gain_ref[...]
            dgain = jnp.zeros((1, hd), F32)
            for h in range(heads):
                sl = slice(h * hd, (h + 1) * hd)
                v, dn = raw_ref[:, sl], g_ref[:, sl]
                r = lax.rsqrt(jnp.mean(v * v, axis=1, keepdims=True) + NORM_EPS)
                vh = v * r
                dgain = dgain + jnp.sum(dn * vh, axis=0, keepdims=True)
                dvh = dn * gain
                draw = r * (dvh - vh * jnp.mean(dvh * vh, axis=1, keepdims=True))
                d_ref[:, idx * bw + h * hd:idx * bw + (h + 1) * hd] = draw.astype(BF16)
            dgain_ref[...] += dgain
        d_ref[:, 2 * bw:3 * bw] = dv_ref[...].astype(BF16)
        d_ref[:, 3 * bw:4 * bw] = dz_ref[...]
        rows = lax.broadcasted_iota(jnp.int32, (tr, tr), 0)
        cols = lax.broadcasted_iota(jnp.int32, (tr, tr), 1)
        upper = jnp.where(cols >= rows, 1.0, 0.0).astype(BF16)
        hi, mid, lo = _split3(dcq_ref[...] - dck_ref[...])
        rc_ref[...] = (_dot(upper, hi) + _dot(upper, mid) + _dot(upper, lo)) + carry_ref[...]
        carry_ref[...] = rc_ref[0:1, :]
        df = rc_ref[...] * (1.0 / (1.0 + jnp.exp(f_ref[...] + fb_ref[...])))
        d_ref[:, 4 * bw:] = df.astype(BF16)
        dfb_ref[...] += jnp.sum(df, axis=0, keepdims=True)

    wide = lambda col: pl.BlockSpec((tr, bw), lambda i: (nb - 1 - i, col))
    lane = lambda col: pl.BlockSpec((tr, LANES), lambda i: (nb - 1 - i, col))
    vec = pl.BlockSpec((1, hd), lambda i: (0, 0))
    vecl = pl.BlockSpec((1, LANES), lambda i: (0, 0))
    return pl.pallas_call(
        body, name=name, grid=(nb,),
        in_specs=[wide(0), wide(0), wide(0), wide(0), wide(0), wide(1), lane(fcol), lane(0), lane(0),
                  vec, vec, vecl],
        out_specs=[pl.BlockSpec((tr, width), lambda i: (nb - 1 - i, 0)), vec, vec, vecl],
        out_shape=[jax.ShapeDtypeStruct((t, width), BF16), jax.ShapeDtypeStruct((1, hd), F32),
                   jax.ShapeDtypeStruct((1, hd), F32), jax.ShapeDtypeStruct((1, LANES), F32)],
        scratch_shapes=[pltpu.VMEM((1, LANES), F32), pltpu.VMEM((tr, LANES), F32)],
        compiler_params=_params(("arbitrary",)),
    )(dqn, dkn, dv, dz, proj, proj, proj, dcq, dck, q_gain, k_gain, f_bias)


def _adamw(w, m, v, parts, name, exchange=None):
    nl, r, c = w.shape
    itemsize = parts[0].dtype.itemsize
    unit = 32 // itemsize
    row_bytes = c * (7 * 4 + N_DEV * itemsize * nl)
    tr = _pick(r, max(unit, 12 * 1024 * 1024 // row_bytes), unit)
    nr = r // tr
    c1 = 1.0 / (1.0 - ADAM_B1 ** ADAM_STEP)
    c2 = 1.0 / (1.0 - ADAM_B2 ** ADAM_STEP)

    def body(*refs):
        w_ref, m_ref, v_ref = refs[:3]
        p_refs = refs[3:3 + nl]
        g_ref, d_ref, nm_ref, nv_ref = refs[3 + nl:]
        layer = pl.program_id(0)

        def partial(j):
            p = p_refs[0][j].astype(F32)
            for q in range(1, nl):
                p = jnp.where(layer == q, p_refs[q][j].astype(F32), p)
            return p

        g = partial(0)
        for j in range(1, N_DEV):
            g = g + partial(j)
        nm = ADAM_B1 * m_ref[0] + (1.0 - ADAM_B1) * g
        nv = ADAM_B2 * v_ref[0] + (1.0 - ADAM_B2) * (g * g)
        g_ref[0] = g
        nm_ref[0] = nm
        nv_ref[0] = nv
        d_ref[0] = -ADAM_LR * ((nm * c1) / (jnp.sqrt(nv * c2) + ADAM_EPS) + ADAM_WD * w_ref[0])

    def part_spec(q):
        rest = 0 if q > 0 else nr - 1
        return pl.BlockSpec((N_DEV, tr, c), lambda l, i: (0, jnp.where(l == q, i, rest), 0))

    row = pl.BlockSpec((1, tr, c), lambda l, i: (l, i, 0))
    in_specs, out_specs = [row, row, row] + [part_spec(q) for q in range(nl)], [row] * 4
    out_shape = [jax.ShapeDtypeStruct((nl, r, c), F32)] * 4
    operands, scratch = [w, m, v, *parts], []
    if exchange is not None:
        any_spec = pl.BlockSpec(memory_space=pl.ANY)
        body = _carry_exchange(
            body, exchange, 3 + nl, 4, 0,
            lambda: (pl.program_id(0) == 0) & (pl.program_id(1) == 0),
            lambda: (pl.program_id(0) == nl - 1) & (pl.program_id(1) == nr - 1))
        in_specs, out_specs = in_specs + [any_spec] * exchange.n, out_specs + [any_spec] * exchange.n
        out_shape, operands = out_shape + exchange.out_shapes(), operands + exchange.arrays
        scratch = exchange.scratch()
    return pl.pallas_call(
        body, name=name, grid=(nl, nr), in_specs=in_specs, out_specs=out_specs,
        out_shape=out_shape, scratch_shapes=scratch,
        compiler_params=_params(("arbitrary", "arbitrary")),
    )(*operands)


def _flat_rows(pieces):
    rows = []
    for p in pieces:
        f = p.reshape(-1)
        f = jnp.pad(f, (0, (-f.shape[0]) % LANES))
        rows.append(f.reshape(-1, LANES))
    out = jnp.concatenate(rows, axis=0)
    return jnp.pad(out, ((0, (-out.shape[0]) % 8), (0, 0)))


def _unflat_rows(flat, shapes):
    outs, r0 = [], 0
    lead = flat.shape[:-2]
    for s in shapes:
        size = 1
        for d in s:
            size *= d
        nr = -(-size // LANES)
        piece = flat[..., r0:r0 + nr, :].reshape(lead + (nr * LANES,))[..., :size]
        outs.append(piece.reshape(lead + tuple(s)))
        r0 += nr
    return outs


def kernel(x, a_norm_g, a_w_in, a_v_norm_g, a_w_s, a_b_s, a_w_out, b_norm_g, b_w_in, b_f_bias, b_q_norm_g, b_k_norm_g, b_w_out, loss_target, m_a_norm_g, m_a_w_in, m_a_v_norm_g, m_a_w_s, m_a_b_s, m_a_w_out, m_b_norm_g, m_b_w_in, m_b_f_bias, m_b_q_norm_g, m_b_k_norm_g, m_b_w_out, v_a_norm_g, v_a_w_in, v_a_v_norm_g, v_a_w_s, v_a_b_s, v_a_w_out, v_b_norm_g, v_b_w_in, v_b_f_bias, v_b_q_norm_g, v_b_k_norm_g, v_b_w_out):
    t, d = x.shape[1], x.shape[2]
    n_a, n_b = a_w_in.shape[0], b_w_in.shape[0]
    depth = n_a + n_b
    aw = a_w_out.shape[1] * N_DEV
    groups, chunk = a_w_s.shape[1], a_w_s.shape[2]
    heads, hd = b_f_bias.shape[1], b_q_norm_g.shape[1]
    bw = heads * hd
    b_cols = b_w_in.shape[2]
    tq = _pick(t, 512, LANES)
    nq = t // tq
    me = _dev_index((lax.axis_index("x"), lax.axis_index("y"), lax.axis_index("c")))

    pad_h = lambda v: jnp.pad(v, ((0, 0), (0, LANES - heads)))
    f_bias_p = pad_h(b_f_bias)
    b_s_col = a_b_s.reshape(n_a, groups, chunk, 1)
    w_s_t = a_w_s.transpose(0, 1, 3, 2)

    def weight_gather(i, extra=()):
        j = i // 2
        if i % 2 == 0:
            return _Gather([a_w_in[j].astype(BF16), a_w_out[j].astype(BF16)] + [a for a, _ in extra],
                           [1, 0] + [ax for _, ax in extra])
        return _Gather([b_w_in[j].astype(BF16), b_w_out[j].astype(BF16)] + [a for a, _ in extra],
                       [None, 0] + [ax for _, ax in extra])

    def whole_w_in(i, w_in):
        if i % 2 == 1:
            w_in = w_in.transpose(1, 0, 2).reshape(d, 4 * bw + heads)
            w_in = jnp.pad(w_in, ((0, 0), (0, LANES - heads)))
        return w_in

    (w_in0,) = _run_exchange(_Gather([a_w_in[0].astype(BF16)], [1]), "gather_weights")
    weights = {0: [w_in0, None]}
    late = ((a_w_out[0].astype(BF16), 0), (b_norm_g, 1))

    xs = [x[0]]
    saved = []
    for i in range(depth):
        j = i // 2
        xi = xs[-1]
        nxt = weight_gather(i + 1, extra=late if i == 0 else ()) if i + 1 < depth else None
        if i % 2 == 0:
            h = _rms_fwd(xi, a_norm_g[j:j + 1], f"a{j}_norm")
            res = _matmul(h, weights[i][0], "nn", F32, f"a{j}_in", tn=2048, tk=d, exchange=nxt)
        else:
            h = _rms_fwd(xi, b_norm_full[j:j + 1], f"b{j}_norm")
            res = _matmul(h, weights[i][0], "nn", F32, f"b{j}_in", tn=1664, tk=d, exchange=nxt)
        if nxt is None:
            pre = res
        else:
            pre = res[0]
            weights[i + 1] = [whole_w_in(i + 1, res[1]), res[2]]
            if i == 0:
                weights[0][1], b_norm_full = res[3], res[4]
        w_in, w_out = weights[i]
        if i % 2 == 0:
            y = _mix_fwd(pre, a_v_norm_g[j:j + 1], a_w_s[j], b_s_col[j], f"a{j}_mix")
            xs.append(_matmul(y, w_out, "nn", F32, f"a{j}_out", residual=xi))
            saved.append((h, pre, y))
        else:
            qn, kn, vb, cum, qsq, ksq, cum_first, cum_last = _attn_prep(
                pre, b_q_norm_g[j:j + 1], b_k_norm_g[j:j + 1], f_bias_p[j:j + 1], heads, hd,
                f"b{j}_prep")
            first = _skip_plan(qsq, ksq, cum_first, cum_last, t, tq, hd, f"b{j}_plan")
            first = first[:, :heads].T
            ck = cum[:, :heads].T.reshape(heads, nq, 1, tq)
            o, y, lse = _flash_fwd(first, qn, kn, vb, ck, pre, heads, hd, tq, f"b{j}_attn")
            xs.append(_matmul(y, w_out, "nn", F32, f"b{j}_out", residual=xi))
            saved.append((h, pre, y, qn, kn, vb, ck, o, lse, first))

    sq, g, gb = _loss_grad(xs[-1], loss_target[0], "loss")
    loss = 0.5 * lax.psum(sq[0, 0], AXES) / d

    d_a_norm, d_a_vnorm, d_a_ws, d_a_bs = [None] * n_a, [None] * n_a, [None] * n_a, [None] * n_a
    d_b_norm, d_b_fb, d_b_gq, d_b_gk = [None] * n_b, [None] * n_b, [None] * n_b, [None] * n_b
    recv_in, recv_out = {}, {}
    pending = None

    def in_blocks(i, dw_in):
        if i % 2 == 0:
            return _AllToAll([dw_in], [1])
        rows = dw_in.shape[0]
        return _AllToAll(
            [dw_in[:, :4 * bw + heads].reshape(rows, N_DEV, b_cols).transpose(1, 0, 2)], [None])

    for i in reversed(range(depth)):
        j = i // 2
        xi = xs[i]
        w_in, w_out = weights[i]
        if i % 2 == 0:
            h, uvz, y = saved[i]
            dy = _matmul(gb, w_out, "nt", F32, f"a{j}_dy", tn=2048, tk=d)
            dw_out = _matmul(y, gb, "tn", BF16, f"a{j}_dwout")
            dpre, d_a_ws[j], dbs, d_a_vnorm[j] = _mix_bwd(
                uvz, dy, a_v_norm_g[j:j + 1], a_w_s[j], w_s_t[j], b_s_col[j], f"a{j}_mixbwd")
            d_a_bs[j] = dbs.reshape(groups, chunk)
            name, gain = f"a{j}", a_norm_g[j:j + 1]
        else:
            h, proj, y, qn, kn, vb, ck, o, lse, first = saved[i]
            dy = _matmul(gb, w_out, "nt", F32, f"b{j}_dy", tn=2048, tk=d)
            dw_out = _matmul(y, gb, "tn", BF16, f"b{j}_dwout")
            do, dz, delta = _attn_bwd_prep(dy, proj, o, heads, hd, f"b{j}_bwdprep")
            dqn, dkn, dv, dcq, dck = _flash_bwd(first, qn, kn, vb, do, ck, lse, delta, heads, hd,
                                                tq, f"b{j}_attnbwd")
            per_token = lambda v: pad_h(v.reshape(heads, t).T)
            dpre, d_b_gq[j], d_b_gk[j], dfb = _attn_bwd_post(
                dqn, dkn, dv, dz, proj, per_token(dcq), per_token(dck), b_q_norm_g[j:j + 1],
                b_k_norm_g[j:j + 1], f_bias_p[j:j + 1], heads, hd, f"b{j}_bwdpost")
            d_b_fb[j] = dfb[:, :heads]
            name, gain = f"b{j}", b_norm_full[j:j + 1]
        if pending is None:
            dh = _matmul(dpre, w_in, "nt", F32, name + "_dh")
        else:
            dh, got = _matmul(dpre, w_in, "nt", F32, name + "_dh", exchange=pending[1])
            parts_in = pending[2] + [got]
            recv_in[pending[0]] = parts_in[0] if len(parts_in) == 1 else jnp.concatenate(
                parts_in, axis=1)
        n_parts = (4 if i == 0 else 2) if i % 2 == 0 else 1
        rows = d // n_parts
        riding = _AllToAll([dw_out], [0])
        landed = []
        for part in range(n_parts):
            h_part = h if n_parts == 1 else h[:, part * rows:(part + 1) * rows]
            dw_part, got = _matmul(h_part, dpre, "tn", BF16, f"{name}_dwin{part}",
                                   tn=1024 if i % 2 == 0 else 1664, exchange=riding)
            if part == 0:
                recv_out[i] = got
            else:
                landed.append(got)
            riding = in_blocks(i, dw_part)
        if i > 0:
            pending = (i, riding, landed)
            g, gb, dgain = _rms_bwd(xi, dh, g, gain, name + "_normbwd")
        else:
            g, gb, dgain, got = _rms_bwd(xi, dh, g, gain, name + "_normbwd", exchange=riding)
            recv_in[i] = jnp.concatenate(landed + [got], axis=1)
        if i % 2 == 0:
            d_a_norm[j] = dgain
        else:
            d_b_norm[j] = dgain
    grad_x = g[None]

    small = [jnp.concatenate(d_a_norm, 0), jnp.concatenate(d_a_vnorm, 0), jnp.stack(d_a_ws, 0),
             jnp.stack(d_a_bs, 0), jnp.concatenate(d_b_fb, 0), jnp.concatenate(d_b_gq, 0),
             jnp.concatenate(d_b_gk, 0)]
    small_w = [a_norm_g, a_v_norm_g, a_w_s, a_b_s, b_f_bias, b_q_norm_g, b_k_norm_g]
    small_m = [m_a_norm_g, m_a_v_norm_g, m_a_w_s, m_a_b_s, m_b_f_bias, m_b_q_norm_g, m_b_k_norm_g]
    small_v = [v_a_norm_g, v_a_v_norm_g, v_a_w_s, v_a_b_s, v_b_f_bias, v_b_q_norm_g, v_b_k_norm_g]
    small_flat = _flat_rows(small)
    n_small = small_flat.shape[0]
    small_gather = _Gather(
        [jnp.concatenate([small_flat, _flat_rows([jnp.concatenate(d_b_norm, 0)])], axis=0)],
        [None])

    a_layers, b_layers = range(0, depth, 2), range(1, depth, 2)
    *u_a_in, gathered_small = _adamw(a_w_in, m_a_w_in, v_a_w_in, [recv_in[i] for i in a_layers],
                                     "adamw_a_w_in", exchange=small_gather)
    parts_small = gathered_small[:, :n_small]
    parts_b_norm = gathered_small[:, n_small:n_small + n_b * d // LANES].reshape(N_DEV, n_b, d)
    parts_b_norm = lax.dynamic_slice_in_dim(parts_b_norm, me * (d // N_DEV), d // N_DEV, axis=2)
    u_a_out = _adamw(a_w_out, m_a_w_out, v_a_w_out, [recv_out[i] for i in a_layers],
                     "adamw_a_w_out")
    u_b_in = _adamw(b_w_in, m_b_w_in, v_b_w_in, [recv_in[i] for i in b_layers], "adamw_b_w_in")
    u_b_out = _adamw(b_w_out, m_b_w_out, v_b_w_out, [recv_out[i] for i in b_layers],
                     "adamw_b_w_out")
    u_b_norm = [o_[0] for o_ in _adamw(b_norm_g[None], m_b_norm_g[None], v_b_norm_g[None],
                                       [parts_b_norm], "adamw_b_norm")]
    u_small = _adamw(_flat_rows(small_w)[None], _flat_rows(small_m)[None],
                     _flat_rows(small_v)[None], [parts_small], "adamw_small")
    shapes = [w.shape for w in small_w]
    u_small = [_unflat_rows(o_[0], shapes) for o_ in u_small]

    def per_kind(k):
        s = u_small[k]
        return [s[0], u_a_in[k], s[1], s[2], s[3], u_a_out[k], u_b_norm[k], u_b_in[k], s[4], s[5],
                s[6], u_b_out[k]]

    return (loss, grad_x, *per_kind(0), *per_kind(1), *per_kind(2), *per_kind(3))
```

```python
import jax
import jax.numpy as jnp
from jax import lax
from jax.experimental import pallas as pl
from jax.experimental.pallas import tpu as pltpu

F32 = jnp.float32
BF16 = jnp.bfloat16
MESH = pl.DeviceIdType.MESH
AXES = ("x", "y", "c")
N_DEV = 8
NORM_EPS = 1e-6
LANES = 128
VMEM_LIMIT = 56 * 1024 * 1024

ADAM_LR = 0.001
ADAM_B1 = 0.9
ADAM_B2 = 0.999
ADAM_EPS = 1e-08
ADAM_WD = 0.01
ADAM_STEP = 10

LOG2E = 1.4426950408889634
SOFTMAX_ROWS = 64
UNDERFLOW_LOG2 = -160.0

GELU_C0 = 0.7978845608028654
GELU_C1 = 0.044715

NT_DIMS = (((1,), (1,)), ((), ()))
TN_DIMS = (((0,), (0,)), ((), ()))


def _params(sem=None):
    return pltpu.CompilerParams(dimension_semantics=sem, vmem_limit_bytes=VMEM_LIMIT)


def _pick(n, target, unit):
    best = None
    for t in range(unit, min(n, target) + 1, unit):
        if n % t == 0:
            best = t
    return n if best is None else best


def _sigmoid(x):
    return 0.5 + 0.5 * jnp.tanh(0.5 * x)


def _gelu(x):
    return x * (0.5 + 0.5 * jnp.tanh(x * (GELU_C0 + (GELU_C0 * GELU_C1) * (x * x))))


def _gelu_and_grad(x):
    x2 = x * x
    a = 0.5 + 0.5 * jnp.tanh(x * (GELU_C0 + (GELU_C0 * GELU_C1) * x2))
    dg = a + (x * (a * (1.0 - a))) * (2.0 * GELU_C0 + (6.0 * GELU_C0 * GELU_C1) * x2)
    return x * a, dg


def _dot(a, b):
    return jnp.dot(a, b, preferred_element_type=F32)


def _dot_nt(a, b):
    return lax.dot_general(a, b, NT_DIMS, preferred_element_type=F32)


def _dot_tn(a, b):
    return lax.dot_general(a, b, TN_DIMS, preferred_element_type=F32)


def _split3(v):
    hi = v.astype(BF16)
    r1 = v - hi.astype(F32)
    mid = r1.astype(BF16)
    lo = (r1 - mid.astype(F32)).astype(BF16)
    return hi, mid, lo


def _dev_index(p):
    return 4 * p[0] + 2 * p[1] + p[2]


def _block(ref, idx, axis, size):
    if axis is None:
        return ref.at[idx]
    start = pl.multiple_of(idx * size, size)
    return ref.at[(slice(None),) * axis + (pl.ds(start, size),)]


class _Exchange:
    def __init__(self, arrays, axes):
        self.arrays, self.axes, self.n = list(arrays), list(axes), len(arrays)

    def scratch(self):
        return [pltpu.SemaphoreType.DMA((self.n, 7)), pltpu.SemaphoreType.DMA((self.n, 7)),
                pltpu.SemaphoreType.DMA((self.n,))]

    @staticmethod
    def _place():
        x, y, c = lax.axis_index("x"), lax.axis_index("y"), lax.axis_index("c")
        return x, y, c


class _Gather(_Exchange):
    def out_shapes(self):
        outs = []
        for a, ax in zip(self.arrays, self.axes):
            if ax is None:
                shape = (N_DEV,) + a.shape
            else:
                shape = a.shape[:ax] + (N_DEV * a.shape[ax],) + a.shape[ax + 1:]
            outs.append(jax.ShapeDtypeStruct(shape, a.dtype))
        return outs

    def _copy(self, ins, outs, sems, a, k, block, to, src=None):
        ax = self.axes[a]
        dst = _block(outs[a], _dev_index(block), ax, None if ax is None else self.arrays[a].shape[ax])
        return pltpu.make_async_remote_copy(
            src_ref=dst if src is None else src, dst_ref=dst,
            send_sem=sems[0].at[a, k], recv_sem=sems[1].at[a, k],
            device_id=to, device_id_type=MESH)

    def _mine(self, ins, outs, sems, a, me):
        ax = self.axes[a]
        dst = _block(outs[a], _dev_index(me), ax, None if ax is None else self.arrays[a].shape[ax])
        return pltpu.make_async_copy(ins[a], dst, sems[2].at[a])

    def _first(self, ins, outs, sems):
        x, y, c = self._place()
        me, sibling = (x, y, c), (x, y, 1 - c)
        chips = [(1 - x, y), (x, 1 - y), (1 - x, 1 - y)]
        first = []
        for a in range(self.n):
            first.append(self._copy(ins, outs, sems, a, 0, me, sibling, src=ins[a]))
            first += [self._copy(ins, outs, sems, a, 1 + j, me, (*chip, c), src=ins[a])
                      for j, chip in enumerate(chips)]
        return first

    def start(self, ins, outs, sems):
        me = self._place()
        for a in range(self.n):
            self._mine(ins, outs, sems, a, me).start()
        for cp in self._first(ins, outs, sems):
            cp.start()

    def finish(self, ins, outs, sems):
        x, y, c = self._place()
        me, sibling = (x, y, c), (x, y, 1 - c)
        chips = [(1 - x, y), (x, 1 - y), (1 - x, 1 - y)]
        passed = []
        for a in range(self.n):
            for j, chip in enumerate(chips):
                self._copy(ins, outs, sems, a, 1 + j, (*chip, c), me).wait_recv()
                cp = self._copy(ins, outs, sems, a, 4 + j, (*chip, c), sibling)
                cp.start()
                passed.append(cp)
        for a in range(self.n):
            self._copy(ins, outs, sems, a, 0, sibling, me).wait_recv()
            for j, chip in enumerate(chips):
                self._copy(ins, outs, sems, a, 4 + j, (*chip, 1 - c), me).wait_recv()
        for cp in self._first(ins, outs, sems) + passed:
            cp.wait_send()
        for a in range(self.n):
            self._mine(ins, outs, sems, a, me).wait()


class _AllToAll(_Exchange):
    def _blk_shape(self, a):
        arr, ax = self.arrays[a], self.axes[a]
        if ax is None:
            return arr.shape[1:]
        return arr.shape[:ax] + (arr.shape[ax] // N_DEV,) + arr.shape[ax + 1:]

    def out_shapes(self):
        return [jax.ShapeDtypeStruct((N_DEV,) + self._blk_shape(a), self.arrays[a].dtype)
                for a in range(self.n)]

    def _src(self, ins, a, idx):
        ax = self.axes[a]
        return _block(ins[a], idx, ax, None if ax is None else self.arrays[a].shape[ax] // N_DEV)

    def _peers(self):
        x, y, c = self._place()
        return [((1 - x) if r & 4 else x, (1 - y) if r & 2 else y, (1 - c) if r & 1 else c)
                for r in range(1, N_DEV)]

    def _sends(self, ins, outs, sems):
        me = _dev_index(self._place())
        return [pltpu.make_async_remote_copy(
            src_ref=self._src(ins, a, _dev_index(peer)), dst_ref=outs[a].at[me],
            send_sem=sems[0].at[a, k], recv_sem=sems[1].at[a, k],
            device_id=peer, device_id_type=MESH)
            for a in range(self.n) for k, peer in enumerate(self._peers())]

    def _mine(self, ins, outs, sems):
        me = _dev_index(self._place())
        return [pltpu.make_async_copy(self._src(ins, a, me), outs[a].at[me], sems[2].at[a])
                for a in range(self.n)]

    def start(self, ins, outs, sems):
        for cp in self._mine(ins, outs, sems) + self._sends(ins, outs, sems):
            cp.start()

    def finish(self, ins, outs, sems):
        for a in range(self.n):
            for k, peer in enumerate(self._peers()):
                landed = outs[a].at[_dev_index(peer)]
                pltpu.make_async_remote_copy(
                    src_ref=landed, dst_ref=landed, send_sem=sems[0].at[a, k],
                    recv_sem=sems[1].at[a, k], device_id=peer, device_id_type=MESH).wait_recv()
        for cp in self._sends(ins, outs, sems):
            cp.wait_send()
        for cp in self._mine(ins, outs, sems):
            cp.wait()


def _run_exchange(exchange, name):
    n = exchange.n

    def body(*refs):
        ins, outs, sems = refs[:n], refs[n:2 * n], refs[2 * n:]
        exchange.start(ins, outs, sems)
        exchange.finish(ins, outs, sems)

    any_spec = pl.BlockSpec(memory_space=pl.ANY)
    return pl.pallas_call(
        body, name=name, out_shape=exchange.out_shapes(),
        in_specs=[any_spec] * n, out_specs=[any_spec] * n, scratch_shapes=exchange.scratch(),
    )(*exchange.arrays)


def _matmul(a, b, mode, out_dtype, name, tm=1024, tn=1024, tk=2048, residual=None, exchange=None):
    if mode == "tn":
        kdim, m = a.shape
    else:
        m, kdim = a.shape
    n = b.shape[0] if mode == "nt" else b.shape[1]
    tm, tn, tk = _pick(m, tm, LANES), _pick(n, tn, LANES), _pick(kdim, tk, LANES)
    nk = kdim // tk
    if mode == "tn":
        a_spec = pl.BlockSpec((tk, tm), lambda i, j, k: (k, i))
    else:
        a_spec = pl.BlockSpec((tm, tk), lambda i, j, k: (i, k))
    if mode == "nt":
        b_spec = pl.BlockSpec((tn, tk), lambda i, j, k: (j, k))
    else:
        b_spec = pl.BlockSpec((tk, tn), lambda i, j, k: (k, j))
    o_spec = pl.BlockSpec((tm, tn), lambda i, j, k: (i, j))
    dot = {"nn": _dot, "nt": _dot_nt, "tn": _dot_tn}[mode]
    has_res = residual is not None
    n_in = 3 if has_res else 2
    n_ex = 0 if exchange is None else exchange.n
    ni, nj = m // tm, n // tn

    def body(*refs):
        a_ref, b_ref = refs[:2]
        r_ref = refs[2] if has_res else None
        ex_ins = refs[n_in:n_in + n_ex]
        o_ref = refs[n_in + n_ex]
        ex_outs = refs[n_in + n_ex + 1:n_in + 2 * n_ex + 1]
        scratch = refs[n_in + 2 * n_ex + 1:]
        i, j, k = pl.program_id(0), pl.program_id(1), pl.program_id(2)

        if exchange is not None:
            sems = scratch[-3:]

            @pl.when((i == 0) & (j == 0) & (k == 0))
            def _():
                exchange.start(ex_ins, ex_outs, sems)

        def finish(acc):
            if has_res:
                acc = acc + r_ref[...]
            o_ref[...] = acc.astype(out_dtype)

        if nk == 1:
            finish(dot(a_ref[...], b_ref[...]))
        else:
            acc_ref = scratch[0]

            @pl.when(k == 0)
            def _():
                acc_ref[...] = jnp.zeros_like(acc_ref)

            acc_ref[...] += dot(a_ref[...], b_ref[...])

            @pl.when(k == nk - 1)
            def _():
                finish(acc_ref[...])

        if exchange is not None:
            @pl.when((i == ni - 1) & (j == nj - 1) & (k == nk - 1))
            def _():
                exchange.finish(ex_ins, ex_outs, sems)

    any_spec = pl.BlockSpec(memory_space=pl.ANY)
    operands = (a, b, residual) if has_res else (a, b)
    out_shape = jax.ShapeDtypeStruct((m, n), out_dtype)
    scratch_shapes = [] if nk == 1 else [pltpu.VMEM((tm, tn), F32)]
    if exchange is None:
        return pl.pallas_call(
            body, name=name, grid=(ni, nj, nk),
            in_specs=[a_spec, b_spec] + ([o_spec] if has_res else []),
            out_specs=o_spec, out_shape=out_shape, scratch_shapes=scratch_shapes,
            compiler_params=_params(("parallel", "parallel", "arbitrary")),
        )(*operands)
    return pl.pallas_call(
        body, name=name, grid=(ni, nj, nk),
        in_specs=[a_spec, b_spec] + ([o_spec] if has_res else []) + [any_spec] * n_ex,
        out_specs=[o_spec] + [any_spec] * n_ex,
        out_shape=[out_shape] + exchange.out_shapes(),
        scratch_shapes=scratch_shapes + exchange.scratch(),
        compiler_params=_params(("arbitrary", "arbitrary", "arbitrary")),
    )(*operands, *exchange.arrays)


def _rms_fwd(x, gain, name):
    t, d = x.shape
    tr = _pick(t, 256, 16)

    def body(x_ref, g_ref, h_ref):
        xv = x_ref[...]
        r = lax.rsqrt(jnp.mean(xv * xv, axis=1, keepdims=True) + NORM_EPS)
        h_ref[...] = (xv * r * g_ref[...]).astype(BF16)

    row = pl.BlockSpec((tr, d), lambda i: (i, 0))
    return pl.pallas_call(
        body, name=name, grid=(t // tr,),
        in_specs=[row, pl.BlockSpec((1, d), lambda i: (0, 0))],
        out_specs=row, out_shape=jax.ShapeDtypeStruct((t, d), BF16),
        compiler_params=_params(("arbitrary",)),
    )(x, gain)


def _carry_exchange(body, exchange, n_in, n_out, n_scratch, is_first, is_last):
    n = exchange.n

    def carrying(*refs):
        ins, ex_ins = refs[:n_in], refs[n_in:n_in + n]
        outs = refs[n_in + n:n_in + n + n_out]
        ex_outs = refs[n_in + n + n_out:n_in + 2 * n + n_out]
        scratch = refs[n_in + 2 * n + n_out:n_in + 2 * n + n_out + n_scratch]
        sems = refs[n_in + 2 * n + n_out + n_scratch:]

        @pl.when(is_first())
        def _():
            exchange.start(ex_ins, ex_outs, sems)

        body(*ins, *outs, *scratch)

        @pl.when(is_last())
        def _():
            exchange.finish(ex_ins, ex_outs, sems)

    return carrying


def _rms_bwd(x, dh, g_in, gain, name, exchange=None):
    t, d = x.shape
    tr = _pick(t, 256, 16)
    nt = t // tr

    def body(x_ref, dh_ref, gin_ref, g_ref, dx_ref, dxb_ref, dg_ref):
        i = pl.program_id(0)
        xv, dhv = x_ref[...], dh_ref[...]
        r = lax.rsqrt(jnp.mean(xv * xv, axis=1, keepdims=True) + NORM_EPS)
        xh = xv * r
        dxh = dhv * g_ref[...]
        dx = gin_ref[...] + r * (dxh - xh * jnp.mean(dxh * xh, axis=1, keepdims=True))
        dx_ref[...] = dx
        dxb_ref[...] = dx.astype(BF16)

        @pl.when(i == 0)
        def _():
            dg_ref[...] = jnp.zeros_like(dg_ref)

        dg_ref[...] += jnp.sum(dhv * xh, axis=0, keepdims=True)

    row = pl.BlockSpec((tr, d), lambda i: (i, 0))
    vec = pl.BlockSpec((1, d), lambda i: (0, 0))
    in_specs, out_specs = [row, row, row, vec], [row, row, vec]
    out_shape = [jax.ShapeDtypeStruct((t, d), F32), jax.ShapeDtypeStruct((t, d), BF16),
                 jax.ShapeDtypeStruct((1, d), F32)]
    operands, scratch = [x, dh, g_in, gain], []
    if exchange is not None:
        any_spec = pl.BlockSpec(memory_space=pl.ANY)
        body = _carry_exchange(body, exchange, 4, 3, 0, lambda: pl.program_id(0) == 0,
                               lambda: pl.program_id(0) == nt - 1)
        in_specs, out_specs = in_specs + [any_spec] * exchange.n, out_specs + [any_spec] * exchange.n
        out_shape, operands = out_shape + exchange.out_shapes(), operands + exchange.arrays
        scratch = exchange.scratch()
    return pl.pallas_call(
        body, name=name, grid=(nt,), in_specs=in_specs, out_specs=out_specs, out_shape=out_shape,
        scratch_shapes=scratch, compiler_params=_params(("arbitrary",)),
    )(*operands)


def _loss_grad(y, target, name):
    t, d = y.shape
    tr = _pick(t, 256, 16)

    def body(y_ref, t_ref, s_ref, g_ref, gb_ref):
        i = pl.program_id(0)
        e = y_ref[...] - t_ref[...]
        g = e * (1.0 / d)
        g_ref[...] = g
        gb_ref[...] = g.astype(BF16)

        @pl.when(i == 0)
        def _():
            s_ref[...] = jnp.zeros_like(s_ref)

        s_ref[...] += jnp.sum(jnp.sum(e * e, axis=1, keepdims=True), axis=0, keepdims=True)

    row = pl.BlockSpec((tr, d), lambda i: (i, 0))
    return pl.pallas_call(
        body, name=name, grid=(t // tr,),
        in_specs=[row, row],
        out_specs=[pl.BlockSpec((1, 1), lambda i: (0, 0)), row, row],
        out_shape=[jax.ShapeDtypeStruct((1, 1), F32), jax.ShapeDtypeStruct((t, d), F32),
                   jax.ShapeDtypeStruct((t, d), BF16)],
        compiler_params=_params(("arbitrary",)),
    )(y, target)


def _causal_weights(ws_ref, g, chunk, transposed):
    rows = lax.broadcasted_iota(jnp.int32, (chunk, chunk), 0)
    cols = lax.broadcasted_iota(jnp.int32, (chunk, chunk), 1)
    keep = (cols >= rows) if transposed else (rows >= cols)
    return jnp.where(keep, ws_ref[g], 0.0).astype(BF16)


def _mix_fwd(uvz, v_gain, w_s, b_s, name):
    t, w3 = uvz.shape
    w = w3 // 3
    groups, chunk = w_s.shape[0], w_s.shape[1]
    gd = w // groups

    def body(uvz_ref, gam_ref, ws_ref, bs_ref, y_ref):
        gv = _gelu(uvz_ref[:, w:2 * w])
        r = lax.rsqrt(jnp.mean(gv * gv, axis=1, keepdims=True) + NORM_EPS)
        vn = (gv * r * gam_ref[...]).astype(BF16)
        for g in range(groups):
            sl = slice(g * gd, (g + 1) * gd)
            mixed = _dot(_causal_weights(ws_ref, g, chunk, False), vn[:, sl]) + bs_ref[g]
            u = uvz_ref[:, g * gd:(g + 1) * gd]
            z = uvz_ref[:, 2 * w + g * gd:2 * w + (g + 1) * gd]
            y_ref[:, sl] = (_gelu(u) * mixed * (z * _sigmoid(z))).astype(BF16)

    return pl.pallas_call(
        body, name=name, grid=(t // chunk,),
        in_specs=[pl.BlockSpec((chunk, w3), lambda i: (i, 0)),
                  pl.BlockSpec((1, w), lambda i: (0, 0)),
                  pl.BlockSpec((groups, chunk, chunk), lambda i: (0, 0, 0)),
                  pl.BlockSpec((groups, chunk, 1), lambda i: (0, 0, 0))],
        out_specs=pl.BlockSpec((chunk, w), lambda i: (i, 0)),
        out_shape=jax.ShapeDtypeStruct((t, w), BF16),
        compiler_params=_params(("arbitrary",)),
    )(uvz, v_gain, w_s, b_s)


def _mix_bwd(uvz, dy, v_gain, w_s, w_s_t, b_s, name):
    t, w3 = uvz.shape
    w = w3 // 3
    groups, chunk = w_s.shape[0], w_s.shape[1]
    gd = w // groups

    def body(uvz_ref, dy_ref, gam_ref, ws_ref, wst_ref, bs_ref, d_ref, dws_ref, dbs_ref, dgam_ref,
             dvn_ref):
        i = pl.program_id(0)

        @pl.when(i == 0)
        def _():
            dws_ref[...] = jnp.zeros_like(dws_ref)
            dbs_ref[...] = jnp.zeros_like(dbs_ref)
            dgam_ref[...] = jnp.zeros_like(dgam_ref)

        gv, dgv = _gelu_and_grad(uvz_ref[:, w:2 * w])
        r = lax.rsqrt(jnp.mean(gv * gv, axis=1, keepdims=True) + NORM_EPS)
        vh = gv * r
        gam = gam_ref[...]
        vn = (vh * gam).astype(BF16)
        rows = lax.broadcasted_iota(jnp.int32, (chunk, chunk), 0)
        cols = lax.broadcasted_iota(jnp.int32, (chunk, chunk), 1)
        for g in range(groups):
            sl = slice(g * gd, (g + 1) * gd)
            mixed = _dot(_causal_weights(ws_ref, g, chunk, False), vn[:, sl]) + bs_ref[g]
            gu, dgu = _gelu_and_grad(uvz_ref[:, g * gd:(g + 1) * gd])
            z = uvz_ref[:, 2 * w + g * gd:2 * w + (g + 1) * gd]
            sz = _sigmoid(z)
            silu = z * sz
            dyv = dy_ref[:, sl]
            dmixed = dyv * gu * silu
            d_ref[:, sl] = (dyv * mixed * silu * dgu).astype(BF16)
            d_ref[:, 2 * w + g * gd:2 * w + (g + 1) * gd] = (
                dyv * gu * mixed * (sz * (1.0 + z * (1.0 - sz)))).astype(BF16)
            dmb = dmixed.astype(BF16)
            dws_ref[g] += jnp.where(rows >= cols, _dot_nt(dmb, vn[:, sl]), 0.0)
            dbs_ref[g] += jnp.sum(dmixed, axis=1, keepdims=True)
            dvn_ref[:, sl] = _dot(_causal_weights(wst_ref, g, chunk, True), dmb)
        dvn = dvn_ref[...]
        dgam_ref[...] += jnp.sum(dvn * vh, axis=0, keepdims=True)
        dvh = dvn * gam
        dgvv = r * (dvh - vh * jnp.mean(dvh * vh, axis=1, keepdims=True))
        d_ref[:, w:2 * w] = (dgvv * dgv).astype(BF16)

    return pl.pallas_call(
        body, name=name, grid=(t // chunk,),
        in_specs=[pl.BlockSpec((chunk, w3), lambda i: (i, 0)),
                  pl.BlockSpec((chunk, w), lambda i: (i, 0)),
                  pl.BlockSpec((1, w), lambda i: (0, 0)),
                  pl.BlockSpec((groups, chunk, chunk), lambda i: (0, 0, 0)),
                  pl.BlockSpec((groups, chunk, chunk), lambda i: (0, 0, 0)),
                  pl.BlockSpec((groups, chunk, 1), lambda i: (0, 0, 0))],
        out_specs=[pl.BlockSpec((chunk, w3), lambda i: (i, 0)),
                   pl.BlockSpec((groups, chunk, chunk), lambda i: (0, 0, 0)),
                   pl.BlockSpec((groups, chunk, 1), lambda i: (0, 0, 0)),
                   pl.BlockSpec((1, w), lambda i: (0, 0))],
        out_shape=[jax.ShapeDtypeStruct((t, w3), BF16),
                   jax.ShapeDtypeStruct((groups, chunk, chunk), F32),
                   jax.ShapeDtypeStruct((groups, chunk, 1), F32),
                   jax.ShapeDtypeStruct((1, w), F32)],
        scratch_shapes=[pltpu.VMEM((chunk, w), F32)],
        compiler_params=_params(("arbitrary",)),
    )(uvz, dy, v_gain, w_s, w_s_t, b_s)


def _attn_prep(proj, q_gain, k_gain, f_bias, heads, hd, name):
    t = proj.shape[0]
    bw = heads * hd
    tr = _pick(t, 256, 16)
    fcol = 4 * bw // LANES

    def body(q_ref, k_ref, v_ref, f_ref, gq_ref, gk_ref, fb_ref, qn_ref, kn_ref, vb_ref, cum_ref,
             qsq_ref, ksq_ref, first_ref, last_ref, carry_ref):
        i = pl.program_id(0)

        @pl.when(i == 0)
        def _():
            carry_ref[...] = jnp.zeros_like(carry_ref)

        lane = lax.broadcasted_iota(jnp.int32, (1, LANES), 1)
        for src, gain, dst, sq_ref in ((q_ref, gq_ref, qn_ref, qsq_ref),
                                       (k_ref, gk_ref, kn_ref, ksq_ref)):
            sq_row = jnp.zeros((1, LANES), F32)
            for h in range(heads):
                sl = slice(h * hd, (h + 1) * hd)
                v = src[:, sl]
                r = lax.rsqrt(jnp.mean(v * v, axis=1, keepdims=True) + NORM_EPS)
                normed = (v * r * gain[...]).astype(BF16)
                dst[:, sl] = normed
                nf = normed.astype(F32)
                sq = jnp.max(jnp.sum(nf * nf, axis=1, keepdims=True), axis=0, keepdims=True)
                sq_row = jnp.where(lane == h, sq, sq_row)
            sq_ref[0] = sq_row
        vb_ref[...] = v_ref[...].astype(BF16)
        fl = f_ref[...] + fb_ref[...]
        log_f = jnp.minimum(fl, 0.0) - jnp.log(1.0 + jnp.exp(-jnp.abs(fl)))
        rows = lax.broadcasted_iota(jnp.int32, (tr, tr), 0)
        cols = lax.broadcasted_iota(jnp.int32, (tr, tr), 1)
        lower = jnp.where(rows >= cols, 1.0, 0.0).astype(BF16)
        hi, mid, lo = _split3(log_f)
        cum_ref[...] = (_dot(lower, hi) + _dot(lower, mid) + _dot(lower, lo)) + carry_ref[...]
        carry_ref[...] = cum_ref[tr - 1:tr, :]
        first_ref[0] = cum_ref[0:1, :]
        last_ref[0] = cum_ref[tr - 1:tr, :]

    wide = lambda col: pl.BlockSpec((tr, bw), lambda i: (i, col))
    vec = pl.BlockSpec((1, hd), lambda i: (0, 0))
    stat = pl.BlockSpec((1, 1, LANES), lambda i: (i, 0, 0))
    return pl.pallas_call(
        body, name=name, grid=(t // tr,),
        in_specs=[wide(0), wide(1), wide(2), pl.BlockSpec((tr, LANES), lambda i: (i, fcol)),
                  vec, vec, pl.BlockSpec((1, LANES), lambda i: (0, 0))],
        out_specs=[wide(0), wide(0), wide(0), pl.BlockSpec((tr, LANES), lambda i: (i, 0))]
        + [stat] * 4,
        out_shape=[jax.ShapeDtypeStruct((t, bw), BF16)] * 3 + [jax.ShapeDtypeStruct((t, LANES), F32)]
        + [jax.ShapeDtypeStruct((t // tr, 1, LANES), F32)] * 4,
        scratch_shapes=[pltpu.VMEM((1, LANES), F32)],
        compiler_params=_params(("arbitrary",)),
    )(proj, proj, proj, proj, q_gain, k_gain, f_bias)


def _skip_plan(qsq, ksq, first, last, t, tq, hd, name):
    nt = qsq.shape[0]
    nq = t // tq
    r = nt // nq
    scale2 = hd ** -0.5 * LOG2E

    def body(qsq_ref, ksq_ref, first_ref, last_ref, out_ref):
        kmax = ksq_ref[0]
        for tile in range(1, nt):
            kmax = jnp.maximum(kmax, ksq_ref[tile])
        for i in range(nq):
            qmax = qsq_ref[i * r]
            for tile in range(i * r + 1, (i + 1) * r):
                qmax = jnp.maximum(qmax, qsq_ref[tile])
            coef = 2.0 * scale2 * jnp.sqrt(qmax * kmax)
            start = first_ref[i * r]
            count = jnp.zeros((1, LANES), jnp.int32)
            for j in range(i):
                bound = coef + (start - last_ref[(j + 1) * r - 1]) * LOG2E
                count = count + jnp.where(bound <= UNDERFLOW_LOG2, 1, 0)
            out_ref[i:i + 1, :] = count

    return pl.pallas_call(
        body, name=name, out_shape=jax.ShapeDtypeStruct((nq, LANES), jnp.int32),
    )(qsq, ksq, first, last)


def _flash_fwd(first_block, qn, kn, vb, ck, proj, heads, hd, tq, name):
    t = qn.shape[0]
    nq = t // tq
    th = tq // 2
    scale2 = hd ** -0.5 * LOG2E
    zcol = 3 * heads
    rc = _pick(tq, SOFTMAX_ROWS, 16)
    reps = th // LANES

    def body(first_ref, q_ref, k_ref, v_ref, ck_ref, z_ref, o_ref, y_ref, lse_ref,
             m_s, l_s, acc_s, s_a, s_b, p_a, p_b, al_a, al_b):
        i = pl.program_id(1)
        j0 = first_ref[pl.program_id(0), i]
        bufs = ((s_a, p_a, al_a), (s_b, p_b, al_b))
        m_s[...] = jnp.full_like(m_s, -jnp.inf)
        l_s[...] = jnp.zeros_like(l_s)
        acc_s[...] = jnp.zeros_like(acc_s)
        p_b[...] = jnp.zeros_like(p_b)
        al_b[...] = jnp.ones_like(al_b)

        def scores(j, half, first_row=0):
            off = pl.multiple_of(j * tq + half * th, th)
            bufs[half][0][first_row:, :] = _dot_nt(q_ref[first_row:, :], k_ref[pl.ds(off, th), :])

        def values(j, half, first_row=0):
            off = pl.multiple_of(j * tq + half * th, th)
            _, p_buf, al = bufs[half]
            acc_s[first_row:, :] = (
                jnp.tile(al[first_row:, :], (1, hd // LANES)) * acc_s[first_row:, :]
                + _dot(p_buf[first_row:, :], v_ref[pl.ds(off, th), :]))

        def softmax(j, half, masked):
            s_buf, p_buf, al = bufs[half]
            ck2 = ck_ref[0, j][:, half * th:(half + 1) * th] * LOG2E
            chunks = range((half * th) // rc if masked else 0, tq // rc)
            for c in chunks:
                r = slice(c * rc, (c + 1) * rc)
                s = s_buf[r, :] * scale2 - ck2
                if masked and c * rc < (half + 1) * th:
                    rows = lax.broadcasted_iota(jnp.int32, (rc, th), 0) + c * rc
                    cols = lax.broadcasted_iota(jnp.int32, (rc, th), 1) + half * th
                    s = jnp.where(rows >= cols, s, -jnp.inf)
                s_buf[r, :] = s
                m_prev = m_s[r, :]
                m_new = jnp.maximum(m_prev, jnp.max(s, axis=1, keepdims=True))
                al[r, :] = jnp.exp2(m_prev - m_new)
                m_s[r, :] = m_new
            for c in chunks:
                r = slice(c * rc, (c + 1) * rc)
                p = jnp.exp2(s_buf[r, :] - jnp.tile(m_s[r, :], (1, reps)))
                p_buf[r, :] = p.astype(BF16)
                lane_sum = p[:, 0:LANES]
                for b in range(1, reps):
                    lane_sum = lane_sum + p[:, b * LANES:(b + 1) * LANES]
                l_s[r, :] = al[r, :] * l_s[r, :] + lane_sum

        scores(j0, 0)

        def loop_body(j, carry):
            scores(j, 1)
            values(jnp.maximum(j - 1, 0), 1)
            softmax(j, 0, False)
            scores(j + 1, 0)
            values(j, 0)
            softmax(j, 1, False)
            return carry

        lax.fori_loop(j0, i, loop_body, 0)
        values(jnp.maximum(i - 1, 0), 1)
        scores(i, 1, first_row=th)
        softmax(i, 0, True)
        values(i, 0)
        softmax(i, 1, True)
        values(i, 1, first_row=th)
        l = jnp.sum(l_s[...], axis=1, keepdims=True)
        o = acc_s[...] / l
        z = z_ref[...]
        o_ref[...] = o
        y_ref[...] = (o * (z * _sigmoid(z))).astype(BF16)
        lse_ref[0] = m_s[:, 0:1] + jnp.log(l) * LOG2E

    blk = pl.BlockSpec((tq, hd), lambda h, i, first: (i, h))
    head = pl.BlockSpec((t, hd), lambda h, i, first: (0, h))
    col = pl.BlockSpec((1, tq, 1), lambda h, i, first: (h, i, 0))
    stat = pltpu.VMEM((tq, LANES), F32)
    return pl.pallas_call(
        body, name=name,
        grid_spec=pltpu.PrefetchScalarGridSpec(
            num_scalar_prefetch=1, grid=(heads, nq),
            in_specs=[blk, head, head,
                      pl.BlockSpec((1, nq, 1, tq), lambda h, i, first: (h, 0, 0, 0)),
                      pl.BlockSpec((tq, hd), lambda h, i, first: (i, zcol + h))],
            out_specs=[blk, blk, col],
            scratch_shapes=[stat, stat, pltpu.VMEM((tq, hd), F32),
                            pltpu.VMEM((tq, th), F32), pltpu.VMEM((tq, th), F32),
                            pltpu.VMEM((tq, th), BF16), pltpu.VMEM((tq, th), BF16), stat, stat]),
        out_shape=[jax.ShapeDtypeStruct((t, heads * hd), F32),
                   jax.ShapeDtypeStruct((t, heads * hd), BF16),
                   jax.ShapeDtypeStruct((heads, t, 1), F32)],
        compiler_params=_params(("arbitrary", "arbitrary")),
    )(first_block, qn, kn, vb, ck, proj)


def _attn_bwd_prep(dy, proj, o, heads, hd, name):
    t, bw = dy.shape
    tr = _pick(t, 256, 16)

    def body(dy_ref, z_ref, o_ref, do_ref, dz_ref, delta_ref):
        dyv, z, ov = dy_ref[...], z_ref[...], o_ref[...]
        sz = _sigmoid(z)
        do = dyv * (z * sz)
        do_ref[...] = do.astype(BF16)
        dz_ref[...] = (dyv * ov * (sz * (1.0 + z * (1.0 - sz)))).astype(BF16)
        prod = do * ov
        for h in range(heads):
            delta_ref[h] = jnp.sum(prod[:, h * hd:(h + 1) * hd], axis=1, keepdims=True)

    row = pl.BlockSpec((tr, bw), lambda i: (i, 0))
    return pl.pallas_call(
        body, name=name, grid=(t // tr,),
        in_specs=[row, pl.BlockSpec((tr, bw), lambda i: (i, 3)), row],
        out_specs=[row, row, pl.BlockSpec((heads, tr, 1), lambda i: (0, i, 0))],
        out_shape=[jax.ShapeDtypeStruct((t, bw), BF16), jax.ShapeDtypeStruct((t, bw), BF16),
                   jax.ShapeDtypeStruct((heads, t, 1), F32)],
        compiler_params=_params(("arbitrary",)),
    )(dy, proj, o)


def _flash_bwd(first_block, qn, kn, vb, do, ck, lse, delta, heads, hd, tq, name):
    t = qn.shape[0]
    nq = t // tq
    scale = hd ** -0.5
    scale2 = scale * LOG2E

    def body(first_ref, q_ref, k_ref, v_ref, do_ref, ck_ref, lse_ref, delta_ref,
             dq_ref, dk_ref, dv_ref, dcq_ref, dck_ref, dq_s, dcq_s):
        i = pl.program_id(1)
        j0 = first_ref[pl.program_id(0), i]

        @pl.when(i == 0)
        def _():
            dk_ref[...] = jnp.zeros_like(dk_ref)
            dv_ref[...] = jnp.zeros_like(dv_ref)
            dck_ref[...] = jnp.zeros_like(dck_ref)

        dq_s[...] = jnp.zeros_like(dq_s)
        dcq_s[...] = jnp.zeros_like(dcq_s)

        def step(j, masked, row0=0, col0=0, ncol=tq):
            off = pl.multiple_of(j * tq + col0, ncol)
            kblk = k_ref[pl.ds(off, ncol), :]
            q, dov = q_ref[row0:, :], do_ref[row0:, :]
            s = _dot_nt(q, kblk) * scale2 - ck_ref[0, j][:, col0:col0 + ncol] * LOG2E
            if masked:
                rows = lax.broadcasted_iota(jnp.int32, (tq - row0, ncol), 0) + row0
                cols = lax.broadcasted_iota(jnp.int32, (tq - row0, ncol), 1) + col0
                s = jnp.where(rows >= cols, s, -jnp.inf)
            p = jnp.exp2(s - lse_ref[0, row0:, :])
            dp = _dot_nt(dov, v_ref[pl.ds(off, ncol), :])
            ds = p * (dp - delta_ref[0, row0:, :])
            dsb = ds.astype(BF16)
            dv_ref[pl.ds(off, ncol), :] += _dot_tn(p.astype(BF16), dov)
            dk_ref[pl.ds(off, ncol), :] += _dot_tn(dsb, q) * scale
            dq_s[row0:, :] += _dot(dsb, kblk) * scale
            dcq_s[row0:, :] += jnp.sum(ds, axis=1, keepdims=True)
            dck_ref[0, j, :, col0:col0 + ncol] += jnp.sum(ds, axis=0, keepdims=True)

        def loop_body(j, carry):
            step(j, False)
            return carry

        lax.fori_loop(j0, i, loop_body, 0)
        step(i, True, 0, 0, tq // 2)
        step(i, True, tq // 2, tq // 2, tq // 2)
        dq_ref[...] = dq_s[...]
        dcq_ref[0] = dcq_s[...]

    blk = pl.BlockSpec((tq, hd), lambda h, i, first: (i, h))
    head = pl.BlockSpec((t, hd), lambda h, i, first: (0, h))
    col = pl.BlockSpec((1, tq, 1), lambda h, i, first: (h, i, 0))
    rowv = pl.BlockSpec((1, nq, 1, tq), lambda h, i, first: (h, 0, 0, 0))
    full = jax.ShapeDtypeStruct((t, heads * hd), F32)
    return pl.pallas_call(
        body, name=name,
        grid_spec=pltpu.PrefetchScalarGridSpec(
            num_scalar_prefetch=1, grid=(heads, nq),
            in_specs=[blk, head, head, blk, rowv, col, col],
            out_specs=[blk, head, head, col, rowv],
            scratch_shapes=[pltpu.VMEM((tq, hd), F32), pltpu.VMEM((tq, 1), F32)]),
        out_shape=[full, full, full, jax.ShapeDtypeStruct((heads, t, 1), F32),
                   jax.ShapeDtypeStruct((heads, nq, 1, tq), F32)],
        compiler_params=_params(("arbitrary", "arbitrary")),
    )(first_block, qn, kn, vb, do, ck, lse, delta)


def _attn_bwd_post(dqn, dkn, dv, dz, proj, dcq, dck, q_gain, k_gain, f_bias, heads, hd, name):
    t, bw = dqn.shape
    tr = _pick(t, 128, 16)
    nb = t // tr
    fcol = 4 * bw // LANES
    width = 4 * bw + LANES

    def body(dq_ref, dk_ref, dv_ref, dz_ref, q_ref, k_ref, f_ref, dcq_ref, dck_ref, gq_ref, gk_ref,
             fb_ref, d_ref, dgq_ref, dgk_ref, dfb_ref, carry_ref, rc_ref):
        i = pl.program_id(0)

        @pl.when(i == 0)
        def _():
            carry_ref[...] = jnp.zeros_like(carry_ref)
            dgq_ref[...] = jnp.zeros_like(dgq_ref)
            dgk_ref[...] = jnp.zeros_like(dgk_ref)
            dfb_ref[...] = jnp.zeros_like(dfb_ref)

        for idx, (g_ref, raw_ref, gain_ref, dgain_ref) in enumerate(
                ((dq_ref, q_ref, gq_ref, dgq_ref), (dk_ref, k_ref, gk_ref, dgk_ref))):
            gain = gain_ref[...]
            dgain = jnp.zeros((1, hd), F32)
            for h in range(heads):
                sl = slice(h * hd, (h + 1) * hd)
                v, dn = raw_ref[:, sl], g_ref[:, sl]
                r = lax.rsqrt(jnp.mean(v * v, axis=1, keepdims=True) + NORM_EPS)
                vh = v * r
                dgain = dgain + jnp.sum(dn * vh, axis=0, keepdims=True)
                dvh = dn * gain
                draw = r * (dvh - vh * jnp.mean(dvh * vh, axis=1, keepdims=True))
                d_ref[:, idx * bw + h * hd:idx * bw + (h + 1) * hd] = draw.astype(BF16)
            dgain_ref[...] += dgain
        d_ref[:, 2 * bw:3 * bw] = dv_ref[...].astype(BF16)
        d_ref[:, 3 * bw:4 * bw] = dz_ref[...]
        rows = lax.broadcasted_iota(jnp.int32, (tr, tr), 0)
        cols = lax.broadcasted_iota(jnp.int32, (tr, tr), 1)
        upper = jnp.where(cols >= rows, 1.0, 0.0).astype(BF16)
        hi, mid, lo = _split3(dcq_ref[...] - dck_ref[...])
        rc_ref[...] = (_dot(upper, hi) + _dot(upper, mid) + _dot(upper, lo)) + carry_ref[...]
        carry_ref[...] = rc_ref[0:1, :]
        df = rc_ref[...] * (1.0 / (1.0 + jnp.exp(f_ref[...] + fb_ref[...])))
        d_ref[:, 4 * bw:] = df.astype(BF16)
        dfb_ref[...] += jnp.sum(df, axis=0, keepdims=True)

    wide = lambda col: pl.BlockSpec((tr, bw), lambda i: (nb - 1 - i, col))
    lane = lambda col: pl.BlockSpec((tr, LANES), lambda i: (nb - 1 - i, col))
    vec = pl.BlockSpec((1, hd), lambda i: (0, 0))
    vecl = pl.BlockSpec((1, LANES), lambda i: (0, 0))
    return pl.pallas_call(
        body, name=name, grid=(nb,),
        in_specs=[wide(0), wide(0), wide(0), wide(0), wide(0), wide(1), lane(fcol), lane(0), lane(0),
                  vec, vec, vecl],
        out_specs=[pl.BlockSpec((tr, width), lambda i: (nb - 1 - i, 0)), vec, vec, vecl],
        out_shape=[jax.ShapeDtypeStruct((t, width), BF16), jax.ShapeDtypeStruct((1, hd), F32),
                   jax.ShapeDtypeStruct((1, hd), F32), jax.ShapeDtypeStruct((1, LANES), F32)],
        scratch_shapes=[pltpu.VMEM((1, LANES), F32), pltpu.VMEM((tr, LANES), F32)],
        compiler_params=_params(("arbitrary",)),
    )(dqn, dkn, dv, dz, proj, proj, proj, dcq, dck, q_gain, k_gain, f_bias)


def _adamw(w, m, v, parts, name, exchange=None):
    nl, r, c = w.shape
    itemsize = parts[0].dtype.itemsize
    unit = 32 // itemsize
    row_bytes = c * (7 * 4 + N_DEV * itemsize * nl)
    tr = _pick(r, max(unit, 12 * 1024 * 1024 // row_bytes), unit)
    nr = r // tr
    c1 = 1.0 / (1.0 - ADAM_B1 ** ADAM_STEP)
    c2 = 1.0 / (1.0 - ADAM_B2 ** ADAM_STEP)

    def body(*refs):
        w_ref, m_ref, v_ref = refs[:3]
        p_refs = refs[3:3 + nl]
        g_ref, d_ref, nm_ref, nv_ref = refs[3 + nl:]
        layer = pl.program_id(0)

        def partial(j):
            p = p_refs[0][j].astype(F32)
            for q in range(1, nl):
                p = jnp.where(layer == q, p_refs[q][j].astype(F32), p)
            return p

        g = partial(0)
        for j in range(1, N_DEV):
            g = g + partial(j)
        nm = ADAM_B1 * m_ref[0] + (1.0 - ADAM_B1) * g
        nv = ADAM_B2 * v_ref[0] + (1.0 - ADAM_B2) * (g * g)
        g_ref[0] = g
        nm_ref[0] = nm
        nv_ref[0] = nv
        d_ref[0] = -ADAM_LR * ((nm * c1) / (jnp.sqrt(nv * c2) + ADAM_EPS) + ADAM_WD * w_ref[0])

    def part_spec(q):
        rest = 0 if q > 0 else nr - 1
        return pl.BlockSpec((N_DEV, tr, c), lambda l, i: (0, jnp.where(l == q, i, rest), 0))

    row = pl.BlockSpec((1, tr, c), lambda l, i: (l, i, 0))
    in_specs, out_specs = [row, row, row] + [part_spec(q) for q in range(nl)], [row] * 4
    out_shape = [jax.ShapeDtypeStruct((nl, r, c), F32)] * 4
    operands, scratch = [w, m, v, *parts], []
    if exchange is not None:
        any_spec = pl.BlockSpec(memory_space=pl.ANY)
        body = _carry_exchange(
            body, exchange, 3 + nl, 4, 0,
            lambda: (pl.program_id(0) == 0) & (pl.program_id(1) == 0),
            lambda: (pl.program_id(0) == nl - 1) & (pl.program_id(1) == nr - 1))
        in_specs, out_specs = in_specs + [any_spec] * exchange.n, out_specs + [any_spec] * exchange.n
        out_shape, operands = out_shape + exchange.out_shapes(), operands + exchange.arrays
        scratch = exchange.scratch()
    return pl.pallas_call(
        body, name=name, grid=(nl, nr), in_specs=in_specs, out_specs=out_specs,
        out_shape=out_shape, scratch_shapes=scratch,
        compiler_params=_params(("arbitrary", "arbitrary")),
    )(*operands)


def _flat_rows(pieces):
    rows = []
    for p in pieces:
        f = p.reshape(-1)
        f = jnp.pad(f, (0, (-f.shape[0]) % LANES))
        rows.append(f.reshape(-1, LANES))
    out = jnp.concatenate(rows, axis=0)
    return jnp.pad(out, ((0, (-out.shape[0]) % 8), (0, 0)))


def _unflat_rows(flat, shapes):
    outs, r0 = [], 0
    lead = flat.shape[:-2]
    for s in shapes:
        size = 1
        for d in s:
            size *= d
        nr = -(-size // LANES)
        piece = flat[..., r0:r0 + nr, :].reshape(lead + (nr * LANES,))[..., :size]
        outs.append(piece.reshape(lead + tuple(s)))
        r0 += nr
    return outs


def kernel(x, a_norm_g, a_w_in, a_v_norm_g, a_w_s, a_b_s, a_w_out, b_norm_g, b_w_in, b_f_bias, b_q_norm_g, b_k_norm_g, b_w_out, loss_target, m_a_norm_g, m_a_w_in, m_a_v_norm_g, m_a_w_s, m_a_b_s, m_a_w_out, m_b_norm_g, m_b_w_in, m_b_f_bias, m_b_q_norm_g, m_b_k_norm_g, m_b_w_out, v_a_norm_g, v_a_w_in, v_a_v_norm_g, v_a_w_s, v_a_b_s, v_a_w_out, v_b_norm_g, v_b_w_in, v_b_f_bias, v_b_q_norm_g, v_b_k_norm_g, v_b_w_out):
    t, d = x.shape[1], x.shape[2]
    n_a, n_b = a_w_in.shape[0], b_w_in.shape[0]
    depth = n_a + n_b
    aw = a_w_out.shape[1] * N_DEV
    groups, chunk = a_w_s.shape[1], a_w_s.shape[2]
    heads, hd = b_f_bias.shape[1], b_q_norm_g.shape[1]
    bw = heads * hd
    b_cols = b_w_in.shape[2]
    tq = _pick(t, 512, LANES)
    nq = t // tq
    me = _dev_index((lax.axis_index("x"), lax.axis_index("y"), lax.axis_index("c")))

    pad_h = lambda v: jnp.pad(v, ((0, 0), (0, LANES - heads)))
    f_bias_p = pad_h(b_f_bias)
    b_s_col = a_b_s.reshape(n_a, groups, chunk, 1)
    w_s_t = a_w_s.transpose(0, 1, 3, 2)

    def weight_gather(i, extra=()):
        j = i // 2
        if i % 2 == 0:
            return _Gather([a_w_in[j].astype(BF16), a_w_out[j].astype(BF16)] + [a for a, _ in extra],
                           [1, 0] + [ax for _, ax in extra])
        return _Gather([b_w_in[j].astype(BF16), b_w_out[j].astype(BF16)] + [a for a, _ in extra],
                       [None, 0] + [ax for _, ax in extra])

    def whole_w_in(i, w_in):
        if i % 2 == 1:
            w_in = w_in.transpose(1, 0, 2).reshape(d, 4 * bw + heads)
            w_in = jnp.pad(w_in, ((0, 0), (0, LANES - heads)))
        return w_in

    (w_in0,) = _run_exchange(_Gather([a_w_in[0].astype(BF16)], [1]), "gather_weights")
    weights = {0: [w_in0, None]}
    late = ((a_w_out[0].astype(BF16), 0), (b_norm_g, 1))

    xs = [x[0]]
    saved = []
    for i in range(depth):
        j = i // 2
        xi = xs[-1]
        nxt = weight_gather(i + 1, extra=late if i == 0 else ()) if i + 1 < depth else None
        if i % 2 == 0:
            h = _rms_fwd(xi, a_norm_g[j:j + 1], f"a{j}_norm")
            res = _matmul(h, weights[i][0], "nn", F32, f"a{j}_in", tn=2048, tk=d, exchange=nxt)
        else:
            h = _rms_fwd(xi, b_norm_full[j:j + 1], f"b{j}_norm")
            res = _matmul(h, weights[i][0], "nn", F32, f"b{j}_in", tn=1664, tk=d, exchange=nxt)
        if nxt is None:
            pre = res
        else:
            pre = res[0]
            weights[i + 1] = [whole_w_in(i + 1, res[1]), res[2]]
            if i == 0:
                weights[0][1], b_norm_full = res[3], res[4]
        w_in, w_out = weights[i]
        if i % 2 == 0:
            y = _mix_fwd(pre, a_v_norm_g[j:j + 1], a_w_s[j], b_s_col[j], f"a{j}_mix")
            xs.append(_matmul(y, w_out, "nn", F32, f"a{j}_out", residual=xi))
            saved.append((h, pre, y))
        else:
            qn, kn, vb, cum, qsq, ksq, cum_first, cum_last = _attn_prep(
                pre, b_q_norm_g[j:j + 1], b_k_norm_g[j:j + 1], f_bias_p[j:j + 1], heads, hd,
                f"b{j}_prep")
            first = _skip_plan(qsq, ksq, cum_first, cum_last, t, tq, hd, f"b{j}_plan")
            first = first[:, :heads].T
            ck = cum[:, :heads].T.reshape(heads, nq, 1, tq)
            o, y, lse = _flash_fwd(first, qn, kn, vb, ck, pre, heads, hd, tq, f"b{j}_attn")
            xs.append(_matmul(y, w_out, "nn", F32, f"b{j}_out", residual=xi))
            saved.append((h, pre, y, qn, kn, vb, ck, o, lse, first))

    sq, g, gb = _loss_grad(xs[-1], loss_target[0], "loss")
    loss = 0.5 * lax.psum(sq[0, 0], AXES) / d

    d_a_norm, d_a_vnorm, d_a_ws, d_a_bs = [None] * n_a, [None] * n_a, [None] * n_a, [None] * n_a
    d_b_norm, d_b_fb, d_b_gq, d_b_gk = [None] * n_b, [None] * n_b, [None] * n_b, [None] * n_b
    recv_in, recv_out = {}, {}
    pending = None

    def in_blocks(i, dw_in):
        if i % 2 == 0:
            return _AllToAll([dw_in], [1])
        rows = dw_in.shape[0]
        return _AllToAll(
            [dw_in[:, :4 * bw + heads].reshape(rows, N_DEV, b_cols).transpose(1, 0, 2)], [None])

    for i in reversed(range(depth)):
        j = i // 2
        xi = xs[i]
        w_in, w_out = weights[i]
        if i % 2 == 0:
            h, uvz, y = saved[i]
            dy = _matmul(gb, w_out, "nt", F32, f"a{j}_dy", tn=2048, tk=d)
            dw_out = _matmul(y, gb, "tn", BF16, f"a{j}_dwout")
            dpre, d_a_ws[j], dbs, d_a_vnorm[j] = _mix_bwd(
                uvz, dy, a_v_norm_g[j:j + 1], a_w_s[j], w_s_t[j], b_s_col[j], f"a{j}_mixbwd")
            d_a_bs[j] = dbs.reshape(groups, chunk)
            name, gain = f"a{j}", a_norm_g[j:j + 1]
        else:
            h, proj, y, qn, kn, vb, ck, o, lse, first = saved[i]
            dy = _matmul(gb, w_out, "nt", F32, f"b{j}_dy", tn=2048, tk=d)
            dw_out = _matmul(y, gb, "tn", BF16, f"b{j}_dwout")
            do, dz, delta = _attn_bwd_prep(dy, proj, o, heads, hd, f"b{j}_bwdprep")
            dqn, dkn, dv, dcq, dck = _flash_bwd(first, qn, kn, vb, do, ck, lse, delta, heads, hd,
                                                tq, f"b{j}_attnbwd")
            per_token = lambda v: pad_h(v.reshape(heads, t).T)
            dpre, d_b_gq[j], d_b_gk[j], dfb = _attn_bwd_post(
                dqn, dkn, dv, dz, proj, per_token(dcq), per_token(dck), b_q_norm_g[j:j + 1],
                b_k_norm_g[j:j + 1], f_bias_p[j:j + 1], heads, hd, f"b{j}_bwdpost")
            d_b_fb[j] = dfb[:, :heads]
            name, gain = f"b{j}", b_norm_full[j:j + 1]
        if pending is None:
            dh = _matmul(dpre, w_in, "nt", F32, name + "_dh")
        else:
            dh, got = _matmul(dpre, w_in, "nt", F32, name + "_dh", exchange=pending[1])
            parts_in = pending[2] + [got]
            recv_in[pending[0]] = parts_in[0] if len(parts_in) == 1 else jnp.concatenate(
                parts_in, axis=1)
        n_parts = (4 if i == 0 else 2) if i % 2 == 0 else 1
        rows = d // n_parts
        riding = _AllToAll([dw_out], [0])
        landed = []
        for part in range(n_parts):
            h_part = h if n_parts == 1 else h[:, part * rows:(part + 1) * rows]
            dw_part, got = _matmul(h_part, dpre, "tn", BF16, f"{name}_dwin{part}",
                                   tn=1024 if i % 2 == 0 else 1664, exchange=riding)
            if part == 0:
                recv_out[i] = got
            else:
                landed.append(got)
            riding = in_blocks(i, dw_part)
        if i > 0:
            pending = (i, riding, landed)
            g, gb, dgain = _rms_bwd(xi, dh, g, gain, name + "_normbwd")
        else:
            g, gb, dgain, got = _rms_bwd(xi, dh, g, gain, name + "_normbwd", exchange=riding)
            recv_in[i] = jnp.concatenate(landed + [got], axis=1)
        if i % 2 == 0:
            d_a_norm[j] = dgain
        else:
            d_b_norm[j] = dgain
    grad_x = g[None]

    small = [jnp.concatenate(d_a_norm, 0), jnp.concatenate(d_a_vnorm, 0), jnp.stack(d_a_ws, 0),
             jnp.stack(d_a_bs, 0), jnp.concatenate(d_b_fb, 0), jnp.concatenate(d_b_gq, 0),
             jnp.concatenate(d_b_gk, 0)]
    small_w = [a_norm_g, a_v_norm_g, a_w_s, a_b_s, b_f_bias, b_q_norm_g, b_k_norm_g]
    small_m = [m_a_norm_g, m_a_v_norm_g, m_a_w_s, m_a_b_s, m_b_f_bias, m_b_q_norm_g, m_b_k_norm_g]
    small_v = [v_a_norm_g, v_a_v_norm_g, v_a_w_s, v_a_b_s, v_b_f_bias, v_b_q_norm_g, v_b_k_norm_g]
    small_flat = _flat_rows(small)
    n_small = small_flat.shape[0]
    small_gather = _Gather(
        [jnp.concatenate([small_flat, _flat_rows([jnp.concatenate(d_b_norm, 0)])], axis=0)],
        [None])

    a_layers, b_layers = range(0, depth, 2), range(1, depth, 2)
    *u_a_in, gathered_small = _adamw(a_w_in, m_a_w_in, v_a_w_in, [recv_in[i] for i in a_layers],
                                     "adamw_a_w_in", exchange=small_gather)
    parts_small = gathered_small[:, :n_small]
    parts_b_norm = gathered_small[:, n_small:n_small + n_b * d // LANES].reshape(N_DEV, n_b, d)
    parts_b_norm = lax.dynamic_slice_in_dim(parts_b_norm, me * (d // N_DEV), d // N_DEV, axis=2)
    u_a_out = _adamw(a_w_out, m_a_w_out, v_a_w_out, [recv_out[i] for i in a_layers],
                     "adamw_a_w_out")
    u_b_in = _adamw(b_w_in, m_b_w_in, v_b_w_in, [recv_in[i] for i in b_layers], "adamw_b_w_in")
    u_b_out = _adamw(b_w_out, m_b_w_out, v_b_w_out, [recv_out[i] for i in b_layers],
                     "adamw_b_w_out")
    u_b_norm = [o_[0] for o_ in _adamw(b_norm_g[None], m_b_norm_g[None], v_b_norm_g[None],
                                       [parts_b_norm], "adamw_b_norm")]
    u_small = _adamw(_flat_rows(small_w)[None], _flat_rows(small_m)[None],
                     _flat_rows(small_v)[None], [parts_small], "adamw_small")
    shapes = [w.shape for w in small_w]
    u_small = [_unflat_rows(o_[0], shapes) for o_ in u_small]

    def per_kind(k):
        s = u_small[k]
        return [s[0], u_a_in[k], s[1], s[2], s[3], u_a_out[k], u_b_norm[k], u_b_in[k], s[4], s[5],
                s[6], u_b_out[k]]

    return (loss, grad_x, *per_kind(0), *per_kind(1), *per_kind(2), *per_kind(3))
```

```python
import jax
import jax.numpy as jnp
from jax import lax
from jax.experimental import pallas as pl
from jax.experimental.pallas import tpu as pltpu

F32 = jnp.float32
BF16 = jnp.bfloat16
MESH = pl.DeviceIdType.MESH
AXES = ("x", "y", "c")
N_DEV = 8
NORM_EPS = 1e-6
LANES = 128
VMEM_LIMIT = 56 * 1024 * 1024

ADAM_LR = 0.001
ADAM_B1 = 0.9
ADAM_B2 = 0.999
ADAM_EPS = 1e-08
ADAM_WD = 0.01
ADAM_STEP = 10

LOG2E = 1.4426950408889634
SOFTMAX_ROWS = 64
UNDERFLOW_LOG2 = -160.0

GELU_C0 = 0.7978845608028654
GELU_C1 = 0.044715

NT_DIMS = (((1,), (1,)), ((), ()))
TN_DIMS = (((0,), (0,)), ((), ()))


def _params(sem=None):
    return pltpu.CompilerParams(dimension_semantics=sem, vmem_limit_bytes=VMEM_LIMIT)


def _pick(n, target, unit):
    best = None
    for t in range(unit, min(n, target) + 1, unit):
        if n % t == 0:
            best = t
    return n if best is None else best


def _sigmoid(x):
    return 0.5 + 0.5 * jnp.tanh(0.5 * x)


def _gelu(x):
    return x * (0.5 + 0.5 * jnp.tanh(x * (GELU_C0 + (GELU_C0 * GELU_C1) * (x * x))))


def _gelu_and_grad(x):
    x2 = x * x
    a = 0.5 + 0.5 * jnp.tanh(x * (GELU_C0 + (GELU_C0 * GELU_C1) * x2))
    dg = a + (x * (a * (1.0 - a))) * (2.0 * GELU_C0 + (6.0 * GELU_C0 * GELU_C1) * x2)
    return x * a, dg


def _dot(a, b):
    return jnp.dot(a, b, preferred_element_type=F32)


def _dot_nt(a, b):
    return lax.dot_general(a, b, NT_DIMS, preferred_element_type=F32)


def _dot_tn(a, b):
    return lax.dot_general(a, b, TN_DIMS, preferred_element_type=F32)


def _split3(v):
    hi = v.astype(BF16)
    r1 = v - hi.astype(F32)
    mid = r1.astype(BF16)
    lo = (r1 - mid.astype(F32)).astype(BF16)
    return hi, mid, lo


def _dev_index(p):
    return 4 * p[0] + 2 * p[1] + p[2]


def _block(ref, idx, axis, size):
    if axis is None:
        return ref.at[idx]
    start = pl.multiple_of(idx * size, size)
    return ref.at[(slice(None),) * axis + (pl.ds(start, size),)]


class _Exchange:
    def __init__(self, arrays, axes):
        self.arrays, self.axes, self.n = list(arrays), list(axes), len(arrays)

    def scratch(self):
        return [pltpu.SemaphoreType.DMA((self.n, 7)), pltpu.SemaphoreType.DMA((self.n, 7)),
                pltpu.SemaphoreType.DMA((self.n,))]

    @staticmethod
    def _place():
        x, y, c = lax.axis_index("x"), lax.axis_index("y"), lax.axis_index("c")
        return x, y, c


class _Gather(_Exchange):
    def out_shapes(self):
        outs = []
        for a, ax in zip(self.arrays, self.axes):
            if ax is None:
                shape = (N_DEV,) + a.shape
            else:
                shape = a.shape[:ax] + (N_DEV * a.shape[ax],) + a.shape[ax + 1:]
            outs.append(jax.ShapeDtypeStruct(shape, a.dtype))
        return outs

    def _copy(self, ins, outs, sems, a, k, block, to, src=None):
        ax = self.axes[a]
        dst = _block(outs[a], _dev_index(block), ax, None if ax is None else self.arrays[a].shape[ax])
        return pltpu.make_async_remote_copy(
            src_ref=dst if src is None else src, dst_ref=dst,
            send_sem=sems[0].at[a, k], recv_sem=sems[1].at[a, k],
            device_id=to, device_id_type=MESH)

    def _mine(self, ins, outs, sems, a, me):
        ax = self.axes[a]
        dst = _block(outs[a], _dev_index(me), ax, None if ax is None else self.arrays[a].shape[ax])
        return pltpu.make_async_copy(ins[a], dst, sems[2].at[a])

    def _first(self, ins, outs, sems):
        x, y, c = self._place()
        me, sibling = (x, y, c), (x, y, 1 - c)
        chips = [(1 - x, y), (x, 1 - y), (1 - x, 1 - y)]
        first = []
        for a in range(self.n):
            first.append(self._copy(ins, outs, sems, a, 0, me, sibling, src=ins[a]))
            first += [self._copy(ins, outs, sems, a, 1 + j, me, (*chip, c), src=ins[a])
                      for j, chip in enumerate(chips)]
        return first

    def start(self, ins, outs, sems):
        me = self._place()
        for a in range(self.n):
            self._mine(ins, outs, sems, a, me).start()
        for cp in self._first(ins, outs, sems):
            cp.start()

    def finish(self, ins, outs, sems):
        x, y, c = self._place()
        me, sibling = (x, y, c), (x, y, 1 - c)
        chips = [(1 - x, y), (x, 1 - y), (1 - x, 1 - y)]
        passed = []
        for a in range(self.n):
            for j, chip in enumerate(chips):
                self._copy(ins, outs, sems, a, 1 + j, (*chip, c), me).wait_recv()
                cp = self._copy(ins, outs, sems, a, 4 + j, (*chip, c), sibling)
                cp.start()
                passed.append(cp)
        for a in range(self.n):
            self._copy(ins, outs, sems, a, 0, sibling, me).wait_recv()
            for j, chip in enumerate(chips):
                self._copy(ins, outs, sems, a, 4 + j, (*chip, 1 - c), me).wait_recv()
        for cp in self._first(ins, outs, sems) + passed:
            cp.wait_send()
        for a in range(self.n):
            self._mine(ins, outs, sems, a, me).wait()


class _AllToAll(_Exchange):
    def _blk_shape(self, a):
        arr, ax = self.arrays[a], self.axes[a]
        if ax is None:
            return arr.shape[1:]
        return arr.shape[:ax] + (arr.shape[ax] // N_DEV,) + arr.shape[ax + 1:]

    def out_shapes(self):
        return [jax.ShapeDtypeStruct((N_DEV,) + self._blk_shape(a), self.arrays[a].dtype)
                for a in range(self.n)]

    def _src(self, ins, a, idx):
        ax = self.axes[a]
        return _block(ins[a], idx, ax, None if ax is None else self.arrays[a].shape[ax] // N_DEV)

    def _peers(self):
        x, y, c = self._place()
        return [((1 - x) if r & 4 else x, (1 - y) if r & 2 else y, (1 - c) if r & 1 else c)
                for r in range(1, N_DEV)]

    def _sends(self, ins, outs, sems):
        me = _dev_index(self._place())
        return [pltpu.make_async_remote_copy(
            src_ref=self._src(ins, a, _dev_index(peer)), dst_ref=outs[a].at[me],
            send_sem=sems[0].at[a, k], recv_sem=sems[1].at[a, k],
            device_id=peer, device_id_type=MESH)
            for a in range(self.n) for k, peer in enumerate(self._peers())]

    def _mine(self, ins, outs, sems):
        me = _dev_index(self._place())
        return [pltpu.make_async_copy(self._src(ins, a, me), outs[a].at[me], sems[2].at[a])
                for a in range(self.n)]

    def start(self, ins, outs, sems):
        for cp in self._mine(ins, outs, sems) + self._sends(ins, outs, sems):
            cp.start()

    def finish(self, ins, outs, sems):
        for a in range(self.n):
            for k, peer in enumerate(self._peers()):
                landed = outs[a].at[_dev_index(peer)]
                pltpu.make_async_remote_copy(
                    src_ref=landed, dst_ref=landed, send_sem=sems[0].at[a, k],
                    recv_sem=sems[1].at[a, k], device_id=peer, device_id_type=MESH).wait_recv()
        for cp in self._sends(ins, outs, sems):
            cp.wait_send()
        for cp in self._mine(ins, outs, sems):
            cp.wait()


def _run_exchange(exchange, name):
    n = exchange.n

    def body(*refs):
        ins, outs, sems = refs[:n], refs[n:2 * n], refs[2 * n:]
        exchange.start(ins, outs, sems)
        exchange.finish(ins, outs, sems)

    any_spec = pl.BlockSpec(memory_space=pl.ANY)
    return pl.pallas_call(
        body, name=name, out_shape=exchange.out_shapes(),
        in_specs=[any_spec] * n, out_specs=[any_spec] * n, scratch_shapes=exchange.scratch(),
    )(*exchange.arrays)


def _matmul(a, b, mode, out_dtype, name, tm=1024, tn=1024, tk=2048, residual=None, exchange=None):
    if mode == "tn":
        kdim, m = a.shape
    else:
        m, kdim = a.shape
    n = b.shape[0] if mode == "nt" else b.shape[1]
    tm, tn, tk = _pick(m, tm, LANES), _pick(n, tn, LANES), _pick(kdim, tk, LANES)
    nk = kdim // tk
    if mode == "tn":
        a_spec = pl.BlockSpec((tk, tm), lambda i, j, k: (k, i))
    else:
        a_spec = pl.BlockSpec((tm, tk), lambda i, j, k: (i, k))
    if mode == "nt":
        b_spec = pl.BlockSpec((tn, tk), lambda i, j, k: (j, k))
    else:
        b_spec = pl.BlockSpec((tk, tn), lambda i, j, k: (k, j))
    o_spec = pl.BlockSpec((tm, tn), lambda i, j, k: (i, j))
    dot = {"nn": _dot, "nt": _dot_nt, "tn": _dot_tn}[mode]
    has_res = residual is not None
    n_in = 3 if has_res else 2
    n_ex = 0 if exchange is None else exchange.n
    ni, nj = m // tm, n // tn

    def body(*refs):
        a_ref, b_ref = refs[:2]
        r_ref = refs[2] if has_res else None
        ex_ins = refs[n_in:n_in + n_ex]
        o_ref = refs[n_in + n_ex]
        ex_outs = refs[n_in + n_ex + 1:n_in + 2 * n_ex + 1]
        scratch = refs[n_in + 2 * n_ex + 1:]
        i, j, k = pl.program_id(0), pl.program_id(1), pl.program_id(2)

        if exchange is not None:
            sems = scratch[-3:]

            @pl.when((i == 0) & (j == 0) & (k == 0))
            def _():
                exchange.start(ex_ins, ex_outs, sems)

        def finish(acc):
            if has_res:
                acc = acc + r_ref[...]
            o_ref[...] = acc.astype(out_dtype)

        if nk == 1:
            finish(dot(a_ref[...], b_ref[...]))
        else:
            acc_ref = scratch[0]

            @pl.when(k == 0)
            def _():
                acc_ref[...] = jnp.zeros_like(acc_ref)

            acc_ref[...] += dot(a_ref[...], b_ref[...])

            @pl.when(k == nk - 1)
            def _():
                finish(acc_ref[...])

        if exchange is not None:
            @pl.when((i == ni - 1) & (j == nj - 1) & (k == nk - 1))
            def _():
                exchange.finish(ex_ins, ex_outs, sems)

    any_spec = pl.BlockSpec(memory_space=pl.ANY)
    operands = (a, b, residual) if has_res else (a, b)
    out_shape = jax.ShapeDtypeStruct((m, n), out_dtype)
    scratch_shapes = [] if nk == 1 else [pltpu.VMEM((tm, tn), F32)]
    if exchange is None:
        return pl.pallas_call(
            body, name=name, grid=(ni, nj, nk),
            in_specs=[a_spec, b_spec] + ([o_spec] if has_res else []),
            out_specs=o_spec, out_shape=out_shape, scratch_shapes=scratch_shapes,
            compiler_params=_params(("parallel", "parallel", "arbitrary")),
        )(*operands)
    return pl.pallas_call(
        body, name=name, grid=(ni, nj, nk),
        in_specs=[a_spec, b_spec] + ([o_spec] if has_res else []) + [any_spec] * n_ex,
        out_specs=[o_spec] + [any_spec] * n_ex,
        out_shape=[out_shape] + exchange.out_shapes(),
        scratch_shapes=scratch_shapes + exchange.scratch(),
        compiler_params=_params(("arbitrary", "arbitrary", "arbitrary")),
    )(*operands, *exchange.arrays)


def _rms_fwd(x, gain, name):
    t, d = x.shape
    tr = _pick(t, 256, 16)

    def body(x_ref, g_ref, h_ref):
        xv = x_ref[...]
        r = lax.rsqrt(jnp.mean(xv * xv, axis=1, keepdims=True) + NORM_EPS)
        h_ref[...] = (xv * r * g_ref[...]).astype(BF16)

    row = pl.BlockSpec((tr, d), lambda i: (i, 0))
    return pl.pallas_call(
        body, name=name, grid=(t // tr,),
        in_specs=[row, pl.BlockSpec((1, d), lambda i: (0, 0))],
        out_specs=row, out_shape=jax.ShapeDtypeStruct((t, d), BF16),
        compiler_params=_params(("arbitrary",)),
    )(x, gain)


def _carry_exchange(body, exchange, n_in, n_out, n_scratch, is_first, is_last):
    n = exchange.n

    def carrying(*refs):
        ins, ex_ins = refs[:n_in], refs[n_in:n_in + n]
        outs = refs[n_in + n:n_in + n + n_out]
        ex_outs = refs[n_in + n + n_out:n_in + 2 * n + n_out]
        scratch = refs[n_in + 2 * n + n_out:n_in + 2 * n + n_out + n_scratch]
        sems = refs[n_in + 2 * n + n_out + n_scratch:]

        @pl.when(is_first())
        def _():
            exchange.start(ex_ins, ex_outs, sems)

        body(*ins, *outs, *scratch)

        @pl.when(is_last())
        def _():
            exchange.finish(ex_ins, ex_outs, sems)

    return carrying


def _rms_bwd(x, dh, g_in, gain, name, exchange=None):
    t, d = x.shape
    tr = _pick(t, 256, 16)
    nt = t // tr

    def body(x_ref, dh_ref, gin_ref, g_ref, dx_ref, dxb_ref, dg_ref):
        i = pl.program_id(0)
        xv, dhv = x_ref[...], dh_ref[...]
        r = lax.rsqrt(jnp.mean(xv * xv, axis=1, keepdims=True) + NORM_EPS)
        xh = xv * r
        dxh = dhv * g_ref[...]
        dx = gin_ref[...] + r * (dxh - xh * jnp.mean(dxh * xh, axis=1, keepdims=True))
        dx_ref[...] = dx
        dxb_ref[...] = dx.astype(BF16)

        @pl.when(i == 0)
        def _():
            dg_ref[...] = jnp.zeros_like(dg_ref)

        dg_ref[...] += jnp.sum(dhv * xh, axis=0, keepdims=True)

    row = pl.BlockSpec((tr, d), lambda i: (i, 0))
    vec = pl.BlockSpec((1, d), lambda i: (0, 0))
    in_specs, out_specs = [row, row, row, vec], [row, row, vec]
    out_shape = [jax.ShapeDtypeStruct((t, d), F32), jax.ShapeDtypeStruct((t, d), BF16),
                 jax.ShapeDtypeStruct((1, d), F32)]
    operands, scratch = [x, dh, g_in, gain], []
    if exchange is not None:
        any_spec = pl.BlockSpec(memory_space=pl.ANY)
        body = _carry_exchange(body, exchange, 4, 3, 0, lambda: pl.program_id(0) == 0,
                               lambda: pl.program_id(0) == nt - 1)
        in_specs, out_specs = in_specs + [any_spec] * exchange.n, out_specs + [any_spec] * exchange.n
        out_shape, operands = out_shape + exchange.out_shapes(), operands + exchange.arrays
        scratch = exchange.scratch()
    return pl.pallas_call(
        body, name=name, grid=(nt,), in_specs=in_specs, out_specs=out_specs, out_shape=out_shape,
        scratch_shapes=scratch, compiler_params=_params(("arbitrary",)),
    )(*operands)


def _loss_grad(y, target, name):
    t, d = y.shape
    tr = _pick(t, 256, 16)

    def body(y_ref, t_ref, s_ref, g_ref, gb_ref):
        i = pl.program_id(0)
        e = y_ref[...] - t_ref[...]
        g = e * (1.0 / d)
        g_ref[...] = g
        gb_ref[...] = g.astype(BF16)

        @pl.when(i == 0)
        def _():
            s_ref[...] = jnp.zeros_like(s_ref)

        s_ref[...] += jnp.sum(jnp.sum(e * e, axis=1, keepdims=True), axis=0, keepdims=True)

    row = pl.BlockSpec((tr, d), lambda i: (i, 0))
    return pl.pallas_call(
        body, name=name, grid=(t // tr,),
        in_specs=[row, row],
        out_specs=[pl.BlockSpec((1, 1), lambda i: (0, 0)), row, row],
        out_shape=[jax.ShapeDtypeStruct((1, 1), F32), jax.ShapeDtypeStruct((t, d), F32),
                   jax.ShapeDtypeStruct((t, d), BF16)],
        compiler_params=_params(("arbitrary",)),
    )(y, target)


def _causal_weights(ws_ref, g, chunk, transposed):
    rows = lax.broadcasted_iota(jnp.int32, (chunk, chunk), 0)
    cols = lax.broadcasted_iota(jnp.int32, (chunk, chunk), 1)
    keep = (cols >= rows) if transposed else (rows >= cols)
    return jnp.where(keep, ws_ref[g], 0.0).astype(BF16)


def _mix_fwd(uvz, v_gain, w_s, b_s, name):
    t, w3 = uvz.shape
    w = w3 // 3
    groups, chunk = w_s.shape[0], w_s.shape[1]
    gd = w // groups

    def body(uvz_ref, gam_ref, ws_ref, bs_ref, y_ref):
        gv = _gelu(uvz_ref[:, w:2 * w])
        r = lax.rsqrt(jnp.mean(gv * gv, axis=1, keepdims=True) + NORM_EPS)
        vn = (gv * r * gam_ref[...]).astype(BF16)
        for g in range(groups):
            sl = slice(g * gd, (g + 1) * gd)
            mixed = _dot(_causal_weights(ws_ref, g, chunk, False), vn[:, sl]) + bs_ref[g]
            u = uvz_ref[:, g * gd:(g + 1) * gd]
            z = uvz_ref[:, 2 * w + g * gd:2 * w + (g + 1) * gd]
            y_ref[:, sl] = (_gelu(u) * mixed * (z * _sigmoid(z))).astype(BF16)

    return pl.pallas_call(
        body, name=name, grid=(t // chunk,),
        in_specs=[pl.BlockSpec((chunk, w3), lambda i: (i, 0)),
                  pl.BlockSpec((1, w), lambda i: (0, 0)),
                  pl.BlockSpec((groups, chunk, chunk), lambda i: (0, 0, 0)),
                  pl.BlockSpec((groups, chunk, 1), lambda i: (0, 0, 0))],
        out_specs=pl.BlockSpec((chunk, w), lambda i: (i, 0)),
        out_shape=jax.ShapeDtypeStruct((t, w), BF16),
        compiler_params=_params(("arbitrary",)),
    )(uvz, v_gain, w_s, b_s)


def _mix_bwd(uvz, dy, v_gain, w_s, w_s_t, b_s, name):
    t, w3 = uvz.shape
    w = w3 // 3
    groups, chunk = w_s.shape[0], w_s.shape[1]
    gd = w // groups

    def body(uvz_ref, dy_ref, gam_ref, ws_ref, wst_ref, bs_ref, d_ref, dws_ref, dbs_ref, dgam_ref,
             dvn_ref):
        i = pl.program_id(0)

        @pl.when(i == 0)
        def _():
            dws_ref[...] = jnp.zeros_like(dws_ref)
            dbs_ref[...] = jnp.zeros_like(dbs_ref)
            dgam_ref[...] = jnp.zeros_like(dgam_ref)

        gv, dgv = _gelu_and_grad(uvz_ref[:, w:2 * w])
        r = lax.rsqrt(jnp.mean(gv * gv, axis=1, keepdims=True) + NORM_EPS)
        vh = gv * r
        gam = gam_ref[...]
        vn = (vh * gam).astype(BF16)
        rows = lax.broadcasted_iota(jnp.int32, (chunk, chunk), 0)
        cols = lax.broadcasted_iota(jnp.int32, (chunk, chunk), 1)
        for g in range(groups):
            sl = slice(g * gd, (g + 1) * gd)
            mixed = _dot(_causal_weights(ws_ref, g, chunk, False), vn[:, sl]) + bs_ref[g]
            gu, dgu = _gelu_and_grad(uvz_ref[:, g * gd:(g + 1) * gd])
            z = uvz_ref[:, 2 * w + g * gd:2 * w + (g + 1) * gd]
            sz = _sigmoid(z)
            silu = z * sz
            dyv = dy_ref[:, sl]
            dmixed = dyv * gu * silu
            d_ref[:, sl] = (dyv * mixed * silu * dgu).astype(BF16)
            d_ref[:, 2 * w + g * gd:2 * w + (g + 1) * gd] = (
                dyv * gu * mixed * (sz * (1.0 + z * (1.0 - sz)))).astype(BF16)
            dmb = dmixed.astype(BF16)
            dws_ref[g] += jnp.where(rows >= cols, _dot_nt(dmb, vn[:, sl]), 0.0)
            dbs_ref[g] += jnp.sum(dmixed, axis=1, keepdims=True)
            dvn_ref[:, sl] = _dot(_causal_weights(wst_ref, g, chunk, True), dmb)
        dvn = dvn_ref[...]
        dgam_ref[...] += jnp.sum(dvn * vh, axis=0, keepdims=True)
        dvh = dvn * gam
        dgvv = r * (dvh - vh * jnp.mean(dvh * vh, axis=1, keepdims=True))
        d_ref[:, w:2 * w] = (dgvv * dgv).astype(BF16)

    return pl.pallas_call(
        body, name=name, grid=(t // chunk,),
        in_specs=[pl.BlockSpec((chunk, w3), lambda i: (i, 0)),
                  pl.BlockSpec((chunk, w), lambda i: (i, 0)),
                  pl.BlockSpec((1, w), lambda i: (0, 0)),
                  pl.BlockSpec((groups, chunk, chunk), lambda i: (0, 0, 0)),
                  pl.BlockSpec((groups, chunk, chunk), lambda i: (0, 0, 0)),
                  pl.BlockSpec((groups, chunk, 1), lambda i: (0, 0, 0))],
        out_specs=[pl.BlockSpec((chunk, w3), lambda i: (i, 0)),
                   pl.BlockSpec((groups, chunk, chunk), lambda i: (0, 0, 0)),
                   pl.BlockSpec((groups, chunk, 1), lambda i: (0, 0, 0)),
                   pl.BlockSpec((1, w), lambda i: (0, 0))],
        out_shape=[jax.ShapeDtypeStruct((t, w3), BF16),
                   jax.ShapeDtypeStruct((groups, chunk, chunk), F32),
                   jax.ShapeDtypeStruct((groups, chunk, 1), F32),
                   jax.ShapeDtypeStruct((1, w), F32)],
        scratch_shapes=[pltpu.VMEM((chunk, w), F32)],
        compiler_params=_params(("arbitrary",)),
    )(uvz, dy, v_gain, w_s, w_s_t, b_s)


def _attn_prep(proj, q_gain, k_gain, f_bias, heads, hd, name):
    t = proj.shape[0]
    bw = heads * hd
    tr = _pick(t, 256, 16)
    fcol = 4 * bw // LANES

    def body(q_ref, k_ref, v_ref, f_ref, gq_ref, gk_ref, fb_ref, qn_ref, kn_ref, vb_ref, cum_ref,
             qsq_ref, ksq_ref, first_ref, last_ref, carry_ref):
        i = pl.program_id(0)

        @pl.when(i == 0)
        def _():
            carry_ref[...] = jnp.zeros_like(carry_ref)

        lane = lax.broadcasted_iota(jnp.int32, (1, LANES), 1)
        for src, gain, dst, sq_ref in ((q_ref, gq_ref, qn_ref, qsq_ref),
                                       (k_ref, gk_ref, kn_ref, ksq_ref)):
            sq_row = jnp.zeros((1, LANES), F32)
            for h in range(heads):
                sl = slice(h * hd, (h + 1) * hd)
                v = src[:, sl]
                r = lax.rsqrt(jnp.mean(v * v, axis=1, keepdims=True) + NORM_EPS)
                normed = (v * r * gain[...]).astype(BF16)
                dst[:, sl] = normed
                nf = normed.astype(F32)
                sq = jnp.max(jnp.sum(nf * nf, axis=1, keepdims=True), axis=0, keepdims=True)
                sq_row = jnp.where(lane == h, sq, sq_row)
            sq_ref[0] = sq_row
        vb_ref[...] = v_ref[...].astype(BF16)
        fl = f_ref[...] + fb_ref[...]
        log_f = jnp.minimum(fl, 0.0) - jnp.log(1.0 + jnp.exp(-jnp.abs(fl)))
        rows = lax.broadcasted_iota(jnp.int32, (tr, tr), 0)
        cols = lax.broadcasted_iota(jnp.int32, (tr, tr), 1)
        lower = jnp.where(rows >= cols, 1.0, 0.0).astype(BF16)
        hi, mid, lo = _split3(log_f)
        cum_ref[...] = (_dot(lower, hi) + _dot(lower, mid) + _dot(lower, lo)) + carry_ref[...]
        carry_ref[...] = cum_ref[tr - 1:tr, :]
        first_ref[0] = cum_ref[0:1, :]
        last_ref[0] = cum_ref[tr - 1:tr, :]

    wide = lambda col: pl.BlockSpec((tr, bw), lambda i: (i, col))
    vec = pl.BlockSpec((1, hd), lambda i: (0, 0))
    stat = pl.BlockSpec((1, 1, LANES), lambda i: (i, 0, 0))
    return pl.pallas_call(
        body, name=name, grid=(t // tr,),
        in_specs=[wide(0), wide(1), wide(2), pl.BlockSpec((tr, LANES), lambda i: (i, fcol)),
                  vec, vec, pl.BlockSpec((1, LANES), lambda i: (0, 0))],
        out_specs=[wide(0), wide(0), wide(0), pl.BlockSpec((tr, LANES), lambda i: (i, 0))]
        + [stat] * 4,
        out_shape=[jax.ShapeDtypeStruct((t, bw), BF16)] * 3 + [jax.ShapeDtypeStruct((t, LANES), F32)]
        + [jax.ShapeDtypeStruct((t // tr, 1, LANES), F32)] * 4,
        scratch_shapes=[pltpu.VMEM((1, LANES), F32)],
        compiler_params=_params(("arbitrary",)),
    )(proj, proj, proj, proj, q_gain, k_gain, f_bias)


def _skip_plan(qsq, ksq, first, last, t, tq, hd, name):
    nt = qsq.shape[0]
    nq = t // tq
    r = nt // nq
    scale2 = hd ** -0.5 * LOG2E

    def body(qsq_ref, ksq_ref, first_ref, last_ref, out_ref):
        kmax = ksq_ref[0]
        for tile in range(1, nt):
            kmax = jnp.maximum(kmax, ksq_ref[tile])
        for i in range(nq):
            qmax = qsq_ref[i * r]
            for tile in range(i * r + 1, (i + 1) * r):
                qmax = jnp.maximum(qmax, qsq_ref[tile])
            coef = 2.0 * scale2 * jnp.sqrt(qmax * kmax)
            start = first_ref[i * r]
            count = jnp.zeros((1, LANES), jnp.int32)
            for j in range(i):
                bound = coef + (start - last_ref[(j + 1) * r - 1]) * LOG2E
                count = count + jnp.where(bound <= UNDERFLOW_LOG2, 1, 0)
            out_ref[i:i + 1, :] = count

    return pl.pallas_call(
        body, name=name, out_shape=jax.ShapeDtypeStruct((nq, LANES), jnp.int32),
    )(qsq, ksq, first, last)


def _flash_fwd(first_block, qn, kn, vb, ck, proj, heads, hd, tq, name):
    t = qn.shape[0]
    nq = t // tq
    th = tq // 2
    scale2 = hd ** -0.5 * LOG2E
    zcol = 3 * heads
    rc = _pick(tq, SOFTMAX_ROWS, 16)
    reps = th // LANES

    def body(first_ref, q_ref, k_ref, v_ref, ck_ref, z_ref, o_ref, y_ref, lse_ref,
             m_s, l_s, acc_s, s_a, s_b, p_a, p_b, al_a, al_b):
        i = pl.program_id(1)
        j0 = first_ref[pl.program_id(0), i]
        bufs = ((s_a, p_a, al_a), (s_b, p_b, al_b))
        m_s[...] = jnp.full_like(m_s, -jnp.inf)
        l_s[...] = jnp.zeros_like(l_s)
        acc_s[...] = jnp.zeros_like(acc_s)
        p_b[...] = jnp.zeros_like(p_b)
        al_b[...] = jnp.ones_like(al_b)

        def scores(j, half, first_row=0):
            off = pl.multiple_of(j * tq + half * th, th)
            bufs[half][0][first_row:, :] = _dot_nt(q_ref[first_row:, :], k_ref[pl.ds(off, th), :])

        def values(j, half, first_row=0):
            off = pl.multiple_of(j * tq + half * th, th)
            _, p_buf, al = bufs[half]
            acc_s[first_row:, :] = (
                jnp.tile(al[first_row:, :], (1, hd // LANES)) * acc_s[first_row:, :]
                + _dot(p_buf[first_row:, :], v_ref[pl.ds(off, th), :]))

        def softmax(j, half, masked):
            s_buf, p_buf, al = bufs[half]
            ck2 = ck_ref[0, j][:, half * th:(half + 1) * th] * LOG2E
            chunks = range((half * th) // rc if masked else 0, tq // rc)
            for c in chunks:
                r = slice(c * rc, (c + 1) * rc)
                s = s_buf[r, :] * scale2 - ck2
                if masked and c * rc < (half + 1) * th:
                    rows = lax.broadcasted_iota(jnp.int32, (rc, th), 0) + c * rc
                    cols = lax.broadcasted_iota(jnp.int32, (rc, th), 1) + half * th
                    s = jnp.where(rows >= cols, s, -jnp.inf)
                s_buf[r, :] = s
                m_prev = m_s[r, :]
                m_new = jnp.maximum(m_prev, jnp.max(s, axis=1, keepdims=True))
                al[r, :] = jnp.exp2(m_prev - m_new)
                m_s[r, :] = m_new
            for c in chunks:
                r = slice(c * rc, (c + 1) * rc)
                p = jnp.exp2(s_buf[r, :] - jnp.tile(m_s[r, :], (1, reps)))
                p_buf[r, :] = p.astype(BF16)
                lane_sum = p[:, 0:LANES]
                for b in range(1, reps):
                    lane_sum = lane_sum + p[:, b * LANES:(b + 1) * LANES]
                l_s[r, :] = al[r, :] * l_s[r, :] + lane_sum

        scores(j0, 0)

        def loop_body(j, carry):
            scores(j, 1)
            values(jnp.maximum(j - 1, 0), 1)
            softmax(j, 0, False)
            scores(j + 1, 0)
            values(j, 0)
            softmax(j, 1, False)
            return carry

        lax.fori_loop(j0, i, loop_body, 0)
        values(jnp.maximum(i - 1, 0), 1)
        scores(i, 1, first_row=th)
        softmax(i, 0, True)
        values(i, 0)
        softmax(i, 1, True)
        values(i, 1, first_row=th)
        l = jnp.sum(l_s[...], axis=1, keepdims=True)
        o = acc_s[...] / l
        z = z_ref[...]
        o_ref[...] = o
        y_ref[...] = (o * (z * _sigmoid(z))).astype(BF16)
        lse_ref[0] = m_s[:, 0:1] + jnp.log(l) * LOG2E

    blk = pl.BlockSpec((tq, hd), lambda h, i, first: (i, h))
    head = pl.BlockSpec((t, hd), lambda h, i, first: (0, h))
    col = pl.BlockSpec((1, tq, 1), lambda h, i, first: (h, i, 0))
    stat = pltpu.VMEM((tq, LANES), F32)
    return pl.pallas_call(
        body, name=name,
        grid_spec=pltpu.PrefetchScalarGridSpec(
            num_scalar_prefetch=1, grid=(heads, nq),
            in_specs=[blk, head, head,
                      pl.BlockSpec((1, nq, 1, tq), lambda h, i, first: (h, 0, 0, 0)),
                      pl.BlockSpec((tq, hd), lambda h, i, first: (i, zcol + h))],
            out_specs=[blk, blk, col],
            scratch_shapes=[stat, stat, pltpu.VMEM((tq, hd), F32),
                            pltpu.VMEM((tq, th), F32), pltpu.VMEM((tq, th), F32),
                            pltpu.VMEM((tq, th), BF16), pltpu.VMEM((tq, th), BF16), stat, stat]),
        out_shape=[jax.ShapeDtypeStruct((t, heads * hd), F32),
                   jax.ShapeDtypeStruct((t, heads * hd), BF16),
                   jax.ShapeDtypeStruct((heads, t, 1), F32)],
        compiler_params=_params(("arbitrary", "arbitrary")),
    )(first_block, qn, kn, vb, ck, proj)


def _attn_bwd_prep(dy, proj, o, heads, hd, name):
    t, bw = dy.shape
    tr = _pick(t, 256, 16)

    def body(dy_ref, z_ref, o_ref, do_ref, dz_ref, delta_ref):
        dyv, z, ov = dy_ref[...], z_ref[...], o_ref[...]
        sz = _sigmoid(z)
        do = dyv * (z * sz)
        do_ref[...] = do.astype(BF16)
        dz_ref[...] = (dyv * ov * (sz * (1.0 + z * (1.0 - sz)))).astype(BF16)
        prod = do * ov
        for h in range(heads):
            delta_ref[h] = jnp.sum(prod[:, h * hd:(h + 1) * hd], axis=1, keepdims=True)

    row = pl.BlockSpec((tr, bw), lambda i: (i, 0))
    return pl.pallas_call(
        body, name=name, grid=(t // tr,),
        in_specs=[row, pl.BlockSpec((tr, bw), lambda i: (i, 3)), row],
        out_specs=[row, row, pl.BlockSpec((heads, tr, 1), lambda i: (0, i, 0))],
        out_shape=[jax.ShapeDtypeStruct((t, bw), BF16), jax.ShapeDtypeStruct((t, bw), BF16),
                   jax.ShapeDtypeStruct((heads, t, 1), F32)],
        compiler_params=_params(("arbitrary",)),
    )(dy, proj, o)


def _flash_bwd(first_block, qn, kn, vb, do, ck, lse, delta, heads, hd, tq, name):
    t = qn.shape[0]
    nq = t // tq
    scale = hd ** -0.5
    scale2 = scale * LOG2E

    def body(first_ref, q_ref, k_ref, v_ref, do_ref, ck_ref, lse_ref, delta_ref,
             dq_ref, dk_ref, dv_ref, dcq_ref, dck_ref, dq_s, dcq_s):
        i = pl.program_id(1)
        j0 = first_ref[pl.program_id(0), i]

        @pl.when(i == 0)
        def _():
            dk_ref[...] = jnp.zeros_like(dk_ref)
            dv_ref[...] = jnp.zeros_like(dv_ref)
            dck_ref[...] = jnp.zeros_like(dck_ref)

        dq_s[...] = jnp.zeros_like(dq_s)
        dcq_s[...] = jnp.zeros_like(dcq_s)

        def step(j, masked, row0=0, col0=0, ncol=tq):
            off = pl.multiple_of(j * tq + col0, ncol)
            kblk = k_ref[pl.ds(off, ncol), :]
            q, dov = q_ref[row0:, :], do_ref[row0:, :]
            s = _dot_nt(q, kblk) * scale2 - ck_ref[0, j][:, col0:col0 + ncol] * LOG2E
            if masked:
                rows = lax.broadcasted_iota(jnp.int32, (tq - row0, ncol), 0) + row0
                cols = lax.broadcasted_iota(jnp.int32, (tq - row0, ncol), 1) + col0
                s = jnp.where(rows >= cols, s, -jnp.inf)
            p = jnp.exp2(s - lse_ref[0, row0:, :])
            dp = _dot_nt(dov, v_ref[pl.ds(off, ncol), :])
            ds = p * (dp - delta_ref[0, row0:, :])
            dsb = ds.astype(BF16)
            dv_ref[pl.ds(off, ncol), :] += _dot_tn(p.astype(BF16), dov)
            dk_ref[pl.ds(off, ncol), :] += _dot_tn(dsb, q) * scale
            dq_s[row0:, :] += _dot(dsb, kblk) * scale
            dcq_s[row0:, :] += jnp.sum(ds, axis=1, keepdims=True)
            dck_ref[0, j, :, col0:col0 + ncol] += jnp.sum(ds, axis=0, keepdims=True)

        def loop_body(j, carry):
            step(j, False)
            return carry

        lax.fori_loop(j0, i, loop_body, 0)
        step(i, True, 0, 0, tq // 2)
        step(i, True, tq // 2, tq // 2, tq // 2)
        dq_ref[...] = dq_s[...]
        dcq_ref[0] = dcq_s[...]

    blk = pl.BlockSpec((tq, hd), lambda h, i, first: (i, h))
    head = pl.BlockSpec((t, hd), lambda h, i, first: (0, h))
    col = pl.BlockSpec((1, tq, 1), lambda h, i, first: (h, i, 0))
    rowv = pl.BlockSpec((1, nq, 1, tq), lambda h, i, first: (h, 0, 0, 0))
    full = jax.ShapeDtypeStruct((t, heads * hd), F32)
    return pl.pallas_call(
        body, name=name,
        grid_spec=pltpu.PrefetchScalarGridSpec(
            num_scalar_prefetch=1, grid=(heads, nq),
            in_specs=[blk, head, head, blk, rowv, col, col],
            out_specs=[blk, head, head, col, rowv],
            scratch_shapes=[pltpu.VMEM((tq, hd), F32), pltpu.VMEM((tq, 1), F32)]),
        out_shape=[full, full, full, jax.ShapeDtypeStruct((heads, t, 1), F32),
                   jax.ShapeDtypeStruct((heads, nq, 1, tq), F32)],
        compiler_params=_params(("arbitrary", "arbitrary")),
    )(first_block, qn, kn, vb, do, ck, lse, delta)


def _attn_bwd_post(dqn, dkn, dv, dz, proj, dcq, dck, q_gain, k_gain, f_bias, heads, hd, name):
    t, bw = dqn.shape
    tr = _pick(t, 128, 16)
    nb = t // tr
    fcol = 4 * bw // LANES
    width = 4 * bw + LANES

    def body(dq_ref, dk_ref, dv_ref, dz_ref, q_ref, k_ref, f_ref, dcq_ref, dck_ref, gq_ref, gk_ref,
             fb_ref, d_ref, dgq_ref, dgk_ref, dfb_ref, carry_ref, rc_ref):
        i = pl.program_id(0)

        @pl.when(i == 0)
        def _():
            carry_ref[...] = jnp.zeros_like(carry_ref)
            dgq_ref[...] = jnp.zeros_like(dgq_ref)
            dgk_ref[...] = jnp.zeros_like(dgk_ref)
            dfb_ref[...] = jnp.zeros_like(dfb_ref)

        for idx, (g_ref, raw_ref, gain_ref, dgain_ref) in enumerate(
                ((dq_ref, q_ref, gq_ref, dgq_ref), (dk_ref, k_ref, gk_ref, dgk_ref))):
            gain = gain_ref[...]
            dgain = jnp.zeros((1, hd), F32)
            for h in range(heads):
                sl = slice(h * hd, (h + 1) * hd)
                v, dn = raw_ref[:, sl], g_ref[:, sl]
                r = lax.rsqrt(jnp.mean(v * v, axis=1, keepdims=True) + NORM_EPS)
                vh = v * r
                dgain = dgain + jnp.sum(dn * vh, axis=0, keepdims=True)
                dvh = dn * gain
                draw = r * (dvh - vh * jnp.mean(dvh * vh, axis=1, keepdims=True))
                d_ref[:, idx * bw + h * hd:idx * bw + (h + 1) * hd] = draw.astype(BF16)
            dgain_ref[...] += dgain
        d_ref[:, 2 * bw:3 * bw] = dv_ref[...].astype(BF16)
        d_ref[:, 3 * bw:4 * bw] = dz_ref[...]
        rows = lax.broadcasted_iota(jnp.int32, (tr, tr), 0)
        cols = lax.broadcasted_iota(jnp.int32, (tr, tr), 1)
        upper = jnp.where(cols >= rows, 1.0, 0.0).astype(BF16)
        hi, mid, lo = _split3(dcq_ref[...] - dck_ref[...])
        rc_ref[...] = (_dot(upper, hi) + _dot(upper, mid) + _dot(upper, lo)) + carry_ref[...]
        carry_ref[...] = rc_ref[0:1, :]
        df = rc_ref[...] * (1.0 / (1.0 + jnp.exp(f_ref[...] + fb_ref[...])))
        d_ref[:, 4 * bw:] = df.astype(BF16)
        dfb_ref[...] += jnp.sum(df, axis=0, keepdims=True)

    wide = lambda col: pl.BlockSpec((tr, bw), lambda i: (nb - 1 - i, col))
    lane = lambda col: pl.BlockSpec((tr, LANES), lambda i: (nb - 1 - i, col))
    vec = pl.BlockSpec((1, hd), lambda i: (0, 0))
    vecl = pl.BlockSpec((1, LANES), lambda i: (0, 0))
    return pl.pallas_call(
        body, name=name, grid=(nb,),
        in_specs=[wide(0), wide(0), wide(0), wide(0), wide(0), wide(1), lane(fcol), lane(0), lane(0),
                  vec, vec, vecl],
        out_specs=[pl.BlockSpec((tr, width), lambda i: (nb - 1 - i, 0)), vec, vec, vecl],
        out_shape=[jax.ShapeDtypeStruct((t, width), BF16), jax.ShapeDtypeStruct((1, hd), F32),
                   jax.ShapeDtypeStruct((1, hd), F32), jax.ShapeDtypeStruct((1, LANES), F32)],
        scratch_shapes=[pltpu.VMEM((1, LANES), F32), pltpu.VMEM((tr, LANES), F32)],
        compiler_params=_params(("arbitrary",)),
    )(dqn, dkn, dv, dz, proj, proj, proj, dcq, dck, q_gain, k_gain, f_bias)


def _adamw(w, m, v, parts, name, exchange=None):
    nl, r, c = w.shape
    itemsize = parts[0].dtype.itemsize
    unit = 32 // itemsize
    row_bytes = c * (7 * 4 + N_DEV * itemsize * nl)
    tr = _pick(r, max(unit, 12 * 1024 * 1024 // row_bytes), unit)
    nr = r // tr
    c1 = 1.0 / (1.0 - ADAM_B1 ** ADAM_STEP)
    c2 = 1.0 / (1.0 - ADAM_B2 ** ADAM_STEP)

    def body(*refs):
        w_ref, m_ref, v_ref = refs[:3]
        p_refs = refs[3:3 + nl]
        g_ref, d_ref, nm_ref, nv_ref = refs[3 + nl:]
        layer = pl.program_id(0)

        def partial(j):
            p = p_refs[0][j].astype(F32)
            for q in range(1, nl):
                p = jnp.where(layer == q, p_refs[q][j].astype(F32), p)
            return p

        g = partial(0)
        for j in range(1, N_DEV):
            g = g + partial(j)
        nm = ADAM_B1 * m_ref[0] + (1.0 - ADAM_B1) * g
        nv = ADAM_B2 * v_ref[0] + (1.0 - ADAM_B2) * (g * g)
        g_ref[0] = g
        nm_ref[0] = nm
        nv_ref[0] = nv
        d_ref[0] = -ADAM_LR * ((nm * c1) / (jnp.sqrt(nv * c2) + ADAM_EPS) + ADAM_WD * w_ref[0])

    def part_spec(q):
        rest = 0 if q > 0 else nr - 1
        return pl.BlockSpec((N_DEV, tr, c), lambda l, i: (0, jnp.where(l == q, i, rest), 0))

    row = pl.BlockSpec((1, tr, c), lambda l, i: (l, i, 0))
    in_specs, out_specs = [row, row, row] + [part_spec(q) for q in range(nl)], [row] * 4
    out_shape = [jax.ShapeDtypeStruct((nl, r, c), F32)] * 4
    operands, scratch = [w, m, v, *parts], []
    if exchange is not None:
        any_spec = pl.BlockSpec(memory_space=pl.ANY)
        body = _carry_exchange(
            body, exchange, 3 + nl, 4, 0,
            lambda: (pl.program_id(0) == 0) & (pl.program_id(1) == 0),
            lambda: (pl.program_id(0) == nl - 1) & (pl.program_id(1) == nr - 1))
        in_specs, out_specs = in_specs + [any_spec] * exchange.n, out_specs + [any_spec] * exchange.n
        out_shape, operands = out_shape + exchange.out_shapes(), operands + exchange.arrays
        scratch = exchange.scratch()
    return pl.pallas_call(
        body, name=name, grid=(nl, nr), in_specs=in_specs, out_specs=out_specs,
        out_shape=out_shape, scratch_shapes=scratch,
        compiler_params=_params(("arbitrary", "arbitrary")),
    )(*operands)


def _flat_rows(pieces):
    rows = []
    for p in pieces:
        f = p.reshape(-1)
        f = jnp.pad(f, (0, (-f.shape[0]) % LANES))
        rows.append(f.reshape(-1, LANES))
    out = jnp.concatenate(rows, axis=0)
    return jnp.pad(out, ((0, (-out.shape[0]) % 8), (0, 0)))


def _unflat_rows(flat, shapes):
    outs, r0 = [], 0
    lead = flat.shape[:-2]
    for s in shapes:
        size = 1
        for d in s:
            size *= d
        nr = -(-size // LANES)
        piece = flat[..., r0:r0 + nr, :].reshape(lead + (nr * LANES,))[..., :size]
        outs.append(piece.reshape(lead + tuple(s)))
        r0 += nr
    return outs


def kernel(x, a_norm_g, a_w_in, a_v_norm_g, a_w_s, a_b_s, a_w_out, b_norm_g, b_w_in, b_f_bias, b_q_norm_g, b_k_norm_g, b_w_out, loss_target, m_a_norm_g, m_a_w_in, m_a_v_norm_g, m_a_w_s, m_a_b_s, m_a_w_out, m_b_norm_g, m_b_w_in, m_b_f_bias, m_b_q_norm_g, m_b_k_norm_g, m_b_w_out, v_a_norm_g, v_a_w_in, v_a_v_norm_g, v_a_w_s, v_a_b_s, v_a_w_out, v_b_norm_g, v_b_w_in, v_b_f_bias, v_b_q_norm_g, v_b_k_norm_g, v_b_w_out):
    t, d = x.shape[1], x.shape[2]
    n_a, n_b = a_w_in.shape[0], b_w_in.shape[0]
    depth = n_a + n_b
    aw = a_w_out.shape[1] * N_DEV
    groups, chunk = a_w_s.shape[1], a_w_s.shape[2]
    heads, hd = b_f_bias.shape[1], b_q_norm_g.shape[1]
    bw = heads * hd
    b_cols = b_w_in.shape[2]
    tq = _pick(t, 512, LANES)
    nq = t // tq
    me = _dev_index((lax.axis_index("x"), lax.axis_index("y"), lax.axis_index("c")))

    pad_h = lambda v: jnp.pad(v, ((0, 0), (0, LANES - heads)))
    f_bias_p = pad_h(b_f_bias)
    b_s_col = a_b_s.reshape(n_a, groups, chunk, 1)
    w_s_t = a_w_s.transpose(0, 1, 3, 2)

    def weight_gather(i, extra=()):
        j = i // 2
        if i % 2 == 0:
            return _Gather([a_w_in[j].astype(BF16), a_w_out[j].astype(BF16)] + [a for a, _ in extra],
                           [1, 0] + [ax for _, ax in extra])
        return _Gather([b_w_in[j].astype(BF16), b_w_out[j].astype(BF16)] + [a for a, _ in extra],
                       [None, 0] + [ax for _, ax in extra])

    def whole_w_in(i, w_in):
        if i % 2 == 1:
            w_in = w_in.transpose(1, 0, 2).reshape(d, 4 * bw + heads)
            w_in = jnp.pad(w_in, ((0, 0), (0, LANES - heads)))
        return w_in

    (w_in0,) = _run_exchange(_Gather([a_w_in[0].astype(BF16)], [1]), "gather_weights")
    weights = {0: [w_in0, None]}
    late = ((a_w_out[0].astype(BF16), 0), (b_norm_g, 1))

    xs = [x[0]]
    saved = []
    for i in range(depth):
        j = i // 2
        xi = xs[-1]
        nxt = weight_gather(i + 1, extra=late if i == 0 else ()) if i + 1 < depth else None
        if i % 2 == 0:
            h = _rms_fwd(xi, a_norm_g[j:j + 1], f"a{j}_norm")
            res = _matmul(h, weights[i][0], "nn", F32, f"a{j}_in", tn=2048, tk=d, exchange=nxt)
        else:
            h = _rms_fwd(xi, b_norm_full[j:j + 1], f"b{j}_norm")
            res = _matmul(h, weights[i][0], "nn", F32, f"b{j}_in", tn=1664, tk=d, exchange=nxt)
        if nxt is None:
            pre = res
        else:
            pre = res[0]
            weights[i + 1] = [whole_w_in(i + 1, res[1]), res[2]]
            if i == 0:
                weights[0][1], b_norm_full = res[3], res[4]
        w_in, w_out = weights[i]
        if i % 2 == 0:
            y = _mix_fwd(pre, a_v_norm_g[j:j + 1], a_w_s[j], b_s_col[j], f"a{j}_mix")
            xs.append(_matmul(y, w_out, "nn", F32, f"a{j}_out", residual=xi))
            saved.append((h, pre, y))
        else:
            qn, kn, vb, cum, qsq, ksq, cum_first, cum_last = _attn_prep(
                pre, b_q_norm_g[j:j + 1], b_k_norm_g[j:j + 1], f_bias_p[j:j + 1], heads, hd,
                f"b{j}_prep")
            first = _skip_plan(qsq, ksq, cum_first, cum_last, t, tq, hd, f"b{j}_plan")
            first = first[:, :heads].T
            ck = cum[:, :heads].T.reshape(heads, nq, 1, tq)
            o, y, lse = _flash_fwd(first, qn, kn, vb, ck, pre, heads, hd, tq, f"b{j}_attn")
            xs.append(_matmul(y, w_out, "nn", F32, f"b{j}_out", residual=xi))
            saved.append((h, pre, y, qn, kn, vb, ck, o, lse, first))

    sq, g, gb = _loss_grad(xs[-1], loss_target[0], "loss")
    loss = 0.5 * lax.psum(sq[0, 0], AXES) / d

    d_a_norm, d_a_vnorm, d_a_ws, d_a_bs = [None] * n_a, [None] * n_a, [None] * n_a, [None] * n_a
    d_b_norm, d_b_fb, d_b_gq, d_b_gk = [None] * n_b, [None] * n_b, [None] * n_b, [None] * n_b
    recv_in, recv_out = {}, {}
    pending = None

    def in_blocks(i, dw_in):
        if i % 2 == 0:
            return _AllToAll([dw_in], [1])
        rows = dw_in.shape[0]
        return _AllToAll(
            [dw_in[:, :4 * bw + heads].reshape(rows, N_DEV, b_cols).transpose(1, 0, 2)], [None])

    for i in reversed(range(depth)):
        j = i // 2
        xi = xs[i]
        w_in, w_out = weights[i]
        if i % 2 == 0:
            h, uvz, y = saved[i]
            dy = _matmul(gb, w_out, "nt", F32, f"a{j}_dy", tn=2048, tk=d)
            dw_out = _matmul(y, gb, "tn", BF16, f"a{j}_dwout", tn=2048)
            dpre, d_a_ws[j], dbs, d_a_vnorm[j] = _mix_bwd(
                uvz, dy, a_v_norm_g[j:j + 1], a_w_s[j], w_s_t[j], b_s_col[j], f"a{j}_mixbwd")
            d_a_bs[j] = dbs.reshape(groups, chunk)
            name, gain = f"a{j}", a_norm_g[j:j + 1]
        else:
            h, proj, y, qn, kn, vb, ck, o, lse, first = saved[i]
            dy = _matmul(gb, w_out, "nt", F32, f"b{j}_dy", tn=2048, tk=d)
            dw_out = _matmul(y, gb, "tn", BF16, f"b{j}_dwout", tn=2048)
            do, dz, delta = _attn_bwd_prep(dy, proj, o, heads, hd, f"b{j}_bwdprep")
            dqn, dkn, dv, dcq, dck = _flash_bwd(first, qn, kn, vb, do, ck, lse, delta, heads, hd,
                                                tq, f"b{j}_attnbwd")
            per_token = lambda v: pad_h(v.reshape(heads, t).T)
            dpre, d_b_gq[j], d_b_gk[j], dfb = _attn_bwd_post(
                dqn, dkn, dv, dz, proj, per_token(dcq), per_token(dck), b_q_norm_g[j:j + 1],
                b_k_norm_g[j:j + 1], f_bias_p[j:j + 1], heads, hd, f"b{j}_bwdpost")
            d_b_fb[j] = dfb[:, :heads]
            name, gain = f"b{j}", b_norm_full[j:j + 1]
        if pending is None:
            dh = _matmul(dpre, w_in, "nt", F32, name + "_dh", tn=2048)
        else:
            dh, got = _matmul(dpre, w_in, "nt", F32, name + "_dh", tn=2048, exchange=pending[1])
            parts_in = pending[2] + [got]
            recv_in[pending[0]] = parts_in[0] if len(parts_in) == 1 else jnp.concatenate(
                parts_in, axis=1)
        n_parts = (4 if i == 0 else 2) if i % 2 == 0 else 1
        rows = d // n_parts
        riding = _AllToAll([dw_out], [0])
        landed = []
        for part in range(n_parts):
            h_part = h if n_parts == 1 else h[:, part * rows:(part + 1) * rows]
            dw_part, got = _matmul(h_part, dpre, "tn", BF16, f"{name}_dwin{part}",
                                   tn=2048 if i % 2 == 0 else 1664, exchange=riding)
            if part == 0:
                recv_out[i] = got
            else:
                landed.append(got)
            riding = in_blocks(i, dw_part)
        if i > 0:
            pending = (i, riding, landed)
            g, gb, dgain = _rms_bwd(xi, dh, g, gain, name + "_normbwd")
        else:
            g, gb, dgain, got = _rms_bwd(xi, dh, g, gain, name + "_normbwd", exchange=riding)
            recv_in[i] = jnp.concatenate(landed + [got], axis=1)
        if i % 2 == 0:
            d_a_norm[j] = dgain
        else:
            d_b_norm[j] = dgain
    grad_x = g[None]

    small = [jnp.concatenate(d_a_norm, 0), jnp.concatenate(d_a_vnorm, 0), jnp.stack(d_a_ws, 0),
             jnp.stack(d_a_bs, 0), jnp.concatenate(d_b_fb, 0), jnp.concatenate(d_b_gq, 0),
             jnp.concatenate(d_b_gk, 0)]
    small_w = [a_norm_g, a_v_norm_g, a_w_s, a_b_s, b_f_bias, b_q_norm_g, b_k_norm_g]
    small_m = [m_a_norm_g, m_a_v_norm_g, m_a_w_s, m_a_b_s, m_b_f_bias, m_b_q_norm_g, m_b_k_norm_g]
    small_v = [v_a_norm_g, v_a_v_norm_g, v_a_w_s, v_a_b_s, v_b_f_bias, v_b_q_norm_g, v_b_k_norm_g]
    small_flat = _flat_rows(small)
    n_small = small_flat.shape[0]
    small_gather = _Gather(
        [jnp.concatenate([small_flat, _flat_rows([jnp.concatenate(d_b_norm, 0)])], axis=0)],
        [None])

    a_layers, b_layers = range(0, depth, 2), range(1, depth, 2)
    *u_a_in, gathered_small = _adamw(a_w_in, m_a_w_in, v_a_w_in, [recv_in[i] for i in a_layers],
                                     "adamw_a_w_in", exchange=small_gather)
    parts_small = gathered_small[:, :n_small]
    parts_b_norm = gathered_small[:, n_small:n_small + n_b * d // LANES].reshape(N_DEV, n_b, d)
    parts_b_norm = lax.dynamic_slice_in_dim(parts_b_norm, me * (d // N_DEV), d // N_DEV, axis=2)
    u_a_out = _adamw(a_w_out, m_a_w_out, v_a_w_out, [recv_out[i] for i in a_layers],
                     "adamw_a_w_out")
    u_b_in = _adamw(b_w_in, m_b_w_in, v_b_w_in, [recv_in[i] for i in b_layers], "adamw_b_w_in")
    u_b_out = _adamw(b_w_out, m_b_w_out, v_b_w_out, [recv_out[i] for i in b_layers],
                     "adamw_b_w_out")
    u_b_norm = [o_[0] for o_ in _adamw(b_norm_g[None], m_b_norm_g[None], v_b_norm_g[None],
                                       [parts_b_norm], "adamw_b_norm")]
    u_small = _adamw(_flat_rows(small_w)[None], _flat_rows(small_m)[None],
                     _flat_rows(small_v)[None], [parts_small], "adamw_small")
    shapes = [w.shape for w in small_w]
    u_small = [_unflat_rows(o_[0], shapes) for o_ in u_small]

    def per_kind(k):
        s = u_small[k]
        return [s[0], u_a_in[k], s[1], s[2], s[3], u_a_out[k], u_b_norm[k], u_b_in[k], s[4], s[5],
                s[6], u_b_out[k]]

    return (loss, grad_x, *per_kind(0), *per_kind(1), *per_kind(2), *per_kind(3))
```

```python
import jax
import jax.numpy as jnp
from jax import lax
from jax.experimental import pallas as pl
from jax.experimental.pallas import tpu as pltpu

F32 = jnp.float32
BF16 = jnp.bfloat16
MESH = pl.DeviceIdType.MESH
AXES = ("x", "y", "c")
N_DEV = 8
NORM_EPS = 1e-6
LANES = 128
VMEM_LIMIT = 56 * 1024 * 1024

ADAM_LR = 0.001
ADAM_B1 = 0.9
ADAM_B2 = 0.999
ADAM_EPS = 1e-08
ADAM_WD = 0.01
ADAM_STEP = 10

LOG2E = 1.4426950408889634
SOFTMAX_ROWS = 64
UNDERFLOW_LOG2 = -160.0

GELU_C0 = 0.7978845608028654
GELU_C1 = 0.044715

NT_DIMS = (((1,), (1,)), ((), ()))
TN_DIMS = (((0,), (0,)), ((), ()))


def _params(sem=None):
    return pltpu.CompilerParams(dimension_semantics=sem, vmem_limit_bytes=VMEM_LIMIT)


def _pick(n, target, unit):
    best = None
    for t in range(unit, min(n, target) + 1, unit):
        if n % t == 0:
            best = t
    return n if best is None else best


def _sigmoid(x):
    return 0.5 + 0.5 * jnp.tanh(0.5 * x)


def _gelu(x):
    return x * (0.5 + 0.5 * jnp.tanh(x * (GELU_C0 + (GELU_C0 * GELU_C1) * (x * x))))


def _gelu_and_grad(x):
    x2 = x * x
    a = 0.5 + 0.5 * jnp.tanh(x * (GELU_C0 + (GELU_C0 * GELU_C1) * x2))
    dg = a + (x * (a * (1.0 - a))) * (2.0 * GELU_C0 + (6.0 * GELU_C0 * GELU_C1) * x2)
    return x * a, dg


def _dot(a, b):
    return jnp.dot(a, b, preferred_element_type=F32)


def _dot_nt(a, b):
    return lax.dot_general(a, b, NT_DIMS, preferred_element_type=F32)


def _dot_tn(a, b):
    return lax.dot_general(a, b, TN_DIMS, preferred_element_type=F32)


def _split3(v):
    hi = v.astype(BF16)
    r1 = v - hi.astype(F32)
    mid = r1.astype(BF16)
    lo = (r1 - mid.astype(F32)).astype(BF16)
    return hi, mid, lo


def _dev_index(p):
    return 4 * p[0] + 2 * p[1] + p[2]


def _block(ref, idx, axis, size):
    if axis is None:
        return ref.at[idx]
    start = pl.multiple_of(idx * size, size)
    return ref.at[(slice(None),) * axis + (pl.ds(start, size),)]


class _Exchange:
    def __init__(self, arrays, axes):
        self.arrays, self.axes, self.n = list(arrays), list(axes), len(arrays)

    def scratch(self):
        return [pltpu.SemaphoreType.DMA((self.n, 7)), pltpu.SemaphoreType.DMA((self.n, 7)),
                pltpu.SemaphoreType.DMA((self.n,))]

    @staticmethod
    def _place():
        x, y, c = lax.axis_index("x"), lax.axis_index("y"), lax.axis_index("c")
        return x, y, c


class _Gather(_Exchange):
    def out_shapes(self):
        outs = []
        for a, ax in zip(self.arrays, self.axes):
            if ax is None:
                shape = (N_DEV,) + a.shape
            else:
                shape = a.shape[:ax] + (N_DEV * a.shape[ax],) + a.shape[ax + 1:]
            outs.append(jax.ShapeDtypeStruct(shape, a.dtype))
        return outs

    def _copy(self, ins, outs, sems, a, k, block, to, src=None):
        ax = self.axes[a]
        dst = _block(outs[a], _dev_index(block), ax, None if ax is None else self.arrays[a].shape[ax])
        return pltpu.make_async_remote_copy(
            src_ref=dst if src is None else src, dst_ref=dst,
            send_sem=sems[0].at[a, k], recv_sem=sems[1].at[a, k],
            device_id=to, device_id_type=MESH)

    def _mine(self, ins, outs, sems, a, me):
        ax = self.axes[a]
        dst = _block(outs[a], _dev_index(me), ax, None if ax is None else self.arrays[a].shape[ax])
        return pltpu.make_async_copy(ins[a], dst, sems[2].at[a])

    def _first(self, ins, outs, sems):
        x, y, c = self._place()
        me, sibling = (x, y, c), (x, y, 1 - c)
        chips = [(1 - x, y), (x, 1 - y), (1 - x, 1 - y)]
        first = []
        for a in range(self.n):
            first.append(self._copy(ins, outs, sems, a, 0, me, sibling, src=ins[a]))
            first += [self._copy(ins, outs, sems, a, 1 + j, me, (*chip, c), src=ins[a])
                      for j, chip in enumerate(chips)]
        return first

    def start(self, ins, outs, sems):
        me = self._place()
        for a in range(self.n):
            self._mine(ins, outs, sems, a, me).start()
        for cp in self._first(ins, outs, sems):
            cp.start()

    def finish(self, ins, outs, sems):
        x, y, c = self._place()
        me, sibling = (x, y, c), (x, y, 1 - c)
        chips = [(1 - x, y), (x, 1 - y), (1 - x, 1 - y)]
        passed = []
        for a in range(self.n):
            for j, chip in enumerate(chips):
                self._copy(ins, outs, sems, a, 1 + j, (*chip, c), me).wait_recv()
                cp = self._copy(ins, outs, sems, a, 4 + j, (*chip, c), sibling)
                cp.start()
                passed.append(cp)
        for a in range(self.n):
            self._copy(ins, outs, sems, a, 0, sibling, me).wait_recv()
            for j, chip in enumerate(chips):
                self._copy(ins, outs, sems, a, 4 + j, (*chip, 1 - c), me).wait_recv()
        for cp in self._first(ins, outs, sems) + passed:
            cp.wait_send()
        for a in range(self.n):
            self._mine(ins, outs, sems, a, me).wait()


class _AllToAll(_Exchange):
    def _blk_shape(self, a):
        arr, ax = self.arrays[a], self.axes[a]
        if ax is None:
            return arr.shape[1:]
        return arr.shape[:ax] + (arr.shape[ax] // N_DEV,) + arr.shape[ax + 1:]

    def out_shapes(self):
        return [jax.ShapeDtypeStruct((N_DEV,) + self._blk_shape(a), self.arrays[a].dtype)
                for a in range(self.n)]

    def _src(self, ins, a, idx):
        ax = self.axes[a]
        return _block(ins[a], idx, ax, None if ax is None else self.arrays[a].shape[ax] // N_DEV)

    def _peers(self):
        x, y, c = self._place()
        return [((1 - x) if r & 4 else x, (1 - y) if r & 2 else y, (1 - c) if r & 1 else c)
                for r in range(1, N_DEV)]

    def _sends(self, ins, outs, sems):
        me = _dev_index(self._place())
        return [pltpu.make_async_remote_copy(
            src_ref=self._src(ins, a, _dev_index(peer)), dst_ref=outs[a].at[me],
            send_sem=sems[0].at[a, k], recv_sem=sems[1].at[a, k],
            device_id=peer, device_id_type=MESH)
            for a in range(self.n) for k, peer in enumerate(self._peers())]

    def _mine(self, ins, outs, sems):
        me = _dev_index(self._place())
        return [pltpu.make_async_copy(self._src(ins, a, me), outs[a].at[me], sems[2].at[a])
                for a in range(self.n)]

    def start(self, ins, outs, sems):
        for cp in self._mine(ins, outs, sems) + self._sends(ins, outs, sems):
            cp.start()

    def finish(self, ins, outs, sems):
        for a in range(self.n):
            for k, peer in enumerate(self._peers()):
                landed = outs[a].at[_dev_index(peer)]
                pltpu.make_async_remote_copy(
                    src_ref=landed, dst_ref=landed, send_sem=sems[0].at[a, k],
                    recv_sem=sems[1].at[a, k], device_id=peer, device_id_type=MESH).wait_recv()
        for cp in self._sends(ins, outs, sems):
            cp.wait_send()
        for cp in self._mine(ins, outs, sems):
            cp.wait()


def _run_exchange(exchange, name):
    n = exchange.n

    def body(*refs):
        ins, outs, sems = refs[:n], refs[n:2 * n], refs[2 * n:]
        exchange.start(ins, outs, sems)
        exchange.finish(ins, outs, sems)

    any_spec = pl.BlockSpec(memory_space=pl.ANY)
    return pl.pallas_call(
        body, name=name, out_shape=exchange.out_shapes(),
        in_specs=[any_spec] * n, out_specs=[any_spec] * n, scratch_shapes=exchange.scratch(),
    )(*exchange.arrays)


def _matmul(a, b, mode, out_dtype, name, tm=1024, tn=1024, tk=2048, residual=None, exchange=None):
    if mode == "tn":
        kdim, m = a.shape
    else:
        m, kdim = a.shape
    n = b.shape[0] if mode == "nt" else b.shape[1]
    tm, tn, tk = _pick(m, tm, LANES), _pick(n, tn, LANES), _pick(kdim, tk, LANES)
    nk = kdim // tk
    if mode == "tn":
        a_spec = pl.BlockSpec((tk, tm), lambda i, j, k: (k, i))
    else:
        a_spec = pl.BlockSpec((tm, tk), lambda i, j, k: (i, k))
    if mode == "nt":
        b_spec = pl.BlockSpec((tn, tk), lambda i, j, k: (j, k))
    else:
        b_spec = pl.BlockSpec((tk, tn), lambda i, j, k: (k, j))
    o_spec = pl.BlockSpec((tm, tn), lambda i, j, k: (i, j))
    dot = {"nn": _dot, "nt": _dot_nt, "tn": _dot_tn}[mode]
    has_res = residual is not None
    n_in = 3 if has_res else 2
    n_ex = 0 if exchange is None else exchange.n
    ni, nj = m // tm, n // tn

    def body(*refs):
        a_ref, b_ref = refs[:2]
        r_ref = refs[2] if has_res else None
        ex_ins = refs[n_in:n_in + n_ex]
        o_ref = refs[n_in + n_ex]
        ex_outs = refs[n_in + n_ex + 1:n_in + 2 * n_ex + 1]
        scratch = refs[n_in + 2 * n_ex + 1:]
        i, j, k = pl.program_id(0), pl.program_id(1), pl.program_id(2)

        if exchange is not None:
            sems = scratch[-3:]

            @pl.when((i == 0) & (j == 0) & (k == 0))
            def _():
                exchange.start(ex_ins, ex_outs, sems)

        def finish(acc):
            if has_res:
                acc = acc + r_ref[...]
            o_ref[...] = acc.astype(out_dtype)

        if nk == 1:
            finish(dot(a_ref[...], b_ref[...]))
        else:
            acc_ref = scratch[0]

            @pl.when(k == 0)
            def _():
                acc_ref[...] = jnp.zeros_like(acc_ref)

            acc_ref[...] += dot(a_ref[...], b_ref[...])

            @pl.when(k == nk - 1)
            def _():
                finish(acc_ref[...])

        if exchange is not None:
            @pl.when((i == ni - 1) & (j == nj - 1) & (k == nk - 1))
            def _():
                exchange.finish(ex_ins, ex_outs, sems)

    any_spec = pl.BlockSpec(memory_space=pl.ANY)
    operands = (a, b, residual) if has_res else (a, b)
    out_shape = jax.ShapeDtypeStruct((m, n), out_dtype)
    scratch_shapes = [] if nk == 1 else [pltpu.VMEM((tm, tn), F32)]
    if exchange is None:
        return pl.pallas_call(
            body, name=name, grid=(ni, nj, nk),
            in_specs=[a_spec, b_spec] + ([o_spec] if has_res else []),
            out_specs=o_spec, out_shape=out_shape, scratch_shapes=scratch_shapes,
            compiler_params=_params(("parallel", "parallel", "arbitrary")),
        )(*operands)
    return pl.pallas_call(
        body, name=name, grid=(ni, nj, nk),
        in_specs=[a_spec, b_spec] + ([o_spec] if has_res else []) + [any_spec] * n_ex,
        out_specs=[o_spec] + [any_spec] * n_ex,
        out_shape=[out_shape] + exchange.out_shapes(),
        scratch_shapes=scratch_shapes + exchange.scratch(),
        compiler_params=_params(("arbitrary", "arbitrary", "arbitrary")),
    )(*operands, *exchange.arrays)


def _rms_fwd(x, gain, name):
    t, d = x.shape
    tr = _pick(t, 512, 16)

    def body(x_ref, g_ref, h_ref):
        xv = x_ref[...]
        r = lax.rsqrt(jnp.mean(xv * xv, axis=1, keepdims=True) + NORM_EPS)
        h_ref[...] = (xv * r * g_ref[...]).astype(BF16)

    row = pl.BlockSpec((tr, d), lambda i: (i, 0))
    return pl.pallas_call(
        body, name=name, grid=(t // tr,),
        in_specs=[row, pl.BlockSpec((1, d), lambda i: (0, 0))],
        out_specs=row, out_shape=jax.ShapeDtypeStruct((t, d), BF16),
        compiler_params=_params(("arbitrary",)),
    )(x, gain)


def _carry_exchange(body, exchange, n_in, n_out, n_scratch, is_first, is_last):
    n = exchange.n

    def carrying(*refs):
        ins, ex_ins = refs[:n_in], refs[n_in:n_in + n]
        outs = refs[n_in + n:n_in + n + n_out]
        ex_outs = refs[n_in + n + n_out:n_in + 2 * n + n_out]
        scratch = refs[n_in + 2 * n + n_out:n_in + 2 * n + n_out + n_scratch]
        sems = refs[n_in + 2 * n + n_out + n_scratch:]

        @pl.when(is_first())
        def _():
            exchange.start(ex_ins, ex_outs, sems)

        body(*ins, *outs, *scratch)

        @pl.when(is_last())
        def _():
            exchange.finish(ex_ins, ex_outs, sems)

    return carrying


def _rms_bwd(x, dh, g_in, gain, name, exchange=None):
    t, d = x.shape
    tr = _pick(t, 512, 16)
    nt = t // tr

    def body(x_ref, dh_ref, gin_ref, g_ref, dx_ref, dxb_ref, dg_ref):
        i = pl.program_id(0)
        xv, dhv = x_ref[...], dh_ref[...]
        r = lax.rsqrt(jnp.mean(xv * xv, axis=1, keepdims=True) + NORM_EPS)
        xh = xv * r
        dxh = dhv * g_ref[...]
        dx = gin_ref[...] + r * (dxh - xh * jnp.mean(dxh * xh, axis=1, keepdims=True))
        dx_ref[...] = dx
        dxb_ref[...] = dx.astype(BF16)

        @pl.when(i == 0)
        def _():
            dg_ref[...] = jnp.zeros_like(dg_ref)

        dg_ref[...] += jnp.sum(dhv * xh, axis=0, keepdims=True)

    row = pl.BlockSpec((tr, d), lambda i: (i, 0))
    vec = pl.BlockSpec((1, d), lambda i: (0, 0))
    in_specs, out_specs = [row, row, row, vec], [row, row, vec]
    out_shape = [jax.ShapeDtypeStruct((t, d), F32), jax.ShapeDtypeStruct((t, d), BF16),
                 jax.ShapeDtypeStruct((1, d), F32)]
    operands, scratch = [x, dh, g_in, gain], []
    if exchange is not None:
        any_spec = pl.BlockSpec(memory_space=pl.ANY)
        body = _carry_exchange(body, exchange, 4, 3, 0, lambda: pl.program_id(0) == 0,
                               lambda: pl.program_id(0) == nt - 1)
        in_specs, out_specs = in_specs + [any_spec] * exchange.n, out_specs + [any_spec] * exchange.n
        out_shape, operands = out_shape + exchange.out_shapes(), operands + exchange.arrays
        scratch = exchange.scratch()
    return pl.pallas_call(
        body, name=name, grid=(nt,), in_specs=in_specs, out_specs=out_specs, out_shape=out_shape,
        scratch_shapes=scratch, compiler_params=_params(("arbitrary",)),
    )(*operands)


def _loss_grad(y, target, name):
    t, d = y.shape
    tr = _pick(t, 512, 16)

    def body(y_ref, t_ref, s_ref, g_ref, gb_ref):
        i = pl.program_id(0)
        e = y_ref[...] - t_ref[...]
        g = e * (1.0 / d)
        g_ref[...] = g
        gb_ref[...] = g.astype(BF16)

        @pl.when(i == 0)
        def _():
            s_ref[...] = jnp.zeros_like(s_ref)

        s_ref[...] += jnp.sum(jnp.sum(e * e, axis=1, keepdims=True), axis=0, keepdims=True)

    row = pl.BlockSpec((tr, d), lambda i: (i, 0))
    return pl.pallas_call(
        body, name=name, grid=(t // tr,),
        in_specs=[row, row],
        out_specs=[pl.BlockSpec((1, 1), lambda i: (0, 0)), row, row],
        out_shape=[jax.ShapeDtypeStruct((1, 1), F32), jax.ShapeDtypeStruct((t, d), F32),
                   jax.ShapeDtypeStruct((t, d), BF16)],
        compiler_params=_params(("arbitrary",)),
    )(y, target)


def _causal_weights(ws_ref, g, chunk, transposed):
    rows = lax.broadcasted_iota(jnp.int32, (chunk, chunk), 0)
    cols = lax.broadcasted_iota(jnp.int32, (chunk, chunk), 1)
    keep = (cols >= rows) if transposed else (rows >= cols)
    return jnp.where(keep, ws_ref[g], 0.0).astype(BF16)


def _mix_fwd(uvz, v_gain, w_s, b_s, name):
    t, w3 = uvz.shape
    w = w3 // 3
    groups, chunk = w_s.shape[0], w_s.shape[1]
    gd = w // groups

    def body(uvz_ref, gam_ref, ws_ref, bs_ref, y_ref):
        gv = _gelu(uvz_ref[:, w:2 * w])
        r = lax.rsqrt(jnp.mean(gv * gv, axis=1, keepdims=True) + NORM_EPS)
        vn = (gv * r * gam_ref[...]).astype(BF16)
        for g in range(groups):
            sl = slice(g * gd, (g + 1) * gd)
            mixed = _dot(_causal_weights(ws_ref, g, chunk, False), vn[:, sl]) + bs_ref[g]
            u = uvz_ref[:, g * gd:(g + 1) * gd]
            z = uvz_ref[:, 2 * w + g * gd:2 * w + (g + 1) * gd]
            y_ref[:, sl] = (_gelu(u) * mixed * (z * _sigmoid(z))).astype(BF16)

    return pl.pallas_call(
        body, name=name, grid=(t // chunk,),
        in_specs=[pl.BlockSpec((chunk, w3), lambda i: (i, 0)),
                  pl.BlockSpec((1, w), lambda i: (0, 0)),
                  pl.BlockSpec((groups, chunk, chunk), lambda i: (0, 0, 0)),
                  pl.BlockSpec((groups, chunk, 1), lambda i: (0, 0, 0))],
        out_specs=pl.BlockSpec((chunk, w), lambda i: (i, 0)),
        out_shape=jax.ShapeDtypeStruct((t, w), BF16),
        compiler_params=_params(("arbitrary",)),
    )(uvz, v_gain, w_s, b_s)


def _mix_bwd(uvz, dy, v_gain, w_s, w_s_t, b_s, name):
    t, w3 = uvz.shape
    w = w3 // 3
    groups, chunk = w_s.shape[0], w_s.shape[1]
    gd = w // groups

    def body(uvz_ref, dy_ref, gam_ref, ws_ref, wst_ref, bs_ref, d_ref, dws_ref, dbs_ref, dgam_ref,
             dvn_ref):
        i = pl.program_id(0)

        @pl.when(i == 0)
        def _():
            dws_ref[...] = jnp.zeros_like(dws_ref)
            dbs_ref[...] = jnp.zeros_like(dbs_ref)
            dgam_ref[...] = jnp.zeros_like(dgam_ref)

        gv, dgv = _gelu_and_grad(uvz_ref[:, w:2 * w])
        r = lax.rsqrt(jnp.mean(gv * gv, axis=1, keepdims=True) + NORM_EPS)
        vh = gv * r
        gam = gam_ref[...]
        vn = (vh * gam).astype(BF16)
        rows = lax.broadcasted_iota(jnp.int32, (chunk, chunk), 0)
        cols = lax.broadcasted_iota(jnp.int32, (chunk, chunk), 1)
        for g in range(groups):
            sl = slice(g * gd, (g + 1) * gd)
            mixed = _dot(_causal_weights(ws_ref, g, chunk, False), vn[:, sl]) + bs_ref[g]
            gu, dgu = _gelu_and_grad(uvz_ref[:, g * gd:(g + 1) * gd])
            z = uvz_ref[:, 2 * w + g * gd:2 * w + (g + 1) * gd]
            sz = _sigmoid(z)
            silu = z * sz
            dyv = dy_ref[:, sl]
            dmixed = dyv * gu * silu
            d_ref[:, sl] = (dyv * mixed * silu * dgu).astype(BF16)
            d_ref[:, 2 * w + g * gd:2 * w + (g + 1) * gd] = (
                dyv * gu * mixed * (sz * (1.0 + z * (1.0 - sz)))).astype(BF16)
            dmb = dmixed.astype(BF16)
            dws_ref[g] += jnp.where(rows >= cols, _dot_nt(dmb, vn[:, sl]), 0.0)
            dbs_ref[g] += jnp.sum(dmixed, axis=1, keepdims=True)
            dvn_ref[:, sl] = _dot(_causal_weights(wst_ref, g, chunk, True), dmb)
        dvn = dvn_ref[...]
        dgam_ref[...] += jnp.sum(dvn * vh, axis=0, keepdims=True)
        dvh = dvn * gam
        dgvv = r * (dvh - vh * jnp.mean(dvh * vh, axis=1, keepdims=True))
        d_ref[:, w:2 * w] = (dgvv * dgv).astype(BF16)

    return pl.pallas_call(
        body, name=name, grid=(t // chunk,),
        in_specs=[pl.BlockSpec((chunk, w3), lambda i: (i, 0)),
                  pl.BlockSpec((chunk, w), lambda i: (i, 0)),
                  pl.BlockSpec((1, w), lambda i: (0, 0)),
                  pl.BlockSpec((groups, chunk, chunk), lambda i: (0, 0, 0)),
                  pl.BlockSpec((groups, chunk, chunk), lambda i: (0, 0, 0)),
                  pl.BlockSpec((groups, chunk, 1), lambda i: (0, 0, 0))],
        out_specs=[pl.BlockSpec((chunk, w3), lambda i: (i, 0)),
                   pl.BlockSpec((groups, chunk, chunk), lambda i: (0, 0, 0)),
                   pl.BlockSpec((groups, chunk, 1), lambda i: (0, 0, 0)),
                   pl.BlockSpec((1, w), lambda i: (0, 0))],
        out_shape=[jax.ShapeDtypeStruct((t, w3), BF16),
                   jax.ShapeDtypeStruct((groups, chunk, chunk), F32),
                   jax.ShapeDtypeStruct((groups, chunk, 1), F32),
                   jax.ShapeDtypeStruct((1, w), F32)],
        scratch_shapes=[pltpu.VMEM((chunk, w), F32)],
        compiler_params=_params(("arbitrary",)),
    )(uvz, dy, v_gain, w_s, w_s_t, b_s)


def _attn_prep(proj, q_gain, k_gain, f_bias, heads, hd, name):
    t = proj.shape[0]
    bw = heads * hd
    tr = _pick(t, 256, 16)
    fcol = 4 * bw // LANES

    def body(q_ref, k_ref, v_ref, f_ref, gq_ref, gk_ref, fb_ref, qn_ref, kn_ref, vb_ref, cum_ref,
             qsq_ref, ksq_ref, first_ref, last_ref, carry_ref):
        i = pl.program_id(0)

        @pl.when(i == 0)
        def _():
            carry_ref[...] = jnp.zeros_like(carry_ref)

        lane = lax.broadcasted_iota(jnp.int32, (1, LANES), 1)
        for src, gain, dst, sq_ref in ((q_ref, gq_ref, qn_ref, qsq_ref),
                                       (k_ref, gk_ref, kn_ref, ksq_ref)):
            sq_row = jnp.zeros((1, LANES), F32)
            for h in range(heads):
                sl = slice(h * hd, (h + 1) * hd)
                v = src[:, sl]
                r = lax.rsqrt(jnp.mean(v * v, axis=1, keepdims=True) + NORM_EPS)
                normed = (v * r * gain[...]).astype(BF16)
                dst[:, sl] = normed
                nf = normed.astype(F32)
                sq = jnp.max(jnp.sum(nf * nf, axis=1, keepdims=True), axis=0, keepdims=True)
                sq_row = jnp.where(lane == h, sq, sq_row)
            sq_ref[0] = sq_row
        vb_ref[...] = v_ref[...].astype(BF16)
        fl = f_ref[...] + fb_ref[...]
        log_f = jnp.minimum(fl, 0.0) - jnp.log(1.0 + jnp.exp(-jnp.abs(fl)))
        rows = lax.broadcasted_iota(jnp.int32, (tr, tr), 0)
        cols = lax.broadcasted_iota(jnp.int32, (tr, tr), 1)
        lower = jnp.where(rows >= cols, 1.0, 0.0).astype(BF16)
        hi, mid, lo = _split3(log_f)
        cum_ref[...] = (_dot(lower, hi) + _dot(lower, mid) + _dot(lower, lo)) + carry_ref[...]
        carry_ref[...] = cum_ref[tr - 1:tr, :]
        first_ref[0] = cum_ref[0:1, :]
        last_ref[0] = cum_ref[tr - 1:tr, :]

    wide = lambda col: pl.BlockSpec((tr, bw), lambda i: (i, col))
    vec = pl.BlockSpec((1, hd), lambda i: (0, 0))
    stat = pl.BlockSpec((1, 1, LANES), lambda i: (i, 0, 0))
    return pl.pallas_call(
        body, name=name, grid=(t // tr,),
        in_specs=[wide(0), wide(1), wide(2), pl.BlockSpec((tr, LANES), lambda i: (i, fcol)),
                  vec, vec, pl.BlockSpec((1, LANES), lambda i: (0, 0))],
        out_specs=[wide(0), wide(0), wide(0), pl.BlockSpec((tr, LANES), lambda i: (i, 0))]
        + [stat] * 4,
        out_shape=[jax.ShapeDtypeStruct((t, bw), BF16)] * 3 + [jax.ShapeDtypeStruct((t, LANES), F32)]
        + [jax.ShapeDtypeStruct((t // tr, 1, LANES), F32)] * 4,
        scratch_shapes=[pltpu.VMEM((1, LANES), F32)],
        compiler_params=_params(("arbitrary",)),
    )(proj, proj, proj, proj, q_gain, k_gain, f_bias)


def _skip_plan(qsq, ksq, first, last, t, tq, hd, name):
    nt = qsq.shape[0]
    nq = t // tq
    r = nt // nq
    scale2 = hd ** -0.5 * LOG2E

    def body(qsq_ref, ksq_ref, first_ref, last_ref, out_ref):
        kmax = ksq_ref[0]
        for tile in range(1, nt):
            kmax = jnp.maximum(kmax, ksq_ref[tile])
        for i in range(nq):
            qmax = qsq_ref[i * r]
            for tile in range(i * r + 1, (i + 1) * r):
                qmax = jnp.maximum(qmax, qsq_ref[tile])
            coef = 2.0 * scale2 * jnp.sqrt(qmax * kmax)
            start = first_ref[i * r]
            count = jnp.zeros((1, LANES), jnp.int32)
            for j in range(i):
                bound = coef + (start - last_ref[(j + 1) * r - 1]) * LOG2E
                count = count + jnp.where(bound <= UNDERFLOW_LOG2, 1, 0)
            out_ref[i:i + 1, :] = count

    return pl.pallas_call(
        body, name=name, out_shape=jax.ShapeDtypeStruct((nq, LANES), jnp.int32),
    )(qsq, ksq, first, last)


def _flash_fwd(first_block, qn, kn, vb, ck, proj, heads, hd, tq, name):
    t = qn.shape[0]
    nq = t // tq
    th = tq // 2
    scale2 = hd ** -0.5 * LOG2E
    zcol = 3 * heads
    rc = _pick(tq, SOFTMAX_ROWS, 16)
    reps = th // LANES

    def body(first_ref, q_ref, k_ref, v_ref, ck_ref, z_ref, o_ref, y_ref, lse_ref,
             m_s, l_s, acc_s, s_a, s_b, p_a, p_b, al_a, al_b):
        i = pl.program_id(1)
        j0 = first_ref[pl.program_id(0), i]
        bufs = ((s_a, p_a, al_a), (s_b, p_b, al_b))
        m_s[...] = jnp.full_like(m_s, -jnp.inf)
        l_s[...] = jnp.zeros_like(l_s)
        acc_s[...] = jnp.zeros_like(acc_s)
        p_b[...] = jnp.zeros_like(p_b)
        al_b[...] = jnp.ones_like(al_b)

        def scores(j, half, first_row=0):
            off = pl.multiple_of(j * tq + half * th, th)
            bufs[half][0][first_row:, :] = _dot_nt(q_ref[first_row:, :], k_ref[pl.ds(off, th), :])

        def values(j, half, first_row=0):
            off = pl.multiple_of(j * tq + half * th, th)
            _, p_buf, al = bufs[half]
            acc_s[first_row:, :] = (
                jnp.tile(al[first_row:, :], (1, hd // LANES)) * acc_s[first_row:, :]
                + _dot(p_buf[first_row:, :], v_ref[pl.ds(off, th), :]))

        def softmax(j, half, masked):
            s_buf, p_buf, al = bufs[half]
            ck2 = ck_ref[0, j][:, half * th:(half + 1) * th] * LOG2E
            chunks = range((half * th) // rc if masked else 0, tq // rc)
            for c in chunks:
                r = slice(c * rc, (c + 1) * rc)
                s = s_buf[r, :] * scale2 - ck2
                if masked and c * rc < (half + 1) * th:
                    rows = lax.broadcasted_iota(jnp.int32, (rc, th), 0) + c * rc
                    cols = lax.broadcasted_iota(jnp.int32, (rc, th), 1) + half * th
                    s = jnp.where(rows >= cols, s, -jnp.inf)
                s_buf[r, :] = s
                m_prev = m_s[r, :]
                m_new = jnp.maximum(m_prev, jnp.max(s, axis=1, keepdims=True))
                al[r, :] = jnp.exp2(m_prev - m_new)
                m_s[r, :] = m_new
            for c in chunks:
                r = slice(c * rc, (c + 1) * rc)
                p = jnp.exp2(s_buf[r, :] - jnp.tile(m_s[r, :], (1, reps)))
                p_buf[r, :] = p.astype(BF16)
                lane_sum = p[:, 0:LANES]
                for b in range(1, reps):
                    lane_sum = lane_sum + p[:, b * LANES:(b + 1) * LANES]
                l_s[r, :] = al[r, :] * l_s[r, :] + lane_sum

        scores(j0, 0)

        def loop_body(j, carry):
            scores(j, 1)
            values(jnp.maximum(j - 1, 0), 1)
            softmax(j, 0, False)
            scores(j + 1, 0)
            values(j, 0)
            softmax(j, 1, False)
            return carry

        lax.fori_loop(j0, i, loop_body, 0)
        values(jnp.maximum(i - 1, 0), 1)
        scores(i, 1, first_row=th)
        softmax(i, 0, True)
        values(i, 0)
        softmax(i, 1, True)
        values(i, 1, first_row=th)
        l = jnp.sum(l_s[...], axis=1, keepdims=True)
        o = acc_s[...] / l
        z = z_ref[...]
        o_ref[...] = o
        y_ref[...] = (o * (z * _sigmoid(z))).astype(BF16)
        lse_ref[0] = m_s[:, 0:1] + jnp.log(l) * LOG2E

    blk = pl.BlockSpec((tq, hd), lambda h, i, first: (i, h))
    head = pl.BlockSpec((t, hd), lambda h, i, first: (0, h))
    col = pl.BlockSpec((1, tq, 1), lambda h, i, first: (h, i, 0))
    stat = pltpu.VMEM((tq, LANES), F32)
    return pl.pallas_call(
        body, name=name,
        grid_spec=pltpu.PrefetchScalarGridSpec(
            num_scalar_prefetch=1, grid=(heads, nq),
            in_specs=[blk, head, head,
                      pl.BlockSpec((1, nq, 1, tq), lambda h, i, first: (h, 0, 0, 0)),
                      pl.BlockSpec((tq, hd), lambda h, i, first: (i, zcol + h))],
            out_specs=[blk, blk, col],
            scratch_shapes=[stat, stat, pltpu.VMEM((tq, hd), F32),
                            pltpu.VMEM((tq, th), F32), pltpu.VMEM((tq, th), F32),
                            pltpu.VMEM((tq, th), BF16), pltpu.VMEM((tq, th), BF16), stat, stat]),
        out_shape=[jax.ShapeDtypeStruct((t, heads * hd), F32),
                   jax.ShapeDtypeStruct((t, heads * hd), BF16),
                   jax.ShapeDtypeStruct((heads, t, 1), F32)],
        compiler_params=_params(("arbitrary", "arbitrary")),
    )(first_block, qn, kn, vb, ck, proj)


def _attn_bwd_prep(dy, proj, o, heads, hd, name):
    t, bw = dy.shape
    tr = _pick(t, 256, 16)

    def body(dy_ref, z_ref, o_ref, do_ref, dz_ref, delta_ref):
        dyv, z, ov = dy_ref[...], z_ref[...], o_ref[...]
        sz = _sigmoid(z)
        do = dyv * (z * sz)
        do_ref[...] = do.astype(BF16)
        dz_ref[...] = (dyv * ov * (sz * (1.0 + z * (1.0 - sz)))).astype(BF16)
        prod = do * ov
        for h in range(heads):
            delta_ref[h] = jnp.sum(prod[:, h * hd:(h + 1) * hd], axis=1, keepdims=True)

    row = pl.BlockSpec((tr, bw), lambda i: (i, 0))
    return pl.pallas_call(
        body, name=name, grid=(t // tr,),
        in_specs=[row, pl.BlockSpec((tr, bw), lambda i: (i, 3)), row],
        out_specs=[row, row, pl.BlockSpec((heads, tr, 1), lambda i: (0, i, 0))],
        out_shape=[jax.ShapeDtypeStruct((t, bw), BF16), jax.ShapeDtypeStruct((t, bw), BF16),
                   jax.ShapeDtypeStruct((heads, t, 1), F32)],
        compiler_params=_params(("arbitrary",)),
    )(dy, proj, o)


def _flash_bwd(first_block, qn, kn, vb, do, ck, lse, delta, heads, hd, tq, name):
    t = qn.shape[0]
    nq = t // tq
    scale = hd ** -0.5
    scale2 = scale * LOG2E

    def body(first_ref, q_ref, k_ref, v_ref, do_ref, ck_ref, lse_ref, delta_ref,
             dq_ref, dk_ref, dv_ref, dcq_ref, dck_ref, dq_s, dcq_s):
        i = pl.program_id(1)
        j0 = first_ref[pl.program_id(0), i]

        @pl.when(i == 0)
        def _():
            dk_ref[...] = jnp.zeros_like(dk_ref)
            dv_ref[...] = jnp.zeros_like(dv_ref)
            dck_ref[...] = jnp.zeros_like(dck_ref)

        dq_s[...] = jnp.zeros_like(dq_s)
        dcq_s[...] = jnp.zeros_like(dcq_s)

        def step(j, masked, row0=0, col0=0, ncol=tq):
            off = pl.multiple_of(j * tq + col0, ncol)
            kblk = k_ref[pl.ds(off, ncol), :]
            q, dov = q_ref[row0:, :], do_ref[row0:, :]
            s = _dot_nt(q, kblk) * scale2 - ck_ref[0, j][:, col0:col0 + ncol] * LOG2E
            if masked:
                rows = lax.broadcasted_iota(jnp.int32, (tq - row0, ncol), 0) + row0
                cols = lax.broadcasted_iota(jnp.int32, (tq - row0, ncol), 1) + col0
                s = jnp.where(rows >= cols, s, -jnp.inf)
            p = jnp.exp2(s - lse_ref[0, row0:, :])
            dp = _dot_nt(dov, v_ref[pl.ds(off, ncol), :])
            ds = p * (dp - delta_ref[0, row0:, :])
            dsb = ds.astype(BF16)
            dv_ref[pl.ds(off, ncol), :] += _dot_tn(p.astype(BF16), dov)
            dk_ref[pl.ds(off, ncol), :] += _dot_tn(dsb, q) * scale
            dq_s[row0:, :] += _dot(dsb, kblk) * scale
            dcq_s[row0:, :] += jnp.sum(ds, axis=1, keepdims=True)
            dck_ref[0, j, :, col0:col0 + ncol] += jnp.sum(ds, axis=0, keepdims=True)

        def loop_body(j, carry):
            step(j, False)
            return carry

        lax.fori_loop(j0, i, loop_body, 0)
        step(i, True, 0, 0, tq // 2)
        step(i, True, tq // 2, tq // 2, tq // 2)
        dq_ref[...] = dq_s[...]
        dcq_ref[0] = dcq_s[...]

    blk = pl.BlockSpec((tq, hd), lambda h, i, first: (i, h))
    head = pl.BlockSpec((t, hd), lambda h, i, first: (0, h))
    col = pl.BlockSpec((1, tq, 1), lambda h, i, first: (h, i, 0))
    rowv = pl.BlockSpec((1, nq, 1, tq), lambda h, i, first: (h, 0, 0, 0))
    full = jax.ShapeDtypeStruct((t, heads * hd), F32)
    return pl.pallas_call(
        body, name=name,
        grid_spec=pltpu.PrefetchScalarGridSpec(
            num_scalar_prefetch=1, grid=(heads, nq),
            in_specs=[blk, head, head, blk, rowv, col, col],
            out_specs=[blk, head, head, col, rowv],
            scratch_shapes=[pltpu.VMEM((tq, hd), F32), pltpu.VMEM((tq, 1), F32)]),
        out_shape=[full, full, full, jax.ShapeDtypeStruct((heads, t, 1), F32),
                   jax.ShapeDtypeStruct((heads, nq, 1, tq), F32)],
        compiler_params=_params(("arbitrary", "arbitrary")),
    )(first_block, qn, kn, vb, do, ck, lse, delta)


def _attn_bwd_post(dqn, dkn, dv, dz, proj, dcq, dck, q_gain, k_gain, f_bias, heads, hd, name):
    t, bw = dqn.shape
    tr = _pick(t, 128, 16)
    nb = t // tr
    fcol = 4 * bw // LANES
    width = 4 * bw + LANES

    def body(dq_ref, dk_ref, dv_ref, dz_ref, q_ref, k_ref, f_ref, dcq_ref, dck_ref, gq_ref, gk_ref,
             fb_ref, d_ref, dgq_ref, dgk_ref, dfb_ref, carry_ref, rc_ref):
        i = pl.program_id(0)

        @pl.when(i == 0)
        def _():
            carry_ref[...] = jnp.zeros_like(carry_ref)
            dgq_ref[...] = jnp.zeros_like(dgq_ref)
            dgk_ref[...] = jnp.zeros_like(dgk_ref)
            dfb_ref[...] = jnp.zeros_like(dfb_ref)

        for idx, (g_ref, raw_ref, gain_ref, dgain_ref) in enumerate(
                ((dq_ref, q_ref, gq_ref, dgq_ref), (dk_ref, k_ref, gk_ref, dgk_ref))):
            gain = gain_ref[...]
            dgain = jnp.zeros((1, hd), F32)
            for h in range(heads):
                sl = slice(h * hd, (h + 1) * hd)
                v, dn = raw_ref[:, sl], g_ref[:, sl]
                r = lax.rsqrt(jnp.mean(v * v, axis=1, keepdims=True) + NORM_EPS)
                vh = v * r
                dgain = dgain + jnp.sum(dn * vh, axis=0, keepdims=True)
                dvh = dn * gain
                draw = r * (dvh - vh * jnp.mean(dvh * vh, axis=1, keepdims=True))
                d_ref[:, idx * bw + h * hd:idx * bw + (h + 1) * hd] = draw.astype(BF16)
            dgain_ref[...] += dgain
        d_ref[:, 2 * bw:3 * bw] = dv_ref[...].astype(BF16)
        d_ref[:, 3 * bw:4 * bw] = dz_ref[...]
        rows = lax.broadcasted_iota(jnp.int32, (tr, tr), 0)
        cols = lax.broadcasted_iota(jnp.int32, (tr, tr), 1)
        upper = jnp.where(cols >= rows, 1.0, 0.0).astype(BF16)
        hi, mid, lo = _split3(dcq_ref[...] - dck_ref[...])
        rc_ref[...] = (_dot(upper, hi) + _dot(upper, mid) + _dot(upper, lo)) + carry_ref[...]
        carry_ref[...] = rc_ref[0:1, :]
        df = rc_ref[...] * (1.0 / (1.0 + jnp.exp(f_ref[...] + fb_ref[...])))
        d_ref[:, 4 * bw:] = df.astype(BF16)
        dfb_ref[...] += jnp.sum(df, axis=0, keepdims=True)

    wide = lambda col: pl.BlockSpec((tr, bw), lambda i: (nb - 1 - i, col))
    lane = lambda col: pl.BlockSpec((tr, LANES), lambda i: (nb - 1 - i, col))
    vec = pl.BlockSpec((1, hd), lambda i: (0, 0))
    vecl = pl.BlockSpec((1, LANES), lambda i: (0, 0))
    return pl.pallas_call(
        body, name=name, grid=(nb,),
        in_specs=[wide(0), wide(0), wide(0), wide(0), wide(0), wide(1), lane(fcol), lane(0), lane(0),
                  vec, vec, vecl],
        out_specs=[pl.BlockSpec((tr, width), lambda i: (nb - 1 - i, 0)), vec, vec, vecl],
        out_shape=[jax.ShapeDtypeStruct((t, width), BF16), jax.ShapeDtypeStruct((1, hd), F32),
                   jax.ShapeDtypeStruct((1, hd), F32), jax.ShapeDtypeStruct((1, LANES), F32)],
        scratch_shapes=[pltpu.VMEM((1, LANES), F32), pltpu.VMEM((tr, LANES), F32)],
        compiler_params=_params(("arbitrary",)),
    )(dqn, dkn, dv, dz, proj, proj, proj, dcq, dck, q_gain, k_gain, f_bias)


def _adamw(w, m, v, parts, name, exchange=None):
    nl, r, c = w.shape
    itemsize = parts[0].dtype.itemsize
    unit = 32 // itemsize
    row_bytes = c * (7 * 4 + N_DEV * itemsize * nl)
    tr = _pick(r, max(unit, 12 * 1024 * 1024 // row_bytes), unit)
    nr = r // tr
    c1 = 1.0 / (1.0 - ADAM_B1 ** ADAM_STEP)
    c2 = 1.0 / (1.0 - ADAM_B2 ** ADAM_STEP)

    def body(*refs):
        w_ref, m_ref, v_ref = refs[:3]
        p_refs = refs[3:3 + nl]
        g_ref, d_ref, nm_ref, nv_ref = refs[3 + nl:]
        layer = pl.program_id(0)

        def partial(j):
            p = p_refs[0][j].astype(F32)
            for q in range(1, nl):
                p = jnp.where(layer == q, p_refs[q][j].astype(F32), p)
            return p

        g = partial(0)
        for j in range(1, N_DEV):
            g = g + partial(j)
        nm = ADAM_B1 * m_ref[0] + (1.0 - ADAM_B1) * g
        nv = ADAM_B2 * v_ref[0] + (1.0 - ADAM_B2) * (g * g)
        g_ref[0] = g
        nm_ref[0] = nm
        nv_ref[0] = nv
        d_ref[0] = -ADAM_LR * ((nm * c1) / (jnp.sqrt(nv * c2) + ADAM_EPS) + ADAM_WD * w_ref[0])

    def part_spec(q):
        rest = 0 if q > 0 else nr - 1
        return pl.BlockSpec((N_DEV, tr, c), lambda l, i: (0, jnp.where(l == q, i, rest), 0))

    row = pl.BlockSpec((1, tr, c), lambda l, i: (l, i, 0))
    in_specs, out_specs = [row, row, row] + [part_spec(q) for q in range(nl)], [row] * 4
    out_shape = [jax.ShapeDtypeStruct((nl, r, c), F32)] * 4
    operands, scratch = [w, m, v, *parts], []
    if exchange is not None:
        any_spec = pl.BlockSpec(memory_space=pl.ANY)
        body = _carry_exchange(
            body, exchange, 3 + nl, 4, 0,
            lambda: (pl.program_id(0) == 0) & (pl.program_id(1) == 0),
            lambda: (pl.program_id(0) == nl - 1) & (pl.program_id(1) == nr - 1))
        in_specs, out_specs = in_specs + [any_spec] * exchange.n, out_specs + [any_spec] * exchange.n
        out_shape, operands = out_shape + exchange.out_shapes(), operands + exchange.arrays
        scratch = exchange.scratch()
    return pl.pallas_call(
        body, name=name, grid=(nl, nr), in_specs=in_specs, out_specs=out_specs,
        out_shape=out_shape, scratch_shapes=scratch,
        compiler_params=_params(("arbitrary", "arbitrary")),
    )(*operands)


def _flat_rows(pieces):
    rows = []
    for p in pieces:
        f = p.reshape(-1)
        f = jnp.pad(f, (0, (-f.shape[0]) % LANES))
        rows.append(f.reshape(-1, LANES))
    out = jnp.concatenate(rows, axis=0)
    return jnp.pad(out, ((0, (-out.shape[0]) % 8), (0, 0)))


def _unflat_rows(flat, shapes):
    outs, r0 = [], 0
    lead = flat.shape[:-2]
    for s in shapes:
        size = 1
        for d in s:
            size *= d
        nr = -(-size // LANES)
        piece = flat[..., r0:r0 + nr, :].reshape(lead + (nr * LANES,))[..., :size]
        outs.append(piece.reshape(lead + tuple(s)))
        r0 += nr
    return outs


def kernel(x, a_norm_g, a_w_in, a_v_norm_g, a_w_s, a_b_s, a_w_out, b_norm_g, b_w_in, b_f_bias, b_q_norm_g, b_k_norm_g, b_w_out, loss_target, m_a_norm_g, m_a_w_in, m_a_v_norm_g, m_a_w_s, m_a_b_s, m_a_w_out, m_b_norm_g, m_b_w_in, m_b_f_bias, m_b_q_norm_g, m_b_k_norm_g, m_b_w_out, v_a_norm_g, v_a_w_in, v_a_v_norm_g, v_a_w_s, v_a_b_s, v_a_w_out, v_b_norm_g, v_b_w_in, v_b_f_bias, v_b_q_norm_g, v_b_k_norm_g, v_b_w_out):
    t, d = x.shape[1], x.shape[2]
    n_a, n_b = a_w_in.shape[0], b_w_in.shape[0]
    depth = n_a + n_b
    aw = a_w_out.shape[1] * N_DEV
    groups, chunk = a_w_s.shape[1], a_w_s.shape[2]
    heads, hd = b_f_bias.shape[1], b_q_norm_g.shape[1]
    bw = heads * hd
    b_cols = b_w_in.shape[2]
    tq = _pick(t, 512, LANES)
    nq = t // tq
    me = _dev_index((lax.axis_index("x"), lax.axis_index("y"), lax.axis_index("c")))

    pad_h = lambda v: jnp.pad(v, ((0, 0), (0, LANES - heads)))
    f_bias_p = pad_h(b_f_bias)
    b_s_col = a_b_s.reshape(n_a, groups, chunk, 1)
    w_s_t = a_w_s.transpose(0, 1, 3, 2)

    def weight_gather(i, extra=()):
        j = i // 2
        if i % 2 == 0:
            return _Gather([a_w_in[j].astype(BF16), a_w_out[j].astype(BF16)] + [a for a, _ in extra],
                           [1, 0] + [ax for _, ax in extra])
        return _Gather([b_w_in[j].astype(BF16), b_w_out[j].astype(BF16)] + [a for a, _ in extra],
                       [None, 0] + [ax for _, ax in extra])

    def whole_w_in(i, w_in):
        if i % 2 == 1:
            w_in = w_in.transpose(1, 0, 2).reshape(d, 4 * bw + heads)
            w_in = jnp.pad(w_in, ((0, 0), (0, LANES - heads)))
        return w_in

    (w_in0,) = _run_exchange(_Gather([a_w_in[0].astype(BF16)], [1]), "gather_weights")
    weights = {0: [w_in0, None]}
    late = ((a_w_out[0].astype(BF16), 0), (b_norm_g, 1))

    xs = [x[0]]
    saved = []
    for i in range(depth):
        j = i // 2
        xi = xs[-1]
        nxt = weight_gather(i + 1, extra=late if i == 0 else ()) if i + 1 < depth else None
        if i % 2 == 0:
            h = _rms_fwd(xi, a_norm_g[j:j + 1], f"a{j}_norm")
            res = _matmul(h, weights[i][0], "nn", F32, f"a{j}_in", tn=2048, tk=d, exchange=nxt)
        else:
            h = _rms_fwd(xi, b_norm_full[j:j + 1], f"b{j}_norm")
            res = _matmul(h, weights[i][0], "nn", F32, f"b{j}_in", tn=1664, tk=d, exchange=nxt)
        if nxt is None:
            pre = res
        else:
            pre = res[0]
            weights[i + 1] = [whole_w_in(i + 1, res[1]), res[2]]
            if i == 0:
                weights[0][1], b_norm_full = res[3], res[4]
        w_in, w_out = weights[i]
        if i % 2 == 0:
            y = _mix_fwd(pre, a_v_norm_g[j:j + 1], a_w_s[j], b_s_col[j], f"a{j}_mix")
            xs.append(_matmul(y, w_out, "nn", F32, f"a{j}_out", residual=xi))
            saved.append((h, pre, y))
        else:
            qn, kn, vb, cum, qsq, ksq, cum_first, cum_last = _attn_prep(
                pre, b_q_norm_g[j:j + 1], b_k_norm_g[j:j + 1], f_bias_p[j:j + 1], heads, hd,
                f"b{j}_prep")
            first = _skip_plan(qsq, ksq, cum_first, cum_last, t, tq, hd, f"b{j}_plan")
            first = first[:, :heads].T
            ck = cum[:, :heads].T.reshape(heads, nq, 1, tq)
            o, y, lse = _flash_fwd(first, qn, kn, vb, ck, pre, heads, hd, tq, f"b{j}_attn")
            xs.append(_matmul(y, w_out, "nn", F32, f"b{j}_out", residual=xi))
            saved.append((h, pre, y, qn, kn, vb, ck, o, lse, first))

    sq, g, gb = _loss_grad(xs[-1], loss_target[0], "loss")
    loss = 0.5 * lax.psum(sq[0, 0], AXES) / d

    d_a_norm, d_a_vnorm, d_a_ws, d_a_bs = [None] * n_a, [None] * n_a, [None] * n_a, [None] * n_a
    d_b_norm, d_b_fb, d_b_gq, d_b_gk = [None] * n_b, [None] * n_b, [None] * n_b, [None] * n_b
    recv_in, recv_out = {}, {}
    pending = None

    def in_blocks(i, dw_in):
        if i % 2 == 0:
            return _AllToAll([dw_in], [1])
        rows = dw_in.shape[0]
        return _AllToAll(
            [dw_in[:, :4 * bw + heads].reshape(rows, N_DEV, b_cols).transpose(1, 0, 2)], [None])

    for i in reversed(range(depth)):
        j = i // 2
        xi = xs[i]
        w_in, w_out = weights[i]
        if i % 2 == 0:
            h, uvz, y = saved[i]
            dy = _matmul(gb, w_out, "nt", F32, f"a{j}_dy", tn=2048, tk=d)
            dw_out = _matmul(y, gb, "tn", BF16, f"a{j}_dwout", tn=2048)
            dpre, d_a_ws[j], dbs, d_a_vnorm[j] = _mix_bwd(
                uvz, dy, a_v_norm_g[j:j + 1], a_w_s[j], w_s_t[j], b_s_col[j], f"a{j}_mixbwd")
            d_a_bs[j] = dbs.reshape(groups, chunk)
            name, gain = f"a{j}", a_norm_g[j:j + 1]
        else:
            h, proj, y, qn, kn, vb, ck, o, lse, first = saved[i]
            dy = _matmul(gb, w_out, "nt", F32, f"b{j}_dy", tn=2048, tk=d)
            dw_out = _matmul(y, gb, "tn", BF16, f"b{j}_dwout", tn=2048)
            do, dz, delta = _attn_bwd_prep(dy, proj, o, heads, hd, f"b{j}_bwdprep")
            dqn, dkn, dv, dcq, dck = _flash_bwd(first, qn, kn, vb, do, ck, lse, delta, heads, hd,
                                                tq, f"b{j}_attnbwd")
            per_token = lambda v: pad_h(v.reshape(heads, t).T)
            dpre, d_b_gq[j], d_b_gk[j], dfb = _attn_bwd_post(
                dqn, dkn, dv, dz, proj, per_token(dcq), per_token(dck), b_q_norm_g[j:j + 1],
                b_k_norm_g[j:j + 1], f_bias_p[j:j + 1], heads, hd, f"b{j}_bwdpost")
            d_b_fb[j] = dfb[:, :heads]
            name, gain = f"b{j}", b_norm_full[j:j + 1]
        if pending is None:
            dh = _matmul(dpre, w_in, "nt", F32, name + "_dh", tn=2048)
        else:
            dh, got = _matmul(dpre, w_in, "nt", F32, name + "_dh", tn=2048, exchange=pending[1])
            parts_in = pending[2] + [got]
            recv_in[pending[0]] = parts_in[0] if len(parts_in) == 1 else jnp.concatenate(
                parts_in, axis=1)
        n_parts = (4 if i == 0 else 2) if i % 2 == 0 else 1
        rows = d // n_parts
        riding = _AllToAll([dw_out], [0])
        landed = []
        for part in range(n_parts):
            h_part = h if n_parts == 1 else h[:, part * rows:(part + 1) * rows]
            dw_part, got = _matmul(h_part, dpre, "tn", BF16, f"{name}_dwin{part}",
                                   tn=2048 if i % 2 == 0 else 1664, exchange=riding)
            if part == 0:
                recv_out[i] = got
            else:
                landed.append(got)
            riding = in_blocks(i, dw_part)
        if i > 0:
            pending = (i, riding, landed)
            g, gb, dgain = _rms_bwd(xi, dh, g, gain, name + "_normbwd")
        else:
            g, gb, dgain, got = _rms_bwd(xi, dh, g, gain, name + "_normbwd", exchange=riding)
            recv_in[i] = jnp.concatenate(landed + [got], axis=1)
        if i % 2 == 0:
            d_a_norm[j] = dgain
        else:
            d_b_norm[j] = dgain
    grad_x = g[None]

    small = [jnp.concatenate(d_a_norm, 0), jnp.concatenate(d_a_vnorm, 0), jnp.stack(d_a_ws, 0),
             jnp.stack(d_a_bs, 0), jnp.concatenate(d_b_fb, 0), jnp.concatenate(d_b_gq, 0),
             jnp.concatenate(d_b_gk, 0)]
    small_w = [a_norm_g, a_v_norm_g, a_w_s, a_b_s, b_f_bias, b_q_norm_g, b_k_norm_g]
    small_m = [m_a_norm_g, m_a_v_norm_g, m_a_w_s, m_a_b_s, m_b_f_bias, m_b_q_norm_g, m_b_k_norm_g]
    small_v = [v_a_norm_g, v_a_v_norm_g, v_a_w_s, v_a_b_s, v_b_f_bias, v_b_q_norm_g, v_b_k_norm_g]
    small_flat = _flat_rows(small)
    n_small = small_flat.shape[0]
    small_gather = _Gather(
        [jnp.concatenate([small_flat, _flat_rows([jnp.concatenate(d_b_norm, 0)])], axis=0)],
        [None])

    a_layers, b_layers = range(0, depth, 2), range(1, depth, 2)
    *u_a_in, gathered_small = _adamw(a_w_in, m_a_w_in, v_a_w_in, [recv_in[i] for i in a_layers],
                                     "adamw_a_w_in", exchange=small_gather)
    parts_small = gathered_small[:, :n_small]
    parts_b_norm = gathered_small[:, n_small:n_small + n_b * d // LANES].reshape(N_DEV, n_b, d)
    parts_b_norm = lax.dynamic_slice_in_dim(parts_b_norm, me * (d // N_DEV), d // N_DEV, axis=2)
    u_a_out = _adamw(a_w_out, m_a_w_out, v_a_w_out, [recv_out[i] for i in a_layers],
                     "adamw_a_w_out")
    u_b_in = _adamw(b_w_in, m_b_w_in, v_b_w_in, [recv_in[i] for i in b_layers], "adamw_b_w_in")
    u_b_out = _adamw(b_w_out, m_b_w_out, v_b_w_out, [recv_out[i] for i in b_layers],
                     "adamw_b_w_out")
    u_b_norm = [o_[0] for o_ in _adamw(b_norm_g[None], m_b_norm_g[None], v_b_norm_g[None],
                                       [parts_b_norm], "adamw_b_norm")]
    u_small = _adamw(_flat_rows(small_w)[None], _flat_rows(small_m)[None],
                     _flat_rows(small_v)[None], [parts_small], "adamw_small")
    shapes = [w.shape for w in small_w]
    u_small = [_unflat_rows(o_[0], shapes) for o_ in u_small]

    def per_kind(k):
        s = u_small[k]
        return [s[0], u_a_in[k], s[1], s[2], s[3], u_a_out[k], u_b_norm[k], u_b_in[k], s[4], s[5],
                s[6], u_b_out[k]]

    return (loss, grad_x, *per_kind(0), *per_kind(1), *per_kind(2), *per_kind(3))
```

```python
import jax
import jax.numpy as jnp
from jax import lax
from jax.experimental import pallas as pl
from jax.experimental.pallas import tpu as pltpu

F32 = jnp.float32
BF16 = jnp.bfloat16
MESH = pl.DeviceIdType.MESH
AXES = ("x", "y", "c")
N_DEV = 8
NORM_EPS = 1e-6
LANES = 128
VMEM_LIMIT = 56 * 1024 * 1024

ADAM_LR = 0.001
ADAM_B1 = 0.9
ADAM_B2 = 0.999
ADAM_EPS = 1e-08
ADAM_WD = 0.01
ADAM_STEP = 10

LOG2E = 1.4426950408889634
SOFTMAX_ROWS = 64
UNDERFLOW_LOG2 = -160.0

GELU_C0 = 0.7978845608028654
GELU_C1 = 0.044715

NT_DIMS = (((1,), (1,)), ((), ()))
TN_DIMS = (((0,), (0,)), ((), ()))


def _params(sem=None):
    return pltpu.CompilerParams(dimension_semantics=sem, vmem_limit_bytes=VMEM_LIMIT)


def _pick(n, target, unit):
    best = None
    for t in range(unit, min(n, target) + 1, unit):
        if n % t == 0:
            best = t
    return n if best is None else best


def _sigmoid(x):
    return 0.5 + 0.5 * jnp.tanh(0.5 * x)


def _gelu(x):
    return x * (0.5 + 0.5 * jnp.tanh(x * (GELU_C0 + (GELU_C0 * GELU_C1) * (x * x))))


def _gelu_and_grad(x):
    x2 = x * x
    a = 0.5 + 0.5 * jnp.tanh(x * (GELU_C0 + (GELU_C0 * GELU_C1) * x2))
    dg = a + (x * (a * (1.0 - a))) * (2.0 * GELU_C0 + (6.0 * GELU_C0 * GELU_C1) * x2)
    return x * a, dg


def _dot(a, b):
    return jnp.dot(a, b, preferred_element_type=F32)


def _dot_nt(a, b):
    return lax.dot_general(a, b, NT_DIMS, preferred_element_type=F32)


def _dot_tn(a, b):
    return lax.dot_general(a, b, TN_DIMS, preferred_element_type=F32)


def _split3(v):
    hi = v.astype(BF16)
    r1 = v - hi.astype(F32)
    mid = r1.astype(BF16)
    lo = (r1 - mid.astype(F32)).astype(BF16)
    return hi, mid, lo


def _dev_index(p):
    return 4 * p[0] + 2 * p[1] + p[2]


def _block(ref, idx, axis, size):
    if axis is None:
        return ref.at[idx]
    start = pl.multiple_of(idx * size, size)
    return ref.at[(slice(None),) * axis + (pl.ds(start, size),)]


class _Exchange:
    def __init__(self, arrays, axes):
        self.arrays, self.axes, self.n = list(arrays), list(axes), len(arrays)

    def scratch(self):
        return [pltpu.SemaphoreType.DMA((self.n, 7)), pltpu.SemaphoreType.DMA((self.n, 7)),
                pltpu.SemaphoreType.DMA((self.n,))]

    @staticmethod
    def _place():
        x, y, c = lax.axis_index("x"), lax.axis_index("y"), lax.axis_index("c")
        return x, y, c


class _Gather(_Exchange):
    def out_shapes(self):
        outs = []
        for a, ax in zip(self.arrays, self.axes):
            if ax is None:
                shape = (N_DEV,) + a.shape
            else:
                shape = a.shape[:ax] + (N_DEV * a.shape[ax],) + a.shape[ax + 1:]
            outs.append(jax.ShapeDtypeStruct(shape, a.dtype))
        return outs

    def _copy(self, ins, outs, sems, a, k, block, to, src=None):
        ax = self.axes[a]
        dst = _block(outs[a], _dev_index(block), ax, None if ax is None else self.arrays[a].shape[ax])
        return pltpu.make_async_remote_copy(
            src_ref=dst if src is None else src, dst_ref=dst,
            send_sem=sems[0].at[a, k], recv_sem=sems[1].at[a, k],
            device_id=to, device_id_type=MESH)

    def _mine(self, ins, outs, sems, a, me):
        ax = self.axes[a]
        dst = _block(outs[a], _dev_index(me), ax, None if ax is None else self.arrays[a].shape[ax])
        return pltpu.make_async_copy(ins[a], dst, sems[2].at[a])

    def _first(self, ins, outs, sems):
        x, y, c = self._place()
        me, sibling = (x, y, c), (x, y, 1 - c)
        chips = [(1 - x, y), (x, 1 - y), (1 - x, 1 - y)]
        first = []
        for a in range(self.n):
            first.append(self._copy(ins, outs, sems, a, 0, me, sibling, src=ins[a]))
            first += [self._copy(ins, outs, sems, a, 1 + j, me, (*chip, c), src=ins[a])
                      for j, chip in enumerate(chips)]
        return first

    def start(self, ins, outs, sems):
        me = self._place()
        for a in range(self.n):
            self._mine(ins, outs, sems, a, me).start()
        for cp in self._first(ins, outs, sems):
            cp.start()

    def finish(self, ins, outs, sems):
        x, y, c = self._place()
        me, sibling = (x, y, c), (x, y, 1 - c)
        chips = [(1 - x, y), (x, 1 - y), (1 - x, 1 - y)]
        passed = []
        for a in range(self.n):
            for j, chip in enumerate(chips):
                self._copy(ins, outs, sems, a, 1 + j, (*chip, c), me).wait_recv()
                cp = self._copy(ins, outs, sems, a, 4 + j, (*chip, c), sibling)
                cp.start()
                passed.append(cp)
        for a in range(self.n):
            self._copy(ins, outs, sems, a, 0, sibling, me).wait_recv()
            for j, chip in enumerate(chips):
                self._copy(ins, outs, sems, a, 4 + j, (*chip, 1 - c), me).wait_recv()
        for cp in self._first(ins, outs, sems) + passed:
            cp.wait_send()
        for a in range(self.n):
            self._mine(ins, outs, sems, a, me).wait()


class _AllToAll(_Exchange):
    def _blk_shape(self, a):
        arr, ax = self.arrays[a], self.axes[a]
        if ax is None:
            return arr.shape[1:]
        return arr.shape[:ax] + (arr.shape[ax] // N_DEV,) + arr.shape[ax + 1:]

    def out_shapes(self):
        return [jax.ShapeDtypeStruct((N_DEV,) + self._blk_shape(a), self.arrays[a].dtype)
                for a in range(self.n)]

    def _src(self, ins, a, idx):
        ax = self.axes[a]
        return _block(ins[a], idx, ax, None if ax is None else self.arrays[a].shape[ax] // N_DEV)

    def _peers(self):
        x, y, c = self._place()
        return [((1 - x) if r & 4 else x, (1 - y) if r & 2 else y, (1 - c) if r & 1 else c)
                for r in range(1, N_DEV)]

    def _sends(self, ins, outs, sems):
        me = _dev_index(self._place())
        return [pltpu.make_async_remote_copy(
            src_ref=self._src(ins, a, _dev_index(peer)), dst_ref=outs[a].at[me],
            send_sem=sems[0].at[a, k], recv_sem=sems[1].at[a, k],
            device_id=peer, device_id_type=MESH)
            for a in range(self.n) for k, peer in enumerate(self._peers())]

    def _mine(self, ins, outs, sems):
        me = _dev_index(self._place())
        return [pltpu.make_async_copy(self._src(ins, a, me), outs[a].at[me], sems[2].at[a])
                for a in range(self.n)]

    def start(self, ins, outs, sems):
        for cp in self._mine(ins, outs, sems) + self._sends(ins, outs, sems):
            cp.start()

    def finish(self, ins, outs, sems):
        for a in range(self.n):
            for k, peer in enumerate(self._peers()):
                landed = outs[a].at[_dev_index(peer)]
                pltpu.make_async_remote_copy(
                    src_ref=landed, dst_ref=landed, send_sem=sems[0].at[a, k],
                    recv_sem=sems[1].at[a, k], device_id=peer, device_id_type=MESH).wait_recv()
        for cp in self._sends(ins, outs, sems):
            cp.wait_send()
        for cp in self._mine(ins, outs, sems):
            cp.wait()


def _run_exchange(exchange, name):
    n = exchange.n

    def body(*refs):
        ins, outs, sems = refs[:n], refs[n:2 * n], refs[2 * n:]
        exchange.start(ins, outs, sems)
        exchange.finish(ins, outs, sems)

    any_spec = pl.BlockSpec(memory_space=pl.ANY)
    return pl.pallas_call(
        body, name=name, out_shape=exchange.out_shapes(),
        in_specs=[any_spec] * n, out_specs=[any_spec] * n, scratch_shapes=exchange.scratch(),
    )(*exchange.arrays)


def _matmul(a, b, mode, out_dtype, name, tm=1024, tn=1024, tk=2048, residual=None, exchange=None):
    if mode == "tn":
        kdim, m = a.shape
    else:
        m, kdim = a.shape
    n = b.shape[0] if mode == "nt" else b.shape[1]
    tm, tn, tk = _pick(m, tm, LANES), _pick(n, tn, LANES), _pick(kdim, tk, LANES)
    nk = kdim // tk
    if mode == "tn":
        a_spec = pl.BlockSpec((tk, tm), lambda i, j, k: (k, i))
    else:
        a_spec = pl.BlockSpec((tm, tk), lambda i, j, k: (i, k))
    if mode == "nt":
        b_spec = pl.BlockSpec((tn, tk), lambda i, j, k: (j, k))
    else:
        b_spec = pl.BlockSpec((tk, tn), lambda i, j, k: (k, j))
    o_spec = pl.BlockSpec((tm, tn), lambda i, j, k: (i, j))
    dot = {"nn": _dot, "nt": _dot_nt, "tn": _dot_tn}[mode]
    has_res = residual is not None
    n_in = 3 if has_res else 2
    n_ex = 0 if exchange is None else exchange.n
    ni, nj = m // tm, n // tn

    def body(*refs):
        a_ref, b_ref = refs[:2]
        r_ref = refs[2] if has_res else None
        ex_ins = refs[n_in:n_in + n_ex]
        o_ref = refs[n_in + n_ex]
        ex_outs = refs[n_in + n_ex + 1:n_in + 2 * n_ex + 1]
        scratch = refs[n_in + 2 * n_ex + 1:]
        i, j, k = pl.program_id(0), pl.program_id(1), pl.program_id(2)

        if exchange is not None:
            sems = scratch[-3:]

            @pl.when((i == 0) & (j == 0) & (k == 0))
            def _():
                exchange.start(ex_ins, ex_outs, sems)

        def finish(acc):
            if has_res:
                acc = acc + r_ref[...]
            o_ref[...] = acc.astype(out_dtype)

        if nk == 1:
            finish(dot(a_ref[...], b_ref[...]))
        else:
            acc_ref = scratch[0]

            @pl.when(k == 0)
            def _():
                acc_ref[...] = jnp.zeros_like(acc_ref)

            acc_ref[...] += dot(a_ref[...], b_ref[...])

            @pl.when(k == nk - 1)
            def _():
                finish(acc_ref[...])

        if exchange is not None:
            @pl.when((i == ni - 1) & (j == nj - 1) & (k == nk - 1))
            def _():
                exchange.finish(ex_ins, ex_outs, sems)

    any_spec = pl.BlockSpec(memory_space=pl.ANY)
    operands = (a, b, residual) if has_res else (a, b)
    out_shape = jax.ShapeDtypeStruct((m, n), out_dtype)
    scratch_shapes = [] if nk == 1 else [pltpu.VMEM((tm, tn), F32)]
    if exchange is None:
        return pl.pallas_call(
            body, name=name, grid=(ni, nj, nk),
            in_specs=[a_spec, b_spec] + ([o_spec] if has_res else []),
            out_specs=o_spec, out_shape=out_shape, scratch_shapes=scratch_shapes,
            compiler_params=_params(("parallel", "parallel", "arbitrary")),
        )(*operands)
    return pl.pallas_call(
        body, name=name, grid=(ni, nj, nk),
        in_specs=[a_spec, b_spec] + ([o_spec] if has_res else []) + [any_spec] * n_ex,
        out_specs=[o_spec] + [any_spec] * n_ex,
        out_shape=[out_shape] + exchange.out_shapes(),
        scratch_shapes=scratch_shapes + exchange.scratch(),
        compiler_params=_params(("arbitrary", "arbitrary", "arbitrary")),
    )(*operands, *exchange.arrays)


def _rms_fwd(x, gain, name):
    t, d = x.shape
    tr = _pick(t, 512, 16)

    def body(x_ref, g_ref, h_ref):
        xv = x_ref[...]
        r = lax.rsqrt(jnp.mean(xv * xv, axis=1, keepdims=True) + NORM_EPS)
        h_ref[...] = (xv * r * g_ref[...]).astype(BF16)

    row = pl.BlockSpec((tr, d), lambda i: (i, 0))
    return pl.pallas_call(
        body, name=name, grid=(t // tr,),
        in_specs=[row, pl.BlockSpec((1, d), lambda i: (0, 0))],
        out_specs=row, out_shape=jax.ShapeDtypeStruct((t, d), BF16),
        compiler_params=_params(("arbitrary",)),
    )(x, gain)


def _carry_exchange(body, exchange, n_in, n_out, n_scratch, is_first, is_last):
    n = exchange.n

    def carrying(*refs):
        ins, ex_ins = refs[:n_in], refs[n_in:n_in + n]
        outs = refs[n_in + n:n_in + n + n_out]
        ex_outs = refs[n_in + n + n_out:n_in + 2 * n + n_out]
        scratch = refs[n_in + 2 * n + n_out:n_in + 2 * n + n_out + n_scratch]
        sems = refs[n_in + 2 * n + n_out + n_scratch:]

        @pl.when(is_first())
        def _():
            exchange.start(ex_ins, ex_outs, sems)

        body(*ins, *outs, *scratch)

        @pl.when(is_last())
        def _():
            exchange.finish(ex_ins, ex_outs, sems)

    return carrying


def _rms_bwd(x, dh, g_in, gain, name, exchange=None):
    t, d = x.shape
    tr = _pick(t, 512, 16)
    nt = t // tr

    def body(x_ref, dh_ref, gin_ref, g_ref, dx_ref, dxb_ref, dg_ref):
        i = pl.program_id(0)
        xv, dhv = x_ref[...], dh_ref[...]
        r = lax.rsqrt(jnp.mean(xv * xv, axis=1, keepdims=True) + NORM_EPS)
        xh = xv * r
        dxh = dhv * g_ref[...]
        dx = gin_ref[...] + r * (dxh - xh * jnp.mean(dxh * xh, axis=1, keepdims=True))
        dx_ref[...] = dx
        dxb_ref[...] = dx.astype(BF16)

        @pl.when(i == 0)
        def _():
            dg_ref[...] = jnp.zeros_like(dg_ref)

        dg_ref[...] += jnp.sum(dhv * xh, axis=0, keepdims=True)

    row = pl.BlockSpec((tr, d), lambda i: (i, 0))
    vec = pl.BlockSpec((1, d), lambda i: (0, 0))
    in_specs, out_specs = [row, row, row, vec], [row, row, vec]
    out_shape = [jax.ShapeDtypeStruct((t, d), F32), jax.ShapeDtypeStruct((t, d), BF16),
                 jax.ShapeDtypeStruct((1, d), F32)]
    operands, scratch = [x, dh, g_in, gain], []
    if exchange is not None:
        any_spec = pl.BlockSpec(memory_space=pl.ANY)
        body = _carry_exchange(body, exchange, 4, 3, 0, lambda: pl.program_id(0) == 0,
                               lambda: pl.program_id(0) == nt - 1)
        in_specs, out_specs = in_specs + [any_spec] * exchange.n, out_specs + [any_spec] * exchange.n
        out_shape, operands = out_shape + exchange.out_shapes(), operands + exchange.arrays
        scratch = exchange.scratch()
    return pl.pallas_call(
        body, name=name, grid=(nt,), in_specs=in_specs, out_specs=out_specs, out_shape=out_shape,
        scratch_shapes=scratch, compiler_params=_params(("arbitrary",)),
    )(*operands)


def _loss_grad(y, target, name):
    t, d = y.shape
    tr = _pick(t, 512, 16)

    def body(y_ref, t_ref, s_ref, g_ref, gb_ref):
        i = pl.program_id(0)
        e = y_ref[...] - t_ref[...]
        g = e * (1.0 / d)
        g_ref[...] = g
        gb_ref[...] = g.astype(BF16)

        @pl.when(i == 0)
        def _():
            s_ref[...] = jnp.zeros_like(s_ref)

        s_ref[...] += jnp.sum(jnp.sum(e * e, axis=1, keepdims=True), axis=0, keepdims=True)

    row = pl.BlockSpec((tr, d), lambda i: (i, 0))
    return pl.pallas_call(
        body, name=name, grid=(t // tr,),
        in_specs=[row, row],
        out_specs=[pl.BlockSpec((1, 1), lambda i: (0, 0)), row, row],
        out_shape=[jax.ShapeDtypeStruct((1, 1), F32), jax.ShapeDtypeStruct((t, d), F32),
                   jax.ShapeDtypeStruct((t, d), BF16)],
        compiler_params=_params(("arbitrary",)),
    )(y, target)


def _causal_weights(ws_ref, g, chunk, transposed):
    rows = lax.broadcasted_iota(jnp.int32, (chunk, chunk), 0)
    cols = lax.broadcasted_iota(jnp.int32, (chunk, chunk), 1)
    keep = (cols >= rows) if transposed else (rows >= cols)
    return jnp.where(keep, ws_ref[g], 0.0).astype(BF16)


def _mix_fwd(uvz, v_gain, w_s, b_s, name):
    t, w3 = uvz.shape
    w = w3 // 3
    groups, chunk = w_s.shape[0], w_s.shape[1]
    gd = w // groups

    def body(uvz_ref, gam_ref, ws_ref, bs_ref, y_ref):
        gv = _gelu(uvz_ref[:, w:2 * w])
        r = lax.rsqrt(jnp.mean(gv * gv, axis=1, keepdims=True) + NORM_EPS)
        vn = (gv * r * gam_ref[...]).astype(BF16)
        for g in range(groups):
            sl = slice(g * gd, (g + 1) * gd)
            mixed = _dot(_causal_weights(ws_ref, g, chunk, False), vn[:, sl]) + bs_ref[g]
            u = uvz_ref[:, g * gd:(g + 1) * gd]
            z = uvz_ref[:, 2 * w + g * gd:2 * w + (g + 1) * gd]
            y_ref[:, sl] = (_gelu(u) * mixed * (z * _sigmoid(z))).astype(BF16)

    return pl.pallas_call(
        body, name=name, grid=(t // chunk,),
        in_specs=[pl.BlockSpec((chunk, w3), lambda i: (i, 0)),
                  pl.BlockSpec((1, w), lambda i: (0, 0)),
                  pl.BlockSpec((groups, chunk, chunk), lambda i: (0, 0, 0)),
                  pl.BlockSpec((groups, chunk, 1), lambda i: (0, 0, 0))],
        out_specs=pl.BlockSpec((chunk, w), lambda i: (i, 0)),
        out_shape=jax.ShapeDtypeStruct((t, w), BF16),
        compiler_params=_params(("arbitrary",)),
    )(uvz, v_gain, w_s, b_s)


def _mix_bwd(uvz, dy, v_gain, w_s, w_s_t, b_s, name):
    t, w3 = uvz.shape
    w = w3 // 3
    groups, chunk = w_s.shape[0], w_s.shape[1]
    gd = w // groups

    def body(uvz_ref, dy_ref, gam_ref, ws_ref, wst_ref, bs_ref, d_ref, dws_ref, dbs_ref, dgam_ref,
             dvn_ref):
        i = pl.program_id(0)

        @pl.when(i == 0)
        def _():
            dws_ref[...] = jnp.zeros_like(dws_ref)
            dbs_ref[...] = jnp.zeros_like(dbs_ref)
            dgam_ref[...] = jnp.zeros_like(dgam_ref)

        gv, dgv = _gelu_and_grad(uvz_ref[:, w:2 * w])
        r = lax.rsqrt(jnp.mean(gv * gv, axis=1, keepdims=True) + NORM_EPS)
        vh = gv * r
        gam = gam_ref[...]
        vn = (vh * gam).astype(BF16)
        rows = lax.broadcasted_iota(jnp.int32, (chunk, chunk), 0)
        cols = lax.broadcasted_iota(jnp.int32, (chunk, chunk), 1)
        for g in range(groups):
            sl = slice(g * gd, (g + 1) * gd)
            mixed = _dot(_causal_weights(ws_ref, g, chunk, False), vn[:, sl]) + bs_ref[g]
            gu, dgu = _gelu_and_grad(uvz_ref[:, g * gd:(g + 1) * gd])
            z = uvz_ref[:, 2 * w + g * gd:2 * w + (g + 1) * gd]
            sz = _sigmoid(z)
            silu = z * sz
            dyv = dy_ref[:, sl]
            dmixed = dyv * gu * silu
            d_ref[:, sl] = (dyv * mixed * silu * dgu).astype(BF16)
            d_ref[:, 2 * w + g * gd:2 * w + (g + 1) * gd] = (
                dyv * gu * mixed * (sz * (1.0 + z * (1.0 - sz)))).astype(BF16)
            dmb = dmixed.astype(BF16)
            dws_ref[g] += jnp.where(rows >= cols, _dot_nt(dmb, vn[:, sl]), 0.0)
            dbs_ref[g] += jnp.sum(dmixed, axis=1, keepdims=True)
            dvn_ref[:, sl] = _dot(_causal_weights(wst_ref, g, chunk, True), dmb)
        dvn = dvn_ref[...]
        dgam_ref[...] += jnp.sum(dvn * vh, axis=0, keepdims=True)
        dvh = dvn * gam
        dgvv = r * (dvh - vh * jnp.mean(dvh * vh, axis=1, keepdims=True))
        d_ref[:, w:2 * w] = (dgvv * dgv).astype(BF16)

    return pl.pallas_call(
        body, name=name, grid=(t // chunk,),
        in_specs=[pl.BlockSpec((chunk, w3), lambda i: (i, 0)),
                  pl.BlockSpec((chunk, w), lambda i: (i, 0)),
                  pl.BlockSpec((1, w), lambda i: (0, 0)),
                  pl.BlockSpec((groups, chunk, chunk), lambda i: (0, 0, 0)),
                  pl.BlockSpec((groups, chunk, chunk), lambda i: (0, 0, 0)),
                  pl.BlockSpec((groups, chunk, 1), lambda i: (0, 0, 0))],
        out_specs=[pl.BlockSpec((chunk, w3), lambda i: (i, 0)),
                   pl.BlockSpec((groups, chunk, chunk), lambda i: (0, 0, 0)),
                   pl.BlockSpec((groups, chunk, 1), lambda i: (0, 0, 0)),
                   pl.BlockSpec((1, w), lambda i: (0, 0))],
        out_shape=[jax.ShapeDtypeStruct((t, w3), BF16),
                   jax.ShapeDtypeStruct((groups, chunk, chunk), F32),
                   jax.ShapeDtypeStruct((groups, chunk, 1), F32),
                   jax.ShapeDtypeStruct((1, w), F32)],
        scratch_shapes=[pltpu.VMEM((chunk, w), F32)],
        compiler_params=_params(("arbitrary",)),
    )(uvz, dy, v_gain, w_s, w_s_t, b_s)


def _attn_prep(proj, q_gain, k_gain, f_bias, heads, hd, name):
    t = proj.shape[0]
    bw = heads * hd
    tr = _pick(t, 256, 16)
    fcol = 4 * bw // LANES

    def body(q_ref, k_ref, v_ref, f_ref, gq_ref, gk_ref, fb_ref, qn_ref, kn_ref, vb_ref, cum_ref,
             qsq_ref, ksq_ref, first_ref, last_ref, carry_ref):
        i = pl.program_id(0)

        @pl.when(i == 0)
        def _():
            carry_ref[...] = jnp.zeros_like(carry_ref)

        lane = lax.broadcasted_iota(jnp.int32, (1, LANES), 1)
        for src, gain, dst, sq_ref in ((q_ref, gq_ref, qn_ref, qsq_ref),
                                       (k_ref, gk_ref, kn_ref, ksq_ref)):
            sq_row = jnp.zeros((1, LANES), F32)
            for h in range(heads):
                sl = slice(h * hd, (h + 1) * hd)
                v = src[:, sl]
                r = lax.rsqrt(jnp.mean(v * v, axis=1, keepdims=True) + NORM_EPS)
                normed = (v * r * gain[...]).astype(BF16)
                dst[:, sl] = normed
                nf = normed.astype(F32)
                sq = jnp.max(jnp.sum(nf * nf, axis=1, keepdims=True), axis=0, keepdims=True)
                sq_row = jnp.where(lane == h, sq, sq_row)
            sq_ref[0] = sq_row
        vb_ref[...] = v_ref[...].astype(BF16)
        fl = f_ref[...] + fb_ref[...]
        log_f = jnp.minimum(fl, 0.0) - jnp.log(1.0 + jnp.exp(-jnp.abs(fl)))
        rows = lax.broadcasted_iota(jnp.int32, (tr, tr), 0)
        cols = lax.broadcasted_iota(jnp.int32, (tr, tr), 1)
        lower = jnp.where(rows >= cols, 1.0, 0.0).astype(BF16)
        hi, mid, lo = _split3(log_f)
        cum_ref[...] = (_dot(lower, hi) + _dot(lower, mid) + _dot(lower, lo)) + carry_ref[...]
        carry_ref[...] = cum_ref[tr - 1:tr, :]
        first_ref[0] = cum_ref[0:1, :]
        last_ref[0] = cum_ref[tr - 1:tr, :]

    wide = lambda col: pl.BlockSpec((tr, bw), lambda i: (i, col))
    vec = pl.BlockSpec((1, hd), lambda i: (0, 0))
    stat = pl.BlockSpec((1, 1, LANES), lambda i: (i, 0, 0))
    return pl.pallas_call(
        body, name=name, grid=(t // tr,),
        in_specs=[wide(0), wide(1), wide(2), pl.BlockSpec((tr, LANES), lambda i: (i, fcol)),
                  vec, vec, pl.BlockSpec((1, LANES), lambda i: (0, 0))],
        out_specs=[wide(0), wide(0), wide(0), pl.BlockSpec((tr, LANES), lambda i: (i, 0))]
        + [stat] * 4,
        out_shape=[jax.ShapeDtypeStruct((t, bw), BF16)] * 3 + [jax.ShapeDtypeStruct((t, LANES), F32)]
        + [jax.ShapeDtypeStruct((t // tr, 1, LANES), F32)] * 4,
        scratch_shapes=[pltpu.VMEM((1, LANES), F32)],
        compiler_params=_params(("arbitrary",)),
    )(proj, proj, proj, proj, q_gain, k_gain, f_bias)


def _skip_plan(qsq, ksq, first, last, t, tq, hd, name):
    nt = qsq.shape[0]
    nq = t // tq
    r = nt // nq
    scale2 = hd ** -0.5 * LOG2E

    def body(qsq_ref, ksq_ref, first_ref, last_ref, out_ref):
        kmax = ksq_ref[0]
        for tile in range(1, nt):
            kmax = jnp.maximum(kmax, ksq_ref[tile])
        for i in range(nq):
            qmax = qsq_ref[i * r]
            for tile in range(i * r + 1, (i + 1) * r):
                qmax = jnp.maximum(qmax, qsq_ref[tile])
            coef = 2.0 * scale2 * jnp.sqrt(qmax * kmax)
            start = first_ref[i * r]
            count = jnp.zeros((1, LANES), jnp.int32)
            for j in range(i):
                bound = coef + (start - last_ref[(j + 1) * r - 1]) * LOG2E
                count = count + jnp.where(bound <= UNDERFLOW_LOG2, 1, 0)
            out_ref[i:i + 1, :] = count

    return pl.pallas_call(
        body, name=name, out_shape=jax.ShapeDtypeStruct((nq, LANES), jnp.int32),
    )(qsq, ksq, first, last)


def _flash_fwd(first_block, qn, kn, vb, ck, proj, heads, hd, tq, name):
    t = qn.shape[0]
    nq = t // tq
    th = tq // 2
    scale2 = hd ** -0.5 * LOG2E
    zcol = 3 * heads
    rc = _pick(tq, SOFTMAX_ROWS, 16)
    reps = th // LANES

    def body(first_ref, q_ref, k_ref, v_ref, ck_ref, z_ref, o_ref, y_ref, lse_ref,
             m_s, l_s, acc_s, s_a, s_b, p_a, p_b, al_a, al_b):
        i = pl.program_id(1)
        j0 = first_ref[pl.program_id(0), i]
        bufs = ((s_a, p_a, al_a), (s_b, p_b, al_b))
        m_s[...] = jnp.full_like(m_s, -jnp.inf)
        l_s[...] = jnp.zeros_like(l_s)
        acc_s[...] = jnp.zeros_like(acc_s)
        p_b[...] = jnp.zeros_like(p_b)
        al_b[...] = jnp.ones_like(al_b)

        def scores(j, half, first_row=0):
            off = pl.multiple_of(j * tq + half * th, th)
            bufs[half][0][first_row:, :] = _dot_nt(q_ref[first_row:, :], k_ref[pl.ds(off, th), :])

        def values(j, half, first_row=0):
            off = pl.multiple_of(j * tq + half * th, th)
            _, p_buf, al = bufs[half]
            acc_s[first_row:, :] = (
                jnp.tile(al[first_row:, :], (1, hd // LANES)) * acc_s[first_row:, :]
                + _dot(p_buf[first_row:, :], v_ref[pl.ds(off, th), :]))

        def softmax(j, half, masked):
            s_buf, p_buf, al = bufs[half]
            ck2 = ck_ref[0, j][:, half * th:(half + 1) * th] * LOG2E
            chunks = range((half * th) // rc if masked else 0, tq // rc)
            for c in chunks:
                r = slice(c * rc, (c + 1) * rc)
                s = s_buf[r, :] * scale2 - ck2
                if masked and c * rc < (half + 1) * th:
                    rows = lax.broadcasted_iota(jnp.int32, (rc, th), 0) + c * rc
                    cols = lax.broadcasted_iota(jnp.int32, (rc, th), 1) + half * th
                    s = jnp.where(rows >= cols, s, -jnp.inf)
                s_buf[r, :] = s
                m_prev = m_s[r, :]
                m_new = jnp.maximum(m_prev, jnp.max(s, axis=1, keepdims=True))
                al[r, :] = jnp.exp2(m_prev - m_new)
                m_s[r, :] = m_new
            for c in chunks:
                r = slice(c * rc, (c + 1) * rc)
                p = jnp.exp2(s_buf[r, :] - jnp.tile(m_s[r, :], (1, reps)))
                p_buf[r, :] = p.astype(BF16)
                lane_sum = p[:, 0:LANES]
                for b in range(1, reps):
                    lane_sum = lane_sum + p[:, b * LANES:(b + 1) * LANES]
                l_s[r, :] = al[r, :] * l_s[r, :] + lane_sum

        scores(j0, 0)

        def loop_body(j, carry):
            scores(j, 1)
            values(jnp.maximum(j - 1, 0), 1)
            softmax(j, 0, False)
            scores(j + 1, 0)
            values(j, 0)
            softmax(j, 1, False)
            return carry

        lax.fori_loop(j0, i, loop_body, 0)
        values(jnp.maximum(i - 1, 0), 1)
        scores(i, 1, first_row=th)
        softmax(i, 0, True)
        values(i, 0)
        softmax(i, 1, True)
        values(i, 1, first_row=th)
        l = jnp.sum(l_s[...], axis=1, keepdims=True)
        o = acc_s[...] / l
        z = z_ref[...]
        o_ref[...] = o
        y_ref[...] = (o * (z * _sigmoid(z))).astype(BF16)
        lse_ref[0] = m_s[:, 0:1] + jnp.log(l) * LOG2E

    blk = pl.BlockSpec((tq, hd), lambda h, i, first: (i, h))
    head = pl.BlockSpec((t, hd), lambda h, i, first: (0, h))
    col = pl.BlockSpec((1, tq, 1), lambda h, i, first: (h, i, 0))
    stat = pltpu.VMEM((tq, LANES), F32)
    return pl.pallas_call(
        body, name=name,
        grid_spec=pltpu.PrefetchScalarGridSpec(
            num_scalar_prefetch=1, grid=(heads, nq),
            in_specs=[blk, head, head,
                      pl.BlockSpec((1, nq, 1, tq), lambda h, i, first: (h, 0, 0, 0)),
                      pl.BlockSpec((tq, hd), lambda h, i, first: (i, zcol + h))],
            out_specs=[blk, blk, col],
            scratch_shapes=[stat, stat, pltpu.VMEM((tq, hd), F32),
                            pltpu.VMEM((tq, th), F32), pltpu.VMEM((tq, th), F32),
                            pltpu.VMEM((tq, th), BF16), pltpu.VMEM((tq, th), BF16), stat, stat]),
        out_shape=[jax.ShapeDtypeStruct((t, heads * hd), F32),
                   jax.ShapeDtypeStruct((t, heads * hd), BF16),
                   jax.ShapeDtypeStruct((heads, t, 1), F32)],
        compiler_params=_params(("arbitrary", "arbitrary")),
    )(first_block, qn, kn, vb, ck, proj)


def _attn_bwd_prep(dy, proj, o, heads, hd, name):
    t, bw = dy.shape
    tr = _pick(t, 512, 16)

    def body(dy_ref, z_ref, o_ref, do_ref, dz_ref, delta_ref):
        dyv, z, ov = dy_ref[...], z_ref[...], o_ref[...]
        sz = _sigmoid(z)
        do = dyv * (z * sz)
        do_ref[...] = do.astype(BF16)
        dz_ref[...] = (dyv * ov * (sz * (1.0 + z * (1.0 - sz)))).astype(BF16)
        prod = do * ov
        for h in range(heads):
            delta_ref[h] = jnp.sum(prod[:, h * hd:(h + 1) * hd], axis=1, keepdims=True)

    row = pl.BlockSpec((tr, bw), lambda i: (i, 0))
    return pl.pallas_call(
        body, name=name, grid=(t // tr,),
        in_specs=[row, pl.BlockSpec((tr, bw), lambda i: (i, 3)), row],
        out_specs=[row, row, pl.BlockSpec((heads, tr, 1), lambda i: (0, i, 0))],
        out_shape=[jax.ShapeDtypeStruct((t, bw), BF16), jax.ShapeDtypeStruct((t, bw), BF16),
                   jax.ShapeDtypeStruct((heads, t, 1), F32)],
        compiler_params=_params(("arbitrary",)),
    )(dy, proj, o)


def _flash_bwd(first_block, qn, kn, vb, do, ck, lse, delta, heads, hd, tq, name):
    t = qn.shape[0]
    nq = t // tq
    scale = hd ** -0.5
    scale2 = scale * LOG2E

    def body(first_ref, q_ref, k_ref, v_ref, do_ref, ck_ref, lse_ref, delta_ref,
             dq_ref, dk_ref, dv_ref, dcq_ref, dck_ref, dq_s, dcq_s):
        i = pl.program_id(1)
        j0 = first_ref[pl.program_id(0), i]

        @pl.when(i == 0)
        def _():
            dk_ref[...] = jnp.zeros_like(dk_ref)
            dv_ref[...] = jnp.zeros_like(dv_ref)
            dck_ref[...] = jnp.zeros_like(dck_ref)

        dq_s[...] = jnp.zeros_like(dq_s)
        dcq_s[...] = jnp.zeros_like(dcq_s)

        def step(j, masked, row0=0, col0=0, ncol=tq):
            off = pl.multiple_of(j * tq + col0, ncol)
            kblk = k_ref[pl.ds(off, ncol), :]
            q, dov = q_ref[row0:, :], do_ref[row0:, :]
            s = _dot_nt(q, kblk) * scale2 - ck_ref[0, j][:, col0:col0 + ncol] * LOG2E
            if masked:
                rows = lax.broadcasted_iota(jnp.int32, (tq - row0, ncol), 0) + row0
                cols = lax.broadcasted_iota(jnp.int32, (tq - row0, ncol), 1) + col0
                s = jnp.where(rows >= cols, s, -jnp.inf)
            p = jnp.exp2(s - lse_ref[0, row0:, :])
            dp = _dot_nt(dov, v_ref[pl.ds(off, ncol), :])
            ds = p * (dp - delta_ref[0, row0:, :])
            dsb = ds.astype(BF16)
            dv_ref[pl.ds(off, ncol), :] += _dot_tn(p.astype(BF16), dov)
            dk_ref[pl.ds(off, ncol), :] += _dot_tn(dsb, q) * scale
            dq_s[row0:, :] += _dot(dsb, kblk) * scale
            dcq_s[row0:, :] += jnp.sum(ds, axis=1, keepdims=True)
            dck_ref[0, j, :, col0:col0 + ncol] += jnp.sum(ds, axis=0, keepdims=True)

        def loop_body(j, carry):
            step(j, False)
            return carry

        lax.fori_loop(j0, i, loop_body, 0)
        step(i, True, 0, 0, tq // 2)
        step(i, True, tq // 2, tq // 2, tq // 2)
        dq_ref[...] = dq_s[...]
        dcq_ref[0] = dcq_s[...]

    blk = pl.BlockSpec((tq, hd), lambda h, i, first: (i, h))
    head = pl.BlockSpec((t, hd), lambda h, i, first: (0, h))
    col = pl.BlockSpec((1, tq, 1), lambda h, i, first: (h, i, 0))
    rowv = pl.BlockSpec((1, nq, 1, tq), lambda h, i, first: (h, 0, 0, 0))
    full = jax.ShapeDtypeStruct((t, heads * hd), F32)
    return pl.pallas_call(
        body, name=name,
        grid_spec=pltpu.PrefetchScalarGridSpec(
            num_scalar_prefetch=1, grid=(heads, nq),
            in_specs=[blk, head, head, blk, rowv, col, col],
            out_specs=[blk, head, head, col, rowv],
            scratch_shapes=[pltpu.VMEM((tq, hd), F32), pltpu.VMEM((tq, 1), F32)]),
        out_shape=[full, full, full, jax.ShapeDtypeStruct((heads, t, 1), F32),
                   jax.ShapeDtypeStruct((heads, nq, 1, tq), F32)],
        compiler_params=_params(("arbitrary", "arbitrary")),
    )(first_block, qn, kn, vb, do, ck, lse, delta)


def _attn_bwd_post(dqn, dkn, dv, dz, proj, dcq, dck, q_gain, k_gain, f_bias, heads, hd, name):
    t, bw = dqn.shape
    tr = _pick(t, 256, 16)
    nb = t // tr
    fcol = 4 * bw // LANES
    width = 4 * bw + LANES

    def body(dq_ref, dk_ref, dv_ref, dz_ref, q_ref, k_ref, f_ref, dcq_ref, dck_ref, gq_ref, gk_ref,
             fb_ref, d_ref, dgq_ref, dgk_ref, dfb_ref, carry_ref, rc_ref):
        i = pl.program_id(0)

        @pl.when(i == 0)
        def _():
            carry_ref[...] = jnp.zeros_like(carry_ref)
            dgq_ref[...] = jnp.zeros_like(dgq_ref)
            dgk_ref[...] = jnp.zeros_like(dgk_ref)
            dfb_ref[...] = jnp.zeros_like(dfb_ref)

        for idx, (g_ref, raw_ref, gain_ref, dgain_ref) in enumerate(
                ((dq_ref, q_ref, gq_ref, dgq_ref), (dk_ref, k_ref, gk_ref, dgk_ref))):
            gain = gain_ref[...]
            dgain = jnp.zeros((1, hd), F32)
            for h in range(heads):
                sl = slice(h * hd, (h + 1) * hd)
                v, dn = raw_ref[:, sl], g_ref[:, sl]
                r = lax.rsqrt(jnp.mean(v * v, axis=1, keepdims=True) + NORM_EPS)
                vh = v * r
                dgain = dgain + jnp.sum(dn * vh, axis=0, keepdims=True)
                dvh = dn * gain
                draw = r * (dvh - vh * jnp.mean(dvh * vh, axis=1, keepdims=True))
                d_ref[:, idx * bw + h * hd:idx * bw + (h + 1) * hd] = draw.astype(BF16)
            dgain_ref[...] += dgain
        d_ref[:, 2 * bw:3 * bw] = dv_ref[...].astype(BF16)
        d_ref[:, 3 * bw:4 * bw] = dz_ref[...]
        rows = lax.broadcasted_iota(jnp.int32, (tr, tr), 0)
        cols = lax.broadcasted_iota(jnp.int32, (tr, tr), 1)
        upper = jnp.where(cols >= rows, 1.0, 0.0).astype(BF16)
        hi, mid, lo = _split3(dcq_ref[...] - dck_ref[...])
        rc_ref[...] = (_dot(upper, hi) + _dot(upper, mid) + _dot(upper, lo)) + carry_ref[...]
        carry_ref[...] = rc_ref[0:1, :]
        df = rc_ref[...] * (1.0 / (1.0 + jnp.exp(f_ref[...] + fb_ref[...])))
        d_ref[:, 4 * bw:] = df.astype(BF16)
        dfb_ref[...] += jnp.sum(df, axis=0, keepdims=True)

    wide = lambda col: pl.BlockSpec((tr, bw), lambda i: (nb - 1 - i, col))
    lane = lambda col: pl.BlockSpec((tr, LANES), lambda i: (nb - 1 - i, col))
    vec = pl.BlockSpec((1, hd), lambda i: (0, 0))
    vecl = pl.BlockSpec((1, LANES), lambda i: (0, 0))
    return pl.pallas_call(
        body, name=name, grid=(nb,),
        in_specs=[wide(0), wide(0), wide(0), wide(0), wide(0), wide(1), lane(fcol), lane(0), lane(0),
                  vec, vec, vecl],
        out_specs=[pl.BlockSpec((tr, width), lambda i: (nb - 1 - i, 0)), vec, vec, vecl],
        out_shape=[jax.ShapeDtypeStruct((t, width), BF16), jax.ShapeDtypeStruct((1, hd), F32),
                   jax.ShapeDtypeStruct((1, hd), F32), jax.ShapeDtypeStruct((1, LANES), F32)],
        scratch_shapes=[pltpu.VMEM((1, LANES), F32), pltpu.VMEM((tr, LANES), F32)],
        compiler_params=_params(("arbitrary",)),
    )(dqn, dkn, dv, dz, proj, proj, proj, dcq, dck, q_gain, k_gain, f_bias)


def _adamw(w, m, v, parts, name, exchange=None):
    nl, r, c = w.shape
    itemsize = parts[0].dtype.itemsize
    unit = 32 // itemsize
    row_bytes = c * (7 * 4 + N_DEV * itemsize * nl)
    tr = _pick(r, max(unit, 12 * 1024 * 1024 // row_bytes), unit)
    nr = r // tr
    c1 = 1.0 / (1.0 - ADAM_B1 ** ADAM_STEP)
    c2 = 1.0 / (1.0 - ADAM_B2 ** ADAM_STEP)

    def body(*refs):
        w_ref, m_ref, v_ref = refs[:3]
        p_refs = refs[3:3 + nl]
        g_ref, d_ref, nm_ref, nv_ref = refs[3 + nl:]
        layer = pl.program_id(0)

        def partial(j):
            p = p_refs[0][j].astype(F32)
            for q in range(1, nl):
                p = jnp.where(layer == q, p_refs[q][j].astype(F32), p)
            return p

        g = partial(0)
        for j in range(1, N_DEV):
            g = g + partial(j)
        nm = ADAM_B1 * m_ref[0] + (1.0 - ADAM_B1) * g
        nv = ADAM_B2 * v_ref[0] + (1.0 - ADAM_B2) * (g * g)
        g_ref[0] = g
        nm_ref[0] = nm
        nv_ref[0] = nv
        d_ref[0] = -ADAM_LR * ((nm * c1) / (jnp.sqrt(nv * c2) + ADAM_EPS) + ADAM_WD * w_ref[0])

    def part_spec(q):
        rest = 0 if q > 0 else nr - 1
        return pl.BlockSpec((N_DEV, tr, c), lambda l, i: (0, jnp.where(l == q, i, rest), 0))

    row = pl.BlockSpec((1, tr, c), lambda l, i: (l, i, 0))
    in_specs, out_specs = [row, row, row] + [part_spec(q) for q in range(nl)], [row] * 4
    out_shape = [jax.ShapeDtypeStruct((nl, r, c), F32)] * 4
    operands, scratch = [w, m, v, *parts], []
    if exchange is not None:
        any_spec = pl.BlockSpec(memory_space=pl.ANY)
        body = _carry_exchange(
            body, exchange, 3 + nl, 4, 0,
            lambda: (pl.program_id(0) == 0) & (pl.program_id(1) == 0),
            lambda: (pl.program_id(0) == nl - 1) & (pl.program_id(1) == nr - 1))
        in_specs, out_specs = in_specs + [any_spec] * exchange.n, out_specs + [any_spec] * exchange.n
        out_shape, operands = out_shape + exchange.out_shapes(), operands + exchange.arrays
        scratch = exchange.scratch()
    return pl.pallas_call(
        body, name=name, grid=(nl, nr), in_specs=in_specs, out_specs=out_specs,
        out_shape=out_shape, scratch_shapes=scratch,
        compiler_params=_params(("arbitrary", "arbitrary")),
    )(*operands)


def _flat_rows(pieces):
    rows = []
    for p in pieces:
        f = p.reshape(-1)
        f = jnp.pad(f, (0, (-f.shape[0]) % LANES))
        rows.append(f.reshape(-1, LANES))
    out = jnp.concatenate(rows, axis=0)
    return jnp.pad(out, ((0, (-out.shape[0]) % 8), (0, 0)))


def _unflat_rows(flat, shapes):
    outs, r0 = [], 0
    lead = flat.shape[:-2]
    for s in shapes:
        size = 1
        for d in s:
            size *= d
        nr = -(-size // LANES)
        piece = flat[..., r0:r0 + nr, :].reshape(lead + (nr * LANES,))[..., :size]
        outs.append(piece.reshape(lead + tuple(s)))
        r0 += nr
    return outs


def kernel(x, a_norm_g, a_w_in, a_v_norm_g, a_w_s, a_b_s, a_w_out, b_norm_g, b_w_in, b_f_bias, b_q_norm_g, b_k_norm_g, b_w_out, loss_target, m_a_norm_g, m_a_w_in, m_a_v_norm_g, m_a_w_s, m_a_b_s, m_a_w_out, m_b_norm_g, m_b_w_in, m_b_f_bias, m_b_q_norm_g, m_b_k_norm_g, m_b_w_out, v_a_norm_g, v_a_w_in, v_a_v_norm_g, v_a_w_s, v_a_b_s, v_a_w_out, v_b_norm_g, v_b_w_in, v_b_f_bias, v_b_q_norm_g, v_b_k_norm_g, v_b_w_out):
    t, d = x.shape[1], x.shape[2]
    n_a, n_b = a_w_in.shape[0], b_w_in.shape[0]
    depth = n_a + n_b
    aw = a_w_out.shape[1] * N_DEV
    groups, chunk = a_w_s.shape[1], a_w_s.shape[2]
    heads, hd = b_f_bias.shape[1], b_q_norm_g.shape[1]
    bw = heads * hd
    b_cols = b_w_in.shape[2]
    tq = _pick(t, 512, LANES)
    nq = t // tq
    me = _dev_index((lax.axis_index("x"), lax.axis_index("y"), lax.axis_index("c")))

    pad_h = lambda v: jnp.pad(v, ((0, 0), (0, LANES - heads)))
    f_bias_p = pad_h(b_f_bias)
    b_s_col = a_b_s.reshape(n_a, groups, chunk, 1)
    w_s_t = a_w_s.transpose(0, 1, 3, 2)

    def weight_gather(i, extra=()):
        j = i // 2
        if i % 2 == 0:
            return _Gather([a_w_in[j].astype(BF16), a_w_out[j].astype(BF16)] + [a for a, _ in extra],
                           [1, 0] + [ax for _, ax in extra])
        return _Gather([b_w_in[j].astype(BF16), b_w_out[j].astype(BF16)] + [a for a, _ in extra],
                       [None, 0] + [ax for _, ax in extra])

    def whole_w_in(i, w_in):
        if i % 2 == 1:
            w_in = w_in.transpose(1, 0, 2).reshape(d, 4 * bw + heads)
            w_in = jnp.pad(w_in, ((0, 0), (0, LANES - heads)))
        return w_in

    (w_in0,) = _run_exchange(_Gather([a_w_in[0].astype(BF16)], [1]), "gather_weights")
    weights = {0: [w_in0, None]}
    late = ((a_w_out[0].astype(BF16), 0), (b_norm_g, 1))

    xs = [x[0]]
    saved = []
    for i in range(depth):
        j = i // 2
        xi = xs[-1]
        nxt = weight_gather(i + 1, extra=late if i == 0 else ()) if i + 1 < depth else None
        if i % 2 == 0:
            h = _rms_fwd(xi, a_norm_g[j:j + 1], f"a{j}_norm")
            res = _matmul(h, weights[i][0], "nn", F32, f"a{j}_in", tn=2048, tk=d, exchange=nxt)
        else:
            h = _rms_fwd(xi, b_norm_full[j:j + 1], f"b{j}_norm")
            res = _matmul(h, weights[i][0], "nn", F32, f"b{j}_in", tn=1664, tk=d, exchange=nxt)
        if nxt is None:
            pre = res
        else:
            pre = res[0]
            weights[i + 1] = [whole_w_in(i + 1, res[1]), res[2]]
            if i == 0:
                weights[0][1], b_norm_full = res[3], res[4]
        w_in, w_out = weights[i]
        if i % 2 == 0:
            y = _mix_fwd(pre, a_v_norm_g[j:j + 1], a_w_s[j], b_s_col[j], f"a{j}_mix")
            xs.append(_matmul(y, w_out, "nn", F32, f"a{j}_out", residual=xi))
            saved.append((h, pre, y))
        else:
            qn, kn, vb, cum, qsq, ksq, cum_first, cum_last = _attn_prep(
                pre, b_q_norm_g[j:j + 1], b_k_norm_g[j:j + 1], f_bias_p[j:j + 1], heads, hd,
                f"b{j}_prep")
            first = _skip_plan(qsq, ksq, cum_first, cum_last, t, tq, hd, f"b{j}_plan")
            first = first[:, :heads].T
            ck = cum[:, :heads].T.reshape(heads, nq, 1, tq)
            o, y, lse = _flash_fwd(first, qn, kn, vb, ck, pre, heads, hd, tq, f"b{j}_attn")
            xs.append(_matmul(y, w_out, "nn", F32, f"b{j}_out", residual=xi))
            saved.append((h, pre, y, qn, kn, vb, ck, o, lse, first))

    sq, g, gb = _loss_grad(xs[-1], loss_target[0], "loss")
    loss = 0.5 * lax.psum(sq[0, 0], AXES) / d

    d_a_norm, d_a_vnorm, d_a_ws, d_a_bs = [None] * n_a, [None] * n_a, [None] * n_a, [None] * n_a
    d_b_norm, d_b_fb, d_b_gq, d_b_gk = [None] * n_b, [None] * n_b, [None] * n_b, [None] * n_b
    recv_in, recv_out = {}, {}
    pending = None

    def in_blocks(i, dw_in):
        if i % 2 == 0:
            return _AllToAll([dw_in], [1])
        rows = dw_in.shape[0]
        return _AllToAll(
            [dw_in[:, :4 * bw + heads].reshape(rows, N_DEV, b_cols).transpose(1, 0, 2)], [None])

    for i in reversed(range(depth)):
        j = i // 2
        xi = xs[i]
        w_in, w_out = weights[i]
        if i % 2 == 0:
            h, uvz, y = saved[i]
            dy = _matmul(gb, w_out, "nt", F32, f"a{j}_dy", tn=2048, tk=d)
            dw_out = _matmul(y, gb, "tn", BF16, f"a{j}_dwout", tn=2048)
            dpre, d_a_ws[j], dbs, d_a_vnorm[j] = _mix_bwd(
                uvz, dy, a_v_norm_g[j:j + 1], a_w_s[j], w_s_t[j], b_s_col[j], f"a{j}_mixbwd")
            d_a_bs[j] = dbs.reshape(groups, chunk)
            name, gain = f"a{j}", a_norm_g[j:j + 1]
        else:
            h, proj, y, qn, kn, vb, ck, o, lse, first = saved[i]
            dy = _matmul(gb, w_out, "nt", F32, f"b{j}_dy", tn=2048, tk=d)
            dw_out = _matmul(y, gb, "tn", BF16, f"b{j}_dwout", tn=2048)
            do, dz, delta = _attn_bwd_prep(dy, proj, o, heads, hd, f"b{j}_bwdprep")
            dqn, dkn, dv, dcq, dck = _flash_bwd(first, qn, kn, vb, do, ck, lse, delta, heads, hd,
                                                tq, f"b{j}_attnbwd")
            per_token = lambda v: pad_h(v.reshape(heads, t).T)
            dpre, d_b_gq[j], d_b_gk[j], dfb = _attn_bwd_post(
                dqn, dkn, dv, dz, proj, per_token(dcq), per_token(dck), b_q_norm_g[j:j + 1],
                b_k_norm_g[j:j + 1], f_bias_p[j:j + 1], heads, hd, f"b{j}_bwdpost")
            d_b_fb[j] = dfb[:, :heads]
            name, gain = f"b{j}", b_norm_full[j:j + 1]
        if pending is None:
            dh = _matmul(dpre, w_in, "nt", F32, name + "_dh", tn=2048)
        else:
            dh, got = _matmul(dpre, w_in, "nt", F32, name + "_dh", tn=2048, exchange=pending[1])
            parts_in = pending[2] + [got]
            recv_in[pending[0]] = parts_in[0] if len(parts_in) == 1 else jnp.concatenate(
                parts_in, axis=1)
        n_parts = (4 if i == 0 else 2) if i % 2 == 0 else 1
        rows = d // n_parts
        riding = _AllToAll([dw_out], [0])
        landed = []
        for part in range(n_parts):
            h_part = h if n_parts == 1 else h[:, part * rows:(part + 1) * rows]
            dw_part, got = _matmul(h_part, dpre, "tn", BF16, f"{name}_dwin{part}",
                                   tn=2048 if i % 2 == 0 else 1664, exchange=riding)
            if part == 0:
                recv_out[i] = got
            else:
                landed.append(got)
            riding = in_blocks(i, dw_part)
        if i > 0:
            pending = (i, riding, landed)
            g, gb, dgain = _rms_bwd(xi, dh, g, gain, name + "_normbwd")
        else:
            g, gb, dgain, got = _rms_bwd(xi, dh, g, gain, name + "_normbwd", exchange=riding)
            recv_in[i] = jnp.concatenate(landed + [got], axis=1)
        if i % 2 == 0:
            d_a_norm[j] = dgain
        else:
            d_b_norm[j] = dgain
    grad_x = g[None]

    small = [jnp.concatenate(d_a_norm, 0), jnp.concatenate(d_a_vnorm, 0), jnp.stack(d_a_ws, 0),
             jnp.stack(d_a_bs, 0), jnp.concatenate(d_b_fb, 0), jnp.concatenate(d_b_gq, 0),
             jnp.concatenate(d_b_gk, 0)]
    small_w = [a_norm_g, a_v_norm_g, a_w_s, a_b_s, b_f_bias, b_q_norm_g, b_k_norm_g]
    small_m = [m_a_norm_g, m_a_v_norm_g, m_a_w_s, m_a_b_s, m_b_f_bias, m_b_q_norm_g, m_b_k_norm_g]
    small_v = [v_a_norm_g, v_a_v_norm_g, v_a_w_s, v_a_b_s, v_b_f_bias, v_b_q_norm_g, v_b_k_norm_g]
    small_flat = _flat_rows(small)
    n_small = small_flat.shape[0]
    small_gather = _Gather(
        [jnp.concatenate([small_flat, _flat_rows([jnp.concatenate(d_b_norm, 0)])], axis=0)],
        [None])

    a_layers, b_layers = range(0, depth, 2), range(1, depth, 2)
    *u_a_in, gathered_small = _adamw(a_w_in, m_a_w_in, v_a_w_in, [recv_in[i] for i in a_layers],
                                     "adamw_a_w_in", exchange=small_gather)
    parts_small = gathered_small[:, :n_small]
    parts_b_norm = gathered_small[:, n_small:n_small + n_b * d // LANES].reshape(N_DEV, n_b, d)
    parts_b_norm = lax.dynamic_slice_in_dim(parts_b_norm, me * (d // N_DEV), d // N_DEV, axis=2)
    u_a_out = _adamw(a_w_out, m_a_w_out, v_a_w_out, [recv_out[i] for i in a_layers],
                     "adamw_a_w_out")
    u_b_in = _adamw(b_w_in, m_b_w_in, v_b_w_in, [recv_in[i] for i in b_layers], "adamw_b_w_in")
    u_b_out = _adamw(b_w_out, m_b_w_out, v_b_w_out, [recv_out[i] for i in b_layers],
                     "adamw_b_w_out")
    u_b_norm = [o_[0] for o_ in _adamw(b_norm_g[None], m_b_norm_g[None], v_b_norm_g[None],
                                       [parts_b_norm], "adamw_b_norm")]
    u_small = _adamw(_flat_rows(small_w)[None], _flat_rows(small_m)[None],
                     _flat_rows(small_v)[None], [parts_small], "adamw_small")
    shapes = [w.shape for w in small_w]
    u_small = [_unflat_rows(o_[0], shapes) for o_ in u_small]

    def per_kind(k):
        s = u_small[k]
        return [s[0], u_a_in[k], s[1], s[2], s[3], u_a_out[k], u_b_norm[k], u_b_in[k], s[4], s[5],
                s[6], u_b_out[k]]

    return (loss, grad_x, *per_kind(0), *per_kind(1), *per_kind(2), *per_kind(3))
```
